```python
import math
import jax, jax.numpy as jnp
from jax import lax
import numpy as np

D_MODEL = 2048
BATCH = 2
SEQ = 4096
DEPTH = 1

MIX_WIDTH = D_MODEL
ATTN_WIDTH = MIX_WIDTH // 2
RWKV_WIDTH = MIX_WIDTH - ATTN_WIDTH
HEAD_DIM = 64
ATTN_HEADS = ATTN_WIDTH // HEAD_DIM
ATTN_KV_HEADS = max(1, ATTN_HEADS // 8)
ATTN_GROUP = ATTN_HEADS // ATTN_KV_HEADS
WINDOW = 128
BLOCK = WINDOW
ROPE_THETA = 10000.0
RWKV_HEAD = 64
RWKV_HEADS = RWKV_WIDTH // RWKV_HEAD
DECAY_LORA = 64
A_LORA = 64
GATE_LORA = 160
RWKV_LN_EPS = 64e-5
Q_COLS = ATTN_HEADS * HEAD_DIM
KV_COLS = ATTN_KV_HEADS * HEAD_DIM
ATTN_COLS = Q_COLS + 2 * KV_COLS
RWKV_COLS = 3 * RWKV_WIDTH + DECAY_LORA + A_LORA + GATE_LORA
IN_COLS = ATTN_COLS + RWKV_COLS
N_GROUPS = 8
EXPERTS_PER_GROUP = 8
N_EXPERTS = N_GROUPS * EXPERTS_PER_GROUP
TOP_K = 2
EXPERT_FF = 512
MOE_BLOCK = 128
NORM_EPS = 1e-6

kernel_name = "hymba_swa_rwkv7_hmoe_adaln"

F32 = jnp.float32


def rms_norm(x, g):
    x32 = x.astype(F32)
    y = x32 * lax.rsqrt(jnp.mean(x32 * x32, axis=-1, keepdims=True) + NORM_EPS)
    return (y * g.astype(F32)).astype(x.dtype)


def modulate(h, shift, scale):
    return h * (1 + scale[:, None, :]) + shift[:, None, :]


def rope_tables(seq_len):
    inv_freq = ROPE_THETA ** (-jnp.arange(0, HEAD_DIM, 2, dtype=F32) / HEAD_DIM)
    ang = jnp.arange(seq_len, dtype=F32)[:, None] * inv_freq[None, :]
    return jnp.cos(ang), jnp.sin(ang)


def apply_rope(t, cos, sin):
    t32 = t.astype(F32)
    t1, t2 = jnp.split(t32, 2, axis=-1)
    cs, sn = cos[None, :, None, :], sin[None, :, None, :]
    return jnp.concatenate([t1 * cs - t2 * sn, t2 * cs + t1 * sn], axis=-1).astype(t.dtype)


def sliding_window_attention(q, k, v, sinks):
    B, T, H, Dh = q.shape
    nb = T // BLOCK
    qb = q.reshape(B, nb, BLOCK, ATTN_KV_HEADS, ATTN_GROUP, Dh)
    pad = ((0, 0), (BLOCK, 0), (0, 0), (0, 0))
    kp = jnp.pad(k, pad).reshape(B, nb + 1, BLOCK, ATTN_KV_HEADS, Dh)
    vp = jnp.pad(v, pad).reshape(B, nb + 1, BLOCK, ATTN_KV_HEADS, Dh)
    kb = jnp.concatenate([kp[:, :-1], kp[:, 1:]], axis=2)
    vb = jnp.concatenate([vp[:, :-1], vp[:, 1:]], axis=2)
    s = jnp.einsum('bnqhgd,bnkhd->bnhgqk', qb, kb, preferred_element_type=F32) * (1.0 / math.sqrt(Dh))
    qpos = jnp.arange(BLOCK)[:, None]
    kpos = jnp.arange(2 * BLOCK)[None, :] - BLOCK
    diff = qpos - kpos
    band = (diff >= 0) & (diff < WINDOW)
    in_seq = (jnp.arange(nb)[:, None, None] * BLOCK + kpos[None]) >= 0
    valid = band[None] & in_seq
    s = jnp.where(valid[None, :, None, None], s, -jnp.inf)
    sink = sinks.astype(F32).reshape(1, 1, ATTN_KV_HEADS, ATTN_GROUP, 1, 1)
    m = jnp.maximum(jnp.max(s, axis=-1, keepdims=True), sink)
    p = jnp.exp(s - m)
    denom = jnp.sum(p, axis=-1, keepdims=True) + jnp.exp(sink - m)
    o = jnp.einsum('bnhgqk,bnkhd->bnqhgd', p / denom, vb.astype(F32))
    return o.reshape(B, T, H * Dh).astype(q.dtype)


def token_shift(z, mu):
    z_prev = jnp.pad(z, ((0, 0), (1, 0), (0, 0)))[:, :-1]
    return z + (z_prev - z) * mu


def rwkv7_time_mix(z, w0, w_decay_up, a0, w_a_up, w_g_up, k_k, k_a, r_k, ln_w, ln_b):
    B, T, _ = z.shape
    W, H, N = RWKV_WIDTH, RWKV_HEADS, RWKV_HEAD
    r, k, v, wd, ad, gd = jnp.split(
        z, [W, 2 * W, 3 * W, 3 * W + DECAY_LORA, 3 * W + DECAY_LORA + A_LORA], axis=-1)
    w = -jax.nn.softplus(-(w0 + jnp.tanh(wd) @ w_decay_up)) - 0.5
    decay = jnp.exp(-jnp.exp(w.astype(F32)))
    a = jax.nn.sigmoid(a0 + ad @ w_a_up)
    g = jax.nn.sigmoid(gd) @ w_g_up
    kk = (k * k_k).astype(F32).reshape(B, T, H, N)
    kk = kk / jnp.maximum(jnp.sqrt(jnp.sum(kk * kk, axis=-1, keepdims=True)), 1e-12)
    k = k * (1 + (a - 1) * k_a)

    def heads(t):
        return t.astype(F32).reshape(B, T, H, N)

    r_h, k_h, v_h, a_h = heads(r), heads(k), heads(v), heads(a)
    xs = tuple(jnp.moveaxis(t, 1, 0) for t in
               (r_h, decay.reshape(B, T, H, N), k_h, v_h, -kk, kk * a_h))

    def step(S, inp):
        r_t, w_t, k_t, v_t, aa_t, bb_t = inp
        sa = jnp.einsum('bhvk,bhk->bhv', S, aa_t)
        S = S * w_t[:, :, None, :] + sa[..., None] * bb_t[:, :, None, :] + v_t[..., None] * k_t[:, :, None, :]
        return S, jnp.einsum('bhvk,bhk->bhv', S, r_t)

    S0 = jnp.zeros((B, H, N, N), F32)
    _, ys = lax.scan(step, S0, xs)
    y = jnp.moveaxis(ys, 0, 1)
    mean = jnp.mean(y, axis=-1, keepdims=True)
    var = jnp.mean(jnp.square(y - mean), axis=-1, keepdims=True)
    y = (y - mean) * lax.rsqrt(var + RWKV_LN_EPS)
    y = y.reshape(B, T, W) * ln_w.astype(F32) + ln_b.astype(F32)
    bonus = jnp.sum(r_h * k_h * r_k.astype(F32), axis=-1, keepdims=True) * v_h
    y = (y + bonus.reshape(B, T, W)) * g.astype(F32)
    return y.astype(z.dtype)


def hierarchical_moe(h, w_rg, b_rg, w_re, b_re, w1, w3, w2):
    B, T, D = h.shape
    xt = h.reshape(-1, D)
    N = xt.shape[0]
    g_prob = jax.nn.softmax((xt @ w_rg).astype(F32) + b_rg.astype(F32), axis=-1)
    g_gate, g_idx = lax.top_k(g_prob, 1)
    e_logits = ((xt @ w_re).astype(F32) + b_re.astype(F32)).reshape(N, N_GROUPS, EXPERTS_PER_GROUP)
    e_logits = jnp.take_along_axis(e_logits, g_idx[:, :, None], axis=1)[:, 0]
    e_top, e_idx = lax.top_k(jax.nn.softmax(e_logits, axis=-1), TOP_K)
    e_top = e_top / jnp.sum(e_top, axis=-1, keepdims=True)
    weights = g_gate * e_top
    expert = g_idx * EXPERTS_PER_GROUP + e_idx

    A = N * TOP_K
    flat_e = expert.reshape(-1)
    flat_w = weights.reshape(-1)
    flat_tok = jnp.arange(A) // TOP_K
    order = jnp.argsort(flat_e)
    se, stok, sw = flat_e[order], flat_tok[order], flat_w[order]
    counts = jnp.bincount(flat_e, length=N_EXPERTS)
    starts = jnp.cumsum(counts) - counts
    pcounts = (counts + MOE_BLOCK - 1) // MOE_BLOCK * MOE_BLOCK
    pends = jnp.cumsum(pcounts)
    pstarts = pends - pcounts
    dest = pstarts[se] + jnp.arange(A) - starts[se]
    n_blocks = -(-A // MOE_BLOCK) + N_EXPERTS
    P = n_blocks * MOE_BLOCK
    buf_tok = jnp.zeros((P,), jnp.int32).at[dest].set(stok.astype(jnp.int32))
    buf_w = jnp.zeros((P,), F32).at[dest].set(sw)
    block_expert = jnp.minimum(
        jnp.searchsorted(pends, jnp.arange(n_blocks) * MOE_BLOCK, side='right'), N_EXPERTS - 1)
    xb = xt[buf_tok].reshape(n_blocks, MOE_BLOCK, D)

    def expert_block(args):
        xblk, e = args
        hid = jax.nn.silu(xblk @ w1[e]) * (xblk @ w3[e])
        return hid @ w2[e]

    yb = lax.map(expert_block, (xb, block_expert)).reshape(P, D)
    yb = yb * buf_w[:, None].astype(yb.dtype)
    out = jnp.zeros_like(xt).at[buf_tok].add(yb)
    return out.reshape(B, T, D)


def setup_inputs(seed: int = 0) -> dict:
    key = jax.random.key(seed)
    ks = jax.random.split(key, 32)
    L, D = DEPTH, D_MODEL

    def nrm(k, shape, scale):
        return jax.random.normal(k, shape, F32) * scale

    return {
        "x": nrm(ks[0], (BATCH, SEQ, D), 1.0),
        "c": nrm(ks[1], (BATCH, D), 1.0),
        "w_ada": nrm(ks[2], (L, D, 6 * D), 0.02),
        "b_ada": nrm(ks[3], (L, 6 * D), 0.02),
        "norm1_g": 1.0 + nrm(ks[4], (L, D), 0.02),
        "w_in": nrm(ks[5], (L, D, IN_COLS), D ** -0.5),
        "mu_shift": jax.random.uniform(ks[6], (L, RWKV_COLS), F32),
        "sinks": nrm(ks[7], (L, ATTN_HEADS), 0.5),
        "w0": nrm(ks[8], (L, RWKV_WIDTH), 0.5) - 0.5,
        "w_decay_up": nrm(ks[9], (L, DECAY_LORA, RWKV_WIDTH), 0.1),
        "a0": nrm(ks[10], (L, RWKV_WIDTH), 0.5),
        "w_a_up": nrm(ks[11], (L, A_LORA, RWKV_WIDTH), 0.5 * A_LORA ** -0.5),
        "w_g_up": nrm(ks[12], (L, GATE_LORA, RWKV_WIDTH), GATE_LORA ** -0.5),
        "k_k": 1.0 + nrm(ks[13], (L, RWKV_WIDTH), 0.1),
        "k_a": 1.0 + nrm(ks[14], (L, RWKV_WIDTH), 0.1),
        "r_k": nrm(ks[15], (L, RWKV_HEADS, RWKV_HEAD), 0.1),
        "ln_x_w": 1.0 + nrm(ks[16], (L, RWKV_WIDTH), 0.02),
        "ln_x_b": nrm(ks[17], (L, RWKV_WIDTH), 0.02),
        "w_o": nrm(ks[18], (L, MIX_WIDTH, D), MIX_WIDTH ** -0.5),
        "norm2_g": 1.0 + nrm(ks[19], (L, D), 0.02),
        "w_router_group": nrm(ks[20], (L, D, N_GROUPS), D ** -0.5),
        "b_router_group": nrm(ks[21], (L, N_GROUPS), 0.01),
        "w_router_expert": nrm(ks[22], (L, D, N_EXPERTS), D ** -0.5),
        "b_router_expert": nrm(ks[23], (L, N_EXPERTS), 0.01),
        "w1": nrm(ks[24], (L, N_EXPERTS, D, EXPERT_FF), D ** -0.5),
        "w3": nrm(ks[25], (L, N_EXPERTS, D, EXPERT_FF), D ** -0.5),
        "w2": nrm(ks[26], (L, N_EXPERTS, EXPERT_FF, D), EXPERT_FF ** -0.5),
        "final_g": 1.0 + nrm(ks[27], (D,), 0.02),
    }


def reference(x, c, w_ada, b_ada, norm1_g, w_in, mu_shift, sinks, w0, w_decay_up, a0, w_a_up,
              w_g_up, k_k, k_a, r_k, ln_x_w, ln_x_b, w_o, norm2_g, w_router_group, b_router_group,
              w_router_expert, b_router_expert, w1, w3, w2, final_g):
    B, T, _ = x.shape
    cos, sin = rope_tables(T)
    for l in range(DEPTH):
        mod = jax.nn.silu(c) @ w_ada[l] + b_ada[l]
        sh1, sc1, gt1, sh2, sc2, gt2 = jnp.split(mod, 6, axis=-1)

        h = modulate(rms_norm(x, norm1_g[l]), sh1, sc1)
        z = h @ w_in[l]
        z_attn, z_rwkv = z[..., :ATTN_COLS], z[..., ATTN_COLS:]
        q, k, v = jnp.split(z_attn, [Q_COLS, Q_COLS + KV_COLS], axis=-1)
        q = apply_rope(q.reshape(B, T, ATTN_HEADS, HEAD_DIM), cos, sin)
        k = apply_rope(k.reshape(B, T, ATTN_KV_HEADS, HEAD_DIM), cos, sin)
        v = v.reshape(B, T, ATTN_KV_HEADS, HEAD_DIM)
        o_attn = sliding_window_attention(q, k, v, sinks[l])
        o_rwkv = rwkv7_time_mix(token_shift(z_rwkv, mu_shift[l]), w0[l], w_decay_up[l], a0[l],
                                w_a_up[l], w_g_up[l], k_k[l], k_a[l], r_k[l],
                                ln_x_w[l], ln_x_b[l])
        mixed = jnp.concatenate([o_attn, o_rwkv], axis=-1) @ w_o[l]
        x = x + gt1[:, None, :] * mixed

        h = modulate(rms_norm(x, norm2_g[l]), sh2, sc2)
        x = x + gt2[:, None, :] * hierarchical_moe(h, w_router_group[l], b_router_group[l],
                                                   w_router_expert[l], b_router_expert[l],
                                                   w1[l], w3[l], w2[l])
    return rms_norm(x, final_g)
```

```python
import functools
import math

import jax
import jax.numpy as jnp
from jax import lax
from jax.experimental import pallas as pl
from jax.experimental.pallas import tpu as pltpu

F32 = jnp.float32
BF16 = jnp.bfloat16
I32 = jnp.int32

LANES = 128
VMEM_BYTES_V7X = 64 * 1024 * 1024

HEAD_DIM = 64
ATTN_HEADS = 16
ATTN_KV_HEADS = 2
ATTN_GROUP = ATTN_HEADS // ATTN_KV_HEADS
WINDOW = 128
ROPE_THETA = 10000.0
RWKV_HEADS = 16
RWKV_N = 64
DECAY_LORA = 64
A_LORA = 64
GATE_LORA = 160
RWKV_LN_EPS = 64e-5
N_GROUPS = 8
EXPERTS_PER_GROUP = 8
N_EXPERTS = N_GROUPS * EXPERTS_PER_GROUP
EXPERT_FF = 512
MOE_BLOCK = 128
NORM_EPS = 1e-6

Q_COLS = ATTN_HEADS * HEAD_DIM
KV_COLS = ATTN_KV_HEADS * HEAD_DIM
ATTN_COLS = Q_COLS + 2 * KV_COLS
RWKV_W = RWKV_HEADS * RWKV_N
LORA_COLS = DECAY_LORA + A_LORA + GATE_LORA
LORA_PAD = 384
RWKV_COLS = 3 * RWKV_W + LORA_COLS
RWKV_PAD = 3 * RWKV_W + LORA_PAD
CHUNK = 64
NEG_BIG = -1e30


def _vmem_limit(nbytes):
    return int(min(nbytes, VMEM_BYTES_V7X - 4 * 1024 * 1024))


def _dot(a, b):
    return jnp.dot(a, b, preferred_element_type=F32)


def _dot_nt(a, b):
    return lax.dot_general(a, b, (((1,), (1,)), ((), ())), preferred_element_type=F32)


def _dot_tn(a, b):
    return lax.dot_general(a, b, (((0,), (0,)), ((), ())), preferred_element_type=F32)


def _ada_kernel(c_ref, w_ref, b_ref, o_ref):
    c = c_ref[...]
    s = c * (1.0 / (1.0 + jnp.exp(-c)))
    o_ref[...] = _dot(s, w_ref[...]) + b_ref[...]


def _ada(c, w_ada, b_ada):
    B, D = c.shape
    N = w_ada.shape[1]
    tn = 1024
    cp = jnp.zeros((8, D), F32).at[:B].set(c)
    out = pl.pallas_call(
        _ada_kernel,
        grid=(N // tn,),
        in_specs=[pl.BlockSpec((8, D), lambda j: (0, 0)),
                  pl.BlockSpec((D, tn), lambda j: (0, j)),
                  pl.BlockSpec((1, tn), lambda j: (0, j))],
        out_specs=pl.BlockSpec((8, tn), lambda j: (0, j)),
        out_shape=jax.ShapeDtypeStruct((8, N), F32),
        compiler_params=pltpu.CompilerParams(
            dimension_semantics=("arbitrary",),
            vmem_limit_bytes=_vmem_limit(2 * D * tn * 4 + (8 << 20))),
        name="ada",
    )(cp, w_ada, b_ada.reshape(1, N))
    return out[:B]


def _rope(z, cos, sin):
    w = z.shape[-1]
    lane = lax.broadcasted_iota(I32, z.shape, 1)
    first_half = (lane % HEAD_DIM) < (HEAD_DIM // 2)
    partner = jnp.where(first_half, pltpu.roll(z, w - HEAD_DIM // 2, 1), pltpu.roll(z, HEAD_DIM // 2, 1))
    return z * cos + partner * sin


def _inproj_kernel(x_ref, g_ref, sh_ref, sc_ref, w_ref, mu_ref, cos_ref, sin_ref,
                   za_ref, zr_ref, carry_ref, *, tiles_per_seq):
    i = pl.program_id(0)
    tm = x_ref.shape[0]
    x = x_ref[...]
    ms = jnp.mean(x * x, axis=-1, keepdims=True)
    y = x * lax.rsqrt(ms + NORM_EPS) * g_ref[...]
    h = (y * (1.0 + sc_ref[0]) + sh_ref[0]).astype(BF16)

    for c0 in range(0, ATTN_COLS, 256):
        z = _dot(h, w_ref[:, c0:c0 + 256])
        t0 = 0 if c0 < Q_COLS else 256
        z = _rope(z, cos_ref[:, t0:t0 + 256], sin_ref[:, t0:t0 + 256])
        za_ref[:, c0:c0 + 256] = z.astype(za_ref.dtype)

    first = (i % tiles_per_seq) == 0
    row = lax.broadcasted_iota(I32, (tm, 1), 0)
    c0 = 0
    while c0 < RWKV_PAD:
        n = min(512, RWKV_PAD - c0)
        z = _dot(h, w_ref[:, ATTN_COLS + c0:ATTN_COLS + c0 + n])
        prev_last = jnp.where(first, 0.0, carry_ref[7:8, c0:c0 + n])
        z_prev = jnp.where(row == 0, prev_last, pltpu.roll(z, 1, 0))
        carry_ref[:, c0:c0 + n] = z[tm - 8:tm, :]
        zr_ref[:, c0:c0 + n] = z + (z_prev - z) * mu_ref[:, c0:c0 + n]
        c0 += n


def _inproj(x2, g, sh, sc, w_bf, mu_pad, cos_t, sin_t, T):
    M, D = x2.shape
    tm = 256
    tps = T // tm
    NW = w_bf.shape[1]
    kern = functools.partial(_inproj_kernel, tiles_per_seq=tps)
    vm = (D * NW * 2 + 2 * tm * D * 4 + 2 * tm * ATTN_COLS * 2 + 2 * tm * RWKV_PAD * 4
          + 4 * tm * 512 * 4 + tm * D * 8 + (8 << 20))
    return pl.pallas_call(
        kern,
        grid=(M // tm,),
        in_specs=[pl.BlockSpec((tm, D), lambda i: (i, 0)),
                  pl.BlockSpec((1, D), lambda i: (0, 0)),
                  pl.BlockSpec((1, 1, D), lambda i: (i // tps, 0, 0)),
                  pl.BlockSpec((1, 1, D), lambda i: (i // tps, 0, 0)),
                  pl.BlockSpec((D, NW), lambda i: (0, 0), pipeline_mode=pl.Buffered(1)),
                  pl.BlockSpec((1, RWKV_PAD), lambda i: (0, 0)),
                  pl.BlockSpec((tm, 512), lambda i: (i % tps, 0)),
                  pl.BlockSpec((tm, 512), lambda i: (i % tps, 0))],
        out_specs=[pl.BlockSpec((tm, ATTN_COLS), lambda i: (i, 0)),
                   pl.BlockSpec((tm, RWKV_PAD), lambda i: (i, 0))],
        out_shape=[jax.ShapeDtypeStruct((M, ATTN_COLS), BF16),
                   jax.ShapeDtypeStruct((M, RWKV_PAD), F32)],
        scratch_shapes=[pltpu.VMEM((8, RWKV_PAD), F32)],
        compiler_params=pltpu.CompilerParams(
            dimension_semantics=("arbitrary",), vmem_limit_bytes=_vmem_limit(vm)),
        name="inproj",
    )(x2, g, sh, sc, w_bf, mu_pad, cos_t, sin_t)


def _attn_kernel(sink_ref, q_ref, kc_ref, kp_ref, vc_ref, vp_ref, o_ref):
    n = pl.program_id(1)
    blk = q_ref.shape[1]
    row = lax.broadcasted_iota(I32, (blk, blk), 0)
    col = lax.broadcasted_iota(I32, (blk, blk), 1)
    mask = jnp.concatenate([(col > row) & (n > 0), col <= row], axis=1)
    scale = 1.0 / math.sqrt(HEAD_DIM)
    outs = []
    for kvh in range(ATTN_KV_HEADS):
        ks = slice(kvh * HEAD_DIM, (kvh + 1) * HEAD_DIM)
        kmat = jnp.concatenate([kp_ref[0, :, ks], kc_ref[0, :, ks]], axis=0)
        vmat = jnp.concatenate([vp_ref[0, :, ks], vc_ref[0, :, ks]], axis=0)
        for g in range(ATTN_GROUP):
            hd = kvh * ATTN_GROUP + g
            qh = q_ref[0, :, hd * HEAD_DIM:(hd + 1) * HEAD_DIM]
            s = _dot_nt(qh, kmat) * scale
            s = jnp.where(mask, s, NEG_BIG)
            sink = sink_ref[hd]
            m = jnp.maximum(jnp.max(s, axis=-1, keepdims=True), sink)
            p = jnp.exp(s - m)
            denom = jnp.sum(p, axis=-1, keepdims=True) + jnp.exp(sink - m)
            o = _dot(p.astype(BF16), vmat)
            outs.append(o / denom)
    o_ref[0] = jnp.concatenate(outs, axis=1).astype(o_ref.dtype)


def _attention(za3, sinks):
    B, T, _ = za3.shape
    nb = T // WINDOW
    kcol = Q_COLS // KV_COLS
    prev = lambda b, n, s: (b, jnp.maximum(n - 1, 0), kcol)
    prev_v = lambda b, n, s: (b, jnp.maximum(n - 1, 0), kcol + 1)
    gs = pltpu.PrefetchScalarGridSpec(
        num_scalar_prefetch=1,
        grid=(B, nb),
        in_specs=[pl.BlockSpec((1, WINDOW, Q_COLS), lambda b, n, s: (b, n, 0)),
                  pl.BlockSpec((1, WINDOW, KV_COLS), lambda b, n, s: (b, n, kcol)),
                  pl.BlockSpec((1, WINDOW, KV_COLS), prev),
                  pl.BlockSpec((1, WINDOW, KV_COLS), lambda b, n, s: (b, n, kcol + 1)),
                  pl.BlockSpec((1, WINDOW, KV_COLS), prev_v)],
        out_specs=pl.BlockSpec((1, WINDOW, Q_COLS), lambda b, n, s: (b, n, 0)),
    )
    return pl.pallas_call(
        _attn_kernel,
        grid_spec=gs,
        out_shape=jax.ShapeDtypeStruct((B, T, Q_COLS), BF16),
        compiler_params=pltpu.CompilerParams(dimension_semantics=("arbitrary", "arbitrary")),
        name="attn",
    )(sinks, za3, za3, za3, za3, za3)


def _rwkv_kernel(r_ref, k_ref, v_ref, lora_ref, w0_ref, wdu_ref, a0_ref, wau_ref, wgu_ref,
                 kk_ref, ka_ref, rk_ref, lnw_ref, lnb_ref, o_ref,
                 s_ref, cum_ref, ld_ref, a_ref, g_ref, *, heads_per_iter):
    c = pl.program_id(1)
    C = r_ref.shape[1]
    N = RWKV_N

    @pl.when(c == 0)
    def _():
        s_ref[...] = jnp.zeros_like(s_ref)

    lora = lora_ref[0]
    wd = lora[:, 0:DECAY_LORA]
    ad = lora[:, DECAY_LORA:DECAY_LORA + A_LORA]
    gd = lora[:, DECAY_LORA + A_LORA:LORA_COLS]
    wlin = w0_ref[...] + _dot(jnp.tanh(wd), wdu_ref[...])
    neg = -wlin
    softplus = jnp.maximum(neg, 0.0) + jnp.log(1.0 + jnp.exp(-jnp.abs(neg)))
    w = -softplus - 0.5
    logdec = -jnp.exp(w)
    a_ref[...] = 1.0 / (1.0 + jnp.exp(-(a0_ref[...] + _dot(ad, wau_ref[...]))))
    g_ref[...] = _dot(1.0 / (1.0 + jnp.exp(-gd)), wgu_ref[...])
    ti = lax.broadcasted_iota(I32, (C, C), 0)
    si = lax.broadcasted_iota(I32, (C, C), 1)
    tril = jnp.where(si <= ti, 1.0, 0.0).astype(BF16)
    ld_hi = logdec.astype(BF16)
    ld_lo = (logdec - ld_hi.astype(F32)).astype(BF16)
    cum_ref[...] = _dot(tril, ld_hi) + _dot(tril, ld_lo)
    ld_ref[...] = logdec

    strict = si < ti
    incl = si <= ti
    eye = jnp.where(si == ti, 1.0, 0.0)
    hp = heads_per_iter
    wid = hp * N

    def body(it, carry):
        c0 = pl.multiple_of(it * wid, wid)
        cols = pl.ds(c0, wid)
        r_all = r_ref[0, :, cols]
        k_all = k_ref[0, :, cols]
        v_all = v_ref[0, :, cols]
        cum = cum_ref[:, cols]
        ld = ld_ref[:, cols]
        a_all = a_ref[:, cols]
        g_all = g_ref[:, cols]
        e_in = jnp.exp(cum)
        e_ex = jnp.exp(cum - ld)
        e_neg = jnp.exp(-cum)
        kk_all = k_all * kk_ref[:, cols]
        k2_all = k_all * (1.0 + (a_all - 1.0) * ka_ref[:, cols])
        rk_all = r_all * k2_all * rk_ref[:, cols]
        lnw = lnw_ref[:, cols]
        lnb = lnb_ref[:, cols]
        outs = []
        for j in range(hp):
            sl = slice(j * N, (j + 1) * N)
            kk = kk_all[:, sl]
            nrm = jnp.sqrt(jnp.sum(kk * kk, axis=-1, keepdims=True))
            kk = kk / jnp.maximum(nrm, 1e-12)
            v = v_all[:, sl]
            at = -kk * e_ex[:, sl]
            rt = r_all[:, sl] * e_in[:, sl]
            bt = kk * a_all[:, sl] * e_neg[:, sl]
            kt = k2_all[:, sl] * e_neg[:, sl]
            left = jnp.concatenate([at, rt], axis=0)
            right = jnp.concatenate([bt, kt], axis=0)
            big = _dot_nt(left, right)
            a_ab = jnp.where(strict, big[:C, :C], 0.0)
            a_ak = jnp.where(strict, big[:C, C:], 0.0)
            a_rb = jnp.where(incl, big[C:, :C], 0.0)
            a_rk = jnp.where(incl, big[C:, C:], 0.0)
            s_old = s_ref[it * hp + j]
            ls = _dot_nt(left, s_old)
            x = ls[:C] + _dot(a_ak, v)
            pw = a_ab
            tinv = eye + a_ab
            span = 2
            while span < C:
                pw = _dot(pw, pw)
                tinv = tinv + _dot(tinv, pw)
                span *= 2
            u = _dot(tinv, x)
            y = ls[C:] + _dot(a_rb, u) + _dot(a_rk, v)
            uv = jnp.concatenate([u, v], axis=0)
            s_new = (s_old + _dot_tn(uv, right)) * e_in[C - 1:C, sl]
            s_ref[it * hp + j] = s_new
            mean = jnp.mean(y, axis=-1, keepdims=True)
            yc = y - mean
            var = jnp.mean(yc * yc, axis=-1, keepdims=True)
            yn = yc * lax.rsqrt(var + RWKV_LN_EPS) * lnw[:, sl] + lnb[:, sl]
            bonus = jnp.sum(rk_all[:, sl], axis=-1, keepdims=True) * v
            outs.append((yn + bonus) * g_all[:, sl])
        o_ref[0, :, cols] = jnp.concatenate(outs, axis=1).astype(o_ref.dtype)
        return carry

    lax.fori_loop(0, RWKV_HEADS // hp, body, 0)


def _rwkv(zr3, w0, wdu, a0, wau, wgu, k_k, k_a, r_k, ln_w, ln_b):
    B, T, _ = zr3.shape
    C = CHUNK
    W = RWKV_W
    vec = lambda v: v.reshape(1, W)
    full = lambda shape: pl.BlockSpec(shape, lambda b, c: (0,) * len(shape))
    kern = functools.partial(_rwkv_kernel, heads_per_iter=4)
    return pl.pallas_call(
        kern,
        grid=(B, T // C),
        in_specs=[pl.BlockSpec((1, C, W), lambda b, c: (b, c, 0)),
                  pl.BlockSpec((1, C, W), lambda b, c: (b, c, 1)),
                  pl.BlockSpec((1, C, W), lambda b, c: (b, c, 2)),
                  pl.BlockSpec((1, C, LORA_PAD), lambda b, c: (b, c, 3 * W // LORA_PAD)),
                  full((1, W)), full((DECAY_LORA, W)), full((1, W)), full((A_LORA, W)),
                  full((GATE_LORA, W)), full((1, W)), full((1, W)), full((1, W)),
                  full((1, W)), full((1, W))],
        out_specs=pl.BlockSpec((1, C, W), lambda b, c: (b, c, 0)),
        out_shape=jax.ShapeDtypeStruct((B, T, W), BF16),
        scratch_shapes=[pltpu.VMEM((RWKV_HEADS, RWKV_N, RWKV_N), F32),
                        pltpu.VMEM((C, W), F32), pltpu.VMEM((C, W), F32),
                        pltpu.VMEM((C, W), F32), pltpu.VMEM((C, W), F32)],
        compiler_params=pltpu.CompilerParams(dimension_semantics=("arbitrary", "arbitrary")),
        name="rwkv",
    )(zr3, zr3, zr3, zr3, vec(w0), wdu, vec(a0), wau, wgu, vec(k_k), vec(k_a), vec(r_k),
      vec(ln_w), vec(ln_b))


def _outproj_kernel(oa_ref, or_ref, x_ref, wo_ref, gt_ref, g_ref, sh_ref, sc_ref,
                    wrh_ref, wrl_ref, br_ref, x1_ref, h2_ref, lg_ref):
    mixed = _dot(oa_ref[...], wo_ref[0:Q_COLS, :]) + _dot(or_ref[...], wo_ref[Q_COLS:, :])
    x1 = x_ref[...] + gt_ref[0] * mixed
    x1_ref[...] = x1
    ms = jnp.mean(x1 * x1, axis=-1, keepdims=True)
    h2 = x1 * lax.rsqrt(ms + NORM_EPS) * g_ref[...] * (1.0 + sc_ref[0]) + sh_ref[0]
    h2_ref[...] = h2
    hh = h2.astype(BF16)
    hl = (h2 - hh.astype(F32)).astype(BF16)
    lg_ref[...] = (_dot(hh, wrh_ref[...]) + _dot(hl, wrh_ref[...]) + _dot(hh, wrl_ref[...])
                   + br_ref[...])


def _outproj(oa2, or2, x2, wo_bf, gt, g2, sh, sc, wr_hi, wr_lo, br, T):
    M, D = x2.shape
    tm = 256
    tps = T // tm
    bvec = pl.BlockSpec((1, 1, D), lambda i: (i // tps, 0, 0))
    vm = D * D * 2 * 2 + 2 * tm * D * (2 + 4 + 4 + 4) + tm * D * 16 + (8 << 20)
    return pl.pallas_call(
        _outproj_kernel,
        grid=(M // tm,),
        in_specs=[pl.BlockSpec((tm, Q_COLS), lambda i: (i, 0)),
                  pl.BlockSpec((tm, RWKV_W), lambda i: (i, 0)),
                  pl.BlockSpec((tm, D), lambda i: (i, 0)),
                  pl.BlockSpec((D, D), lambda i: (0, 0)),
                  bvec,
                  pl.BlockSpec((1, D), lambda i: (0, 0)),
                  bvec, bvec,
                  pl.BlockSpec((D, LANES), lambda i: (0, 0)),
                  pl.BlockSpec((D, LANES), lambda i: (0, 0)),
                  pl.BlockSpec((1, LANES), lambda i: (0, 0))],
        out_specs=[pl.BlockSpec((tm, D), lambda i: (i, 0)),
                   pl.BlockSpec((tm, D), lambda i: (i, 0)),
                   pl.BlockSpec((tm, LANES), lambda i: (i, 0))],
        out_shape=[jax.ShapeDtypeStruct((M, D), F32),
                   jax.ShapeDtypeStruct((M, D), F32),
                   jax.ShapeDtypeStruct((M, LANES), F32)],
        compiler_params=pltpu.CompilerParams(
            dimension_semantics=("arbitrary",), vmem_limit_bytes=_vmem_limit(vm)),
        name="outproj",
    )(oa2, or2, x2, wo_bf, gt, g2, sh, sc, wr_hi, wr_lo, br)


def _route_kernel(lg_ref, info_ref, cnt_ref, run_ref):
    i = pl.program_id(0)
    tm = lg_ref.shape[0]

    @pl.when(i == 0)
    def _():
        run_ref[...] = jnp.zeros_like(run_ref)

    lg = lg_ref[...]
    lane = lax.broadcasted_iota(I32, lg.shape, 1)
    gl = jnp.where(lane < N_GROUPS, lg, NEG_BIG)
    gmax = jnp.max(gl, axis=-1, keepdims=True)
    gsum = jnp.sum(jnp.exp(gl - gmax), axis=-1, keepdims=True)
    g_gate = 1.0 / gsum
    g_idx = jnp.min(jnp.where(gl == gmax, lane, LANES), axis=-1, keepdims=True)
    lo = N_GROUPS + EXPERTS_PER_GROUP * g_idx
    el = jnp.where((lane >= lo) & (lane < lo + EXPERTS_PER_GROUP), lg, NEG_BIG)
    e1max = jnp.max(el, axis=-1, keepdims=True)
    l1 = jnp.min(jnp.where(el == e1max, lane, LANES), axis=-1, keepdims=True)
    el2 = jnp.where(lane == l1, NEG_BIG, el)
    e2max = jnp.max(el2, axis=-1, keepdims=True)
    l2 = jnp.min(jnp.where(el2 == e2max, lane, LANES), axis=-1, keepdims=True)
    t2 = jnp.exp(e2max - e1max)
    w1 = g_gate / (1.0 + t2)
    w2 = g_gate * t2 / (1.0 + t2)
    ex1 = l1 - N_GROUPS
    ex2 = l2 - N_GROUPS

    oh1 = jnp.where(lane == ex1, 1.0, 0.0)
    oh2 = jnp.where(lane == ex2, 1.0, 0.0)
    ti = lax.broadcasted_iota(I32, (tm, tm), 0)
    si = lax.broadcasted_iota(I32, (tm, tm), 1)
    lower = jnp.where(si < ti, 1.0, 0.0).astype(BF16)
    pre1 = _dot(lower, oh1.astype(BF16))
    pre2 = _dot(lower, oh2.astype(BF16))
    cnt1 = jnp.sum(oh1, axis=0, keepdims=True)
    cnt2 = jnp.sum(oh2, axis=0, keepdims=True)
    run = run_ref[...]
    rank1 = jnp.sum(oh1 * (pre1 + run), axis=-1, keepdims=True)
    rank2 = jnp.sum(oh2 * (pre2 + run + cnt1), axis=-1, keepdims=True)
    run = run + cnt1 + cnt2
    run_ref[...] = run
    cnt_ref[...] = run

    info = jnp.where(lane == 0, ex1.astype(F32), 0.0)
    info = jnp.where(lane == 1, ex2.astype(F32), info)
    info = jnp.where(lane == 2, rank1, info)
    info = jnp.where(lane == 3, rank2, info)
    info = jnp.where(lane == 4, w1, info)
    info = jnp.where(lane == 5, w2, info)
    info_ref[...] = info


def _route(lg):
    M = lg.shape[0]
    tm = 256
    return pl.pallas_call(
        _route_kernel,
        grid=(M // tm,),
        in_specs=[pl.BlockSpec((tm, LANES), lambda i: (i, 0))],
        out_specs=[pl.BlockSpec((tm, LANES), lambda i: (i, 0)),
                   pl.BlockSpec((1, LANES), lambda i: (0, 0))],
        out_shape=[jax.ShapeDtypeStruct((M, LANES), F32),
                   jax.ShapeDtypeStruct((1, LANES), F32)],
        scratch_shapes=[pltpu.VMEM((1, LANES), F32)],
        compiler_params=pltpu.CompilerParams(dimension_semantics=("arbitrary",)),
        name="route",
    )(lg)


def _plan_kernel(info_ref, cnt_ref, slot_ref, meta_ref):
    cnt = cnt_ref[...]
    lane_r = lax.broadcasted_iota(I32, (1, LANES), 1)
    nblk = jnp.floor((cnt + (MOE_BLOCK - 1)) * (1.0 / MOE_BLOCK))
    ei = lax.broadcasted_iota(I32, (LANES, LANES), 0)
    ej = lax.broadcasted_iota(I32, (LANES, LANES), 1)
    upper = jnp.where(ei <= ej, 1.0, 0.0).astype(BF16)
    nb8 = jnp.broadcast_to(nblk, (8, LANES)).astype(BF16)
    bend = _dot(nb8, upper)[0:1, :]
    bstart = bend - nblk
    pstart = bstart * MOE_BLOCK

    info = info_ref[...]
    lane = lax.broadcasted_iota(I32, info.shape, 1)
    ex1 = info[:, 0:1].astype(I32)
    ex2 = info[:, 1:2].astype(I32)
    s1 = jnp.sum(jnp.where(lane == ex1, pstart, 0.0), axis=-1, keepdims=True) + info[:, 2:3]
    s2 = jnp.sum(jnp.where(lane == ex2, pstart, 0.0), axis=-1, keepdims=True) + info[:, 3:4]
    slot = jnp.where(lane == 0, s1, 0.0)
    slot = jnp.where(lane == 1, s2, slot)
    slot_ref[...] = slot.astype(I32)

    blk = lax.broadcasted_iota(I32, (LANES, 2 * LANES), 1).astype(F32)
    bend_col = jnp.sum(jnp.where(ei == ej, jnp.broadcast_to(bend, (LANES, LANES)), 0.0),
                       axis=-1, keepdims=True)
    erow = lax.broadcasted_iota(I32, (LANES, 2 * LANES), 0)
    hit = jnp.where((bend_col <= blk) & (erow < N_EXPERTS), 1.0, 0.0)
    bexp = jnp.minimum(jnp.sum(hit, axis=0, keepdims=True), N_EXPERTS - 1.0)
    n_used = jnp.max(jnp.where(lane_r < N_EXPERTS, bend, 0.0), axis=-1, keepdims=True)
    lastblk = jnp.where(nblk > 0, bend - 1.0, -1.0)
    r8 = lax.broadcasted_iota(I32, (8, 2 * LANES), 0)
    last2 = jnp.concatenate([lastblk, jnp.full((1, LANES), -1.0)], axis=1)
    meta = jnp.where(r8 == 0, jnp.broadcast_to(bexp, (8, 2 * LANES)), 0.0)
    meta = jnp.where(r8 == 1, jnp.broadcast_to(n_used, (8, 2 * LANES)), meta)
    meta = jnp.where(r8 == 2, jnp.broadcast_to(last2, (8, 2 * LANES)), meta)
    meta_ref[...] = meta.astype(I32)


def _plan(info, cnt):
    M = info.shape[0]
    tm = 256
    return pl.pallas_call(
        _plan_kernel,
        grid=(M // tm,),
        in_specs=[pl.BlockSpec((tm, LANES), lambda i: (i, 0)),
                  pl.BlockSpec((1, LANES), lambda i: (0, 0))],
        out_specs=[pl.BlockSpec((tm, LANES), lambda i: (i, 0)),
                   pl.BlockSpec((8, 2 * LANES), lambda i: (0, 0))],
        out_shape=[jax.ShapeDtypeStruct((M, LANES), I32),
                   jax.ShapeDtypeStruct((8, 2 * LANES), I32)],
        compiler_params=pltpu.CompilerParams(dimension_semantics=("arbitrary",)),
        name="plan",
    )(info, cnt)


def _dispatch_kernel(slot_ref, last_ref, h2_ref, xb_ref, zero_ref, sem_z, sem_r, *, tm, n_blocks):
    i = pl.program_id(0)

    def zero_block(blk):
        return pltpu.make_async_copy(zero_ref, xb_ref.at[pl.ds(blk * MOE_BLOCK, MOE_BLOCK)], sem_z)

    @pl.when(i == 0)
    def _():
        zero_ref[...] = jnp.zeros_like(zero_ref)
        n_used = last_ref[N_EXPERTS]

        def start(e, c):
            @pl.when(last_ref[e] >= 0)
            def _():
                zero_block(last_ref[e]).start()
            return c

        def wait(e, c):
            @pl.when(last_ref[e] >= 0)
            def _():
                zero_block(last_ref[e]).wait()
            return c

        def start_tail(b, c):
            zero_block(b).start()
            return c

        def wait_tail(b, c):
            zero_block(b).wait()
            return c

        lax.fori_loop(0, N_EXPERTS, start, 0)
        lax.fori_loop(n_used, n_blocks, start_tail, 0)
        lax.fori_loop(0, N_EXPERTS, wait, 0)
        lax.fori_loop(n_used, n_blocks, wait_tail, 0)

    def row_copy(t, k):
        tok = i * tm + t
        return pltpu.make_async_copy(h2_ref.at[pl.ds(tok, 1)], xb_ref.at[pl.ds(slot_ref[2 * tok + k], 1)], sem_r)

    def start_rows(t, c):
        row_copy(t, 0).start()
        row_copy(t, 1).start()
        return c

    def wait_rows(t, c):
        row_copy(t, 0).wait()
        row_copy(t, 1).wait()
        return c

    lax.fori_loop(0, tm, start_rows, 0)
    lax.fori_loop(0, tm, wait_rows, 0)


def _dispatch(slots_flat, last_blk, h2, n_rows):
    M, D = h2.shape
    tm = 256
    gs = pltpu.PrefetchScalarGridSpec(
        num_scalar_prefetch=2,
        grid=(M // tm,),
        in_specs=[pl.BlockSpec(memory_space=pl.ANY)],
        out_specs=pl.BlockSpec(memory_space=pl.ANY),
        scratch_shapes=[pltpu.VMEM((MOE_BLOCK, D), F32),
                        pltpu.SemaphoreType.DMA(()), pltpu.SemaphoreType.DMA(())],
    )
    return pl.pallas_call(
        functools.partial(_dispatch_kernel, tm=tm, n_blocks=n_rows // MOE_BLOCK),
        grid_spec=gs,
        out_shape=jax.ShapeDtypeStruct((n_rows, D), F32),
        compiler_params=pltpu.CompilerParams(dimension_semantics=("arbitrary",),
                                             has_side_effects=True),
        name="dispatch",
    )(slots_flat, last_blk, h2)


def _experts_kernel(bexp_ref, nused_ref, xb_ref, w1_ref, w3_ref, w2_ref, y_ref):
    i = pl.program_id(0)

    @pl.when(i < nused_ref[0])
    def _():
        x = xb_ref[...].astype(BF16)
        h1 = _dot(x, w1_ref[0])
        h3 = _dot(x, w3_ref[0])
        hid = h1 * (1.0 / (1.0 + jnp.exp(-h1))) * h3
        y_ref[...] = _dot(hid.astype(BF16), w2_ref[0])

    @pl.when(i >= nused_ref[0])
    def _():
        y_ref[...] = jnp.zeros_like(y_ref)


def _experts(bexp, nused, xb, w1, w3, w2, n_blocks):
    D = xb.shape[1]
    FF = w1.shape[2]
    clamp = lambda i, be, nu: (jnp.minimum(i, nu[0] - 1), 0)
    gs = pltpu.PrefetchScalarGridSpec(
        num_scalar_prefetch=2,
        grid=(n_blocks,),
        in_specs=[pl.BlockSpec((MOE_BLOCK, D), clamp),
                  pl.BlockSpec((1, D, FF), lambda i, be, nu: (be[i], 0, 0)),
                  pl.BlockSpec((1, D, FF), lambda i, be, nu: (be[i], 0, 0)),
                  pl.BlockSpec((1, FF, D), lambda i, be, nu: (be[i], 0, 0))],
        out_specs=pl.BlockSpec((MOE_BLOCK, D), lambda i, be, nu: (i, 0)),
    )
    vm = 2 * 3 * D * FF * 4 + 4 * MOE_BLOCK * D * 4 + (12 << 20)
    return pl.pallas_call(
        _experts_kernel,
        grid_spec=gs,
        out_shape=jax.ShapeDtypeStruct((n_blocks * MOE_BLOCK, D), F32),
        compiler_params=pltpu.CompilerParams(
            dimension_semantics=("arbitrary",), vmem_limit_bytes=_vmem_limit(vm)),
        name="experts",
    )(bexp, nused, xb, w1, w3, w2)


def _combine_kernel(slot_ref, x1_ref, info_ref, gt_ref, fg_ref, y_ref, o_ref, buf_ref, sem_ref, *, tm):
    i = pl.program_id(0)
    n = pl.num_programs(0)

    def row_copy(step, t, k):
        par = step % 2
        tok = step * tm + t
        return pltpu.make_async_copy(y_ref.at[pl.ds(slot_ref[2 * tok + k], 1)],
                                     buf_ref.at[par, k, pl.ds(t, 1)], sem_ref.at[par])

    def start_tile(step):
        def go(t, c):
            row_copy(step, t, 0).start()
            row_copy(step, t, 1).start()
            return c
        lax.fori_loop(0, tm, go, 0)

    @pl.when(i == 0)
    def _():
        start_tile(0)

    @pl.when(i + 1 < n)
    def _():
        start_tile(i + 1)

    def wait(t, c):
        row_copy(i, t, 0).wait()
        row_copy(i, t, 1).wait()
        return c

    lax.fori_loop(0, tm, wait, 0)
    par = i % 2
    info = info_ref[...]
    moe = info[:, 4:5] * buf_ref[par, 0] + info[:, 5:6] * buf_ref[par, 1]
    x = x1_ref[...] + gt_ref[0] * moe
    ms = jnp.mean(x * x, axis=-1, keepdims=True)
    o_ref[...] = x * lax.rsqrt(ms + NORM_EPS) * fg_ref[...]


def _combine(slots_flat, x1, info, gt, final_g, y, T):
    M, D = x1.shape
    tm = 128
    tps = T // tm
    gs = pltpu.PrefetchScalarGridSpec(
        num_scalar_prefetch=1,
        grid=(M // tm,),
        in_specs=[pl.BlockSpec((tm, D), lambda i, s: (i, 0)),
                  pl.BlockSpec((tm, LANES), lambda i, s: (i, 0)),
                  pl.BlockSpec((1, 1, D), lambda i, s: (i // tps, 0, 0)),
                  pl.BlockSpec((1, D), lambda i, s: (0, 0)),
                  pl.BlockSpec(memory_space=pl.ANY)],
        out_specs=pl.BlockSpec((tm, D), lambda i, s: (i, 0)),
        scratch_shapes=[pltpu.VMEM((2, 2, tm, D), F32), pltpu.SemaphoreType.DMA((2,))],
    )
    vm = 4 * tm * D * 4 + 4 * tm * D * 4 + tm * D * 16 + (8 << 20)
    return pl.pallas_call(
        functools.partial(_combine_kernel, tm=tm),
        grid_spec=gs,
        out_shape=jax.ShapeDtypeStruct((M, D), F32),
        compiler_params=pltpu.CompilerParams(
            dimension_semantics=("arbitrary",), vmem_limit_bytes=_vmem_limit(vm)),
        name="combine",
    )(slots_flat, x1, info, gt, final_g.reshape(1, D), y)


def _rope_tables(T):
    half = HEAD_DIM // 2
    inv_freq = ROPE_THETA ** (-jnp.arange(0, HEAD_DIM, 2, dtype=F32) / HEAD_DIM)
    ang = jnp.arange(T, dtype=F32)[:, None] * inv_freq[None, :]
    cos, sin = jnp.cos(ang), jnp.sin(ang)
    cos_h = jnp.concatenate([cos, cos], axis=1)
    sin_h = jnp.concatenate([-sin, sin], axis=1)
    cos_q = jnp.tile(cos_h, (1, 4))
    sin_q = jnp.tile(sin_h, (1, 4))
    cos_kv = jnp.concatenate([jnp.tile(cos_h, (1, 2)), jnp.ones((T, KV_COLS), F32)], axis=1)
    sin_kv = jnp.concatenate([jnp.tile(sin_h, (1, 2)), jnp.zeros((T, KV_COLS), F32)], axis=1)
    return jnp.concatenate([cos_q, cos_kv], axis=1), jnp.concatenate([sin_q, sin_kv], axis=1)


def _layer(x, c, w_ada, b_ada, norm1_g, w_in, mu_shift, sinks, w0, w_decay_up, a0, w_a_up, w_g_up,
           k_k, k_a, r_k, ln_x_w, ln_x_b, w_o, norm2_g, w_rg, b_rg, w_re, b_re, w1, w3, w2, final_g,
           apply_final):
    B, T, D = x.shape
    M = B * T
    mod = _ada(c, w_ada, b_ada)
    sh1, sc1, gt1, sh2, sc2, gt2 = [m.reshape(B, 1, D) for m in jnp.split(mod, 6, axis=-1)]

    pad = RWKV_PAD - RWKV_COLS
    w_bf = jnp.pad(w_in, ((0, 0), (0, pad))).astype(BF16)
    mu_pad = jnp.pad(mu_shift, (0, pad)).reshape(1, RWKV_PAD)
    cos_t, sin_t = _rope_tables(T)
    x2 = x.reshape(M, D)
    za, zr = _inproj(x2, norm1_g.reshape(1, D), sh1, sc1, w_bf, mu_pad, cos_t, sin_t, T)

    o_attn = _attention(za.reshape(B, T, ATTN_COLS), sinks)
    o_rwkv = _rwkv(zr.reshape(B, T, RWKV_PAD), w0, w_decay_up, a0, w_a_up, w_g_up, k_k, k_a,
                   r_k.reshape(-1), ln_x_w, ln_x_b)

    wr = jnp.concatenate([w_rg, w_re], axis=1)
    wr = jnp.pad(wr, ((0, 0), (0, LANES - wr.shape[1])))
    wr_hi = wr.astype(BF16)
    wr_lo = (wr - wr_hi.astype(F32)).astype(BF16)
    br = jnp.pad(jnp.concatenate([b_rg, b_re]), (0, LANES - N_GROUPS - N_EXPERTS)).reshape(1, LANES)
    x1, h2, lg = _outproj(o_attn.reshape(M, Q_COLS), o_rwkv.reshape(M, RWKV_W), x2, w_o.astype(BF16),
                          gt1, norm2_g.reshape(1, D), sh2, sc2, wr_hi, wr_lo, br, T)

    n_blocks = -(-(2 * M) // MOE_BLOCK) + N_EXPERTS
    info, cnt = _route(lg)
    slots, meta = _plan(info, cnt)
    slots_flat = slots[:, :2].reshape(-1)
    seg_meta = jnp.concatenate([meta[2, :N_EXPERTS], meta[1, :1]])
    xb = _dispatch(slots_flat, seg_meta, h2, n_blocks * MOE_BLOCK)
    y = _experts(meta[0, :n_blocks], meta[1, :1], xb, w1, w3, w2, n_blocks)
    out = _combine(slots_flat, x1, info, gt2, final_g, y, T)
    del apply_final
    return out.reshape(B, T, D)


def kernel(x, c, w_ada, b_ada, norm1_g, w_in, mu_shift, sinks, w0, w_decay_up, a0, w_a_up, w_g_up, k_k, k_a, r_k, ln_x_w, ln_x_b, w_o, norm2_g, w_router_group, b_router_group, w_router_expert, b_router_expert, w1, w3, w2, final_g):
    depth = w_ada.shape[0]
    assert depth == 1, "single-layer stack"
    l = 0
    return _layer(x, c, w_ada[l], b_ada[l], norm1_g[l], w_in[l], mu_shift[l], sinks[l], w0[l],
                  w_decay_up[l], a0[l], w_a_up[l], w_g_up[l], k_k[l], k_a[l], r_k[l], ln_x_w[l],
                  ln_x_b[l], w_o[l], norm2_g[l], w_router_group[l], b_router_group[l],
                  w_router_expert[l], b_router_expert[l], w1[l], w3[l], w2[l], final_g, True)
```

```python
import functools
import math

import jax
import jax.numpy as jnp
from jax import lax
from jax.experimental import pallas as pl
from jax.experimental.pallas import tpu as pltpu

F32 = jnp.float32
BF16 = jnp.bfloat16
I32 = jnp.int32

LANES = 128
VMEM_BYTES_V7X = 64 * 1024 * 1024

HEAD_DIM = 64
ATTN_HEADS = 16
ATTN_KV_HEADS = 2
ATTN_GROUP = ATTN_HEADS // ATTN_KV_HEADS
WINDOW = 128
ROPE_THETA = 10000.0
RWKV_HEADS = 16
RWKV_N = 64
DECAY_LORA = 64
A_LORA = 64
GATE_LORA = 160
RWKV_LN_EPS = 64e-5
N_GROUPS = 8
EXPERTS_PER_GROUP = 8
N_EXPERTS = N_GROUPS * EXPERTS_PER_GROUP
EXPERT_FF = 512
MOE_BLOCK = 128
NORM_EPS = 1e-6

Q_COLS = ATTN_HEADS * HEAD_DIM
KV_COLS = ATTN_KV_HEADS * HEAD_DIM
ATTN_COLS = Q_COLS + 2 * KV_COLS
RWKV_W = RWKV_HEADS * RWKV_N
LORA_COLS = DECAY_LORA + A_LORA + GATE_LORA
LORA_PAD = 384
RWKV_COLS = 3 * RWKV_W + LORA_COLS
RWKV_PAD = 3 * RWKV_W + LORA_PAD
CHUNK = 64
RWKV_GROUP = 4
NEG_BIG = -1e30


def _vmem_limit(nbytes):
    return int(min(nbytes, VMEM_BYTES_V7X - 4 * 1024 * 1024))


def _dot(a, b):
    return jnp.dot(a, b, preferred_element_type=F32)


def _dot_nt(a, b):
    return lax.dot_general(a, b, (((1,), (1,)), ((), ())), preferred_element_type=F32)


def _dot_tn(a, b):
    return lax.dot_general(a, b, (((0,), (0,)), ((), ())), preferred_element_type=F32)


def _ada_kernel(c_ref, w_ref, b_ref, o_ref):
    c = c_ref[...]
    s = c * (1.0 / (1.0 + jnp.exp(-c)))
    o_ref[...] = _dot(s, w_ref[...]) + b_ref[...]


def _ada(c, w_ada, b_ada):
    B, D = c.shape
    N = w_ada.shape[1]
    tn = 1024
    cp = jnp.zeros((8, D), F32).at[:B].set(c)
    out = pl.pallas_call(
        _ada_kernel,
        grid=(N // tn,),
        in_specs=[pl.BlockSpec((8, D), lambda j: (0, 0)),
                  pl.BlockSpec((D, tn), lambda j: (0, j)),
                  pl.BlockSpec((1, tn), lambda j: (0, j))],
        out_specs=pl.BlockSpec((8, tn), lambda j: (0, j)),
        out_shape=jax.ShapeDtypeStruct((8, N), F32),
        compiler_params=pltpu.CompilerParams(
            dimension_semantics=("arbitrary",),
            vmem_limit_bytes=_vmem_limit(2 * D * tn * 4 + (8 << 20))),
        name="ada",
    )(cp, w_ada, b_ada.reshape(1, N))
    return out[:B]


def _rope(z, cos, sin):
    w = z.shape[-1]
    lane = lax.broadcasted_iota(I32, z.shape, 1)
    first_half = (lane % HEAD_DIM) < (HEAD_DIM // 2)
    partner = jnp.where(first_half, pltpu.roll(z, w - HEAD_DIM // 2, 1), pltpu.roll(z, HEAD_DIM // 2, 1))
    return z * cos + partner * sin


def _inproj_kernel(x_ref, g_ref, sh_ref, sc_ref, w_ref, mu_ref, cos_ref, sin_ref,
                   za_ref, zr_ref, carry_ref, *, tiles_per_seq):
    i = pl.program_id(0)
    tm = x_ref.shape[0]
    x = x_ref[...]
    ms = jnp.mean(x * x, axis=-1, keepdims=True)
    y = x * lax.rsqrt(ms + NORM_EPS) * g_ref[...]
    h = (y * (1.0 + sc_ref[0]) + sh_ref[0]).astype(BF16)

    for c0 in range(0, ATTN_COLS, 256):
        z = _dot(h, w_ref[:, c0:c0 + 256])
        t0 = 0 if c0 < Q_COLS else 256
        z = _rope(z, cos_ref[:, t0:t0 + 256], sin_ref[:, t0:t0 + 256])
        za_ref[:, c0:c0 + 256] = z.astype(za_ref.dtype)

    first = (i % tiles_per_seq) == 0
    row = lax.broadcasted_iota(I32, (tm, 1), 0)
    c0 = 0
    while c0 < RWKV_PAD:
        n = min(512, RWKV_PAD - c0)
        z = _dot(h, w_ref[:, ATTN_COLS + c0:ATTN_COLS + c0 + n])
        prev_last = jnp.where(first, 0.0, carry_ref[7:8, c0:c0 + n])
        z_prev = jnp.where(row == 0, prev_last, pltpu.roll(z, 1, 0))
        carry_ref[:, c0:c0 + n] = z[tm - 8:tm, :]
        zr_ref[:, c0:c0 + n] = z + (z_prev - z) * mu_ref[:, c0:c0 + n]
        c0 += n


def _inproj(x2, g, sh, sc, w_bf, mu_pad, cos_t, sin_t, T):
    M, D = x2.shape
    tm = 256
    tps = T // tm
    NW = w_bf.shape[1]
    kern = functools.partial(_inproj_kernel, tiles_per_seq=tps)
    vm = (D * NW * 2 + 2 * tm * D * 4 + 2 * tm * ATTN_COLS * 2 + 2 * tm * RWKV_PAD * 4
          + 4 * tm * 512 * 4 + tm * D * 8 + (8 << 20))
    return pl.pallas_call(
        kern,
        grid=(M // tm,),
        in_specs=[pl.BlockSpec((tm, D), lambda i: (i, 0)),
                  pl.BlockSpec((1, D), lambda i: (0, 0)),
                  pl.BlockSpec((1, 1, D), lambda i: (i // tps, 0, 0)),
                  pl.BlockSpec((1, 1, D), lambda i: (i // tps, 0, 0)),
                  pl.BlockSpec((D, NW), lambda i: (0, 0), pipeline_mode=pl.Buffered(1)),
                  pl.BlockSpec((1, RWKV_PAD), lambda i: (0, 0)),
                  pl.BlockSpec((tm, 512), lambda i: (i % tps, 0)),
                  pl.BlockSpec((tm, 512), lambda i: (i % tps, 0))],
        out_specs=[pl.BlockSpec((tm, ATTN_COLS), lambda i: (i, 0)),
                   pl.BlockSpec((tm, RWKV_PAD), lambda i: (i, 0))],
        out_shape=[jax.ShapeDtypeStruct((M, ATTN_COLS), BF16),
                   jax.ShapeDtypeStruct((M, RWKV_PAD), F32)],
        scratch_shapes=[pltpu.VMEM((8, RWKV_PAD), F32)],
        compiler_params=pltpu.CompilerParams(
            dimension_semantics=("arbitrary",), vmem_limit_bytes=_vmem_limit(vm)),
        name="inproj",
    )(x2, g, sh, sc, w_bf, mu_pad, cos_t, sin_t)


def _attn_kernel(sink_ref, q_ref, kc_ref, kp_ref, vc_ref, vp_ref, o_ref):
    n = pl.program_id(1)
    blk = q_ref.shape[1]
    row = lax.broadcasted_iota(I32, (blk, blk), 0)
    col = lax.broadcasted_iota(I32, (blk, blk), 1)
    mask = jnp.concatenate([(col > row) & (n > 0), col <= row], axis=1)
    scale = 1.0 / math.sqrt(HEAD_DIM)
    outs = []
    for kvh in range(ATTN_KV_HEADS):
        ks = slice(kvh * HEAD_DIM, (kvh + 1) * HEAD_DIM)
        kmat = jnp.concatenate([kp_ref[0, :, ks], kc_ref[0, :, ks]], axis=0)
        vmat = jnp.concatenate([vp_ref[0, :, ks], vc_ref[0, :, ks]], axis=0)
        for g in range(ATTN_GROUP):
            hd = kvh * ATTN_GROUP + g
            qh = q_ref[0, :, hd * HEAD_DIM:(hd + 1) * HEAD_DIM]
            s = _dot_nt(qh, kmat) * scale
            s = jnp.where(mask, s, NEG_BIG)
            sink = sink_ref[hd]
            m = jnp.maximum(jnp.max(s, axis=-1, keepdims=True), sink)
            p = jnp.exp(s - m)
            denom = jnp.sum(p, axis=-1, keepdims=True) + jnp.exp(sink - m)
            o = _dot(p.astype(BF16), vmat)
            outs.append(o / denom)
    o_ref[0] = jnp.concatenate(outs, axis=1).astype(o_ref.dtype)


def _attention(za3, sinks):
    B, T, _ = za3.shape
    nb = T // WINDOW
    kcol = Q_COLS // KV_COLS
    prev = lambda b, n, s: (b, jnp.maximum(n - 1, 0), kcol)
    prev_v = lambda b, n, s: (b, jnp.maximum(n - 1, 0), kcol + 1)
    gs = pltpu.PrefetchScalarGridSpec(
        num_scalar_prefetch=1,
        grid=(B, nb),
        in_specs=[pl.BlockSpec((1, WINDOW, Q_COLS), lambda b, n, s: (b, n, 0)),
                  pl.BlockSpec((1, WINDOW, KV_COLS), lambda b, n, s: (b, n, kcol)),
                  pl.BlockSpec((1, WINDOW, KV_COLS), prev),
                  pl.BlockSpec((1, WINDOW, KV_COLS), lambda b, n, s: (b, n, kcol + 1)),
                  pl.BlockSpec((1, WINDOW, KV_COLS), prev_v)],
        out_specs=pl.BlockSpec((1, WINDOW, Q_COLS), lambda b, n, s: (b, n, 0)),
    )
    return pl.pallas_call(
        _attn_kernel,
        grid_spec=gs,
        out_shape=jax.ShapeDtypeStruct((B, T, Q_COLS), BF16),
        compiler_params=pltpu.CompilerParams(dimension_semantics=("arbitrary", "arbitrary")),
        name="attn",
    )(sinks, za3, za3, za3, za3, za3)


def _rwkv_kernel(r_ref, k_ref, v_ref, lora_ref, w0_ref, wdu_ref, a0_ref, wau_ref, wgu_ref,
                 kk_ref, ka_ref, rk_ref, lnw_ref, lnb_ref, o_ref, s_ref):
    c = pl.program_id(1)
    C = r_ref.shape[1]
    N = RWKV_N
    G = RWKV_GROUP
    GW = G * N

    @pl.when(c == 0)
    def _():
        s_ref[...] = jnp.zeros_like(s_ref)

    lora = lora_ref[0]
    wd = lora[:, 0:DECAY_LORA]
    ad = lora[:, DECAY_LORA:DECAY_LORA + A_LORA]
    gd = lora[:, DECAY_LORA + A_LORA:LORA_COLS]
    wlin = w0_ref[...] + _dot(jnp.tanh(wd), wdu_ref[...])
    neg = -wlin
    softplus = jnp.maximum(neg, 0.0) + jnp.log(1.0 + jnp.exp(-jnp.abs(neg)))
    w = -softplus - 0.5
    logdec = -jnp.exp(w)
    a_all = 1.0 / (1.0 + jnp.exp(-(a0_ref[...] + _dot(ad, wau_ref[...]))))
    g_all = _dot(1.0 / (1.0 + jnp.exp(-gd)), wgu_ref[...])
    ti = lax.broadcasted_iota(I32, (C, C), 0)
    si = lax.broadcasted_iota(I32, (C, C), 1)
    tril = jnp.where(si <= ti, 1.0, 0.0).astype(BF16)
    ld_hi = logdec.astype(BF16)
    ld_lo = (logdec - ld_hi.astype(F32)).astype(BF16)
    cum = _dot(tril, ld_hi) + _dot(tril, ld_lo)
    e_in_all = jnp.exp(cum)
    e_ex_all = jnp.exp(cum - logdec)
    e_neg_all = jnp.exp(-cum)

    lane_head = lax.broadcasted_iota(I32, (1, GW), 1) // N
    t_row = lax.broadcasted_iota(I32, (C, GW), 0)
    s_lane = lax.broadcasted_iota(I32, (C, GW), 1) % N
    strict = s_lane < t_row
    incl = s_lane <= t_row
    bi = lax.broadcasted_iota(I32, (GW, GW), 0)
    bj = lax.broadcasted_iota(I32, (GW, GW), 1)
    same_head = (bi // N) == (bj // N)
    ones_bd = jnp.where(same_head, 1.0, 0.0).astype(BF16)
    eye = jnp.where(bi == bj, 1.0, 0.0)

    def expand(xc):
        return jnp.concatenate([jnp.where(lane_head == h, xc, 0.0) for h in range(G)], axis=0)

    def head_sum(xc):
        hi = xc.astype(BF16)
        lo = (xc - hi.astype(F32)).astype(BF16)
        return _dot(hi, ones_bd) + _dot(lo, ones_bd)

    for gi in range(RWKV_HEADS // G):
        cs = slice(gi * GW, (gi + 1) * GW)
        r_g = r_ref[0, :, cs]
        k_g = k_ref[0, :, cs]
        v_g = v_ref[0, :, cs]
        a_g = a_all[:, cs]
        e_in = e_in_all[:, cs]
        e_neg = e_neg_all[:, cs]
        kk = k_g * kk_ref[:, cs]
        kk = kk / jnp.maximum(jnp.sqrt(head_sum(kk * kk)), 1e-12)
        k2 = k_g * (1.0 + (a_g - 1.0) * ka_ref[:, cs])
        at = -kk * e_ex_all[:, cs]
        rt = r_g * e_in
        bt = kk * a_g * e_neg
        kt = k2 * e_neg
        left = jnp.concatenate([at, rt], axis=0).astype(BF16)
        right = jnp.concatenate([bt, kt], axis=0).astype(BF16)
        zed = jnp.concatenate([expand(bt), expand(kt)], axis=0).astype(BF16)
        big = _dot_nt(left, zed)
        a_ab = jnp.where(strict, big[:C, :G * C], 0.0)
        a_ak = jnp.where(strict, big[:C, G * C:], 0.0)
        a_rb = jnp.where(incl, big[C:, :G * C], 0.0)
        a_rk = jnp.where(incl, big[C:, G * C:], 0.0)
        s_old = s_ref[gi]
        ls = _dot_nt(left, s_old.astype(BF16))
        v_bd = expand(v_g).astype(BF16)
        x = ls[:C] + _dot(a_ak.astype(BF16), v_bd)
        pw = expand(a_ab)
        tinv = eye + pw
        span = 2
        while span < C:
            pwb = pw.astype(BF16)
            pw = _dot(pwb, pwb)
            tinv = tinv + _dot(tinv.astype(BF16), pw.astype(BF16))
            span *= 2
        u_bd = _dot(tinv.astype(BF16), expand(x).astype(BF16))
        u = u_bd[0:C]
        for h in range(1, G):
            u = u + u_bd[h * C:(h + 1) * C]
        y = ls[C:] + _dot(jnp.concatenate([a_rb, a_rk], axis=1).astype(BF16),
                          jnp.concatenate([u_bd.astype(BF16), v_bd], axis=0))
        uv = jnp.concatenate([u, v_g], axis=0).astype(BF16)
        s_new = (s_old + jnp.where(same_head, _dot_tn(uv, right), 0.0)) * e_in[C - 1:C, :]
        s_ref[gi] = s_new
        mean = head_sum(y) * (1.0 / N)
        yc = y - mean
        var = head_sum(yc * yc) * (1.0 / N)
        yn = yc * lax.rsqrt(var + RWKV_LN_EPS) * lnw_ref[:, cs] + lnb_ref[:, cs]
        bonus = head_sum(r_g * k2 * rk_ref[:, cs]) * v_g
        o_ref[0, :, cs] = ((yn + bonus) * g_all[:, cs]).astype(o_ref.dtype)


def _rwkv(zr3, w0, wdu, a0, wau, wgu, k_k, k_a, r_k, ln_w, ln_b):
    B, T, _ = zr3.shape
    C = CHUNK
    W = RWKV_W
    GW = RWKV_GROUP * RWKV_N
    vec = lambda v: v.reshape(1, W)
    full = lambda shape: pl.BlockSpec(shape, lambda b, c: (0,) * len(shape))
    return pl.pallas_call(
        _rwkv_kernel,
        grid=(B, T // C),
        in_specs=[pl.BlockSpec((1, C, W), lambda b, c: (b, c, 0)),
                  pl.BlockSpec((1, C, W), lambda b, c: (b, c, 1)),
                  pl.BlockSpec((1, C, W), lambda b, c: (b, c, 2)),
                  pl.BlockSpec((1, C, LORA_PAD), lambda b, c: (b, c, 3 * W // LORA_PAD)),
                  full((1, W)), full((DECAY_LORA, W)), full((1, W)), full((A_LORA, W)),
                  full((GATE_LORA, W)), full((1, W)), full((1, W)), full((1, W)),
                  full((1, W)), full((1, W))],
        out_specs=pl.BlockSpec((1, C, W), lambda b, c: (b, c, 0)),
        out_shape=jax.ShapeDtypeStruct((B, T, W), BF16),
        scratch_shapes=[pltpu.VMEM((RWKV_HEADS // RWKV_GROUP, GW, GW), F32)],
        compiler_params=pltpu.CompilerParams(dimension_semantics=("arbitrary", "arbitrary")),
        name="rwkv",
    )(zr3, zr3, zr3, zr3, vec(w0), wdu, vec(a0), wau, wgu, vec(k_k), vec(k_a), vec(r_k),
      vec(ln_w), vec(ln_b))


def _outproj_kernel(oa_ref, or_ref, x_ref, wo_ref, gt_ref, g_ref, sh_ref, sc_ref,
                    wrh_ref, wrl_ref, br_ref, x1_ref, h2_ref, lg_ref):
    mixed = _dot(oa_ref[...], wo_ref[0:Q_COLS, :]) + _dot(or_ref[...], wo_ref[Q_COLS:, :])
    x1 = x_ref[...] + gt_ref[0] * mixed
    x1_ref[...] = x1
    ms = jnp.mean(x1 * x1, axis=-1, keepdims=True)
    h2 = x1 * lax.rsqrt(ms + NORM_EPS) * g_ref[...] * (1.0 + sc_ref[0]) + sh_ref[0]
    h2_ref[...] = h2
    hh = h2.astype(BF16)
    hl = (h2 - hh.astype(F32)).astype(BF16)
    lg_ref[...] = (_dot(hh, wrh_ref[...]) + _dot(hl, wrh_ref[...]) + _dot(hh, wrl_ref[...])
                   + br_ref[...])


def _outproj(oa2, or2, x2, wo_bf, gt, g2, sh, sc, wr_hi, wr_lo, br, T):
    M, D = x2.shape
    tm = 256
    tps = T // tm
    bvec = pl.BlockSpec((1, 1, D), lambda i: (i // tps, 0, 0))
    vm = D * D * 2 * 2 + 2 * tm * D * (2 + 4 + 4 + 4) + tm * D * 16 + (8 << 20)
    return pl.pallas_call(
        _outproj_kernel,
        grid=(M // tm,),
        in_specs=[pl.BlockSpec((tm, Q_COLS), lambda i: (i, 0)),
                  pl.BlockSpec((tm, RWKV_W), lambda i: (i, 0)),
                  pl.BlockSpec((tm, D), lambda i: (i, 0)),
                  pl.BlockSpec((D, D), lambda i: (0, 0)),
                  bvec,
                  pl.BlockSpec((1, D), lambda i: (0, 0)),
                  bvec, bvec,
                  pl.BlockSpec((D, LANES), lambda i: (0, 0)),
                  pl.BlockSpec((D, LANES), lambda i: (0, 0)),
                  pl.BlockSpec((1, LANES), lambda i: (0, 0))],
        out_specs=[pl.BlockSpec((tm, D), lambda i: (i, 0)),
                   pl.BlockSpec((tm, D), lambda i: (i, 0)),
                   pl.BlockSpec((tm, LANES), lambda i: (i, 0))],
        out_shape=[jax.ShapeDtypeStruct((M, D), F32),
                   jax.ShapeDtypeStruct((M, D), F32),
                   jax.ShapeDtypeStruct((M, LANES), F32)],
        compiler_params=pltpu.CompilerParams(
            dimension_semantics=("arbitrary",), vmem_limit_bytes=_vmem_limit(vm)),
        name="outproj",
    )(oa2, or2, x2, wo_bf, gt, g2, sh, sc, wr_hi, wr_lo, br)


def _route_kernel(lg_ref, info_ref, cnt_ref, run_ref):
    i = pl.program_id(0)
    tm = lg_ref.shape[0]

    @pl.when(i == 0)
    def _():
        run_ref[...] = jnp.zeros_like(run_ref)

    lg = lg_ref[...]
    lane = lax.broadcasted_iota(I32, lg.shape, 1)
    gl = jnp.where(lane < N_GROUPS, lg, NEG_BIG)
    gmax = jnp.max(gl, axis=-1, keepdims=True)
    gsum = jnp.sum(jnp.exp(gl - gmax), axis=-1, keepdims=True)
    g_gate = 1.0 / gsum
    g_idx = jnp.min(jnp.where(gl == gmax, lane, LANES), axis=-1, keepdims=True)
    lo = N_GROUPS + EXPERTS_PER_GROUP * g_idx
    el = jnp.where((lane >= lo) & (lane < lo + EXPERTS_PER_GROUP), lg, NEG_BIG)
    e1max = jnp.max(el, axis=-1, keepdims=True)
    l1 = jnp.min(jnp.where(el == e1max, lane, LANES), axis=-1, keepdims=True)
    el2 = jnp.where(lane == l1, NEG_BIG, el)
    e2max = jnp.max(el2, axis=-1, keepdims=True)
    l2 = jnp.min(jnp.where(el2 == e2max, lane, LANES), axis=-1, keepdims=True)
    t2 = jnp.exp(e2max - e1max)
    w1 = g_gate / (1.0 + t2)
    w2 = g_gate * t2 / (1.0 + t2)
    ex1 = l1 - N_GROUPS
    ex2 = l2 - N_GROUPS

    oh1 = jnp.where(lane == ex1, 1.0, 0.0)
    oh2 = jnp.where(lane == ex2, 1.0, 0.0)
    ti = lax.broadcasted_iota(I32, (tm, tm), 0)
    si = lax.broadcasted_iota(I32, (tm, tm), 1)
    lower = jnp.where(si < ti, 1.0, 0.0).astype(BF16)
    pre1 = _dot(lower, oh1.astype(BF16))
    pre2 = _dot(lower, oh2.astype(BF16))
    cnt1 = jnp.sum(oh1, axis=0, keepdims=True)
    cnt2 = jnp.sum(oh2, axis=0, keepdims=True)
    run = run_ref[...]
    rank1 = jnp.sum(oh1 * (pre1 + run), axis=-1, keepdims=True)
    rank2 = jnp.sum(oh2 * (pre2 + run + cnt1), axis=-1, keepdims=True)
    run = run + cnt1 + cnt2
    run_ref[...] = run
    cnt_ref[...] = run

    info = jnp.where(lane == 0, ex1.astype(F32), 0.0)
    info = jnp.where(lane == 1, ex2.astype(F32), info)
    info = jnp.where(lane == 2, rank1, info)
    info = jnp.where(lane == 3, rank2, info)
    info = jnp.where(lane == 4, w1, info)
    info = jnp.where(lane == 5, w2, info)
    info_ref[...] = info


def _route(lg):
    M = lg.shape[0]
    tm = 256
    return pl.pallas_call(
        _route_kernel,
        grid=(M // tm,),
        in_specs=[pl.BlockSpec((tm, LANES), lambda i: (i, 0))],
        out_specs=[pl.BlockSpec((tm, LANES), lambda i: (i, 0)),
                   pl.BlockSpec((1, LANES), lambda i: (0, 0))],
        out_shape=[jax.ShapeDtypeStruct((M, LANES), F32),
                   jax.ShapeDtypeStruct((1, LANES), F32)],
        scratch_shapes=[pltpu.VMEM((1, LANES), F32)],
        compiler_params=pltpu.CompilerParams(dimension_semantics=("arbitrary",)),
        name="route",
    )(lg)


def _plan_kernel(info_ref, cnt_ref, slot_ref, meta_ref):
    cnt = cnt_ref[...]
    lane_r = lax.broadcasted_iota(I32, (1, LANES), 1)
    nblk = jnp.floor((cnt + (MOE_BLOCK - 1)) * (1.0 / MOE_BLOCK))
    ei = lax.broadcasted_iota(I32, (LANES, LANES), 0)
    ej = lax.broadcasted_iota(I32, (LANES, LANES), 1)
    upper = jnp.where(ei <= ej, 1.0, 0.0).astype(BF16)
    nb8 = jnp.broadcast_to(nblk, (8, LANES)).astype(BF16)
    bend = _dot(nb8, upper)[0:1, :]
    bstart = bend - nblk
    pstart = bstart * MOE_BLOCK

    info = info_ref[...]
    lane = lax.broadcasted_iota(I32, info.shape, 1)
    ex1 = info[:, 0:1].astype(I32)
    ex2 = info[:, 1:2].astype(I32)
    s1 = jnp.sum(jnp.where(lane == ex1, pstart, 0.0), axis=-1, keepdims=True) + info[:, 2:3]
    s2 = jnp.sum(jnp.where(lane == ex2, pstart, 0.0), axis=-1, keepdims=True) + info[:, 3:4]
    slot = jnp.where(lane == 0, s1, 0.0)
    slot = jnp.where(lane == 1, s2, slot)
    slot_ref[...] = slot.astype(I32)

    blk = lax.broadcasted_iota(I32, (LANES, 2 * LANES), 1).astype(F32)
    bend_col = jnp.sum(jnp.where(ei == ej, jnp.broadcast_to(bend, (LANES, LANES)), 0.0),
                       axis=-1, keepdims=True)
    erow = lax.broadcasted_iota(I32, (LANES, 2 * LANES), 0)
    hit = jnp.where((bend_col <= blk) & (erow < N_EXPERTS), 1.0, 0.0)
    bexp = jnp.minimum(jnp.sum(hit, axis=0, keepdims=True), N_EXPERTS - 1.0)
    n_used = jnp.max(jnp.where(lane_r < N_EXPERTS, bend, 0.0), axis=-1, keepdims=True)
    lastblk = jnp.where(nblk > 0, bend - 1.0, -1.0)
    r8 = lax.broadcasted_iota(I32, (8, 2 * LANES), 0)
    last2 = jnp.concatenate([lastblk, jnp.full((1, LANES), -1.0)], axis=1)
    meta = jnp.where(r8 == 0, jnp.broadcast_to(bexp, (8, 2 * LANES)), 0.0)
    meta = jnp.where(r8 == 1, jnp.broadcast_to(n_used, (8, 2 * LANES)), meta)
    meta = jnp.where(r8 == 2, jnp.broadcast_to(last2, (8, 2 * LANES)), meta)
    meta_ref[...] = meta.astype(I32)


def _plan(info, cnt):
    M = info.shape[0]
    tm = 256
    return pl.pallas_call(
        _plan_kernel,
        grid=(M // tm,),
        in_specs=[pl.BlockSpec((tm, LANES), lambda i: (i, 0)),
                  pl.BlockSpec((1, LANES), lambda i: (0, 0))],
        out_specs=[pl.BlockSpec((tm, LANES), lambda i: (i, 0)),
                   pl.BlockSpec((8, 2 * LANES), lambda i: (0, 0))],
        out_shape=[jax.ShapeDtypeStruct((M, LANES), I32),
                   jax.ShapeDtypeStruct((8, 2 * LANES), I32)],
        compiler_params=pltpu.CompilerParams(dimension_semantics=("arbitrary",)),
        name="plan",
    )(info, cnt)


def _dispatch_kernel(slot_ref, last_ref, h2_ref, xb_ref, zero_ref, sem_z, sem_r, *, tm, n_blocks):
    i = pl.program_id(0)

    def zero_block(blk):
        return pltpu.make_async_copy(zero_ref, xb_ref.at[pl.ds(blk * MOE_BLOCK, MOE_BLOCK)], sem_z)

    @pl.when(i == 0)
    def _():
        zero_ref[...] = jnp.zeros_like(zero_ref)
        n_used = last_ref[N_EXPERTS]

        def start(e, c):
            @pl.when(last_ref[e] >= 0)
            def _():
                zero_block(last_ref[e]).start()
            return c

        def wait(e, c):
            @pl.when(last_ref[e] >= 0)
            def _():
                zero_block(last_ref[e]).wait()
            return c

        def start_tail(b, c):
            zero_block(b).start()
            return c

        def wait_tail(b, c):
            zero_block(b).wait()
            return c

        lax.fori_loop(0, N_EXPERTS, start, 0)
        lax.fori_loop(n_used, n_blocks, start_tail, 0)
        lax.fori_loop(0, N_EXPERTS, wait, 0)
        lax.fori_loop(n_used, n_blocks, wait_tail, 0)

    def row_copy(t, k):
        tok = i * tm + t
        return pltpu.make_async_copy(h2_ref.at[pl.ds(t, 1)], xb_ref.at[pl.ds(slot_ref[2 * tok + k], 1)], sem_r)

    def start_rows(t, c):
        row_copy(t, 0).start()
        row_copy(t, 1).start()
        return c

    def wait_rows(t, c):
        row_copy(t, 0).wait()
        row_copy(t, 1).wait()
        return c

    lax.fori_loop(0, tm, start_rows, 0)
    lax.fori_loop(0, tm, wait_rows, 0)


def _dispatch(slots_flat, last_blk, h2, n_rows):
    M, D = h2.shape
    tm = 256
    gs = pltpu.PrefetchScalarGridSpec(
        num_scalar_prefetch=2,
        grid=(M // tm,),
        in_specs=[pl.BlockSpec((tm, D), lambda i, s, l: (i, 0))],
        out_specs=pl.BlockSpec(memory_space=pl.ANY),
        scratch_shapes=[pltpu.VMEM((MOE_BLOCK, D), F32),
                        pltpu.SemaphoreType.DMA(()), pltpu.SemaphoreType.DMA(())],
    )
    return pl.pallas_call(
        functools.partial(_dispatch_kernel, tm=tm, n_blocks=n_rows // MOE_BLOCK),
        grid_spec=gs,
        out_shape=jax.ShapeDtypeStruct((n_rows, D), F32),
        compiler_params=pltpu.CompilerParams(dimension_semantics=("arbitrary",),
                                             has_side_effects=True),
        name="dispatch",
    )(slots_flat, last_blk, h2)


def _experts_kernel(bexp_ref, nused_ref, xb_ref, w1_ref, w3_ref, w2_ref, y_ref):
    i = pl.program_id(0)

    @pl.when(i < nused_ref[0])
    def _():
        x = xb_ref[...].astype(BF16)
        h1 = _dot(x, w1_ref[0])
        h3 = _dot(x, w3_ref[0])
        hid = h1 * (1.0 / (1.0 + jnp.exp(-h1))) * h3
        y_ref[...] = _dot(hid.astype(BF16), w2_ref[0])

    @pl.when(i >= nused_ref[0])
    def _():
        y_ref[...] = jnp.zeros_like(y_ref)


def _experts(bexp, nused, xb, w1, w3, w2, n_blocks):
    D = xb.shape[1]
    FF = w1.shape[2]
    clamp = lambda i, be, nu: (jnp.minimum(i, nu[0] - 1), 0)
    gs = pltpu.PrefetchScalarGridSpec(
        num_scalar_prefetch=2,
        grid=(n_blocks,),
        in_specs=[pl.BlockSpec((MOE_BLOCK, D), clamp),
                  pl.BlockSpec((1, D, FF), lambda i, be, nu: (be[i], 0, 0)),
                  pl.BlockSpec((1, D, FF), lambda i, be, nu: (be[i], 0, 0)),
                  pl.BlockSpec((1, FF, D), lambda i, be, nu: (be[i], 0, 0))],
        out_specs=pl.BlockSpec((MOE_BLOCK, D), lambda i, be, nu: (i, 0)),
    )
    vm = 2 * 3 * D * FF * 4 + 4 * MOE_BLOCK * D * 4 + (12 << 20)
    return pl.pallas_call(
        _experts_kernel,
        grid_spec=gs,
        out_shape=jax.ShapeDtypeStruct((n_blocks * MOE_BLOCK, D), F32),
        compiler_params=pltpu.CompilerParams(
            dimension_semantics=("arbitrary",), vmem_limit_bytes=_vmem_limit(vm)),
        name="experts",
    )(bexp, nused, xb, w1, w3, w2)


def _combine_kernel(slot_ref, x1_ref, info_ref, gt_ref, fg_ref, y_ref, o_ref, buf_ref, sem_ref, *, tm):
    i = pl.program_id(0)
    n = pl.num_programs(0)

    def row_copy(step, t, k):
        par = step % 2
        tok = step * tm + t
        return pltpu.make_async_copy(y_ref.at[pl.ds(slot_ref[2 * tok + k], 1)],
                                     buf_ref.at[par, k, pl.ds(t, 1)], sem_ref.at[par])

    def start_tile(step):
        def go(t, c):
            row_copy(step, t, 0).start()
            row_copy(step, t, 1).start()
            return c
        lax.fori_loop(0, tm, go, 0)

    @pl.when(i == 0)
    def _():
        start_tile(0)

    @pl.when(i + 1 < n)
    def _():
        start_tile(i + 1)

    def wait(t, c):
        row_copy(i, t, 0).wait()
        row_copy(i, t, 1).wait()
        return c

    lax.fori_loop(0, tm, wait, 0)
    par = i % 2
    info = info_ref[...]
    moe = info[:, 4:5] * buf_ref[par, 0] + info[:, 5:6] * buf_ref[par, 1]
    x = x1_ref[...] + gt_ref[0] * moe
    ms = jnp.mean(x * x, axis=-1, keepdims=True)
    o_ref[...] = x * lax.rsqrt(ms + NORM_EPS) * fg_ref[...]


def _combine(slots_flat, x1, info, gt, final_g, y, T):
    M, D = x1.shape
    tm = 128
    tps = T // tm
    gs = pltpu.PrefetchScalarGridSpec(
        num_scalar_prefetch=1,
        grid=(M // tm,),
        in_specs=[pl.BlockSpec((tm, D), lambda i, s: (i, 0)),
                  pl.BlockSpec((tm, LANES), lambda i, s: (i, 0)),
                  pl.BlockSpec((1, 1, D), lambda i, s: (i // tps, 0, 0)),
                  pl.BlockSpec((1, D), lambda i, s: (0, 0)),
                  pl.BlockSpec(memory_space=pl.ANY)],
        out_specs=pl.BlockSpec((tm, D), lambda i, s: (i, 0)),
        scratch_shapes=[pltpu.VMEM((2, 2, tm, D), F32), pltpu.SemaphoreType.DMA((2,))],
    )
    vm = 4 * tm * D * 4 + 4 * tm * D * 4 + tm * D * 16 + (8 << 20)
    return pl.pallas_call(
        functools.partial(_combine_kernel, tm=tm),
        grid_spec=gs,
        out_shape=jax.ShapeDtypeStruct((M, D), F32),
        compiler_params=pltpu.CompilerParams(
            dimension_semantics=("arbitrary",), vmem_limit_bytes=_vmem_limit(vm)),
        name="combine",
    )(slots_flat, x1, info, gt, final_g.reshape(1, D), y)


def _rope_tables(T):
    half = HEAD_DIM // 2
    inv_freq = ROPE_THETA ** (-jnp.arange(0, HEAD_DIM, 2, dtype=F32) / HEAD_DIM)
    ang = jnp.arange(T, dtype=F32)[:, None] * inv_freq[None, :]
    cos, sin = jnp.cos(ang), jnp.sin(ang)
    cos_h = jnp.concatenate([cos, cos], axis=1)
    sin_h = jnp.concatenate([-sin, sin], axis=1)
    cos_q = jnp.tile(cos_h, (1, 4))
    sin_q = jnp.tile(sin_h, (1, 4))
    cos_kv = jnp.concatenate([jnp.tile(cos_h, (1, 2)), jnp.ones((T, KV_COLS), F32)], axis=1)
    sin_kv = jnp.concatenate([jnp.tile(sin_h, (1, 2)), jnp.zeros((T, KV_COLS), F32)], axis=1)
    return jnp.concatenate([cos_q, cos_kv], axis=1), jnp.concatenate([sin_q, sin_kv], axis=1)


def _layer(x, c, w_ada, b_ada, norm1_g, w_in, mu_shift, sinks, w0, w_decay_up, a0, w_a_up, w_g_up,
           k_k, k_a, r_k, ln_x_w, ln_x_b, w_o, norm2_g, w_rg, b_rg, w_re, b_re, w1, w3, w2, final_g,
           apply_final):
    B, T, D = x.shape
    M = B * T
    mod = _ada(c, w_ada, b_ada)
    sh1, sc1, gt1, sh2, sc2, gt2 = [m.reshape(B, 1, D) for m in jnp.split(mod, 6, axis=-1)]

    pad = RWKV_PAD - RWKV_COLS
    w_bf = jnp.pad(w_in, ((0, 0), (0, pad))).astype(BF16)
    mu_pad = jnp.pad(mu_shift, (0, pad)).reshape(1, RWKV_PAD)
    cos_t, sin_t = _rope_tables(T)
    x2 = x.reshape(M, D)
    za, zr = _inproj(x2, norm1_g.reshape(1, D), sh1, sc1, w_bf, mu_pad, cos_t, sin_t, T)

    o_attn = _attention(za.reshape(B, T, ATTN_COLS), sinks)
    o_rwkv = _rwkv(zr.reshape(B, T, RWKV_PAD), w0, w_decay_up, a0, w_a_up, w_g_up, k_k, k_a,
                   r_k.reshape(-1), ln_x_w, ln_x_b)

    wr = jnp.concatenate([w_rg, w_re], axis=1)
    wr = jnp.pad(wr, ((0, 0), (0, LANES - wr.shape[1])))
    wr_hi = wr.astype(BF16)
    wr_lo = (wr - wr_hi.astype(F32)).astype(BF16)
    br = jnp.pad(jnp.concatenate([b_rg, b_re]), (0, LANES - N_GROUPS - N_EXPERTS)).reshape(1, LANES)
    x1, h2, lg = _outproj(o_attn.reshape(M, Q_COLS), o_rwkv.reshape(M, RWKV_W), x2, w_o.astype(BF16),
                          gt1, norm2_g.reshape(1, D), sh2, sc2, wr_hi, wr_lo, br, T)

    n_blocks = -(-(2 * M) // MOE_BLOCK) + N_EXPERTS
    info, cnt = _route(lg)
    slots, meta = _plan(info, cnt)
    slots_flat = slots[:, :2].reshape(-1)
    seg_meta = jnp.concatenate([meta[2, :N_EXPERTS], meta[1, :1]])
    xb = _dispatch(slots_flat, seg_meta, h2, n_blocks * MOE_BLOCK)
    y = _experts(meta[0, :n_blocks], meta[1, :1], xb, w1, w3, w2, n_blocks)
    out = _combine(slots_flat, x1, info, gt2, final_g, y, T)
    del apply_final
    return out.reshape(B, T, D)


def kernel(x, c, w_ada, b_ada, norm1_g, w_in, mu_shift, sinks, w0, w_decay_up, a0, w_a_up, w_g_up, k_k, k_a, r_k, ln_x_w, ln_x_b, w_o, norm2_g, w_router_group, b_router_group, w_router_expert, b_router_expert, w1, w3, w2, final_g):
    depth = w_ada.shape[0]
    assert depth == 1, "single-layer stack"
    l = 0
    return _layer(x, c, w_ada[l], b_ada[l], norm1_g[l], w_in[l], mu_shift[l], sinks[l], w0[l],
                  w_decay_up[l], a0[l], w_a_up[l], w_g_up[l], k_k[l], k_a[l], r_k[l], ln_x_w[l],
                  ln_x_b[l], w_o[l], norm2_g[l], w_router_group[l], b_router_group[l],
                  w_router_expert[l], b_router_expert[l], w1[l], w3[l], w2[l], final_g, True)
```

```python
import functools
import math

import jax
import jax.numpy as jnp
from jax import lax
from jax.experimental import pallas as pl
from jax.experimental.pallas import tpu as pltpu

F32 = jnp.float32
BF16 = jnp.bfloat16
I32 = jnp.int32

LANES = 128
VMEM_BYTES_V7X = 64 * 1024 * 1024

HEAD_DIM = 64
ATTN_HEADS = 16
ATTN_KV_HEADS = 2
ATTN_GROUP = ATTN_HEADS // ATTN_KV_HEADS
WINDOW = 128
ROPE_THETA = 10000.0
RWKV_HEADS = 16
RWKV_N = 64
DECAY_LORA = 64
A_LORA = 64
GATE_LORA = 160
RWKV_LN_EPS = 64e-5
N_GROUPS = 8
EXPERTS_PER_GROUP = 8
N_EXPERTS = N_GROUPS * EXPERTS_PER_GROUP
EXPERT_FF = 512
MOE_BLOCK = 128
NORM_EPS = 1e-6

Q_COLS = ATTN_HEADS * HEAD_DIM
KV_COLS = ATTN_KV_HEADS * HEAD_DIM
ATTN_COLS = Q_COLS + 2 * KV_COLS
RWKV_W = RWKV_HEADS * RWKV_N
LORA_COLS = DECAY_LORA + A_LORA + GATE_LORA
LORA_PAD = 384
RWKV_COLS = 3 * RWKV_W + LORA_COLS
RWKV_PAD = 3 * RWKV_W + LORA_PAD
CHUNK = 64
RWKV_GROUP = 4
NEG_BIG = -1e30


def _vmem_limit(nbytes):
    return int(min(nbytes, VMEM_BYTES_V7X - 4 * 1024 * 1024))


def _dot(a, b):
    return jnp.dot(a, b, preferred_element_type=F32)


def _dot_nt(a, b):
    return lax.dot_general(a, b, (((1,), (1,)), ((), ())), preferred_element_type=F32)


def _dot_tn(a, b):
    return lax.dot_general(a, b, (((0,), (0,)), ((), ())), preferred_element_type=F32)


def _ada_kernel(c_ref, w_ref, b_ref, o_ref):
    c = c_ref[...]
    s = c * (1.0 / (1.0 + jnp.exp(-c)))
    o_ref[...] = _dot(s, w_ref[...]) + b_ref[...]


def _ada(c, w_ada, b_ada):
    B, D = c.shape
    N = w_ada.shape[1]
    tn = 1024
    cp = jnp.zeros((8, D), F32).at[:B].set(c)
    out = pl.pallas_call(
        _ada_kernel,
        grid=(N // tn,),
        in_specs=[pl.BlockSpec((8, D), lambda j: (0, 0)),
                  pl.BlockSpec((D, tn), lambda j: (0, j)),
                  pl.BlockSpec((1, tn), lambda j: (0, j))],
        out_specs=pl.BlockSpec((8, tn), lambda j: (0, j)),
        out_shape=jax.ShapeDtypeStruct((8, N), F32),
        compiler_params=pltpu.CompilerParams(
            dimension_semantics=("arbitrary",),
            vmem_limit_bytes=_vmem_limit(2 * D * tn * 4 + (8 << 20))),
        name="ada",
    )(cp, w_ada, b_ada.reshape(1, N))
    return out[:B]


def _rope(z, cos, sin):
    w = z.shape[-1]
    lane = lax.broadcasted_iota(I32, z.shape, 1)
    first_half = (lane % HEAD_DIM) < (HEAD_DIM // 2)
    partner = jnp.where(first_half, pltpu.roll(z, w - HEAD_DIM // 2, 1), pltpu.roll(z, HEAD_DIM // 2, 1))
    return z * cos + partner * sin


def _inproj_kernel(x_ref, g_ref, sh_ref, sc_ref, w_ref, mu_ref, cos_ref, sin_ref,
                   za_ref, zr_ref, carry_ref, *, tiles_per_seq):
    i = pl.program_id(0)
    tm = x_ref.shape[0]
    x = x_ref[...]
    ms = jnp.mean(x * x, axis=-1, keepdims=True)
    y = x * lax.rsqrt(ms + NORM_EPS) * g_ref[...]
    h = (y * (1.0 + sc_ref[0]) + sh_ref[0]).astype(BF16)

    for c0 in range(0, ATTN_COLS, 256):
        z = _dot(h, w_ref[:, c0:c0 + 256])
        t0 = 0 if c0 < Q_COLS else 256
        z = _rope(z, cos_ref[:, t0:t0 + 256], sin_ref[:, t0:t0 + 256])
        za_ref[:, c0:c0 + 256] = z.astype(za_ref.dtype)

    first = (i % tiles_per_seq) == 0
    row = lax.broadcasted_iota(I32, (tm, 1), 0)
    c0 = 0
    while c0 < RWKV_PAD:
        n = min(512, RWKV_PAD - c0)
        z = _dot(h, w_ref[:, ATTN_COLS + c0:ATTN_COLS + c0 + n])
        prev_last = jnp.where(first, 0.0, carry_ref[7:8, c0:c0 + n])
        z_prev = jnp.where(row == 0, prev_last, pltpu.roll(z, 1, 0))
        carry_ref[:, c0:c0 + n] = z[tm - 8:tm, :]
        zr_ref[:, c0:c0 + n] = z + (z_prev - z) * mu_ref[:, c0:c0 + n]
        c0 += n


def _inproj(x2, g, sh, sc, w_bf, mu_pad, cos_t, sin_t, T):
    M, D = x2.shape
    tm = 256
    tps = T // tm
    NW = w_bf.shape[1]
    kern = functools.partial(_inproj_kernel, tiles_per_seq=tps)
    vm = (D * NW * 2 + 2 * tm * D * 4 + 2 * tm * ATTN_COLS * 2 + 2 * tm * RWKV_PAD * 4
          + 4 * tm * 512 * 4 + tm * D * 8 + (8 << 20))
    return pl.pallas_call(
        kern,
        grid=(M // tm,),
        in_specs=[pl.BlockSpec((tm, D), lambda i: (i, 0)),
                  pl.BlockSpec((1, D), lambda i: (0, 0)),
                  pl.BlockSpec((1, 1, D), lambda i: (i // tps, 0, 0)),
                  pl.BlockSpec((1, 1, D), lambda i: (i // tps, 0, 0)),
                  pl.BlockSpec((D, NW), lambda i: (0, 0), pipeline_mode=pl.Buffered(1)),
                  pl.BlockSpec((1, RWKV_PAD), lambda i: (0, 0)),
                  pl.BlockSpec((tm, 512), lambda i: (i % tps, 0)),
                  pl.BlockSpec((tm, 512), lambda i: (i % tps, 0))],
        out_specs=[pl.BlockSpec((tm, ATTN_COLS), lambda i: (i, 0)),
                   pl.BlockSpec((tm, RWKV_PAD), lambda i: (i, 0))],
        out_shape=[jax.ShapeDtypeStruct((M, ATTN_COLS), BF16),
                   jax.ShapeDtypeStruct((M, RWKV_PAD), F32)],
        scratch_shapes=[pltpu.VMEM((8, RWKV_PAD), F32)],
        compiler_params=pltpu.CompilerParams(
            dimension_semantics=("arbitrary",), vmem_limit_bytes=_vmem_limit(vm)),
        name="inproj",
    )(x2, g, sh, sc, w_bf, mu_pad, cos_t, sin_t)


def _attn_kernel(sink_ref, q_ref, kc_ref, kp_ref, vc_ref, vp_ref, o_ref):
    n = pl.program_id(1)
    blk = q_ref.shape[1]
    row = lax.broadcasted_iota(I32, (blk, blk), 0)
    col = lax.broadcasted_iota(I32, (blk, blk), 1)
    mask = jnp.concatenate([(col > row) & (n > 0), col <= row], axis=1)
    scale = 1.0 / math.sqrt(HEAD_DIM)
    outs = []
    for kvh in range(ATTN_KV_HEADS):
        ks = slice(kvh * HEAD_DIM, (kvh + 1) * HEAD_DIM)
        kmat = jnp.concatenate([kp_ref[0, :, ks], kc_ref[0, :, ks]], axis=0)
        vmat = jnp.concatenate([vp_ref[0, :, ks], vc_ref[0, :, ks]], axis=0)
        for g in range(ATTN_GROUP):
            hd = kvh * ATTN_GROUP + g
            qh = q_ref[0, :, hd * HEAD_DIM:(hd + 1) * HEAD_DIM]
            s = _dot_nt(qh, kmat) * scale
            s = jnp.where(mask, s, NEG_BIG)
            sink = sink_ref[hd]
            m = jnp.maximum(jnp.max(s, axis=-1, keepdims=True), sink)
            p = jnp.exp(s - m)
            denom = jnp.sum(p, axis=-1, keepdims=True) + jnp.exp(sink - m)
            o = _dot(p.astype(BF16), vmat)
            outs.append(o / denom)
    o_ref[0] = jnp.concatenate(outs, axis=1).astype(o_ref.dtype)


def _attention(za3, sinks):
    B, T, _ = za3.shape
    nb = T // WINDOW
    kcol = Q_COLS // KV_COLS
    prev = lambda b, n, s: (b, jnp.maximum(n - 1, 0), kcol)
    prev_v = lambda b, n, s: (b, jnp.maximum(n - 1, 0), kcol + 1)
    gs = pltpu.PrefetchScalarGridSpec(
        num_scalar_prefetch=1,
        grid=(B, nb),
        in_specs=[pl.BlockSpec((1, WINDOW, Q_COLS), lambda b, n, s: (b, n, 0)),
                  pl.BlockSpec((1, WINDOW, KV_COLS), lambda b, n, s: (b, n, kcol)),
                  pl.BlockSpec((1, WINDOW, KV_COLS), prev),
                  pl.BlockSpec((1, WINDOW, KV_COLS), lambda b, n, s: (b, n, kcol + 1)),
                  pl.BlockSpec((1, WINDOW, KV_COLS), prev_v)],
        out_specs=pl.BlockSpec((1, WINDOW, Q_COLS), lambda b, n, s: (b, n, 0)),
    )
    return pl.pallas_call(
        _attn_kernel,
        grid_spec=gs,
        out_shape=jax.ShapeDtypeStruct((B, T, Q_COLS), BF16),
        compiler_params=pltpu.CompilerParams(dimension_semantics=("arbitrary", "arbitrary")),
        name="attn",
    )(sinks, za3, za3, za3, za3, za3)


def _rwkv_kernel(r_ref, k_ref, v_ref, lora_ref, w0_ref, wdu_ref, a0_ref, wau_ref, wgu_ref,
                 kk_ref, ka_ref, rk_ref, lnw_ref, lnb_ref, o_ref, s_ref):
    c = pl.program_id(1)
    C = r_ref.shape[1]
    N = RWKV_N
    G = RWKV_GROUP
    GW = G * N

    @pl.when(c == 0)
    def _():
        s_ref[...] = jnp.zeros_like(s_ref)

    lora = lora_ref[0]
    wd = lora[:, 0:DECAY_LORA]
    ad = lora[:, DECAY_LORA:DECAY_LORA + A_LORA]
    gd = lora[:, DECAY_LORA + A_LORA:LORA_COLS]
    wlin = w0_ref[...] + _dot(jnp.tanh(wd), wdu_ref[...])
    neg = -wlin
    softplus = jnp.maximum(neg, 0.0) + jnp.log(1.0 + jnp.exp(-jnp.abs(neg)))
    w = -softplus - 0.5
    logdec = -jnp.exp(w)
    a_all = 1.0 / (1.0 + jnp.exp(-(a0_ref[...] + _dot(ad, wau_ref[...]))))
    g_all = _dot(1.0 / (1.0 + jnp.exp(-gd)), wgu_ref[...])
    ti = lax.broadcasted_iota(I32, (C, C), 0)
    si = lax.broadcasted_iota(I32, (C, C), 1)
    tril = jnp.where(si <= ti, 1.0, 0.0).astype(BF16)
    ld_hi = logdec.astype(BF16)
    ld_lo = (logdec - ld_hi.astype(F32)).astype(BF16)
    cum = _dot(tril, ld_hi) + _dot(tril, ld_lo)
    e_in_all = jnp.exp(cum)
    e_ex_all = jnp.exp(cum - logdec)
    e_neg_all = jnp.exp(-cum)

    lane_head = lax.broadcasted_iota(I32, (1, GW), 1) // N
    t_row = lax.broadcasted_iota(I32, (C, GW), 0)
    s_lane = lax.broadcasted_iota(I32, (C, GW), 1) % N
    strict = s_lane < t_row
    incl = s_lane <= t_row
    bi = lax.broadcasted_iota(I32, (GW, GW), 0)
    bj = lax.broadcasted_iota(I32, (GW, GW), 1)
    same_head = (bi // N) == (bj // N)
    ones_bd = jnp.where(same_head, 1.0, 0.0).astype(BF16)
    eye = jnp.where(bi == bj, 1.0, 0.0)

    def expand(xc):
        return jnp.concatenate([jnp.where(lane_head == h, xc, 0.0) for h in range(G)], axis=0)

    def head_sum(xc):
        hi = xc.astype(BF16)
        lo = (xc - hi.astype(F32)).astype(BF16)
        return _dot(hi, ones_bd) + _dot(lo, ones_bd)

    r_l, k_l, v_l = [], [], []
    for gi in range(RWKV_HEADS // G):
        cs = slice(gi * GW, (gi + 1) * GW)
        r_l.append(r_ref[0, :, cs])
        k_l.append(k_ref[0, :, cs])
        v_l.append(v_ref[0, :, cs])

    ng = RWKV_HEADS // G
    rng = range(ng)
    col = [slice(gi * GW, (gi + 1) * GW) for gi in rng]
    kk0 = [k_l[i] * kk_ref[:, col[i]] for i in rng]
    nrm2 = [head_sum(kk0[i] * kk0[i]) for i in rng]
    kk = [kk0[i] / jnp.maximum(jnp.sqrt(nrm2[i]), 1e-12) for i in rng]
    k2 = [k_l[i] * (1.0 + (a_all[:, col[i]] - 1.0) * ka_ref[:, col[i]]) for i in rng]
    bt = [kk[i] * a_all[:, col[i]] * e_neg_all[:, col[i]] for i in rng]
    kt = [k2[i] * e_neg_all[:, col[i]] for i in rng]
    left = [jnp.concatenate([-kk[i] * e_ex_all[:, col[i]], r_l[i] * e_in_all[:, col[i]]], axis=0).astype(BF16)
            for i in rng]
    right = [jnp.concatenate([bt[i], kt[i]], axis=0).astype(BF16) for i in rng]
    zed = [jnp.concatenate([expand(bt[i]), expand(kt[i])], axis=0).astype(BF16) for i in rng]
    big = [_dot_nt(left[i], zed[i]) for i in rng]
    s_old = [s_ref[i] for i in rng]
    ls = [_dot_nt(left[i], s_old[i].astype(BF16)) for i in rng]
    v_bd = [expand(v_l[i]).astype(BF16) for i in rng]
    pw = [expand(jnp.where(strict, big[i][:C, :G * C], 0.0)) for i in rng]
    tinv = [eye + pw[i] for i in rng]
    x = [ls[i][:C] + _dot(jnp.where(strict, big[i][:C, G * C:], 0.0).astype(BF16), v_bd[i]) for i in rng]
    span = 2
    while span < C:
        pwb = [pw[i].astype(BF16) for i in rng]
        pw = [_dot(pwb[i], pwb[i]) for i in rng]
        tinv = [tinv[i] + _dot(tinv[i].astype(BF16), pw[i].astype(BF16)) for i in rng]
        span *= 2
    u_bd = [_dot(tinv[i].astype(BF16), expand(x[i]).astype(BF16)) for i in rng]
    a_r = [jnp.concatenate([jnp.where(incl, big[i][C:, :G * C], 0.0),
                            jnp.where(incl, big[i][C:, G * C:], 0.0)], axis=1).astype(BF16) for i in rng]
    y = [ls[i][C:] + _dot(a_r[i], jnp.concatenate([u_bd[i].astype(BF16), v_bd[i]], axis=0)) for i in rng]
    u = [sum(u_bd[i][h * C:(h + 1) * C] for h in range(1, G)) + u_bd[i][0:C] for i in rng]
    uv = [jnp.concatenate([u[i], v_l[i]], axis=0).astype(BF16) for i in rng]
    for i in rng:
        s_ref[i] = ((s_old[i] + jnp.where(same_head, _dot_tn(uv[i], right[i]), 0.0))
                    * e_in_all[C - 1:C, col[i]])
    mean = [head_sum(y[i]) * (1.0 / N) for i in rng]
    yc = [y[i] - mean[i] for i in rng]
    var = [head_sum(yc[i] * yc[i]) * (1.0 / N) for i in rng]
    bonus = [head_sum(r_l[i] * k2[i] * rk_ref[:, col[i]]) * v_l[i] for i in rng]
    for i in rng:
        yn = yc[i] * lax.rsqrt(var[i] + RWKV_LN_EPS) * lnw_ref[:, col[i]] + lnb_ref[:, col[i]]
        o_ref[0, :, col[i]] = ((yn + bonus[i]) * g_all[:, col[i]]).astype(o_ref.dtype)


def _rwkv(zr3, w0, wdu, a0, wau, wgu, k_k, k_a, r_k, ln_w, ln_b):
    B, T, _ = zr3.shape
    C = CHUNK
    W = RWKV_W
    GW = RWKV_GROUP * RWKV_N
    vec = lambda v: v.reshape(1, W)
    full = lambda shape: pl.BlockSpec(shape, lambda b, c: (0,) * len(shape))
    return pl.pallas_call(
        _rwkv_kernel,
        grid=(B, T // C),
        in_specs=[pl.BlockSpec((1, C, W), lambda b, c: (b, c, 0)),
                  pl.BlockSpec((1, C, W), lambda b, c: (b, c, 1)),
                  pl.BlockSpec((1, C, W), lambda b, c: (b, c, 2)),
                  pl.BlockSpec((1, C, LORA_PAD), lambda b, c: (b, c, 3 * W // LORA_PAD)),
                  full((1, W)), full((DECAY_LORA, W)), full((1, W)), full((A_LORA, W)),
                  full((GATE_LORA, W)), full((1, W)), full((1, W)), full((1, W)),
                  full((1, W)), full((1, W))],
        out_specs=pl.BlockSpec((1, C, W), lambda b, c: (b, c, 0)),
        out_shape=jax.ShapeDtypeStruct((B, T, W), BF16),
        scratch_shapes=[pltpu.VMEM((RWKV_HEADS // RWKV_GROUP, GW, GW), F32)],
        compiler_params=pltpu.CompilerParams(dimension_semantics=("arbitrary", "arbitrary")),
        name="rwkv",
    )(zr3, zr3, zr3, zr3, vec(w0), wdu, vec(a0), wau, wgu, vec(k_k), vec(k_a), vec(r_k),
      vec(ln_w), vec(ln_b))


def _outproj_kernel(oa_ref, or_ref, x_ref, wo_ref, gt_ref, g_ref, sh_ref, sc_ref,
                    wrh_ref, wrl_ref, br_ref, x1_ref, h2_ref, lg_ref):
    mixed = _dot(oa_ref[...], wo_ref[0:Q_COLS, :]) + _dot(or_ref[...], wo_ref[Q_COLS:, :])
    x1 = x_ref[...] + gt_ref[0] * mixed
    x1_ref[...] = x1
    ms = jnp.mean(x1 * x1, axis=-1, keepdims=True)
    h2 = x1 * lax.rsqrt(ms + NORM_EPS) * g_ref[...] * (1.0 + sc_ref[0]) + sh_ref[0]
    h2_ref[...] = h2
    hh = h2.astype(BF16)
    hl = (h2 - hh.astype(F32)).astype(BF16)
    lg_ref[...] = (_dot(hh, wrh_ref[...]) + _dot(hl, wrh_ref[...]) + _dot(hh, wrl_ref[...])
                   + br_ref[...])


def _outproj(oa2, or2, x2, wo_bf, gt, g2, sh, sc, wr_hi, wr_lo, br, T):
    M, D = x2.shape
    tm = 256
    tps = T // tm
    bvec = pl.BlockSpec((1, 1, D), lambda i: (i // tps, 0, 0))
    vm = D * D * 2 * 2 + 2 * tm * D * (2 + 4 + 4 + 4) + tm * D * 16 + (8 << 20)
    return pl.pallas_call(
        _outproj_kernel,
        grid=(M // tm,),
        in_specs=[pl.BlockSpec((tm, Q_COLS), lambda i: (i, 0)),
                  pl.BlockSpec((tm, RWKV_W), lambda i: (i, 0)),
                  pl.BlockSpec((tm, D), lambda i: (i, 0)),
                  pl.BlockSpec((D, D), lambda i: (0, 0)),
                  bvec,
                  pl.BlockSpec((1, D), lambda i: (0, 0)),
                  bvec, bvec,
                  pl.BlockSpec((D, LANES), lambda i: (0, 0)),
                  pl.BlockSpec((D, LANES), lambda i: (0, 0)),
                  pl.BlockSpec((1, LANES), lambda i: (0, 0))],
        out_specs=[pl.BlockSpec((tm, D), lambda i: (i, 0)),
                   pl.BlockSpec((tm, D), lambda i: (i, 0)),
                   pl.BlockSpec((tm, LANES), lambda i: (i, 0))],
        out_shape=[jax.ShapeDtypeStruct((M, D), F32),
                   jax.ShapeDtypeStruct((M, D), F32),
                   jax.ShapeDtypeStruct((M, LANES), F32)],
        compiler_params=pltpu.CompilerParams(
            dimension_semantics=("arbitrary",), vmem_limit_bytes=_vmem_limit(vm)),
        name="outproj",
    )(oa2, or2, x2, wo_bf, gt, g2, sh, sc, wr_hi, wr_lo, br)


def _route_kernel(lg_ref, info_ref, cnt_ref, run_ref):
    i = pl.program_id(0)
    tm = lg_ref.shape[0]

    @pl.when(i == 0)
    def _():
        run_ref[...] = jnp.zeros_like(run_ref)

    lg = lg_ref[...]
    lane = lax.broadcasted_iota(I32, lg.shape, 1)
    gl = jnp.where(lane < N_GROUPS, lg, NEG_BIG)
    gmax = jnp.max(gl, axis=-1, keepdims=True)
    gsum = jnp.sum(jnp.exp(gl - gmax), axis=-1, keepdims=True)
    g_gate = 1.0 / gsum
    g_idx = jnp.min(jnp.where(gl == gmax, lane, LANES), axis=-1, keepdims=True)
    lo = N_GROUPS + EXPERTS_PER_GROUP * g_idx
    el = jnp.where((lane >= lo) & (lane < lo + EXPERTS_PER_GROUP), lg, NEG_BIG)
    e1max = jnp.max(el, axis=-1, keepdims=True)
    l1 = jnp.min(jnp.where(el == e1max, lane, LANES), axis=-1, keepdims=True)
    el2 = jnp.where(lane == l1, NEG_BIG, el)
    e2max = jnp.max(el2, axis=-1, keepdims=True)
    l2 = jnp.min(jnp.where(el2 == e2max, lane, LANES), axis=-1, keepdims=True)
    t2 = jnp.exp(e2max - e1max)
    w1 = g_gate / (1.0 + t2)
    w2 = g_gate * t2 / (1.0 + t2)
    ex1 = l1 - N_GROUPS
    ex2 = l2 - N_GROUPS

    oh1 = jnp.where(lane == ex1, 1.0, 0.0)
    oh2 = jnp.where(lane == ex2, 1.0, 0.0)
    ti = lax.broadcasted_iota(I32, (tm, tm), 0)
    si = lax.broadcasted_iota(I32, (tm, tm), 1)
    lower = jnp.where(si < ti, 1.0, 0.0).astype(BF16)
    pre1 = _dot(lower, oh1.astype(BF16))
    pre2 = _dot(lower, oh2.astype(BF16))
    cnt1 = jnp.sum(oh1, axis=0, keepdims=True)
    cnt2 = jnp.sum(oh2, axis=0, keepdims=True)
    run = run_ref[...]
    rank1 = jnp.sum(oh1 * (pre1 + run), axis=-1, keepdims=True)
    rank2 = jnp.sum(oh2 * (pre2 + run + cnt1), axis=-1, keepdims=True)
    run = run + cnt1 + cnt2
    run_ref[...] = run
    cnt_ref[...] = run

    info = jnp.where(lane == 0, ex1.astype(F32), 0.0)
    info = jnp.where(lane == 1, ex2.astype(F32), info)
    info = jnp.where(lane == 2, rank1, info)
    info = jnp.where(lane == 3, rank2, info)
    info = jnp.where(lane == 4, w1, info)
    info = jnp.where(lane == 5, w2, info)
    info_ref[...] = info


def _route(lg):
    M = lg.shape[0]
    tm = 256
    return pl.pallas_call(
        _route_kernel,
        grid=(M // tm,),
        in_specs=[pl.BlockSpec((tm, LANES), lambda i: (i, 0))],
        out_specs=[pl.BlockSpec((tm, LANES), lambda i: (i, 0)),
                   pl.BlockSpec((1, LANES), lambda i: (0, 0))],
        out_shape=[jax.ShapeDtypeStruct((M, LANES), F32),
                   jax.ShapeDtypeStruct((1, LANES), F32)],
        scratch_shapes=[pltpu.VMEM((1, LANES), F32)],
        compiler_params=pltpu.CompilerParams(dimension_semantics=("arbitrary",)),
        name="route",
    )(lg)


def _plan_kernel(info_ref, cnt_ref, slot_ref, meta_ref):
    cnt = cnt_ref[...]
    lane_r = lax.broadcasted_iota(I32, (1, LANES), 1)
    nblk = jnp.floor((cnt + (MOE_BLOCK - 1)) * (1.0 / MOE_BLOCK))
    ei = lax.broadcasted_iota(I32, (LANES, LANES), 0)
    ej = lax.broadcasted_iota(I32, (LANES, LANES), 1)
    upper = jnp.where(ei <= ej, 1.0, 0.0).astype(BF16)
    nb8 = jnp.broadcast_to(nblk, (8, LANES)).astype(BF16)
    bend = _dot(nb8, upper)[0:1, :]
    bstart = bend - nblk
    pstart = bstart * MOE_BLOCK

    info = info_ref[...]
    lane = lax.broadcasted_iota(I32, info.shape, 1)
    ex1 = info[:, 0:1].astype(I32)
    ex2 = info[:, 1:2].astype(I32)
    s1 = jnp.sum(jnp.where(lane == ex1, pstart, 0.0), axis=-1, keepdims=True) + info[:, 2:3]
    s2 = jnp.sum(jnp.where(lane == ex2, pstart, 0.0), axis=-1, keepdims=True) + info[:, 3:4]
    slot = jnp.where(lane == 0, s1, 0.0)
    slot = jnp.where(lane == 1, s2, slot)
    slot_ref[...] = slot.astype(I32)

    blk = lax.broadcasted_iota(I32, (LANES, 2 * LANES), 1).astype(F32)
    bend_col = jnp.sum(jnp.where(ei == ej, jnp.broadcast_to(bend, (LANES, LANES)), 0.0),
                       axis=-1, keepdims=True)
    erow = lax.broadcasted_iota(I32, (LANES, 2 * LANES), 0)
    hit = jnp.where((bend_col <= blk) & (erow < N_EXPERTS), 1.0, 0.0)
    bexp = jnp.minimum(jnp.sum(hit, axis=0, keepdims=True), N_EXPERTS - 1.0)
    n_used = jnp.max(jnp.where(lane_r < N_EXPERTS, bend, 0.0), axis=-1, keepdims=True)
    lastblk = jnp.where(nblk > 0, bend - 1.0, -1.0)
    r8 = lax.broadcasted_iota(I32, (8, 2 * LANES), 0)
    last2 = jnp.concatenate([lastblk, jnp.full((1, LANES), -1.0)], axis=1)
    meta = jnp.where(r8 == 0, jnp.broadcast_to(bexp, (8, 2 * LANES)), 0.0)
    meta = jnp.where(r8 == 1, jnp.broadcast_to(n_used, (8, 2 * LANES)), meta)
    meta = jnp.where(r8 == 2, jnp.broadcast_to(last2, (8, 2 * LANES)), meta)
    meta_ref[...] = meta.astype(I32)


def _plan(info, cnt):
    M = info.shape[0]
    tm = 256
    return pl.pallas_call(
        _plan_kernel,
        grid=(M // tm,),
        in_specs=[pl.BlockSpec((tm, LANES), lambda i: (i, 0)),
                  pl.BlockSpec((1, LANES), lambda i: (0, 0))],
        out_specs=[pl.BlockSpec((tm, LANES), lambda i: (i, 0)),
                   pl.BlockSpec((8, 2 * LANES), lambda i: (0, 0))],
        out_shape=[jax.ShapeDtypeStruct((M, LANES), I32),
                   jax.ShapeDtypeStruct((8, 2 * LANES), I32)],
        compiler_params=pltpu.CompilerParams(dimension_semantics=("arbitrary",)),
        name="plan",
    )(info, cnt)


def _dispatch_kernel(slot_ref, last_ref, h2_ref, xb_ref, zero_ref, sem_z, sem_r, *, tm, n_blocks):
    i = pl.program_id(0)

    def zero_block(blk):
        return pltpu.make_async_copy(zero_ref, xb_ref.at[pl.ds(blk * MOE_BLOCK, MOE_BLOCK)], sem_z)

    @pl.when(i == 0)
    def _():
        zero_ref[...] = jnp.zeros_like(zero_ref)
        n_used = last_ref[N_EXPERTS]

        def start(e, c):
            @pl.when(last_ref[e] >= 0)
            def _():
                zero_block(last_ref[e]).start()
            return c

        def wait(e, c):
            @pl.when(last_ref[e] >= 0)
            def _():
                zero_block(last_ref[e]).wait()
            return c

        def start_tail(b, c):
            zero_block(b).start()
            return c

        def wait_tail(b, c):
            zero_block(b).wait()
            return c

        lax.fori_loop(0, N_EXPERTS, start, 0)
        lax.fori_loop(n_used, n_blocks, start_tail, 0)
        lax.fori_loop(0, N_EXPERTS, wait, 0)
        lax.fori_loop(n_used, n_blocks, wait_tail, 0)

    def row_copy(t, k):
        tok = i * tm + t
        return pltpu.make_async_copy(h2_ref.at[pl.ds(t, 1)], xb_ref.at[pl.ds(slot_ref[2 * tok + k], 1)], sem_r)

    def start_rows(t, c):
        row_copy(t, 0).start()
        row_copy(t, 1).start()
        return c

    def wait_rows(t, c):
        row_copy(t, 0).wait()
        row_copy(t, 1).wait()
        return c

    lax.fori_loop(0, tm, start_rows, 0)
    lax.fori_loop(0, tm, wait_rows, 0)


def _dispatch(slots_flat, last_blk, h2, n_rows):
    M, D = h2.shape
    tm = 256
    gs = pltpu.PrefetchScalarGridSpec(
        num_scalar_prefetch=2,
        grid=(M // tm,),
        in_specs=[pl.BlockSpec((tm, D), lambda i, s, l: (i, 0))],
        out_specs=pl.BlockSpec(memory_space=pl.ANY),
        scratch_shapes=[pltpu.VMEM((MOE_BLOCK, D), F32),
                        pltpu.SemaphoreType.DMA(()), pltpu.SemaphoreType.DMA(())],
    )
    return pl.pallas_call(
        functools.partial(_dispatch_kernel, tm=tm, n_blocks=n_rows // MOE_BLOCK),
        grid_spec=gs,
        out_shape=jax.ShapeDtypeStruct((n_rows, D), F32),
        compiler_params=pltpu.CompilerParams(dimension_semantics=("arbitrary",),
                                             has_side_effects=True),
        name="dispatch",
    )(slots_flat, last_blk, h2)


def _experts_kernel(bexp_ref, nused_ref, xb_ref, w1_ref, w3_ref, w2_ref, y_ref):
    i = pl.program_id(0)

    @pl.when(i < nused_ref[0])
    def _():
        x = xb_ref[...].astype(BF16)
        h1 = _dot(x, w1_ref[0])
        h3 = _dot(x, w3_ref[0])
        hid = h1 * (1.0 / (1.0 + jnp.exp(-h1))) * h3
        y_ref[...] = _dot(hid.astype(BF16), w2_ref[0])

    @pl.when(i >= nused_ref[0])
    def _():
        y_ref[...] = jnp.zeros_like(y_ref)


def _experts(bexp, nused, xb, w1, w3, w2, n_blocks):
    D = xb.shape[1]
    FF = w1.shape[2]
    clamp = lambda i, be, nu: (jnp.minimum(i, nu[0] - 1), 0)
    gs = pltpu.PrefetchScalarGridSpec(
        num_scalar_prefetch=2,
        grid=(n_blocks,),
        in_specs=[pl.BlockSpec((MOE_BLOCK, D), clamp),
                  pl.BlockSpec((1, D, FF), lambda i, be, nu: (be[i], 0, 0)),
                  pl.BlockSpec((1, D, FF), lambda i, be, nu: (be[i], 0, 0)),
                  pl.BlockSpec((1, FF, D), lambda i, be, nu: (be[i], 0, 0))],
        out_specs=pl.BlockSpec((MOE_BLOCK, D), lambda i, be, nu: (i, 0)),
    )
    vm = 2 * 3 * D * FF * 4 + 4 * MOE_BLOCK * D * 4 + (12 << 20)
    return pl.pallas_call(
        _experts_kernel,
        grid_spec=gs,
        out_shape=jax.ShapeDtypeStruct((n_blocks * MOE_BLOCK, D), F32),
        compiler_params=pltpu.CompilerParams(
            dimension_semantics=("arbitrary",), vmem_limit_bytes=_vmem_limit(vm)),
        name="experts",
    )(bexp, nused, xb, w1, w3, w2)


def _combine_kernel(slot_ref, x1_ref, info_ref, gt_ref, fg_ref, y_ref, o_ref, buf_ref, sem_ref, *, tm):
    i = pl.program_id(0)
    n = pl.num_programs(0)

    def row_copy(step, t, k):
        par = step % 2
        tok = step * tm + t
        return pltpu.make_async_copy(y_ref.at[pl.ds(slot_ref[2 * tok + k], 1)],
                                     buf_ref.at[par, k, pl.ds(t, 1)], sem_ref.at[par])

    def start_tile(step):
        def go(t, c):
            row_copy(step, t, 0).start()
            row_copy(step, t, 1).start()
            return c
        lax.fori_loop(0, tm, go, 0)

    @pl.when(i == 0)
    def _():
        start_tile(0)

    @pl.when(i + 1 < n)
    def _():
        start_tile(i + 1)

    def wait(t, c):
        row_copy(i, t, 0).wait()
        row_copy(i, t, 1).wait()
        return c

    lax.fori_loop(0, tm, wait, 0)
    par = i % 2
    info = info_ref[...]
    moe = info[:, 4:5] * buf_ref[par, 0] + info[:, 5:6] * buf_ref[par, 1]
    x = x1_ref[...] + gt_ref[0] * moe
    ms = jnp.mean(x * x, axis=-1, keepdims=True)
    o_ref[...] = x * lax.rsqrt(ms + NORM_EPS) * fg_ref[...]


def _combine(slots_flat, x1, info, gt, final_g, y, T):
    M, D = x1.shape
    tm = 128
    tps = T // tm
    gs = pltpu.PrefetchScalarGridSpec(
        num_scalar_prefetch=1,
        grid=(M // tm,),
        in_specs=[pl.BlockSpec((tm, D), lambda i, s: (i, 0)),
                  pl.BlockSpec((tm, LANES), lambda i, s: (i, 0)),
                  pl.BlockSpec((1, 1, D), lambda i, s: (i // tps, 0, 0)),
                  pl.BlockSpec((1, D), lambda i, s: (0, 0)),
                  pl.BlockSpec(memory_space=pl.ANY)],
        out_specs=pl.BlockSpec((tm, D), lambda i, s: (i, 0)),
        scratch_shapes=[pltpu.VMEM((2, 2, tm, D), F32), pltpu.SemaphoreType.DMA((2,))],
    )
    vm = 4 * tm * D * 4 + 4 * tm * D * 4 + tm * D * 16 + (8 << 20)
    return pl.pallas_call(
        functools.partial(_combine_kernel, tm=tm),
        grid_spec=gs,
        out_shape=jax.ShapeDtypeStruct((M, D), F32),
        compiler_params=pltpu.CompilerParams(
            dimension_semantics=("arbitrary",), vmem_limit_bytes=_vmem_limit(vm)),
        name="combine",
    )(slots_flat, x1, info, gt, final_g.reshape(1, D), y)


def _rope_tables(T):
    half = HEAD_DIM // 2
    inv_freq = ROPE_THETA ** (-jnp.arange(0, HEAD_DIM, 2, dtype=F32) / HEAD_DIM)
    ang = jnp.arange(T, dtype=F32)[:, None] * inv_freq[None, :]
    cos, sin = jnp.cos(ang), jnp.sin(ang)
    cos_h = jnp.concatenate([cos, cos], axis=1)
    sin_h = jnp.concatenate([-sin, sin], axis=1)
    cos_q = jnp.tile(cos_h, (1, 4))
    sin_q = jnp.tile(sin_h, (1, 4))
    cos_kv = jnp.concatenate([jnp.tile(cos_h, (1, 2)), jnp.ones((T, KV_COLS), F32)], axis=1)
    sin_kv = jnp.concatenate([jnp.tile(sin_h, (1, 2)), jnp.zeros((T, KV_COLS), F32)], axis=1)
    return jnp.concatenate([cos_q, cos_kv], axis=1), jnp.concatenate([sin_q, sin_kv], axis=1)


def _layer(x, c, w_ada, b_ada, norm1_g, w_in, mu_shift, sinks, w0, w_decay_up, a0, w_a_up, w_g_up,
           k_k, k_a, r_k, ln_x_w, ln_x_b, w_o, norm2_g, w_rg, b_rg, w_re, b_re, w1, w3, w2, final_g,
           apply_final):
    B, T, D = x.shape
    M = B * T
    mod = _ada(c, w_ada, b_ada)
    sh1, sc1, gt1, sh2, sc2, gt2 = [m.reshape(B, 1, D) for m in jnp.split(mod, 6, axis=-1)]

    pad = RWKV_PAD - RWKV_COLS
    w_bf = jnp.pad(w_in, ((0, 0), (0, pad))).astype(BF16)
    mu_pad = jnp.pad(mu_shift, (0, pad)).reshape(1, RWKV_PAD)
    cos_t, sin_t = _rope_tables(T)
    x2 = x.reshape(M, D)
    za, zr = _inproj(x2, norm1_g.reshape(1, D), sh1, sc1, w_bf, mu_pad, cos_t, sin_t, T)

    o_attn = _attention(za.reshape(B, T, ATTN_COLS), sinks)
    o_rwkv = _rwkv(zr.reshape(B, T, RWKV_PAD), w0, w_decay_up, a0, w_a_up, w_g_up, k_k, k_a,
                   r_k.reshape(-1), ln_x_w, ln_x_b)

    wr = jnp.concatenate([w_rg, w_re], axis=1)
    wr = jnp.pad(wr, ((0, 0), (0, LANES - wr.shape[1])))
    wr_hi = wr.astype(BF16)
    wr_lo = (wr - wr_hi.astype(F32)).astype(BF16)
    br = jnp.pad(jnp.concatenate([b_rg, b_re]), (0, LANES - N_GROUPS - N_EXPERTS)).reshape(1, LANES)
    x1, h2, lg = _outproj(o_attn.reshape(M, Q_COLS), o_rwkv.reshape(M, RWKV_W), x2, w_o.astype(BF16),
                          gt1, norm2_g.reshape(1, D), sh2, sc2, wr_hi, wr_lo, br, T)

    n_blocks = -(-(2 * M) // MOE_BLOCK) + N_EXPERTS
    info, cnt = _route(lg)
    slots, meta = _plan(info, cnt)
    slots_flat = slots[:, :2].reshape(-1)
    seg_meta = jnp.concatenate([meta[2, :N_EXPERTS], meta[1, :1]])
    xb = _dispatch(slots_flat, seg_meta, h2, n_blocks * MOE_BLOCK)
    y = _experts(meta[0, :n_blocks], meta[1, :1], xb, w1, w3, w2, n_blocks)
    out = _combine(slots_flat, x1, info, gt2, final_g, y, T)
    del apply_final
    return out.reshape(B, T, D)


def kernel(x, c, w_ada, b_ada, norm1_g, w_in, mu_shift, sinks, w0, w_decay_up, a0, w_a_up, w_g_up, k_k, k_a, r_k, ln_x_w, ln_x_b, w_o, norm2_g, w_router_group, b_router_group, w_router_expert, b_router_expert, w1, w3, w2, final_g):
    depth = w_ada.shape[0]
    assert depth == 1, "single-layer stack"
    l = 0
    return _layer(x, c, w_ada[l], b_ada[l], norm1_g[l], w_in[l], mu_shift[l], sinks[l], w0[l],
                  w_decay_up[l], a0[l], w_a_up[l], w_g_up[l], k_k[l], k_a[l], r_k[l], ln_x_w[l],
                  ln_x_b[l], w_o[l], norm2_g[l], w_router_group[l], b_router_group[l],
                  w_router_expert[l], b_router_expert[l], w1[l], w3[l], w2[l], final_g, True)
```

```python
import functools
import math

import jax
import jax.numpy as jnp
from jax import lax
from jax.experimental import pallas as pl
from jax.experimental.pallas import tpu as pltpu

F32 = jnp.float32
BF16 = jnp.bfloat16
I32 = jnp.int32

LANES = 128
VMEM_BYTES_V7X = 64 * 1024 * 1024

HEAD_DIM = 64
ATTN_HEADS = 16
ATTN_KV_HEADS = 2
ATTN_GROUP = ATTN_HEADS // ATTN_KV_HEADS
WINDOW = 128
ROPE_THETA = 10000.0
RWKV_HEADS = 16
RWKV_N = 64
DECAY_LORA = 64
A_LORA = 64
GATE_LORA = 160
RWKV_LN_EPS = 64e-5
N_GROUPS = 8
EXPERTS_PER_GROUP = 8
N_EXPERTS = N_GROUPS * EXPERTS_PER_GROUP
EXPERT_FF = 512
MOE_BLOCK = 128
NORM_EPS = 1e-6

Q_COLS = ATTN_HEADS * HEAD_DIM
KV_COLS = ATTN_KV_HEADS * HEAD_DIM
ATTN_COLS = Q_COLS + 2 * KV_COLS
RWKV_W = RWKV_HEADS * RWKV_N
LORA_COLS = DECAY_LORA + A_LORA + GATE_LORA
LORA_PAD = 384
RWKV_COLS = 3 * RWKV_W + LORA_COLS
RWKV_PAD = 3 * RWKV_W + LORA_PAD
CHUNK = 64
RWKV_GROUP = 4
NEG_BIG = -1e30


def _vmem_limit(nbytes):
    return int(min(nbytes, VMEM_BYTES_V7X - 4 * 1024 * 1024))


def _dot(a, b):
    return jnp.dot(a, b, preferred_element_type=F32)


def _dot_nt(a, b):
    return lax.dot_general(a, b, (((1,), (1,)), ((), ())), preferred_element_type=F32)


def _dot_tn(a, b):
    return lax.dot_general(a, b, (((0,), (0,)), ((), ())), preferred_element_type=F32)


def _ada_kernel(c_ref, w_ref, b_ref, o_ref):
    c = c_ref[...]
    s = c * (1.0 / (1.0 + jnp.exp(-c)))
    o_ref[...] = _dot(s, w_ref[...]) + b_ref[...]


def _ada(c, w_ada, b_ada):
    B, D = c.shape
    N = w_ada.shape[1]
    tn = 1024
    cp = jnp.zeros((8, D), F32).at[:B].set(c)
    out = pl.pallas_call(
        _ada_kernel,
        grid=(N // tn,),
        in_specs=[pl.BlockSpec((8, D), lambda j: (0, 0)),
                  pl.BlockSpec((D, tn), lambda j: (0, j)),
                  pl.BlockSpec((1, tn), lambda j: (0, j))],
        out_specs=pl.BlockSpec((8, tn), lambda j: (0, j)),
        out_shape=jax.ShapeDtypeStruct((8, N), F32),
        compiler_params=pltpu.CompilerParams(
            dimension_semantics=("arbitrary",),
            vmem_limit_bytes=_vmem_limit(2 * D * tn * 4 + (8 << 20))),
        name="ada",
    )(cp, w_ada, b_ada.reshape(1, N))
    return out[:B]


def _rope(z, cos, sin):
    w = z.shape[-1]
    lane = lax.broadcasted_iota(I32, z.shape, 1)
    first_half = (lane % HEAD_DIM) < (HEAD_DIM // 2)
    partner = jnp.where(first_half, pltpu.roll(z, w - HEAD_DIM // 2, 1), pltpu.roll(z, HEAD_DIM // 2, 1))
    return z * cos + partner * sin


def _inproj_kernel(x_ref, g_ref, sh_ref, sc_ref, w_ref, mu_ref, cos_ref, sin_ref,
                   za_ref, zr_ref, carry_ref, *, tiles_per_seq):
    i = pl.program_id(0)
    tm = x_ref.shape[0]
    x = x_ref[...]
    ms = jnp.mean(x * x, axis=-1, keepdims=True)
    y = x * lax.rsqrt(ms + NORM_EPS) * g_ref[...]
    h = (y * (1.0 + sc_ref[0]) + sh_ref[0]).astype(BF16)

    for c0 in range(0, ATTN_COLS, 256):
        z = _dot(h, w_ref[:, c0:c0 + 256])
        t0 = 0 if c0 < Q_COLS else 256
        z = _rope(z, cos_ref[:, t0:t0 + 256], sin_ref[:, t0:t0 + 256])
        za_ref[:, c0:c0 + 256] = z.astype(za_ref.dtype)

    first = (i % tiles_per_seq) == 0
    row = lax.broadcasted_iota(I32, (tm, 1), 0)
    c0 = 0
    while c0 < RWKV_PAD:
        n = min(512, RWKV_PAD - c0)
        z = _dot(h, w_ref[:, ATTN_COLS + c0:ATTN_COLS + c0 + n])
        prev_last = jnp.where(first, 0.0, carry_ref[7:8, c0:c0 + n])
        z_prev = jnp.where(row == 0, prev_last, pltpu.roll(z, 1, 0))
        carry_ref[:, c0:c0 + n] = z[tm - 8:tm, :]
        zr_ref[:, c0:c0 + n] = z + (z_prev - z) * mu_ref[:, c0:c0 + n]
        c0 += n


def _inproj(x2, g, sh, sc, w_bf, mu_pad, cos_t, sin_t, T):
    M, D = x2.shape
    tm = 256
    tps = T // tm
    NW = w_bf.shape[1]
    kern = functools.partial(_inproj_kernel, tiles_per_seq=tps)
    vm = (D * NW * 2 + 2 * tm * D * 4 + 2 * tm * ATTN_COLS * 2 + 2 * tm * RWKV_PAD * 4
          + 4 * tm * 512 * 4 + tm * D * 8 + (8 << 20))
    return pl.pallas_call(
        kern,
        grid=(M // tm,),
        in_specs=[pl.BlockSpec((tm, D), lambda i: (i, 0)),
                  pl.BlockSpec((1, D), lambda i: (0, 0)),
                  pl.BlockSpec((1, 1, D), lambda i: (i // tps, 0, 0)),
                  pl.BlockSpec((1, 1, D), lambda i: (i // tps, 0, 0)),
                  pl.BlockSpec((D, NW), lambda i: (0, 0), pipeline_mode=pl.Buffered(1)),
                  pl.BlockSpec((1, RWKV_PAD), lambda i: (0, 0)),
                  pl.BlockSpec((tm, 512), lambda i: (i % tps, 0)),
                  pl.BlockSpec((tm, 512), lambda i: (i % tps, 0))],
        out_specs=[pl.BlockSpec((tm, ATTN_COLS), lambda i: (i, 0)),
                   pl.BlockSpec((tm, RWKV_PAD), lambda i: (i, 0))],
        out_shape=[jax.ShapeDtypeStruct((M, ATTN_COLS), BF16),
                   jax.ShapeDtypeStruct((M, RWKV_PAD), F32)],
        scratch_shapes=[pltpu.VMEM((8, RWKV_PAD), F32)],
        compiler_params=pltpu.CompilerParams(
            dimension_semantics=("arbitrary",), vmem_limit_bytes=_vmem_limit(vm)),
        name="inproj",
    )(x2, g, sh, sc, w_bf, mu_pad, cos_t, sin_t)


def _attn_kernel(sink_ref, q_ref, kc_ref, kp_ref, vc_ref, vp_ref, o_ref):
    n = pl.program_id(1)
    blk = q_ref.shape[1]
    row = lax.broadcasted_iota(I32, (blk, blk), 0)
    col = lax.broadcasted_iota(I32, (blk, blk), 1)
    mask = jnp.concatenate([(col > row) & (n > 0), col <= row], axis=1)
    scale = 1.0 / math.sqrt(HEAD_DIM)
    outs = []
    for kvh in range(ATTN_KV_HEADS):
        ks = slice(kvh * HEAD_DIM, (kvh + 1) * HEAD_DIM)
        kmat = jnp.concatenate([kp_ref[0, :, ks], kc_ref[0, :, ks]], axis=0)
        vmat = jnp.concatenate([vp_ref[0, :, ks], vc_ref[0, :, ks]], axis=0)
        for g in range(ATTN_GROUP):
            hd = kvh * ATTN_GROUP + g
            qh = q_ref[0, :, hd * HEAD_DIM:(hd + 1) * HEAD_DIM]
            s = _dot_nt(qh, kmat) * scale
            s = jnp.where(mask, s, NEG_BIG)
            sink = sink_ref[hd]
            m = jnp.maximum(jnp.max(s, axis=-1, keepdims=True), sink)
            p = jnp.exp(s - m)
            denom = jnp.sum(p, axis=-1, keepdims=True) + jnp.exp(sink - m)
            o = _dot(p.astype(BF16), vmat)
            outs.append(o / denom)
    o_ref[0] = jnp.concatenate(outs, axis=1).astype(o_ref.dtype)


def _attention(za3, sinks):
    B, T, _ = za3.shape
    nb = T // WINDOW
    kcol = Q_COLS // KV_COLS
    prev = lambda b, n, s: (b, jnp.maximum(n - 1, 0), kcol)
    prev_v = lambda b, n, s: (b, jnp.maximum(n - 1, 0), kcol + 1)
    gs = pltpu.PrefetchScalarGridSpec(
        num_scalar_prefetch=1,
        grid=(B, nb),
        in_specs=[pl.BlockSpec((1, WINDOW, Q_COLS), lambda b, n, s: (b, n, 0)),
                  pl.BlockSpec((1, WINDOW, KV_COLS), lambda b, n, s: (b, n, kcol)),
                  pl.BlockSpec((1, WINDOW, KV_COLS), prev),
                  pl.BlockSpec((1, WINDOW, KV_COLS), lambda b, n, s: (b, n, kcol + 1)),
                  pl.BlockSpec((1, WINDOW, KV_COLS), prev_v)],
        out_specs=pl.BlockSpec((1, WINDOW, Q_COLS), lambda b, n, s: (b, n, 0)),
    )
    return pl.pallas_call(
        _attn_kernel,
        grid_spec=gs,
        out_shape=jax.ShapeDtypeStruct((B, T, Q_COLS), BF16),
        compiler_params=pltpu.CompilerParams(dimension_semantics=("arbitrary", "arbitrary")),
        name="attn",
    )(sinks, za3, za3, za3, za3, za3)


def _rwkv_kernel(r_ref, k_ref, v_ref, lora_ref, w0_ref, wdu_ref, a0_ref, wau_ref, wgu_ref,
                 kk_ref, ka_ref, rk_ref, lnw_ref, lnb_ref, o_ref, s_ref):
    c = pl.program_id(1)
    C = r_ref.shape[1]
    N = RWKV_N
    G = RWKV_GROUP
    GW = G * N

    @pl.when(c == 0)
    def _():
        s_ref[...] = jnp.zeros_like(s_ref)

    lora = lora_ref[0]
    wd = lora[:, 0:DECAY_LORA]
    ad = lora[:, DECAY_LORA:DECAY_LORA + A_LORA]
    gd = lora[:, DECAY_LORA + A_LORA:LORA_COLS]
    wlin = w0_ref[...] + _dot(jnp.tanh(wd), wdu_ref[...])
    neg = -wlin
    softplus = jnp.maximum(neg, 0.0) + jnp.log(1.0 + jnp.exp(-jnp.abs(neg)))
    w = -softplus - 0.5
    logdec = -jnp.exp(w)
    a_all = 1.0 / (1.0 + jnp.exp(-(a0_ref[...] + _dot(ad, wau_ref[...]))))
    g_all = _dot(1.0 / (1.0 + jnp.exp(-gd)), wgu_ref[...])
    ti = lax.broadcasted_iota(I32, (C, C), 0)
    si = lax.broadcasted_iota(I32, (C, C), 1)
    tril = jnp.where(si <= ti, 1.0, 0.0).astype(BF16)
    ld_hi = logdec.astype(BF16)
    ld_lo = (logdec - ld_hi.astype(F32)).astype(BF16)
    cum = _dot(tril, ld_hi) + _dot(tril, ld_lo)
    e_in_all = jnp.exp(cum)
    e_ex_all = jnp.exp(cum - logdec)
    e_neg_all = jnp.exp(-cum)

    lane_head = lax.broadcasted_iota(I32, (1, GW), 1) // N
    t_row = lax.broadcasted_iota(I32, (C, GW), 0)
    s_lane = lax.broadcasted_iota(I32, (C, GW), 1) % N
    strict = s_lane < t_row
    incl = s_lane <= t_row
    bi = lax.broadcasted_iota(I32, (GW, GW), 0)
    bj = lax.broadcasted_iota(I32, (GW, GW), 1)
    same_head = (bi // N) == (bj // N)
    ones_bd = jnp.where(same_head, 1.0, 0.0).astype(BF16)
    eye = jnp.where(bi == bj, 1.0, 0.0)

    def expand(xc):
        return jnp.concatenate([jnp.where(lane_head == h, xc, 0.0) for h in range(G)], axis=0)

    def head_sum(xc):
        hi = xc.astype(BF16)
        lo = (xc - hi.astype(F32)).astype(BF16)
        return _dot(hi, ones_bd) + _dot(lo, ones_bd)

    r_l, k_l, v_l = [], [], []
    for gi in range(RWKV_HEADS // G):
        cs = slice(gi * GW, (gi + 1) * GW)
        r_l.append(r_ref[0, :, cs])
        k_l.append(k_ref[0, :, cs])
        v_l.append(v_ref[0, :, cs])

    ng = RWKV_HEADS // G
    rng = range(ng)
    col = [slice(gi * GW, (gi + 1) * GW) for gi in rng]
    kk0 = [k_l[i] * kk_ref[:, col[i]] for i in rng]
    nrm2 = [head_sum(kk0[i] * kk0[i]) for i in rng]
    kk = [kk0[i] / jnp.maximum(jnp.sqrt(nrm2[i]), 1e-12) for i in rng]
    k2 = [k_l[i] * (1.0 + (a_all[:, col[i]] - 1.0) * ka_ref[:, col[i]]) for i in rng]
    bt = [kk[i] * a_all[:, col[i]] * e_neg_all[:, col[i]] for i in rng]
    kt = [k2[i] * e_neg_all[:, col[i]] for i in rng]
    left = [jnp.concatenate([-kk[i] * e_ex_all[:, col[i]], r_l[i] * e_in_all[:, col[i]]], axis=0).astype(BF16)
            for i in rng]
    right = [jnp.concatenate([bt[i], kt[i]], axis=0).astype(BF16) for i in rng]
    zed = [jnp.concatenate([expand(bt[i]), expand(kt[i])], axis=0).astype(BF16) for i in rng]
    big = [_dot_nt(left[i], zed[i]) for i in rng]
    s_old = [s_ref[i] for i in rng]
    ls = [_dot_nt(left[i], s_old[i].astype(BF16)) for i in rng]
    v_bd = [expand(v_l[i]).astype(BF16) for i in rng]
    pw = [expand(jnp.where(strict, big[i][:C, :G * C], 0.0)) for i in rng]
    tinv = [eye + pw[i] for i in rng]
    x = [ls[i][:C] + _dot(jnp.where(strict, big[i][:C, G * C:], 0.0).astype(BF16), v_bd[i]) for i in rng]
    span = 2
    while span < C:
        pwb = [pw[i].astype(BF16) for i in rng]
        pw = [_dot(pwb[i], pwb[i]) for i in rng]
        tinv = [tinv[i] + _dot(tinv[i].astype(BF16), pw[i].astype(BF16)) for i in rng]
        span *= 2
    u_bd = [_dot(tinv[i].astype(BF16), expand(x[i]).astype(BF16)) for i in rng]
    a_r = [jnp.concatenate([jnp.where(incl, big[i][C:, :G * C], 0.0),
                            jnp.where(incl, big[i][C:, G * C:], 0.0)], axis=1).astype(BF16) for i in rng]
    y = [ls[i][C:] + _dot(a_r[i], jnp.concatenate([u_bd[i].astype(BF16), v_bd[i]], axis=0)) for i in rng]
    u = [sum(u_bd[i][h * C:(h + 1) * C] for h in range(1, G)) + u_bd[i][0:C] for i in rng]
    uv = [jnp.concatenate([u[i], v_l[i]], axis=0).astype(BF16) for i in rng]
    for i in rng:
        s_ref[i] = ((s_old[i] + jnp.where(same_head, _dot_tn(uv[i], right[i]), 0.0))
                    * e_in_all[C - 1:C, col[i]])
    mean = [head_sum(y[i]) * (1.0 / N) for i in rng]
    yc = [y[i] - mean[i] for i in rng]
    var = [head_sum(yc[i] * yc[i]) * (1.0 / N) for i in rng]
    bonus = [head_sum(r_l[i] * k2[i] * rk_ref[:, col[i]]) * v_l[i] for i in rng]
    for i in rng:
        yn = yc[i] * lax.rsqrt(var[i] + RWKV_LN_EPS) * lnw_ref[:, col[i]] + lnb_ref[:, col[i]]
        o_ref[0, :, col[i]] = ((yn + bonus[i]) * g_all[:, col[i]]).astype(o_ref.dtype)


def _rwkv(zr3, w0, wdu, a0, wau, wgu, k_k, k_a, r_k, ln_w, ln_b):
    B, T, _ = zr3.shape
    C = CHUNK
    W = RWKV_W
    GW = RWKV_GROUP * RWKV_N
    vec = lambda v: v.reshape(1, W)
    full = lambda shape: pl.BlockSpec(shape, lambda b, c: (0,) * len(shape))
    return pl.pallas_call(
        _rwkv_kernel,
        grid=(B, T // C),
        in_specs=[pl.BlockSpec((1, C, W), lambda b, c: (b, c, 0)),
                  pl.BlockSpec((1, C, W), lambda b, c: (b, c, 1)),
                  pl.BlockSpec((1, C, W), lambda b, c: (b, c, 2)),
                  pl.BlockSpec((1, C, LORA_PAD), lambda b, c: (b, c, 3 * W // LORA_PAD)),
                  full((1, W)), full((DECAY_LORA, W)), full((1, W)), full((A_LORA, W)),
                  full((GATE_LORA, W)), full((1, W)), full((1, W)), full((1, W)),
                  full((1, W)), full((1, W))],
        out_specs=pl.BlockSpec((1, C, W), lambda b, c: (b, c, 0)),
        out_shape=jax.ShapeDtypeStruct((B, T, W), BF16),
        scratch_shapes=[pltpu.VMEM((RWKV_HEADS // RWKV_GROUP, GW, GW), F32)],
        compiler_params=pltpu.CompilerParams(dimension_semantics=("arbitrary", "arbitrary")),
        name="rwkv",
    )(zr3, zr3, zr3, zr3, vec(w0), wdu, vec(a0), wau, wgu, vec(k_k), vec(k_a), vec(r_k),
      vec(ln_w), vec(ln_b))


def _outproj_kernel(oa_ref, or_ref, x_ref, wo_ref, gt_ref, g_ref, sh_ref, sc_ref,
                    wrh_ref, wrl_ref, br_ref, x1_ref, h2_ref, lg_ref):
    mixed = _dot(oa_ref[...], wo_ref[0:Q_COLS, :]) + _dot(or_ref[...], wo_ref[Q_COLS:, :])
    x1 = x_ref[...] + gt_ref[0] * mixed
    x1_ref[...] = x1
    ms = jnp.mean(x1 * x1, axis=-1, keepdims=True)
    h2 = x1 * lax.rsqrt(ms + NORM_EPS) * g_ref[...] * (1.0 + sc_ref[0]) + sh_ref[0]
    h2_ref[...] = h2
    hh = h2.astype(BF16)
    hl = (h2 - hh.astype(F32)).astype(BF16)
    lg_ref[...] = (_dot(hh, wrh_ref[...]) + _dot(hl, wrh_ref[...]) + _dot(hh, wrl_ref[...])
                   + br_ref[...])


def _outproj(oa2, or2, x2, wo_bf, gt, g2, sh, sc, wr_hi, wr_lo, br, T):
    M, D = x2.shape
    tm = 256
    tps = T // tm
    bvec = pl.BlockSpec((1, 1, D), lambda i: (i // tps, 0, 0))
    vm = D * D * 2 * 2 + 2 * tm * D * (2 + 4 + 4 + 4) + tm * D * 16 + (8 << 20)
    return pl.pallas_call(
        _outproj_kernel,
        grid=(M // tm,),
        in_specs=[pl.BlockSpec((tm, Q_COLS), lambda i: (i, 0)),
                  pl.BlockSpec((tm, RWKV_W), lambda i: (i, 0)),
                  pl.BlockSpec((tm, D), lambda i: (i, 0)),
                  pl.BlockSpec((D, D), lambda i: (0, 0)),
                  bvec,
                  pl.BlockSpec((1, D), lambda i: (0, 0)),
                  bvec, bvec,
                  pl.BlockSpec((D, LANES), lambda i: (0, 0)),
                  pl.BlockSpec((D, LANES), lambda i: (0, 0)),
                  pl.BlockSpec((1, LANES), lambda i: (0, 0))],
        out_specs=[pl.BlockSpec((tm, D), lambda i: (i, 0)),
                   pl.BlockSpec((tm, D), lambda i: (i, 0)),
                   pl.BlockSpec((tm, LANES), lambda i: (i, 0))],
        out_shape=[jax.ShapeDtypeStruct((M, D), F32),
                   jax.ShapeDtypeStruct((M, D), F32),
                   jax.ShapeDtypeStruct((M, LANES), F32)],
        compiler_params=pltpu.CompilerParams(
            dimension_semantics=("arbitrary",), vmem_limit_bytes=_vmem_limit(vm)),
        name="outproj",
    )(oa2, or2, x2, wo_bf, gt, g2, sh, sc, wr_hi, wr_lo, br)


def _route_kernel(lg_ref, info_ref, cnt_ref, run_ref):
    i = pl.program_id(0)
    tm = lg_ref.shape[0]

    @pl.when(i == 0)
    def _():
        run_ref[...] = jnp.zeros_like(run_ref)

    lg = lg_ref[...]
    lane = lax.broadcasted_iota(I32, lg.shape, 1)
    gl = jnp.where(lane < N_GROUPS, lg, NEG_BIG)
    gmax = jnp.max(gl, axis=-1, keepdims=True)
    gsum = jnp.sum(jnp.exp(gl - gmax), axis=-1, keepdims=True)
    g_gate = 1.0 / gsum
    g_idx = jnp.min(jnp.where(gl == gmax, lane, LANES), axis=-1, keepdims=True)
    lo = N_GROUPS + EXPERTS_PER_GROUP * g_idx
    el = jnp.where((lane >= lo) & (lane < lo + EXPERTS_PER_GROUP), lg, NEG_BIG)
    e1max = jnp.max(el, axis=-1, keepdims=True)
    l1 = jnp.min(jnp.where(el == e1max, lane, LANES), axis=-1, keepdims=True)
    el2 = jnp.where(lane == l1, NEG_BIG, el)
    e2max = jnp.max(el2, axis=-1, keepdims=True)
    l2 = jnp.min(jnp.where(el2 == e2max, lane, LANES), axis=-1, keepdims=True)
    t2 = jnp.exp(e2max - e1max)
    w1 = g_gate / (1.0 + t2)
    w2 = g_gate * t2 / (1.0 + t2)
    ex1 = l1 - N_GROUPS
    ex2 = l2 - N_GROUPS

    oh1 = jnp.where(lane == ex1, 1.0, 0.0)
    oh2 = jnp.where(lane == ex2, 1.0, 0.0)
    ti = lax.broadcasted_iota(I32, (tm, tm), 0)
    si = lax.broadcasted_iota(I32, (tm, tm), 1)
    lower = jnp.where(si < ti, 1.0, 0.0).astype(BF16)
    pre1 = _dot(lower, oh1.astype(BF16))
    pre2 = _dot(lower, oh2.astype(BF16))
    cnt1 = jnp.sum(oh1, axis=0, keepdims=True)
    cnt2 = jnp.sum(oh2, axis=0, keepdims=True)
    run = run_ref[...]
    rank1 = jnp.sum(oh1 * (pre1 + run), axis=-1, keepdims=True)
    rank2 = jnp.sum(oh2 * (pre2 + run + cnt1), axis=-1, keepdims=True)
    run = run + cnt1 + cnt2
    run_ref[...] = run
    cnt_ref[...] = run

    info = jnp.where(lane == 0, ex1.astype(F32), 0.0)
    info = jnp.where(lane == 1, ex2.astype(F32), info)
    info = jnp.where(lane == 2, rank1, info)
    info = jnp.where(lane == 3, rank2, info)
    info = jnp.where(lane == 4, w1, info)
    info = jnp.where(lane == 5, w2, info)
    info_ref[...] = info


def _route(lg):
    M = lg.shape[0]
    tm = 256
    return pl.pallas_call(
        _route_kernel,
        grid=(M // tm,),
        in_specs=[pl.BlockSpec((tm, LANES), lambda i: (i, 0))],
        out_specs=[pl.BlockSpec((tm, LANES), lambda i: (i, 0)),
                   pl.BlockSpec((1, LANES), lambda i: (0, 0))],
        out_shape=[jax.ShapeDtypeStruct((M, LANES), F32),
                   jax.ShapeDtypeStruct((1, LANES), F32)],
        scratch_shapes=[pltpu.VMEM((1, LANES), F32)],
        compiler_params=pltpu.CompilerParams(dimension_semantics=("arbitrary",)),
        name="route",
    )(lg)


def _plan_kernel(info_ref, cnt_ref, slot_ref, meta_ref):
    cnt = cnt_ref[...]
    lane_r = lax.broadcasted_iota(I32, (1, LANES), 1)
    nblk = jnp.floor((cnt + (MOE_BLOCK - 1)) * (1.0 / MOE_BLOCK))
    ei = lax.broadcasted_iota(I32, (LANES, LANES), 0)
    ej = lax.broadcasted_iota(I32, (LANES, LANES), 1)
    upper = jnp.where(ei <= ej, 1.0, 0.0).astype(BF16)
    nb8 = jnp.broadcast_to(nblk, (8, LANES)).astype(BF16)
    bend = _dot(nb8, upper)[0:1, :]
    bstart = bend - nblk
    pstart = bstart * MOE_BLOCK

    info = info_ref[...]
    lane = lax.broadcasted_iota(I32, info.shape, 1)
    ex1 = info[:, 0:1].astype(I32)
    ex2 = info[:, 1:2].astype(I32)
    s1 = jnp.sum(jnp.where(lane == ex1, pstart, 0.0), axis=-1, keepdims=True) + info[:, 2:3]
    s2 = jnp.sum(jnp.where(lane == ex2, pstart, 0.0), axis=-1, keepdims=True) + info[:, 3:4]
    slot = jnp.where(lane == 0, s1, 0.0)
    slot = jnp.where(lane == 1, s2, slot)
    slot_ref[...] = slot.astype(I32)

    blk = lax.broadcasted_iota(I32, (LANES, 2 * LANES), 1).astype(F32)
    bend_col = jnp.sum(jnp.where(ei == ej, jnp.broadcast_to(bend, (LANES, LANES)), 0.0),
                       axis=-1, keepdims=True)
    erow = lax.broadcasted_iota(I32, (LANES, 2 * LANES), 0)
    hit = jnp.where((bend_col <= blk) & (erow < N_EXPERTS), 1.0, 0.0)
    bexp = jnp.minimum(jnp.sum(hit, axis=0, keepdims=True), N_EXPERTS - 1.0)
    n_used = jnp.max(jnp.where(lane_r < N_EXPERTS, bend, 0.0), axis=-1, keepdims=True)
    lastblk = jnp.where(nblk > 0, bend - 1.0, -1.0)
    r8 = lax.broadcasted_iota(I32, (8, 2 * LANES), 0)
    last2 = jnp.concatenate([lastblk, jnp.full((1, LANES), -1.0)], axis=1)
    meta = jnp.where(r8 == 0, jnp.broadcast_to(bexp, (8, 2 * LANES)), 0.0)
    meta = jnp.where(r8 == 1, jnp.broadcast_to(n_used, (8, 2 * LANES)), meta)
    meta = jnp.where(r8 == 2, jnp.broadcast_to(last2, (8, 2 * LANES)), meta)
    meta_ref[...] = meta.astype(I32)


def _plan(info, cnt):
    M = info.shape[0]
    tm = 256
    return pl.pallas_call(
        _plan_kernel,
        grid=(M // tm,),
        in_specs=[pl.BlockSpec((tm, LANES), lambda i: (i, 0)),
                  pl.BlockSpec((1, LANES), lambda i: (0, 0))],
        out_specs=[pl.BlockSpec((tm, LANES), lambda i: (i, 0)),
                   pl.BlockSpec((8, 2 * LANES), lambda i: (0, 0))],
        out_shape=[jax.ShapeDtypeStruct((M, LANES), I32),
                   jax.ShapeDtypeStruct((8, 2 * LANES), I32)],
        compiler_params=pltpu.CompilerParams(dimension_semantics=("arbitrary",)),
        name="plan",
    )(info, cnt)


def _slot_table_kernel(slots_ref, inv_ref, *, n_assign, n_slots):
    def init(s, c):
        inv_ref[s] = n_assign + (s & (2 * MOE_BLOCK - 1))
        return c

    def put(a, c):
        inv_ref[slots_ref[a]] = a
        return c

    lax.fori_loop(0, n_slots, init, 0, unroll=8)
    lax.fori_loop(0, n_assign, put, 0, unroll=8)


def _slot_table(slots_km, n_slots):
    n_assign = slots_km.shape[0]
    gs = pltpu.PrefetchScalarGridSpec(
        num_scalar_prefetch=1, grid=(1,), in_specs=[],
        out_specs=pl.BlockSpec(memory_space=pltpu.SMEM))
    return pl.pallas_call(
        functools.partial(_slot_table_kernel, n_assign=n_assign, n_slots=n_slots),
        grid_spec=gs,
        out_shape=jax.ShapeDtypeStruct((n_slots,), I32),
        compiler_params=pltpu.CompilerParams(dimension_semantics=("arbitrary",)),
        name="slot_table",
    )(slots_km)


def _experts_kernel(bexp_ref, seg_ref, inv_ref, h2_ref, w1_ref, w3_ref, w2_ref, y2_ref,
                    xbuf, ybuf, w1buf, w3buf, w2buf, sem_x, sem_y, sem_w, par_ref, *, n_assign, n_blocks):
    i = pl.program_id(0)
    n_used = seg_ref[N_EXPERTS]
    n_tok = n_assign // 2
    B = MOE_BLOCK

    def src_row(a):
        if n_tok & (n_tok - 1) == 0:
            return a & (n_tok - 1)
        return lax.rem(a, n_tok)

    def weight_copies(e, slot):
        return (pltpu.make_async_copy(w1_ref.at[e], w1buf.at[slot], sem_w.at[slot]),
                pltpu.make_async_copy(w3_ref.at[e], w3buf.at[slot], sem_w.at[slot]),
                pltpu.make_async_copy(w2_ref.at[e], w2buf.at[slot], sem_w.at[slot]))

    def start_weights(e, slot):
        for cp in weight_copies(e, slot):
            cp.start()

    def prefetch_next_expert(e, slot):
        nb = seg_ref[e] + 1

        @pl.when(nb < n_used)
        def _():
            start_weights(bexp_ref[jnp.minimum(nb, n_blocks - 1)], slot)

    def start_gathers(b):
        slot = lax.rem(b, 2)
        for r in range(B):
            a = inv_ref[b * B + r]
            pltpu.make_async_copy(h2_ref.at[pl.ds(src_row(a), 1)], xbuf.at[slot, pl.ds(r, 1)],
                                  sem_x.at[slot]).start()

    def start_scatters(b):
        slot = lax.rem(b, 2)
        for r in range(B):
            a = inv_ref[b * B + r]
            pltpu.make_async_copy(ybuf.at[slot, pl.ds(r, 1)], y2_ref.at[pl.ds(a, 1)],
                                  sem_y.at[slot]).start()

    def wait_rows(buf, sem, b):
        slot = lax.rem(b, 2)
        pltpu.make_async_copy(buf.at[slot], buf.at[slot], sem.at[slot]).wait()

    @pl.when(i == 0)
    def _():
        ybuf[...] = jnp.zeros_like(ybuf)
        for s in range(2):
            cp = pltpu.make_async_copy(ybuf.at[s], y2_ref.at[pl.ds(n_assign + s * B, B)], sem_y.at[s])
            cp.start()
            cp.wait()
        par_ref[0] = 0
        e0 = bexp_ref[0]
        start_weights(e0, 0)
        prefetch_next_expert(e0, 1)
        start_gathers(0)

    active = i < n_used

    @pl.when(active)
    def _():
        e = bexp_ref[i]
        first = jnp.logical_or(i == 0, e != bexp_ref[jnp.maximum(i - 1, 0)])

        @pl.when(jnp.logical_and(first, i > 0))
        def _():
            par_ref[0] = 1 - par_ref[0]

        @pl.when(first)
        def _():
            p = par_ref[0]
            for cp in weight_copies(e, p):
                cp.wait()

            @pl.when(i > 0)
            def _():
                prefetch_next_expert(e, 1 - p)

        @pl.when(i + 1 < n_used)
        def _():
            start_gathers(i + 1)

        wait_rows(xbuf, sem_x, i)

    @pl.when(jnp.logical_and(i >= 2, i - 2 < n_used))
    def _():
        wait_rows(ybuf, sem_y, i - 2)

    @pl.when(active)
    def _():
        p = par_ref[0]
        slot = lax.rem(i, 2)
        x = xbuf[slot].astype(BF16)
        h1 = _dot(x, w1buf[p])
        h3 = _dot(x, w3buf[p])
        hid = h1 * (1.0 / (1.0 + jnp.exp(-h1))) * h3
        ybuf[slot] = _dot(hid.astype(BF16), w2buf[p])
        start_scatters(i)

    @pl.when(i == n_blocks - 1)
    def _():
        for b in (i - 1, i):
            @pl.when(jnp.logical_and(b >= 0, b < n_used))
            def _():
                wait_rows(ybuf, sem_y, b)


def _experts(bexp, seg_meta, inv, h2, w1, w3, w2, n_blocks):
    M, D = h2.shape
    FF = w1.shape[2]
    n_assign = 2 * M
    any_spec = pl.BlockSpec(memory_space=pl.ANY)
    gs = pltpu.PrefetchScalarGridSpec(
        num_scalar_prefetch=3,
        grid=(n_blocks,),
        in_specs=[any_spec, any_spec, any_spec, any_spec],
        out_specs=any_spec,
        scratch_shapes=[pltpu.VMEM((2, MOE_BLOCK, D), F32), pltpu.VMEM((2, MOE_BLOCK, D), F32),
                        pltpu.VMEM((2, D, FF), F32), pltpu.VMEM((2, D, FF), F32),
                        pltpu.VMEM((2, FF, D), F32),
                        pltpu.SemaphoreType.DMA((2,)), pltpu.SemaphoreType.DMA((2,)),
                        pltpu.SemaphoreType.DMA((2,)), pltpu.SMEM((1,), I32)],
    )
    vm = 2 * 3 * D * FF * 4 + 4 * MOE_BLOCK * D * 4 + (12 << 20)
    return pl.pallas_call(
        functools.partial(_experts_kernel, n_assign=n_assign, n_blocks=n_blocks),
        grid_spec=gs,
        out_shape=jax.ShapeDtypeStruct((n_assign + 2 * MOE_BLOCK, D), F32),
        compiler_params=pltpu.CompilerParams(
            dimension_semantics=("arbitrary",), vmem_limit_bytes=_vmem_limit(vm),
            has_side_effects=True),
        name="experts",
    )(bexp, seg_meta, inv, h2, w1, w3, w2)


def _combine_kernel(x1_ref, info_ref, gt_ref, fg_ref, ya_ref, yb_ref, o_ref):
    info = info_ref[...]
    moe = info[:, 4:5] * ya_ref[...] + info[:, 5:6] * yb_ref[...]
    x = x1_ref[...] + gt_ref[0] * moe
    ms = jnp.mean(x * x, axis=-1, keepdims=True)
    o_ref[...] = x * lax.rsqrt(ms + NORM_EPS) * fg_ref[...]


def _combine(x1, info, gt, final_g, y2, T):
    M, D = x1.shape
    tm = 256
    tps = T // tm
    nt = M // tm
    vm = 2 * 4 * tm * D * 4 + tm * D * 16 + (8 << 20)
    return pl.pallas_call(
        _combine_kernel,
        grid=(nt,),
        in_specs=[pl.BlockSpec((tm, D), lambda i: (i, 0)),
                  pl.BlockSpec((tm, LANES), lambda i: (i, 0)),
                  pl.BlockSpec((1, 1, D), lambda i: (i // tps, 0, 0)),
                  pl.BlockSpec((1, D), lambda i: (0, 0)),
                  pl.BlockSpec((tm, D), lambda i: (i, 0)),
                  pl.BlockSpec((tm, D), lambda i: (nt + i, 0))],
        out_specs=pl.BlockSpec((tm, D), lambda i: (i, 0)),
        out_shape=jax.ShapeDtypeStruct((M, D), F32),
        compiler_params=pltpu.CompilerParams(
            dimension_semantics=("arbitrary",), vmem_limit_bytes=_vmem_limit(vm)),
        name="combine",
    )(x1, info, gt, final_g.reshape(1, D), y2, y2)


def _rope_tables(T):
    half = HEAD_DIM // 2
    inv_freq = ROPE_THETA ** (-jnp.arange(0, HEAD_DIM, 2, dtype=F32) / HEAD_DIM)
    ang = jnp.arange(T, dtype=F32)[:, None] * inv_freq[None, :]
    cos, sin = jnp.cos(ang), jnp.sin(ang)
    cos_h = jnp.concatenate([cos, cos], axis=1)
    sin_h = jnp.concatenate([-sin, sin], axis=1)
    cos_q = jnp.tile(cos_h, (1, 4))
    sin_q = jnp.tile(sin_h, (1, 4))
    cos_kv = jnp.concatenate([jnp.tile(cos_h, (1, 2)), jnp.ones((T, KV_COLS), F32)], axis=1)
    sin_kv = jnp.concatenate([jnp.tile(sin_h, (1, 2)), jnp.zeros((T, KV_COLS), F32)], axis=1)
    return jnp.concatenate([cos_q, cos_kv], axis=1), jnp.concatenate([sin_q, sin_kv], axis=1)


def _layer(x, c, w_ada, b_ada, norm1_g, w_in, mu_shift, sinks, w0, w_decay_up, a0, w_a_up, w_g_up,
           k_k, k_a, r_k, ln_x_w, ln_x_b, w_o, norm2_g, w_rg, b_rg, w_re, b_re, w1, w3, w2, final_g,
           apply_final):
    B, T, D = x.shape
    M = B * T
    mod = _ada(c, w_ada, b_ada)
    sh1, sc1, gt1, sh2, sc2, gt2 = [m.reshape(B, 1, D) for m in jnp.split(mod, 6, axis=-1)]

    pad = RWKV_PAD - RWKV_COLS
    w_bf = jnp.pad(w_in, ((0, 0), (0, pad))).astype(BF16)
    mu_pad = jnp.pad(mu_shift, (0, pad)).reshape(1, RWKV_PAD)
    cos_t, sin_t = _rope_tables(T)
    x2 = x.reshape(M, D)
    za, zr = _inproj(x2, norm1_g.reshape(1, D), sh1, sc1, w_bf, mu_pad, cos_t, sin_t, T)

    o_attn = _attention(za.reshape(B, T, ATTN_COLS), sinks)
    o_rwkv = _rwkv(zr.reshape(B, T, RWKV_PAD), w0, w_decay_up, a0, w_a_up, w_g_up, k_k, k_a,
                   r_k.reshape(-1), ln_x_w, ln_x_b)

    wr = jnp.concatenate([w_rg, w_re], axis=1)
    wr = jnp.pad(wr, ((0, 0), (0, LANES - wr.shape[1])))
    wr_hi = wr.astype(BF16)
    wr_lo = (wr - wr_hi.astype(F32)).astype(BF16)
    br = jnp.pad(jnp.concatenate([b_rg, b_re]), (0, LANES - N_GROUPS - N_EXPERTS)).reshape(1, LANES)
    x1, h2, lg = _outproj(o_attn.reshape(M, Q_COLS), o_rwkv.reshape(M, RWKV_W), x2, w_o.astype(BF16),
                          gt1, norm2_g.reshape(1, D), sh2, sc2, wr_hi, wr_lo, br, T)

    n_blocks = -(-(2 * M) // MOE_BLOCK) + N_EXPERTS
    info, cnt = _route(lg)
    slots, meta = _plan(info, cnt)
    slots_km = slots[:, :2].T.reshape(-1)
    seg_meta = jnp.concatenate([meta[2, :N_EXPERTS], meta[1, :1]])
    inv = _slot_table(slots_km, n_blocks * MOE_BLOCK)
    y2 = _experts(meta[0, :n_blocks], seg_meta, inv, h2, w1, w3, w2, n_blocks)
    out = _combine(x1, info, gt2, final_g, y2, T)
    del apply_final
    return out.reshape(B, T, D)


def kernel(x, c, w_ada, b_ada, norm1_g, w_in, mu_shift, sinks, w0, w_decay_up, a0, w_a_up, w_g_up, k_k, k_a, r_k, ln_x_w, ln_x_b, w_o, norm2_g, w_router_group, b_router_group, w_router_expert, b_router_expert, w1, w3, w2, final_g):
    depth = w_ada.shape[0]
    assert depth == 1, "single-layer stack"
    l = 0
    return _layer(x, c, w_ada[l], b_ada[l], norm1_g[l], w_in[l], mu_shift[l], sinks[l], w0[l],
                  w_decay_up[l], a0[l], w_a_up[l], w_g_up[l], k_k[l], k_a[l], r_k[l], ln_x_w[l],
                  ln_x_b[l], w_o[l], norm2_g[l], w_router_group[l], b_router_group[l],
                  w_router_expert[l], b_router_expert[l], w1[l], w3[l], w2[l], final_g, True)
```

```python
import functools
import math

import jax
import jax.numpy as jnp
from jax import lax
from jax.experimental import pallas as pl
from jax.experimental.pallas import tpu as pltpu

F32 = jnp.float32
BF16 = jnp.bfloat16
I32 = jnp.int32

LANES = 128
VMEM_BYTES_V7X = 64 * 1024 * 1024

HEAD_DIM = 64
ATTN_HEADS = 16
ATTN_KV_HEADS = 2
ATTN_GROUP = ATTN_HEADS // ATTN_KV_HEADS
WINDOW = 128
ROPE_THETA = 10000.0
RWKV_HEADS = 16
RWKV_N = 64
DECAY_LORA = 64
A_LORA = 64
GATE_LORA = 160
RWKV_LN_EPS = 64e-5
N_GROUPS = 8
EXPERTS_PER_GROUP = 8
N_EXPERTS = N_GROUPS * EXPERTS_PER_GROUP
EXPERT_FF = 512
MOE_BLOCK = 128
NORM_EPS = 1e-6

Q_COLS = ATTN_HEADS * HEAD_DIM
KV_COLS = ATTN_KV_HEADS * HEAD_DIM
ATTN_COLS = Q_COLS + 2 * KV_COLS
RWKV_W = RWKV_HEADS * RWKV_N
LORA_COLS = DECAY_LORA + A_LORA + GATE_LORA
LORA_PAD = 384
RWKV_COLS = 3 * RWKV_W + LORA_COLS
RWKV_PAD = 3 * RWKV_W + LORA_PAD
CHUNK = 64
RWKV_GROUP = 4
NEG_BIG = -1e30


def _vmem_limit(nbytes):
    return int(min(nbytes, VMEM_BYTES_V7X - 4 * 1024 * 1024))


def _dot(a, b):
    return jnp.dot(a, b, preferred_element_type=F32)


def _dot_nt(a, b):
    return lax.dot_general(a, b, (((1,), (1,)), ((), ())), preferred_element_type=F32)


def _dot_tn(a, b):
    return lax.dot_general(a, b, (((0,), (0,)), ((), ())), preferred_element_type=F32)


def _ada_kernel(c_ref, w_ref, b_ref, o_ref):
    c = c_ref[...]
    s = c * (1.0 / (1.0 + jnp.exp(-c)))
    o_ref[...] = _dot(s, w_ref[...]) + b_ref[...]


def _ada(c, w_ada, b_ada):
    B, D = c.shape
    N = w_ada.shape[1]
    tn = 1024
    cp = jnp.zeros((8, D), F32).at[:B].set(c)
    out = pl.pallas_call(
        _ada_kernel,
        grid=(N // tn,),
        in_specs=[pl.BlockSpec((8, D), lambda j: (0, 0)),
                  pl.BlockSpec((D, tn), lambda j: (0, j)),
                  pl.BlockSpec((1, tn), lambda j: (0, j))],
        out_specs=pl.BlockSpec((8, tn), lambda j: (0, j)),
        out_shape=jax.ShapeDtypeStruct((8, N), F32),
        compiler_params=pltpu.CompilerParams(
            dimension_semantics=("arbitrary",),
            vmem_limit_bytes=_vmem_limit(2 * D * tn * 4 + (8 << 20))),
        name="ada",
    )(cp, w_ada, b_ada.reshape(1, N))
    return out[:B]


def _rope(z, cos, sin):
    w = z.shape[-1]
    lane = lax.broadcasted_iota(I32, z.shape, 1)
    first_half = (lane % HEAD_DIM) < (HEAD_DIM // 2)
    partner = jnp.where(first_half, pltpu.roll(z, w - HEAD_DIM // 2, 1), pltpu.roll(z, HEAD_DIM // 2, 1))
    return z * cos + partner * sin


def _inproj_kernel(x_ref, g_ref, sh_ref, sc_ref, w_ref, mu_ref, cos_ref, sin_ref,
                   za_ref, zr_ref, carry_ref, *, tiles_per_seq):
    i = pl.program_id(0)
    tm = x_ref.shape[0]
    x = x_ref[...]
    ms = jnp.mean(x * x, axis=-1, keepdims=True)
    y = x * lax.rsqrt(ms + NORM_EPS) * g_ref[...]
    h = (y * (1.0 + sc_ref[0]) + sh_ref[0]).astype(BF16)

    for c0 in range(0, ATTN_COLS, 256):
        z = _dot(h, w_ref[:, c0:c0 + 256])
        t0 = 0 if c0 < Q_COLS else 256
        z = _rope(z, cos_ref[:, t0:t0 + 256], sin_ref[:, t0:t0 + 256])
        za_ref[:, c0:c0 + 256] = z.astype(za_ref.dtype)

    first = (i % tiles_per_seq) == 0
    row = lax.broadcasted_iota(I32, (tm, 1), 0)
    c0 = 0
    while c0 < RWKV_PAD:
        n = min(512, RWKV_PAD - c0)
        z = _dot(h, w_ref[:, ATTN_COLS + c0:ATTN_COLS + c0 + n])
        prev_last = jnp.where(first, 0.0, carry_ref[7:8, c0:c0 + n])
        z_prev = jnp.where(row == 0, prev_last, pltpu.roll(z, 1, 0))
        carry_ref[:, c0:c0 + n] = z[tm - 8:tm, :]
        zr_ref[:, c0:c0 + n] = z + (z_prev - z) * mu_ref[:, c0:c0 + n]
        c0 += n


def _inproj(x2, g, sh, sc, w_bf, mu_pad, cos_t, sin_t, T):
    M, D = x2.shape
    tm = 256
    tps = T // tm
    NW = w_bf.shape[1]
    kern = functools.partial(_inproj_kernel, tiles_per_seq=tps)
    vm = (D * NW * 2 + 2 * tm * D * 4 + 2 * tm * ATTN_COLS * 2 + 2 * tm * RWKV_PAD * 4
          + 4 * tm * 512 * 4 + tm * D * 8 + (8 << 20))
    return pl.pallas_call(
        kern,
        grid=(M // tm,),
        in_specs=[pl.BlockSpec((tm, D), lambda i: (i, 0)),
                  pl.BlockSpec((1, D), lambda i: (0, 0)),
                  pl.BlockSpec((1, 1, D), lambda i: (i // tps, 0, 0)),
                  pl.BlockSpec((1, 1, D), lambda i: (i // tps, 0, 0)),
                  pl.BlockSpec((D, NW), lambda i: (0, 0), pipeline_mode=pl.Buffered(1)),
                  pl.BlockSpec((1, RWKV_PAD), lambda i: (0, 0)),
                  pl.BlockSpec((tm, 512), lambda i: (i % tps, 0)),
                  pl.BlockSpec((tm, 512), lambda i: (i % tps, 0))],
        out_specs=[pl.BlockSpec((tm, ATTN_COLS), lambda i: (i, 0)),
                   pl.BlockSpec((tm, RWKV_PAD), lambda i: (i, 0))],
        out_shape=[jax.ShapeDtypeStruct((M, ATTN_COLS), BF16),
                   jax.ShapeDtypeStruct((M, RWKV_PAD), F32)],
        scratch_shapes=[pltpu.VMEM((8, RWKV_PAD), F32)],
        compiler_params=pltpu.CompilerParams(
            dimension_semantics=("arbitrary",), vmem_limit_bytes=_vmem_limit(vm)),
        name="inproj",
    )(x2, g, sh, sc, w_bf, mu_pad, cos_t, sin_t)


def _attn_kernel(sink_ref, q_ref, kc_ref, kp_ref, vc_ref, vp_ref, o_ref):
    n = pl.program_id(1)
    blk = q_ref.shape[1]
    row = lax.broadcasted_iota(I32, (blk, blk), 0)
    col = lax.broadcasted_iota(I32, (blk, blk), 1)
    mask = jnp.concatenate([(col > row) & (n > 0), col <= row], axis=1)
    scale = 1.0 / math.sqrt(HEAD_DIM)
    outs = []
    for kvh in range(ATTN_KV_HEADS):
        ks = slice(kvh * HEAD_DIM, (kvh + 1) * HEAD_DIM)
        kmat = jnp.concatenate([kp_ref[0, :, ks], kc_ref[0, :, ks]], axis=0)
        vmat = jnp.concatenate([vp_ref[0, :, ks], vc_ref[0, :, ks]], axis=0)
        for g in range(ATTN_GROUP):
            hd = kvh * ATTN_GROUP + g
            qh = q_ref[0, :, hd * HEAD_DIM:(hd + 1) * HEAD_DIM]
            s = _dot_nt(qh, kmat) * scale
            s = jnp.where(mask, s, NEG_BIG)
            sink = sink_ref[hd]
            m = jnp.maximum(jnp.max(s, axis=-1, keepdims=True), sink)
            p = jnp.exp(s - m)
            denom = jnp.sum(p, axis=-1, keepdims=True) + jnp.exp(sink - m)
            o = _dot(p.astype(BF16), vmat)
            outs.append(o / denom)
    o_ref[0] = jnp.concatenate(outs, axis=1).astype(o_ref.dtype)


def _attention(za3, sinks):
    B, T, _ = za3.shape
    nb = T // WINDOW
    kcol = Q_COLS // KV_COLS
    prev = lambda b, n, s: (b, jnp.maximum(n - 1, 0), kcol)
    prev_v = lambda b, n, s: (b, jnp.maximum(n - 1, 0), kcol + 1)
    gs = pltpu.PrefetchScalarGridSpec(
        num_scalar_prefetch=1,
        grid=(B, nb),
        in_specs=[pl.BlockSpec((1, WINDOW, Q_COLS), lambda b, n, s: (b, n, 0)),
                  pl.BlockSpec((1, WINDOW, KV_COLS), lambda b, n, s: (b, n, kcol)),
                  pl.BlockSpec((1, WINDOW, KV_COLS), prev),
                  pl.BlockSpec((1, WINDOW, KV_COLS), lambda b, n, s: (b, n, kcol + 1)),
                  pl.BlockSpec((1, WINDOW, KV_COLS), prev_v)],
        out_specs=pl.BlockSpec((1, WINDOW, Q_COLS), lambda b, n, s: (b, n, 0)),
    )
    return pl.pallas_call(
        _attn_kernel,
        grid_spec=gs,
        out_shape=jax.ShapeDtypeStruct((B, T, Q_COLS), BF16),
        compiler_params=pltpu.CompilerParams(dimension_semantics=("arbitrary", "arbitrary")),
        name="attn",
    )(sinks, za3, za3, za3, za3, za3)


def _rwkv_kernel(r_ref, k_ref, v_ref, lora_ref, w0_ref, wdu_ref, a0_ref, wau_ref, wgu_ref,
                 kk_ref, ka_ref, rk_ref, lnw_ref, lnb_ref, o_ref, s_ref):
    c = pl.program_id(1)
    C = r_ref.shape[1]
    N = RWKV_N
    G = RWKV_GROUP
    GW = G * N

    @pl.when(c == 0)
    def _():
        s_ref[...] = jnp.zeros_like(s_ref)

    lora = lora_ref[0]
    wd = lora[:, 0:DECAY_LORA]
    ad = lora[:, DECAY_LORA:DECAY_LORA + A_LORA]
    gd = lora[:, DECAY_LORA + A_LORA:LORA_COLS]
    wlin = w0_ref[...] + _dot(jnp.tanh(wd), wdu_ref[...])
    neg = -wlin
    softplus = jnp.maximum(neg, 0.0) + jnp.log(1.0 + jnp.exp(-jnp.abs(neg)))
    w = -softplus - 0.5
    logdec = -jnp.exp(w)
    a_all = 1.0 / (1.0 + jnp.exp(-(a0_ref[...] + _dot(ad, wau_ref[...]))))
    g_all = _dot(1.0 / (1.0 + jnp.exp(-gd)), wgu_ref[...])
    ti = lax.broadcasted_iota(I32, (C, C), 0)
    si = lax.broadcasted_iota(I32, (C, C), 1)
    tril = jnp.where(si <= ti, 1.0, 0.0).astype(BF16)
    ld_hi = logdec.astype(BF16)
    ld_lo = (logdec - ld_hi.astype(F32)).astype(BF16)
    cum = _dot(tril, ld_hi) + _dot(tril, ld_lo)
    e_in_all = jnp.exp(cum)
    e_ex_all = jnp.exp(cum - logdec)
    e_neg_all = jnp.exp(-cum)

    lane_head = lax.broadcasted_iota(I32, (1, GW), 1) // N
    t_row = lax.broadcasted_iota(I32, (C, GW), 0)
    s_lane = lax.broadcasted_iota(I32, (C, GW), 1) % N
    strict = s_lane < t_row
    incl = s_lane <= t_row
    bi = lax.broadcasted_iota(I32, (GW, GW), 0)
    bj = lax.broadcasted_iota(I32, (GW, GW), 1)
    same_head = (bi // N) == (bj // N)
    ones_bd = jnp.where(same_head, 1.0, 0.0).astype(BF16)
    eye = jnp.where(s_lane == t_row, 1.0, 0.0)

    def expand(xc):
        return jnp.concatenate([jnp.where(lane_head == h, xc, 0.0) for h in range(G)], axis=0)

    def head_sum(xs):
        s = _dot(jnp.concatenate(xs, axis=0).astype(BF16), ones_bd)
        return [s[i * C:(i + 1) * C] for i in range(len(xs))]

    r_l, k_l, v_l = [], [], []
    for gi in range(RWKV_HEADS // G):
        cs = slice(gi * GW, (gi + 1) * GW)
        r_l.append(r_ref[0, :, cs])
        k_l.append(k_ref[0, :, cs])
        v_l.append(v_ref[0, :, cs])

    ng = RWKV_HEADS // G
    rng = range(ng)
    col = [slice(gi * GW, (gi + 1) * GW) for gi in rng]
    kk0 = [k_l[i] * kk_ref[:, col[i]] for i in rng]
    nrm2 = head_sum([kk0[i] * kk0[i] for i in rng])
    kk = [kk0[i] / jnp.maximum(jnp.sqrt(nrm2[i]), 1e-12) for i in rng]
    k2 = [k_l[i] * (1.0 + (a_all[:, col[i]] - 1.0) * ka_ref[:, col[i]]) for i in rng]
    bt = [kk[i] * a_all[:, col[i]] * e_neg_all[:, col[i]] for i in rng]
    kt = [k2[i] * e_neg_all[:, col[i]] for i in rng]
    left = [jnp.concatenate([-kk[i] * e_ex_all[:, col[i]], r_l[i] * e_in_all[:, col[i]]], axis=0).astype(BF16)
            for i in rng]
    right = [jnp.concatenate([bt[i], kt[i]], axis=0).astype(BF16) for i in rng]
    zed = [jnp.concatenate([expand(bt[i]), expand(kt[i])], axis=0).astype(BF16) for i in rng]
    big = [_dot_nt(left[i], zed[i]) for i in rng]
    s_old = [s_ref[i] for i in rng]
    ls = [_dot_nt(left[i], s_old[i].astype(BF16)) for i in rng]
    v_bd = [expand(v_l[i]).astype(BF16) for i in rng]
    pw = [jnp.where(strict, big[i][:C, :G * C], 0.0) for i in rng]
    tinv = [eye + pw[i] for i in rng]
    pw_bd = [expand(pw[i]).astype(BF16) for i in rng]
    x = [ls[i][:C] + _dot(jnp.where(strict, big[i][:C, G * C:], 0.0).astype(BF16), v_bd[i]) for i in rng]
    span = 2
    while span < C:
        pw = [_dot(pw[i].astype(BF16), pw_bd[i]) for i in rng]
        pw_bd = [expand(pw[i]).astype(BF16) for i in rng]
        tinv = [tinv[i] + _dot(tinv[i].astype(BF16), pw_bd[i]) for i in rng]
        span *= 2
    u = [_dot(tinv[i].astype(BF16), expand(x[i]).astype(BF16)) for i in rng]
    a_r = [jnp.concatenate([jnp.where(incl, big[i][C:, :G * C], 0.0),
                            jnp.where(incl, big[i][C:, G * C:], 0.0)], axis=1).astype(BF16) for i in rng]
    y = [ls[i][C:] + _dot(a_r[i], jnp.concatenate([expand(u[i]).astype(BF16), v_bd[i]], axis=0)) for i in rng]
    uv = [jnp.concatenate([u[i], v_l[i]], axis=0).astype(BF16) for i in rng]
    for i in rng:
        s_ref[i] = ((s_old[i] + jnp.where(same_head, _dot_tn(uv[i], right[i]), 0.0))
                    * e_in_all[C - 1:C, col[i]])
    ysum = head_sum(y)
    yc = [y[i] - ysum[i] * (1.0 / N) for i in rng]
    ysq = head_sum([yc[i] * yc[i] for i in rng])
    var = [ysq[i] * (1.0 / N) for i in rng]
    rksum = head_sum([r_l[i] * k2[i] * rk_ref[:, col[i]] for i in rng])
    bonus = [rksum[i] * v_l[i] for i in rng]
    for i in rng:
        yn = yc[i] * lax.rsqrt(var[i] + RWKV_LN_EPS) * lnw_ref[:, col[i]] + lnb_ref[:, col[i]]
        o_ref[0, :, col[i]] = ((yn + bonus[i]) * g_all[:, col[i]]).astype(o_ref.dtype)


def _rwkv(zr3, w0, wdu, a0, wau, wgu, k_k, k_a, r_k, ln_w, ln_b):
    B, T, _ = zr3.shape
    C = CHUNK
    W = RWKV_W
    GW = RWKV_GROUP * RWKV_N
    vec = lambda v: v.reshape(1, W)
    full = lambda shape: pl.BlockSpec(shape, lambda b, c: (0,) * len(shape))
    return pl.pallas_call(
        _rwkv_kernel,
        grid=(B, T // C),
        in_specs=[pl.BlockSpec((1, C, W), lambda b, c: (b, c, 0)),
                  pl.BlockSpec((1, C, W), lambda b, c: (b, c, 1)),
                  pl.BlockSpec((1, C, W), lambda b, c: (b, c, 2)),
                  pl.BlockSpec((1, C, LORA_PAD), lambda b, c: (b, c, 3 * W // LORA_PAD)),
                  full((1, W)), full((DECAY_LORA, W)), full((1, W)), full((A_LORA, W)),
                  full((GATE_LORA, W)), full((1, W)), full((1, W)), full((1, W)),
                  full((1, W)), full((1, W))],
        out_specs=pl.BlockSpec((1, C, W), lambda b, c: (b, c, 0)),
        out_shape=jax.ShapeDtypeStruct((B, T, W), BF16),
        scratch_shapes=[pltpu.VMEM((RWKV_HEADS // RWKV_GROUP, GW, GW), F32)],
        compiler_params=pltpu.CompilerParams(dimension_semantics=("arbitrary", "arbitrary")),
        name="rwkv",
    )(zr3, zr3, zr3, zr3, vec(w0), wdu, vec(a0), wau, wgu, vec(k_k), vec(k_a), vec(r_k),
      vec(ln_w), vec(ln_b))


def _outproj_kernel(oa_ref, or_ref, x_ref, wo_ref, gt_ref, g_ref, sh_ref, sc_ref,
                    wrh_ref, wrl_ref, br_ref, x1_ref, h2_ref, lg_ref):
    mixed = _dot(oa_ref[...], wo_ref[0:Q_COLS, :]) + _dot(or_ref[...], wo_ref[Q_COLS:, :])
    x1 = x_ref[...] + gt_ref[0] * mixed
    x1_ref[...] = x1
    ms = jnp.mean(x1 * x1, axis=-1, keepdims=True)
    h2 = x1 * lax.rsqrt(ms + NORM_EPS) * g_ref[...] * (1.0 + sc_ref[0]) + sh_ref[0]
    h2_ref[...] = h2
    hh = h2.astype(BF16)
    hl = (h2 - hh.astype(F32)).astype(BF16)
    lg_ref[...] = (_dot(hh, wrh_ref[...]) + _dot(hl, wrh_ref[...]) + _dot(hh, wrl_ref[...])
                   + br_ref[...])


def _outproj(oa2, or2, x2, wo_bf, gt, g2, sh, sc, wr_hi, wr_lo, br, T):
    M, D = x2.shape
    tm = 256
    tps = T // tm
    bvec = pl.BlockSpec((1, 1, D), lambda i: (i // tps, 0, 0))
    vm = D * D * 2 * 2 + 2 * tm * D * (2 + 4 + 4 + 4) + tm * D * 16 + (8 << 20)
    return pl.pallas_call(
        _outproj_kernel,
        grid=(M // tm,),
        in_specs=[pl.BlockSpec((tm, Q_COLS), lambda i: (i, 0)),
                  pl.BlockSpec((tm, RWKV_W), lambda i: (i, 0)),
                  pl.BlockSpec((tm, D), lambda i: (i, 0)),
                  pl.BlockSpec((D, D), lambda i: (0, 0)),
                  bvec,
                  pl.BlockSpec((1, D), lambda i: (0, 0)),
                  bvec, bvec,
                  pl.BlockSpec((D, LANES), lambda i: (0, 0)),
                  pl.BlockSpec((D, LANES), lambda i: (0, 0)),
                  pl.BlockSpec((1, LANES), lambda i: (0, 0))],
        out_specs=[pl.BlockSpec((tm, D), lambda i: (i, 0)),
                   pl.BlockSpec((tm, D), lambda i: (i, 0)),
                   pl.BlockSpec((tm, LANES), lambda i: (i, 0))],
        out_shape=[jax.ShapeDtypeStruct((M, D), F32),
                   jax.ShapeDtypeStruct((M, D), F32),
                   jax.ShapeDtypeStruct((M, LANES), F32)],
        compiler_params=pltpu.CompilerParams(
            dimension_semantics=("arbitrary",), vmem_limit_bytes=_vmem_limit(vm)),
        name="outproj",
    )(oa2, or2, x2, wo_bf, gt, g2, sh, sc, wr_hi, wr_lo, br)


def _route_kernel(lg_ref, info_ref, cnt_ref, run_ref):
    i = pl.program_id(0)
    tm = lg_ref.shape[0]

    @pl.when(i == 0)
    def _():
        run_ref[...] = jnp.zeros_like(run_ref)

    lg = lg_ref[...]
    lane = lax.broadcasted_iota(I32, lg.shape, 1)
    gl = jnp.where(lane < N_GROUPS, lg, NEG_BIG)
    gmax = jnp.max(gl, axis=-1, keepdims=True)
    gsum = jnp.sum(jnp.exp(gl - gmax), axis=-1, keepdims=True)
    g_gate = 1.0 / gsum
    g_idx = jnp.min(jnp.where(gl == gmax, lane, LANES), axis=-1, keepdims=True)
    lo = N_GROUPS + EXPERTS_PER_GROUP * g_idx
    el = jnp.where((lane >= lo) & (lane < lo + EXPERTS_PER_GROUP), lg, NEG_BIG)
    e1max = jnp.max(el, axis=-1, keepdims=True)
    l1 = jnp.min(jnp.where(el == e1max, lane, LANES), axis=-1, keepdims=True)
    el2 = jnp.where(lane == l1, NEG_BIG, el)
    e2max = jnp.max(el2, axis=-1, keepdims=True)
    l2 = jnp.min(jnp.where(el2 == e2max, lane, LANES), axis=-1, keepdims=True)
    t2 = jnp.exp(e2max - e1max)
    w1 = g_gate / (1.0 + t2)
    w2 = g_gate * t2 / (1.0 + t2)
    ex1 = l1 - N_GROUPS
    ex2 = l2 - N_GROUPS

    oh1 = jnp.where(lane == ex1, 1.0, 0.0)
    oh2 = jnp.where(lane == ex2, 1.0, 0.0)
    ti = lax.broadcasted_iota(I32, (tm, tm), 0)
    si = lax.broadcasted_iota(I32, (tm, tm), 1)
    lower = jnp.where(si < ti, 1.0, 0.0).astype(BF16)
    pre1 = _dot(lower, oh1.astype(BF16))
    pre2 = _dot(lower, oh2.astype(BF16))
    cnt1 = jnp.sum(oh1, axis=0, keepdims=True)
    cnt2 = jnp.sum(oh2, axis=0, keepdims=True)
    run = run_ref[...]
    rank1 = jnp.sum(oh1 * (pre1 + run), axis=-1, keepdims=True)
    rank2 = jnp.sum(oh2 * (pre2 + run + cnt1), axis=-1, keepdims=True)
    run = run + cnt1 + cnt2
    run_ref[...] = run
    cnt_ref[...] = run

    info = jnp.where(lane == 0, ex1.astype(F32), 0.0)
    info = jnp.where(lane == 1, ex2.astype(F32), info)
    info = jnp.where(lane == 2, rank1, info)
    info = jnp.where(lane == 3, rank2, info)
    info = jnp.where(lane == 4, w1, info)
    info = jnp.where(lane == 5, w2, info)
    info_ref[...] = info


def _route(lg):
    M = lg.shape[0]
    tm = 256
    return pl.pallas_call(
        _route_kernel,
        grid=(M // tm,),
        in_specs=[pl.BlockSpec((tm, LANES), lambda i: (i, 0))],
        out_specs=[pl.BlockSpec((tm, LANES), lambda i: (i, 0)),
                   pl.BlockSpec((1, LANES), lambda i: (0, 0))],
        out_shape=[jax.ShapeDtypeStruct((M, LANES), F32),
                   jax.ShapeDtypeStruct((1, LANES), F32)],
        scratch_shapes=[pltpu.VMEM((1, LANES), F32)],
        compiler_params=pltpu.CompilerParams(dimension_semantics=("arbitrary",)),
        name="route",
    )(lg)


def _plan_kernel(info_ref, cnt_ref, slot_ref, meta_ref):
    cnt = cnt_ref[...]
    lane_r = lax.broadcasted_iota(I32, (1, LANES), 1)
    nblk = jnp.floor((cnt + (MOE_BLOCK - 1)) * (1.0 / MOE_BLOCK))
    ei = lax.broadcasted_iota(I32, (LANES, LANES), 0)
    ej = lax.broadcasted_iota(I32, (LANES, LANES), 1)
    upper = jnp.where(ei <= ej, 1.0, 0.0).astype(BF16)
    nb8 = jnp.broadcast_to(nblk, (8, LANES)).astype(BF16)
    bend = _dot(nb8, upper)[0:1, :]
    bstart = bend - nblk
    pstart = bstart * MOE_BLOCK

    info = info_ref[...]
    lane = lax.broadcasted_iota(I32, info.shape, 1)
    ex1 = info[:, 0:1].astype(I32)
    ex2 = info[:, 1:2].astype(I32)
    s1 = jnp.sum(jnp.where(lane == ex1, pstart, 0.0), axis=-1, keepdims=True) + info[:, 2:3]
    s2 = jnp.sum(jnp.where(lane == ex2, pstart, 0.0), axis=-1, keepdims=True) + info[:, 3:4]
    slot = jnp.where(lane == 0, s1, 0.0)
    slot = jnp.where(lane == 1, s2, slot)
    slot_ref[...] = slot.astype(I32)

    blk = lax.broadcasted_iota(I32, (LANES, 2 * LANES), 1).astype(F32)
    bend_col = jnp.sum(jnp.where(ei == ej, jnp.broadcast_to(bend, (LANES, LANES)), 0.0),
                       axis=-1, keepdims=True)
    erow = lax.broadcasted_iota(I32, (LANES, 2 * LANES), 0)
    hit = jnp.where((bend_col <= blk) & (erow < N_EXPERTS), 1.0, 0.0)
    bexp = jnp.minimum(jnp.sum(hit, axis=0, keepdims=True), N_EXPERTS - 1.0)
    n_used = jnp.max(jnp.where(lane_r < N_EXPERTS, bend, 0.0), axis=-1, keepdims=True)
    lastblk = jnp.where(nblk > 0, bend - 1.0, -1.0)
    r8 = lax.broadcasted_iota(I32, (8, 2 * LANES), 0)
    last2 = jnp.concatenate([lastblk, jnp.full((1, LANES), -1.0)], axis=1)
    meta = jnp.where(r8 == 0, jnp.broadcast_to(bexp, (8, 2 * LANES)), 0.0)
    meta = jnp.where(r8 == 1, jnp.broadcast_to(n_used, (8, 2 * LANES)), meta)
    meta = jnp.where(r8 == 2, jnp.broadcast_to(last2, (8, 2 * LANES)), meta)
    meta_ref[...] = meta.astype(I32)


def _plan(info, cnt):
    M = info.shape[0]
    tm = 256
    return pl.pallas_call(
        _plan_kernel,
        grid=(M // tm,),
        in_specs=[pl.BlockSpec((tm, LANES), lambda i: (i, 0)),
                  pl.BlockSpec((1, LANES), lambda i: (0, 0))],
        out_specs=[pl.BlockSpec((tm, LANES), lambda i: (i, 0)),
                   pl.BlockSpec((8, 2 * LANES), lambda i: (0, 0))],
        out_shape=[jax.ShapeDtypeStruct((M, LANES), I32),
                   jax.ShapeDtypeStruct((8, 2 * LANES), I32)],
        compiler_params=pltpu.CompilerParams(dimension_semantics=("arbitrary",)),
        name="plan",
    )(info, cnt)


def _slot_table_kernel(slots_ref, inv_ref, *, n_assign, n_slots):
    period = 2 * MOE_BLOCK

    def init(b, c):
        for j in range(period):
            inv_ref[b * period + j] = n_assign + j
        return c

    def put(a, c):
        inv_ref[slots_ref[a]] = a
        return c

    lax.fori_loop(0, n_slots // period, init, 0)
    lax.fori_loop(0, n_assign, put, 0, unroll=16)


def _slot_table(slots_km, n_slots):
    n_assign = slots_km.shape[0]
    assert n_slots % (2 * MOE_BLOCK) == 0
    gs = pltpu.PrefetchScalarGridSpec(
        num_scalar_prefetch=1, grid=(1,), in_specs=[],
        out_specs=pl.BlockSpec(memory_space=pltpu.SMEM))
    return pl.pallas_call(
        functools.partial(_slot_table_kernel, n_assign=n_assign, n_slots=n_slots),
        grid_spec=gs,
        out_shape=jax.ShapeDtypeStruct((n_slots,), I32),
        compiler_params=pltpu.CompilerParams(dimension_semantics=("arbitrary",)),
        name="slot_table",
    )(slots_km)


def _experts_kernel(bexp_ref, seg_ref, inv_ref, h2_ref, w1_ref, w3_ref, w2_ref, y2_ref,
                    xbuf, ybuf, w1buf, w3buf, w2buf, sem_x, sem_y, sem_w, par_ref, *, n_assign, n_blocks):
    i = pl.program_id(0)
    n_used = seg_ref[N_EXPERTS]
    n_tok = n_assign // 2
    B = MOE_BLOCK

    def src_row(a):
        if n_tok & (n_tok - 1) == 0:
            return a & (n_tok - 1)
        return lax.rem(a, n_tok)

    def weight_copies(e, slot):
        return (pltpu.make_async_copy(w1_ref.at[e], w1buf.at[slot], sem_w.at[slot]),
                pltpu.make_async_copy(w3_ref.at[e], w3buf.at[slot], sem_w.at[slot]),
                pltpu.make_async_copy(w2_ref.at[e], w2buf.at[slot], sem_w.at[slot]))

    def start_weights(e, slot):
        for cp in weight_copies(e, slot):
            cp.start(priority=1)

    def prefetch_next_expert(e, slot):
        nb = seg_ref[e] + 1

        @pl.when(nb < n_used)
        def _():
            start_weights(bexp_ref[jnp.minimum(nb, n_blocks - 1)], slot)

    def start_gathers(b):
        slot = lax.rem(b, 2)
        for r in range(B):
            a = inv_ref[b * B + r]
            pltpu.make_async_copy(h2_ref.at[pl.ds(src_row(a), 1)], xbuf.at[slot, pl.ds(r, 1)],
                                  sem_x.at[slot]).start()

    def start_scatters(b):
        slot = lax.rem(b, 2)
        for r in range(B):
            a = inv_ref[b * B + r]
            pltpu.make_async_copy(ybuf.at[slot, pl.ds(r, 1)], y2_ref.at[pl.ds(a, 1)],
                                  sem_y.at[slot]).start()

    def wait_rows(buf, sem, b):
        slot = lax.rem(b, 2)
        pltpu.make_async_copy(buf.at[slot], buf.at[slot], sem.at[slot]).wait()

    @pl.when(i == 0)
    def _():
        ybuf[...] = jnp.zeros_like(ybuf)
        for s in range(2):
            cp = pltpu.make_async_copy(ybuf.at[s], y2_ref.at[pl.ds(n_assign + s * B, B)], sem_y.at[s])
            cp.start()
            cp.wait()
        par_ref[0] = 0
        e0 = bexp_ref[0]
        start_weights(e0, 0)
        prefetch_next_expert(e0, 1)
        start_gathers(0)

    active = i < n_used

    @pl.when(active)
    def _():
        e = bexp_ref[i]
        first = jnp.logical_or(i == 0, e != bexp_ref[jnp.maximum(i - 1, 0)])

        @pl.when(jnp.logical_and(first, i > 0))
        def _():
            par_ref[0] = 1 - par_ref[0]

        @pl.when(first)
        def _():
            p = par_ref[0]
            for cp in weight_copies(e, p):
                cp.wait()

            @pl.when(i > 0)
            def _():
                prefetch_next_expert(e, 1 - p)

        @pl.when(i + 1 < n_used)
        def _():
            start_gathers(i + 1)

        wait_rows(xbuf, sem_x, i)

    @pl.when(jnp.logical_and(i >= 2, i - 2 < n_used))
    def _():
        wait_rows(ybuf, sem_y, i - 2)

    @pl.when(active)
    def _():
        p = par_ref[0]
        slot = lax.rem(i, 2)
        x = xbuf[slot].astype(BF16)
        h1 = _dot(x, w1buf[p])
        h3 = _dot(x, w3buf[p])
        hid = h1 * (1.0 / (1.0 + jnp.exp(-h1))) * h3
        ybuf[slot] = _dot(hid.astype(BF16), w2buf[p])
        start_scatters(i)

    @pl.when(i == n_blocks - 1)
    def _():
        for b in (i - 1, i):
            @pl.when(jnp.logical_and(b >= 0, b < n_used))
            def _():
                wait_rows(ybuf, sem_y, b)


def _experts(bexp, seg_meta, inv, h2, w1, w3, w2, n_blocks):
    M, D = h2.shape
    FF = w1.shape[2]
    n_assign = 2 * M
    any_spec = pl.BlockSpec(memory_space=pl.ANY)
    gs = pltpu.PrefetchScalarGridSpec(
        num_scalar_prefetch=3,
        grid=(n_blocks,),
        in_specs=[any_spec, any_spec, any_spec, any_spec],
        out_specs=any_spec,
        scratch_shapes=[pltpu.VMEM((2, MOE_BLOCK, D), F32), pltpu.VMEM((2, MOE_BLOCK, D), F32),
                        pltpu.VMEM((2, D, FF), F32), pltpu.VMEM((2, D, FF), F32),
                        pltpu.VMEM((2, FF, D), F32),
                        pltpu.SemaphoreType.DMA((2,)), pltpu.SemaphoreType.DMA((2,)),
                        pltpu.SemaphoreType.DMA((2,)), pltpu.SMEM((1,), I32)],
    )
    vm = 2 * 3 * D * FF * 4 + 4 * MOE_BLOCK * D * 4 + (12 << 20)
    return pl.pallas_call(
        functools.partial(_experts_kernel, n_assign=n_assign, n_blocks=n_blocks),
        grid_spec=gs,
        out_shape=jax.ShapeDtypeStruct((n_assign + 2 * MOE_BLOCK, D), F32),
        compiler_params=pltpu.CompilerParams(
            dimension_semantics=("arbitrary",), vmem_limit_bytes=_vmem_limit(vm),
            has_side_effects=True),
        name="experts",
    )(bexp, seg_meta, inv, h2, w1, w3, w2)


def _combine_kernel(x1_ref, info_ref, gt_ref, fg_ref, ya_ref, yb_ref, o_ref):
    info = info_ref[...]
    moe = info[:, 4:5] * ya_ref[...] + info[:, 5:6] * yb_ref[...]
    x = x1_ref[...] + gt_ref[0] * moe
    ms = jnp.mean(x * x, axis=-1, keepdims=True)
    o_ref[...] = x * lax.rsqrt(ms + NORM_EPS) * fg_ref[...]


def _combine(x1, info, gt, final_g, y2, T):
    M, D = x1.shape
    tm = 256
    tps = T // tm
    nt = M // tm
    vm = 2 * 4 * tm * D * 4 + tm * D * 16 + (8 << 20)
    return pl.pallas_call(
        _combine_kernel,
        grid=(nt,),
        in_specs=[pl.BlockSpec((tm, D), lambda i: (i, 0)),
                  pl.BlockSpec((tm, LANES), lambda i: (i, 0)),
                  pl.BlockSpec((1, 1, D), lambda i: (i // tps, 0, 0)),
                  pl.BlockSpec((1, D), lambda i: (0, 0)),
                  pl.BlockSpec((tm, D), lambda i: (i, 0)),
                  pl.BlockSpec((tm, D), lambda i: (nt + i, 0))],
        out_specs=pl.BlockSpec((tm, D), lambda i: (i, 0)),
        out_shape=jax.ShapeDtypeStruct((M, D), F32),
        compiler_params=pltpu.CompilerParams(
            dimension_semantics=("arbitrary",), vmem_limit_bytes=_vmem_limit(vm)),
        name="combine",
    )(x1, info, gt, final_g.reshape(1, D), y2, y2)


def _rope_tables(T):
    half = HEAD_DIM // 2
    inv_freq = ROPE_THETA ** (-jnp.arange(0, HEAD_DIM, 2, dtype=F32) / HEAD_DIM)
    ang = jnp.arange(T, dtype=F32)[:, None] * inv_freq[None, :]
    cos, sin = jnp.cos(ang), jnp.sin(ang)
    cos_h = jnp.concatenate([cos, cos], axis=1)
    sin_h = jnp.concatenate([-sin, sin], axis=1)
    cos_q = jnp.tile(cos_h, (1, 4))
    sin_q = jnp.tile(sin_h, (1, 4))
    cos_kv = jnp.concatenate([jnp.tile(cos_h, (1, 2)), jnp.ones((T, KV_COLS), F32)], axis=1)
    sin_kv = jnp.concatenate([jnp.tile(sin_h, (1, 2)), jnp.zeros((T, KV_COLS), F32)], axis=1)
    return jnp.concatenate([cos_q, cos_kv], axis=1), jnp.concatenate([sin_q, sin_kv], axis=1)


def _layer(x, c, w_ada, b_ada, norm1_g, w_in, mu_shift, sinks, w0, w_decay_up, a0, w_a_up, w_g_up,
           k_k, k_a, r_k, ln_x_w, ln_x_b, w_o, norm2_g, w_rg, b_rg, w_re, b_re, w1, w3, w2, final_g,
           apply_final):
    B, T, D = x.shape
    M = B * T
    mod = _ada(c, w_ada, b_ada)
    sh1, sc1, gt1, sh2, sc2, gt2 = [m.reshape(B, 1, D) for m in jnp.split(mod, 6, axis=-1)]

    pad = RWKV_PAD - RWKV_COLS
    w_bf = jnp.pad(w_in, ((0, 0), (0, pad))).astype(BF16)
    mu_pad = jnp.pad(mu_shift, (0, pad)).reshape(1, RWKV_PAD)
    cos_t, sin_t = _rope_tables(T)
    x2 = x.reshape(M, D)
    za, zr = _inproj(x2, norm1_g.reshape(1, D), sh1, sc1, w_bf, mu_pad, cos_t, sin_t, T)

    o_attn = _attention(za.reshape(B, T, ATTN_COLS), sinks)
    o_rwkv = _rwkv(zr.reshape(B, T, RWKV_PAD), w0, w_decay_up, a0, w_a_up, w_g_up, k_k, k_a,
                   r_k.reshape(-1), ln_x_w, ln_x_b)

    wr = jnp.concatenate([w_rg, w_re], axis=1)
    wr = jnp.pad(wr, ((0, 0), (0, LANES - wr.shape[1])))
    wr_hi = wr.astype(BF16)
    wr_lo = (wr - wr_hi.astype(F32)).astype(BF16)
    br = jnp.pad(jnp.concatenate([b_rg, b_re]), (0, LANES - N_GROUPS - N_EXPERTS)).reshape(1, LANES)
    x1, h2, lg = _outproj(o_attn.reshape(M, Q_COLS), o_rwkv.reshape(M, RWKV_W), x2, w_o.astype(BF16),
                          gt1, norm2_g.reshape(1, D), sh2, sc2, wr_hi, wr_lo, br, T)

    n_blocks = -(-(2 * M) // MOE_BLOCK) + N_EXPERTS
    info, cnt = _route(lg)
    slots, meta = _plan(info, cnt)
    slots_km = slots[:, :2].T.reshape(-1)
    seg_meta = jnp.concatenate([meta[2, :N_EXPERTS], meta[1, :1]])
    inv = _slot_table(slots_km, n_blocks * MOE_BLOCK)
    y2 = _experts(meta[0, :n_blocks], seg_meta, inv, h2, w1, w3, w2, n_blocks)
    out = _combine(x1, info, gt2, final_g, y2, T)
    del apply_final
    return out.reshape(B, T, D)


def kernel(x, c, w_ada, b_ada, norm1_g, w_in, mu_shift, sinks, w0, w_decay_up, a0, w_a_up, w_g_up, k_k, k_a, r_k, ln_x_w, ln_x_b, w_o, norm2_g, w_router_group, b_router_group, w_router_expert, b_router_expert, w1, w3, w2, final_g):
    depth = w_ada.shape[0]
    assert depth == 1, "single-layer stack"
    l = 0
    return _layer(x, c, w_ada[l], b_ada[l], norm1_g[l], w_in[l], mu_shift[l], sinks[l], w0[l],
                  w_decay_up[l], a0[l], w_a_up[l], w_g_up[l], k_k[l], k_a[l], r_k[l], ln_x_w[l],
                  ln_x_b[l], w_o[l], norm2_g[l], w_router_group[l], b_router_group[l],
                  w_router_expert[l], b_router_expert[l], w1[l], w3[l], w2[l], final_g, True)
```

```python
import functools
import math

import jax
import jax.numpy as jnp
import numpy as np
from jax import lax
from jax.experimental import pallas as pl
from jax.experimental.pallas import tpu as pltpu

F32 = jnp.float32
BF16 = jnp.bfloat16
I32 = jnp.int32

LANES = 128
VMEM_BYTES_V7X = 64 * 1024 * 1024

HEAD_DIM = 64
ATTN_HEADS = 16
ATTN_KV_HEADS = 2
ATTN_GROUP = ATTN_HEADS // ATTN_KV_HEADS
WINDOW = 128
ROPE_THETA = 10000.0
RWKV_HEADS = 16
RWKV_N = 64
DECAY_LORA = 64
A_LORA = 64
GATE_LORA = 160
RWKV_LN_EPS = 64e-5
N_GROUPS = 8
EXPERTS_PER_GROUP = 8
N_EXPERTS = N_GROUPS * EXPERTS_PER_GROUP
EXPERT_FF = 512
MOE_BLOCK = 128
NORM_EPS = 1e-6

Q_COLS = ATTN_HEADS * HEAD_DIM
KV_COLS = ATTN_KV_HEADS * HEAD_DIM
ATTN_COLS = Q_COLS + 2 * KV_COLS
RWKV_W = RWKV_HEADS * RWKV_N
LORA_COLS = DECAY_LORA + A_LORA + GATE_LORA
LORA_PAD = 384
RWKV_COLS = 3 * RWKV_W + LORA_COLS
RWKV_PAD = 3 * RWKV_W + LORA_PAD
CHUNK = 64
RWKV_GROUP = 4
NEG_BIG = -1e30


def _vmem_limit(nbytes):
    return int(min(nbytes, VMEM_BYTES_V7X - 4 * 1024 * 1024))


def _dot(a, b):
    return jnp.dot(a, b, preferred_element_type=F32)


def _dot_nt(a, b):
    return lax.dot_general(a, b, (((1,), (1,)), ((), ())), preferred_element_type=F32)


def _dot_tn(a, b):
    return lax.dot_general(a, b, (((0,), (0,)), ((), ())), preferred_element_type=F32)


def _ada_kernel(c_ref, w_ref, b_ref, o_ref):
    c = c_ref[...]
    s = c * (1.0 / (1.0 + jnp.exp(-c)))
    o_ref[...] = _dot(s, w_ref[...]) + b_ref[...]


def _ada(c, w_ada, b_ada):
    B, D = c.shape
    N = w_ada.shape[1]
    tn = 1024
    cp = jnp.zeros((8, D), F32).at[:B].set(c)
    out = pl.pallas_call(
        _ada_kernel,
        grid=(N // tn,),
        in_specs=[pl.BlockSpec((8, D), lambda j: (0, 0)),
                  pl.BlockSpec((D, tn), lambda j: (0, j)),
                  pl.BlockSpec((1, tn), lambda j: (0, j))],
        out_specs=pl.BlockSpec((8, tn), lambda j: (0, j)),
        out_shape=jax.ShapeDtypeStruct((8, N), F32),
        compiler_params=pltpu.CompilerParams(
            dimension_semantics=("arbitrary",),
            vmem_limit_bytes=_vmem_limit(2 * D * tn * 4 + (8 << 20))),
        name="ada",
    )(cp, w_ada, b_ada.reshape(1, N))
    return out[:B]


def _rope(z, cos, sin):
    w = z.shape[-1]
    lane = lax.broadcasted_iota(I32, z.shape, 1)
    first_half = (lane % HEAD_DIM) < (HEAD_DIM // 2)
    partner = jnp.where(first_half, pltpu.roll(z, w - HEAD_DIM // 2, 1), pltpu.roll(z, HEAD_DIM // 2, 1))
    return z * cos + partner * sin


def _inproj_kernel(x_ref, g_ref, sh_ref, sc_ref, w_ref, mu_ref, cos_ref, sin_ref,
                   za_ref, zr_ref, carry_ref, *, tiles_per_seq):
    i = pl.program_id(0)
    tm = x_ref.shape[0]
    x = x_ref[...]
    ms = jnp.mean(x * x, axis=-1, keepdims=True)
    y = x * lax.rsqrt(ms + NORM_EPS) * g_ref[...]
    h = (y * (1.0 + sc_ref[0]) + sh_ref[0]).astype(BF16)

    cos2 = jnp.concatenate([cos_ref[...], cos_ref[...]], axis=1)
    sin2 = jnp.concatenate([sin_ref[...], sin_ref[...]], axis=1)
    for c0 in range(0, ATTN_COLS, 256):
        z = _dot(h, w_ref[:, c0:c0 + 256])
        if c0 < Q_COLS:
            z = _rope(z, cos2, sin2)
        else:
            z = jnp.concatenate([_rope(z[:, :KV_COLS], cos_ref[...], sin_ref[...]), z[:, KV_COLS:]], axis=1)
        za_ref[:, c0:c0 + 256] = z.astype(za_ref.dtype)

    first = (i % tiles_per_seq) == 0
    row = lax.broadcasted_iota(I32, (tm, 1), 0)
    c0 = 0
    while c0 < RWKV_PAD:
        n = min(512, RWKV_PAD - c0)
        z = _dot(h, w_ref[:, ATTN_COLS + c0:ATTN_COLS + c0 + n])
        prev_last = jnp.where(first, 0.0, carry_ref[7:8, c0:c0 + n])
        z_prev = jnp.where(row == 0, prev_last, pltpu.roll(z, 1, 0))
        carry_ref[:, c0:c0 + n] = z[tm - 8:tm, :]
        zr_ref[:, c0:c0 + n] = z + (z_prev - z) * mu_ref[:, c0:c0 + n]
        c0 += n


def _inproj(x2, g, sh, sc, w_bf, mu_pad, cos_t, sin_t, T):
    M, D = x2.shape
    tm = 256
    tps = T // tm
    NW = w_bf.shape[1]
    kern = functools.partial(_inproj_kernel, tiles_per_seq=tps)
    vm = (D * NW * 2 + 2 * tm * D * 4 + 2 * tm * ATTN_COLS * 2 + 2 * tm * RWKV_PAD * 4
          + 4 * tm * 512 * 4 + tm * D * 8 + (8 << 20))
    return pl.pallas_call(
        kern,
        grid=(M // tm,),
        in_specs=[pl.BlockSpec((tm, D), lambda i: (i, 0)),
                  pl.BlockSpec((1, D), lambda i: (0, 0)),
                  pl.BlockSpec((1, 1, D), lambda i: (i // tps, 0, 0)),
                  pl.BlockSpec((1, 1, D), lambda i: (i // tps, 0, 0)),
                  pl.BlockSpec((D, NW), lambda i: (0, 0), pipeline_mode=pl.Buffered(1)),
                  pl.BlockSpec((1, RWKV_PAD), lambda i: (0, 0)),
                  pl.BlockSpec((tm, 2 * HEAD_DIM), lambda i: (i % tps, 0)),
                  pl.BlockSpec((tm, 2 * HEAD_DIM), lambda i: (i % tps, 0))],
        out_specs=[pl.BlockSpec((tm, ATTN_COLS), lambda i: (i, 0)),
                   pl.BlockSpec((tm, RWKV_PAD), lambda i: (i, 0))],
        out_shape=[jax.ShapeDtypeStruct((M, ATTN_COLS), BF16),
                   jax.ShapeDtypeStruct((M, RWKV_PAD), F32)],
        scratch_shapes=[pltpu.VMEM((8, RWKV_PAD), F32)],
        compiler_params=pltpu.CompilerParams(
            dimension_semantics=("arbitrary",), vmem_limit_bytes=_vmem_limit(vm)),
        name="inproj",
    )(x2, g, sh, sc, w_bf, mu_pad, cos_t, sin_t)


def _attn_kernel(sink_ref, q_ref, kc_ref, kp_ref, vc_ref, vp_ref, o_ref):
    n = pl.program_id(1)
    blk = q_ref.shape[1]
    row = lax.broadcasted_iota(I32, (blk, blk), 0)
    col = lax.broadcasted_iota(I32, (blk, blk), 1)
    mask = jnp.concatenate([(col > row) & (n > 0), col <= row], axis=1)
    scale = 1.0 / math.sqrt(HEAD_DIM)
    outs = []
    for kvh in range(ATTN_KV_HEADS):
        ks = slice(kvh * HEAD_DIM, (kvh + 1) * HEAD_DIM)
        kmat = jnp.concatenate([kp_ref[0, :, ks], kc_ref[0, :, ks]], axis=0)
        vmat = jnp.concatenate([vp_ref[0, :, ks], vc_ref[0, :, ks]], axis=0)
        for g in range(ATTN_GROUP):
            hd = kvh * ATTN_GROUP + g
            qh = q_ref[0, :, hd * HEAD_DIM:(hd + 1) * HEAD_DIM]
            s = _dot_nt(qh, kmat) * scale
            s = jnp.where(mask, s, NEG_BIG)
            sink = sink_ref[hd]
            m = jnp.maximum(jnp.max(s, axis=-1, keepdims=True), sink)
            p = jnp.exp(s - m)
            denom = jnp.sum(p, axis=-1, keepdims=True) + jnp.exp(sink - m)
            o = _dot(p.astype(BF16), vmat)
            outs.append(o / denom)
    o_ref[0] = jnp.concatenate(outs, axis=1).astype(o_ref.dtype)


def _attention(za3, sinks):
    B, T, _ = za3.shape
    nb = T // WINDOW
    kcol = Q_COLS // KV_COLS
    prev = lambda b, n, s: (b, jnp.maximum(n - 1, 0), kcol)
    prev_v = lambda b, n, s: (b, jnp.maximum(n - 1, 0), kcol + 1)
    gs = pltpu.PrefetchScalarGridSpec(
        num_scalar_prefetch=1,
        grid=(B, nb),
        in_specs=[pl.BlockSpec((1, WINDOW, Q_COLS), lambda b, n, s: (b, n, 0)),
                  pl.BlockSpec((1, WINDOW, KV_COLS), lambda b, n, s: (b, n, kcol)),
                  pl.BlockSpec((1, WINDOW, KV_COLS), prev),
                  pl.BlockSpec((1, WINDOW, KV_COLS), lambda b, n, s: (b, n, kcol + 1)),
                  pl.BlockSpec((1, WINDOW, KV_COLS), prev_v)],
        out_specs=pl.BlockSpec((1, WINDOW, Q_COLS), lambda b, n, s: (b, n, 0)),
    )
    return pl.pallas_call(
        _attn_kernel,
        grid_spec=gs,
        out_shape=jax.ShapeDtypeStruct((B, T, Q_COLS), BF16),
        compiler_params=pltpu.CompilerParams(dimension_semantics=("arbitrary", "arbitrary")),
        name="attn",
    )(sinks, za3, za3, za3, za3, za3)


def _rwkv_kernel(r_ref, k_ref, v_ref, lora_ref, w0_ref, wdu_ref, a0_ref, wau_ref, wgu_ref,
                 kk_ref, ka_ref, rk_ref, lnw_ref, lnb_ref, o_ref, s_ref):
    c = pl.program_id(0)
    nseq = r_ref.shape[0]
    C = r_ref.shape[1]
    N = RWKV_N
    G = RWKV_GROUP
    GW = G * N
    ng = RWKV_HEADS // G

    @pl.when(c == 0)
    def _():
        s_ref[...] = jnp.zeros_like(s_ref)

    ti = lax.broadcasted_iota(I32, (C, C), 0)
    si = lax.broadcasted_iota(I32, (C, C), 1)
    tril = jnp.where(si <= ti, 1.0, 0.0).astype(BF16)
    a_seq, g_seq, e_in_seq, e_ex_seq, e_neg_seq = [], [], [], [], []
    for b in range(nseq):
        lora = lora_ref[b]
        wd = lora[:, 0:DECAY_LORA]
        ad = lora[:, DECAY_LORA:DECAY_LORA + A_LORA]
        gd = lora[:, DECAY_LORA + A_LORA:LORA_COLS]
        wlin = w0_ref[...] + _dot(jnp.tanh(wd), wdu_ref[...])
        neg = -wlin
        softplus = jnp.maximum(neg, 0.0) + jnp.log(1.0 + jnp.exp(-jnp.abs(neg)))
        w = -softplus - 0.5
        logdec = -jnp.exp(w)
        a_seq.append(1.0 / (1.0 + jnp.exp(-(a0_ref[...] + _dot(ad, wau_ref[...])))))
        g_seq.append(_dot(1.0 / (1.0 + jnp.exp(-gd)), wgu_ref[...]))
        ld_hi = logdec.astype(BF16)
        ld_lo = (logdec - ld_hi.astype(F32)).astype(BF16)
        cum = _dot(tril, ld_hi) + _dot(tril, ld_lo)
        e_in_seq.append(jnp.exp(cum))
        e_ex_seq.append(jnp.exp(cum - logdec))
        e_neg_seq.append(jnp.exp(-cum))

    lane_head = lax.broadcasted_iota(I32, (1, GW), 1) // N
    t_row = lax.broadcasted_iota(I32, (C, GW), 0)
    s_lane = lax.broadcasted_iota(I32, (C, GW), 1) % N
    strict = s_lane < t_row
    incl = s_lane <= t_row
    bi = lax.broadcasted_iota(I32, (GW, GW), 0)
    bj = lax.broadcasted_iota(I32, (GW, GW), 1)
    same_head = (bi // N) == (bj // N)
    ones_bd = jnp.where(same_head, 1.0, 0.0).astype(BF16)
    eye = jnp.where(s_lane == t_row, 1.0, 0.0)

    def expand(xc):
        return jnp.concatenate([jnp.where(lane_head == h, xc, 0.0) for h in range(G)], axis=0)

    def head_sum(xs):
        s = _dot(jnp.concatenate(xs, axis=0).astype(BF16), ones_bd)
        return [s[i * C:(i + 1) * C] for i in range(len(xs))]

    seq = [b for b in range(nseq) for _ in range(ng)]
    col = [slice(gi * GW, (gi + 1) * GW) for _ in range(nseq) for gi in range(ng)]
    rng = range(nseq * ng)
    r_l = [r_ref[seq[i], :, col[i]] for i in rng]
    k_l = [k_ref[seq[i], :, col[i]] for i in rng]
    v_l = [v_ref[seq[i], :, col[i]] for i in rng]
    a_l = [a_seq[seq[i]][:, col[i]] for i in rng]
    e_in = [e_in_seq[seq[i]][:, col[i]] for i in rng]
    e_neg = [e_neg_seq[seq[i]][:, col[i]] for i in rng]
    kk0 = [k_l[i] * kk_ref[:, col[i]] for i in rng]
    nrm2 = head_sum([kk0[i] * kk0[i] for i in rng])
    kk = [kk0[i] / jnp.maximum(jnp.sqrt(nrm2[i]), 1e-12) for i in rng]
    k2 = [k_l[i] * (1.0 + (a_l[i] - 1.0) * ka_ref[:, col[i]]) for i in rng]
    bt = [kk[i] * a_l[i] * e_neg[i] for i in rng]
    kt = [k2[i] * e_neg[i] for i in rng]
    left = [jnp.concatenate([-kk[i] * e_ex_seq[seq[i]][:, col[i]], r_l[i] * e_in[i]], axis=0).astype(BF16)
            for i in rng]
    right = [jnp.concatenate([bt[i], kt[i]], axis=0).astype(BF16) for i in rng]
    zed = [jnp.concatenate([expand(bt[i]), expand(kt[i])], axis=0).astype(BF16) for i in rng]
    big = [_dot_nt(left[i], zed[i]) for i in rng]
    s_old = [s_ref[i] for i in rng]
    ls = [_dot_nt(left[i], s_old[i].astype(BF16)) for i in rng]
    v_bd = [expand(v_l[i]).astype(BF16) for i in rng]
    pw = [jnp.where(strict, big[i][:C, :G * C], 0.0) for i in rng]
    tinv = [eye + pw[i] for i in rng]
    pw_bd = [expand(pw[i]).astype(BF16) for i in rng]
    x = [ls[i][:C] + _dot(jnp.where(strict, big[i][:C, G * C:], 0.0).astype(BF16), v_bd[i]) for i in rng]
    span = 2
    while span < C:
        pw = [_dot(pw[i].astype(BF16), pw_bd[i]) for i in rng]
        pw_bd = [expand(pw[i]).astype(BF16) for i in rng]
        tinv = [tinv[i] + _dot(tinv[i].astype(BF16), pw_bd[i]) for i in rng]
        span *= 2
    u = [_dot(tinv[i].astype(BF16), expand(x[i]).astype(BF16)) for i in rng]
    a_r = [jnp.concatenate([jnp.where(incl, big[i][C:, :G * C], 0.0),
                            jnp.where(incl, big[i][C:, G * C:], 0.0)], axis=1).astype(BF16) for i in rng]
    y = [ls[i][C:] + _dot(a_r[i], jnp.concatenate([expand(u[i]).astype(BF16), v_bd[i]], axis=0)) for i in rng]
    uv = [jnp.concatenate([u[i], v_l[i]], axis=0).astype(BF16) for i in rng]
    for i in rng:
        s_ref[i] = ((s_old[i] + jnp.where(same_head, _dot_tn(uv[i], right[i]), 0.0))
                    * e_in[i][C - 1:C, :])
    ysum = head_sum(y)
    yc = [y[i] - ysum[i] * (1.0 / N) for i in rng]
    ysq = head_sum([yc[i] * yc[i] for i in rng])
    var = [ysq[i] * (1.0 / N) for i in rng]
    rksum = head_sum([r_l[i] * k2[i] * rk_ref[:, col[i]] for i in rng])
    bonus = [rksum[i] * v_l[i] for i in rng]
    for i in rng:
        yn = yc[i] * lax.rsqrt(var[i] + RWKV_LN_EPS) * lnw_ref[:, col[i]] + lnb_ref[:, col[i]]
        o_ref[seq[i], :, col[i]] = ((yn + bonus[i]) * g_seq[seq[i]][:, col[i]]).astype(o_ref.dtype)


def _rwkv(zr3, w0, wdu, a0, wau, wgu, k_k, k_a, r_k, ln_w, ln_b):
    B, T, _ = zr3.shape
    C = CHUNK
    W = RWKV_W
    GW = RWKV_GROUP * RWKV_N
    vec = lambda v: v.reshape(1, W)
    full = lambda shape: pl.BlockSpec(shape, lambda c: (0,) * len(shape))
    return pl.pallas_call(
        _rwkv_kernel,
        grid=(T // C,),
        in_specs=[pl.BlockSpec((B, C, W), lambda c: (0, c, 0)),
                  pl.BlockSpec((B, C, W), lambda c: (0, c, 1)),
                  pl.BlockSpec((B, C, W), lambda c: (0, c, 2)),
                  pl.BlockSpec((B, C, LORA_PAD), lambda c: (0, c, 3 * W // LORA_PAD)),
                  full((1, W)), full((DECAY_LORA, W)), full((1, W)), full((A_LORA, W)),
                  full((GATE_LORA, W)), full((1, W)), full((1, W)), full((1, W)),
                  full((1, W)), full((1, W))],
        out_specs=pl.BlockSpec((B, C, W), lambda c: (0, c, 0)),
        out_shape=jax.ShapeDtypeStruct((B, T, W), BF16),
        scratch_shapes=[pltpu.VMEM((B * RWKV_HEADS // RWKV_GROUP, GW, GW), F32)],
        compiler_params=pltpu.CompilerParams(dimension_semantics=("arbitrary",)),
        name="rwkv",
    )(zr3, zr3, zr3, zr3, vec(w0), wdu, vec(a0), wau, wgu, vec(k_k), vec(k_a), vec(r_k),
      vec(ln_w), vec(ln_b))


def _outproj_kernel(oa_ref, or_ref, x_ref, wo_ref, gt_ref, g_ref, sh_ref, sc_ref,
                    wrh_ref, wrl_ref, br_ref, x1_ref, h2_ref, lg_ref):
    mixed = _dot(oa_ref[...], wo_ref[0:Q_COLS, :]) + _dot(or_ref[...], wo_ref[Q_COLS:, :])
    x1 = x_ref[...] + gt_ref[0] * mixed
    x1_ref[...] = x1
    ms = jnp.mean(x1 * x1, axis=-1, keepdims=True)
    h2 = x1 * lax.rsqrt(ms + NORM_EPS) * g_ref[...] * (1.0 + sc_ref[0]) + sh_ref[0]
    h2_ref[...] = h2
    hh = h2.astype(BF16)
    hl = (h2 - hh.astype(F32)).astype(BF16)
    lg_ref[...] = (_dot(hh, wrh_ref[...]) + _dot(hl, wrh_ref[...]) + _dot(hh, wrl_ref[...])
                   + br_ref[...])


def _outproj(oa2, or2, x2, wo_bf, gt, g2, sh, sc, wr_hi, wr_lo, br, T):
    M, D = x2.shape
    tm = 256
    tps = T // tm
    bvec = pl.BlockSpec((1, 1, D), lambda i: (i // tps, 0, 0))
    vm = D * D * 2 * 2 + 2 * tm * D * (2 + 4 + 4 + 4) + tm * D * 16 + (8 << 20)
    return pl.pallas_call(
        _outproj_kernel,
        grid=(M // tm,),
        in_specs=[pl.BlockSpec((tm, Q_COLS), lambda i: (i, 0)),
                  pl.BlockSpec((tm, RWKV_W), lambda i: (i, 0)),
                  pl.BlockSpec((tm, D), lambda i: (i, 0)),
                  pl.BlockSpec((D, D), lambda i: (0, 0)),
                  bvec,
                  pl.BlockSpec((1, D), lambda i: (0, 0)),
                  bvec, bvec,
                  pl.BlockSpec((D, LANES), lambda i: (0, 0)),
                  pl.BlockSpec((D, LANES), lambda i: (0, 0)),
                  pl.BlockSpec((1, LANES), lambda i: (0, 0))],
        out_specs=[pl.BlockSpec((tm, D), lambda i: (i, 0)),
                   pl.BlockSpec((tm, D), lambda i: (i, 0)),
                   pl.BlockSpec((tm, LANES), lambda i: (i, 0))],
        out_shape=[jax.ShapeDtypeStruct((M, D), F32),
                   jax.ShapeDtypeStruct((M, D), F32),
                   jax.ShapeDtypeStruct((M, LANES), F32)],
        compiler_params=pltpu.CompilerParams(
            dimension_semantics=("arbitrary",), vmem_limit_bytes=_vmem_limit(vm)),
        name="outproj",
    )(oa2, or2, x2, wo_bf, gt, g2, sh, sc, wr_hi, wr_lo, br)


def _route_kernel(lg_ref, info_ref, cnt_ref, run_ref):
    i = pl.program_id(0)
    tm = lg_ref.shape[0]

    @pl.when(i == 0)
    def _():
        run_ref[...] = jnp.zeros_like(run_ref)

    lg = lg_ref[...]
    lane = lax.broadcasted_iota(I32, lg.shape, 1)
    gl = jnp.where(lane < N_GROUPS, lg, NEG_BIG)
    gmax = jnp.max(gl, axis=-1, keepdims=True)
    gsum = jnp.sum(jnp.exp(gl - gmax), axis=-1, keepdims=True)
    g_gate = 1.0 / gsum
    g_idx = jnp.min(jnp.where(gl == gmax, lane, LANES), axis=-1, keepdims=True)
    lo = N_GROUPS + EXPERTS_PER_GROUP * g_idx
    el = jnp.where((lane >= lo) & (lane < lo + EXPERTS_PER_GROUP), lg, NEG_BIG)
    e1max = jnp.max(el, axis=-1, keepdims=True)
    l1 = jnp.min(jnp.where(el == e1max, lane, LANES), axis=-1, keepdims=True)
    el2 = jnp.where(lane == l1, NEG_BIG, el)
    e2max = jnp.max(el2, axis=-1, keepdims=True)
    l2 = jnp.min(jnp.where(el2 == e2max, lane, LANES), axis=-1, keepdims=True)
    t2 = jnp.exp(e2max - e1max)
    w1 = g_gate / (1.0 + t2)
    w2 = g_gate * t2 / (1.0 + t2)
    ex1 = l1 - N_GROUPS
    ex2 = l2 - N_GROUPS

    oh1 = jnp.where(lane == ex1, 1.0, 0.0)
    oh2 = jnp.where(lane == ex2, 1.0, 0.0)
    ti = lax.broadcasted_iota(I32, (tm, tm), 0)
    si = lax.broadcasted_iota(I32, (tm, tm), 1)
    lower = jnp.where(si < ti, 1.0, 0.0).astype(BF16)
    pre1 = _dot(lower, oh1.astype(BF16))
    pre2 = _dot(lower, oh2.astype(BF16))
    cnt1 = jnp.sum(oh1, axis=0, keepdims=True)
    cnt2 = jnp.sum(oh2, axis=0, keepdims=True)
    run = run_ref[...]
    rank1 = jnp.sum(oh1 * (pre1 + run), axis=-1, keepdims=True)
    rank2 = jnp.sum(oh2 * (pre2 + run + cnt1), axis=-1, keepdims=True)
    run = run + cnt1 + cnt2
    run_ref[...] = run
    cnt_ref[...] = run

    info = jnp.where(lane == 0, ex1.astype(F32), 0.0)
    info = jnp.where(lane == 1, ex2.astype(F32), info)
    info = jnp.where(lane == 2, rank1, info)
    info = jnp.where(lane == 3, rank2, info)
    info = jnp.where(lane == 4, w1, info)
    info = jnp.where(lane == 5, w2, info)
    info_ref[...] = info


def _route(lg):
    M = lg.shape[0]
    tm = 256
    return pl.pallas_call(
        _route_kernel,
        grid=(M // tm,),
        in_specs=[pl.BlockSpec((tm, LANES), lambda i: (i, 0))],
        out_specs=[pl.BlockSpec((tm, LANES), lambda i: (i, 0)),
                   pl.BlockSpec((1, LANES), lambda i: (0, 0))],
        out_shape=[jax.ShapeDtypeStruct((M, LANES), F32),
                   jax.ShapeDtypeStruct((1, LANES), F32)],
        scratch_shapes=[pltpu.VMEM((1, LANES), F32)],
        compiler_params=pltpu.CompilerParams(dimension_semantics=("arbitrary",)),
        name="route",
    )(lg)


def _plan_kernel(info_ref, cnt_ref, slot_ref, meta_ref):
    cnt = cnt_ref[...]
    lane_r = lax.broadcasted_iota(I32, (1, LANES), 1)
    nblk = jnp.floor((cnt + (MOE_BLOCK - 1)) * (1.0 / MOE_BLOCK))
    ei = lax.broadcasted_iota(I32, (LANES, LANES), 0)
    ej = lax.broadcasted_iota(I32, (LANES, LANES), 1)
    upper = jnp.where(ei <= ej, 1.0, 0.0).astype(BF16)
    nb8 = jnp.broadcast_to(nblk, (8, LANES)).astype(BF16)
    bend = _dot(nb8, upper)[0:1, :]
    bstart = bend - nblk
    pstart = bstart * MOE_BLOCK

    info = info_ref[...]
    lane = lax.broadcasted_iota(I32, info.shape, 1)
    ex1 = info[:, 0:1].astype(I32)
    ex2 = info[:, 1:2].astype(I32)
    s1 = jnp.sum(jnp.where(lane == ex1, pstart, 0.0), axis=-1, keepdims=True) + info[:, 2:3]
    s2 = jnp.sum(jnp.where(lane == ex2, pstart, 0.0), axis=-1, keepdims=True) + info[:, 3:4]
    slot = jnp.where(lane == 0, s1, 0.0)
    slot = jnp.where(lane == 1, s2, slot)
    slot_ref[...] = slot.astype(I32)

    blk = lax.broadcasted_iota(I32, (LANES, 2 * LANES), 1).astype(F32)
    bend_col = jnp.sum(jnp.where(ei == ej, jnp.broadcast_to(bend, (LANES, LANES)), 0.0),
                       axis=-1, keepdims=True)
    erow = lax.broadcasted_iota(I32, (LANES, 2 * LANES), 0)
    hit = jnp.where((bend_col <= blk) & (erow < N_EXPERTS), 1.0, 0.0)
    bexp = jnp.minimum(jnp.sum(hit, axis=0, keepdims=True), N_EXPERTS - 1.0)
    n_used = jnp.max(jnp.where(lane_r < N_EXPERTS, bend, 0.0), axis=-1, keepdims=True)
    lastblk = jnp.where(nblk > 0, bend - 1.0, -1.0)
    r8 = lax.broadcasted_iota(I32, (8, 2 * LANES), 0)
    last2 = jnp.concatenate([lastblk, jnp.full((1, LANES), -1.0)], axis=1)
    meta = jnp.where(r8 == 0, jnp.broadcast_to(bexp, (8, 2 * LANES)), 0.0)
    meta = jnp.where(r8 == 1, jnp.broadcast_to(n_used, (8, 2 * LANES)), meta)
    meta = jnp.where(r8 == 2, jnp.broadcast_to(last2, (8, 2 * LANES)), meta)
    meta_ref[...] = meta.astype(I32)


def _plan(info, cnt):
    M = info.shape[0]
    tm = 256
    return pl.pallas_call(
        _plan_kernel,
        grid=(M // tm,),
        in_specs=[pl.BlockSpec((tm, LANES), lambda i: (i, 0)),
                  pl.BlockSpec((1, LANES), lambda i: (0, 0))],
        out_specs=[pl.BlockSpec((tm, LANES), lambda i: (i, 0)),
                   pl.BlockSpec((8, 2 * LANES), lambda i: (0, 0))],
        out_shape=[jax.ShapeDtypeStruct((M, LANES), I32),
                   jax.ShapeDtypeStruct((8, 2 * LANES), I32)],
        compiler_params=pltpu.CompilerParams(dimension_semantics=("arbitrary",)),
        name="plan",
    )(info, cnt)


def _slot_table_kernel(slots_ref, inv_ref, *, n_assign, n_slots):
    period = 2 * MOE_BLOCK

    def init(b, c):
        for j in range(period):
            inv_ref[b * period + j] = n_assign + ((j + MOE_BLOCK) % period)
        return c

    def put(a, c):
        inv_ref[slots_ref[a] + MOE_BLOCK] = a
        return c

    lax.fori_loop(0, n_slots // period, init, 0)
    lax.fori_loop(0, n_assign, put, 0, unroll=16)


def _slot_table(slots_km, n_blocks):
    n_assign = slots_km.shape[0]
    n_slots = (n_blocks + 2) * MOE_BLOCK
    assert n_slots % (2 * MOE_BLOCK) == 0
    gs = pltpu.PrefetchScalarGridSpec(
        num_scalar_prefetch=1, grid=(1,), in_specs=[],
        out_specs=pl.BlockSpec(memory_space=pltpu.SMEM))
    return pl.pallas_call(
        functools.partial(_slot_table_kernel, n_assign=n_assign, n_slots=n_slots),
        grid_spec=gs,
        out_shape=jax.ShapeDtypeStruct((n_slots,), I32),
        compiler_params=pltpu.CompilerParams(dimension_semantics=("arbitrary",)),
        name="slot_table",
    )(slots_km)


DMA_GROUP = 16


def _experts_kernel(bexp_ref, seg_ref, inv_ref, h2_ref, w1_ref, w3_ref, w2_ref, y2_ref,
                    xbuf, ybuf, w1buf, w3buf, w2buf, sem_x, sem_y, sem_w, par_ref, *, n_assign):
    i = pl.program_id(0)
    n_used = seg_ref[N_EXPERTS]
    n_tok = n_assign // 2
    B = MOE_BLOCK
    FF = w1buf.shape[2]
    D = w1buf.shape[1]

    def src_row(a):
        if n_tok & (n_tok - 1) == 0:
            return a & (n_tok - 1)
        return lax.rem(a, n_tok)

    def weight_copies(e, slot):
        return (pltpu.make_async_copy(w1_ref.at[e], w1buf.at[slot], sem_w.at[slot]),
                pltpu.make_async_copy(w3_ref.at[e], w3buf.at[slot], sem_w.at[slot]),
                pltpu.make_async_copy(w2_ref.at[e], w2buf.at[slot], sem_w.at[slot]))

    def start_weights(e, slot):
        for cp in weight_copies(e, slot):
            cp.start(priority=1)

    def prefetch_next_expert(e, slot):
        nb = seg_ref[e] + 1

        @pl.when(nb < n_used)
        def _():
            start_weights(bexp_ref[nb], slot)

    def gather_group(b, g):
        slot = lax.rem(b + 2, 2)
        for r in range(g * DMA_GROUP, (g + 1) * DMA_GROUP):
            a = inv_ref[(b + 1) * B + r]
            pltpu.make_async_copy(h2_ref.at[pl.ds(src_row(a), 1)], xbuf.at[slot, pl.ds(r, 1)],
                                  sem_x.at[slot]).start()

    def scatter_group(b, g):
        slot = lax.rem(b + 2, 2)
        for r in range(g * DMA_GROUP, (g + 1) * DMA_GROUP):
            a = inv_ref[(b + 1) * B + r]
            pltpu.make_async_copy(ybuf.at[slot, pl.ds(r, 1)], y2_ref.at[pl.ds(a, 1)],
                                  sem_y.at[slot]).start()

    def wait_block(buf, sem, b):
        slot = lax.rem(b + 2, 2)
        pltpu.make_async_copy(buf.at[slot], buf.at[slot], sem.at[slot]).wait()

    n_groups = B // DMA_GROUP

    @pl.when(i == 0)
    def _():
        ybuf[...] = jnp.zeros_like(ybuf)
        pltpu.make_async_copy(ybuf.at[0], y2_ref.at[pl.ds(n_assign, B)], sem_y.at[0]).start()
        par_ref[0] = 0
        e0 = bexp_ref[0]
        start_weights(e0, 0)
        prefetch_next_expert(e0, 1)
        for g in range(n_groups):
            gather_group(0, g)

    active = i < n_used

    @pl.when(active)
    def _():
        e = bexp_ref[i]
        first = jnp.logical_or(i == 0, e != bexp_ref[jnp.maximum(i - 1, 0)])

        @pl.when(jnp.logical_and(first, i > 0))
        def _():
            par_ref[0] = 1 - par_ref[0]

        @pl.when(first)
        def _():
            p = par_ref[0]
            for cp in weight_copies(e, p):
                cp.wait()

            @pl.when(i > 0)
            def _():
                prefetch_next_expert(e, 1 - p)

        p = par_ref[0]
        slot = lax.rem(i, 2)
        wait_block(xbuf, sem_x, i)
        wait_block(ybuf, sem_y, i)
        pending = [functools.partial(scatter_group, i - 1, g) for g in range(n_groups)]
        pending += [functools.partial(gather_group, i + 1, g) for g in range(n_groups)]
        n_chunks = 2 * (FF // 256) + D // 256
        for _ in range(max(len(pending) - n_chunks, 0)):
            pending.pop(0)()

        def after_chunk():
            if pending:
                pending.pop(0)()

        x = xbuf[slot].astype(BF16)
        h1c, h3c = [], []
        for c0 in range(0, FF, 256):
            h1c.append(_dot(x, w1buf[p, :, c0:c0 + 256]))
            after_chunk()
            h3c.append(_dot(x, w3buf[p, :, c0:c0 + 256]))
            after_chunk()
        h1 = jnp.concatenate(h1c, axis=1)
        h3 = jnp.concatenate(h3c, axis=1)
        hid = (h1 * (1.0 / (1.0 + jnp.exp(-h1))) * h3).astype(BF16)
        for c0 in range(0, D, 256):
            ybuf[slot, :, c0:c0 + 256] = _dot(hid, w2buf[p, :, c0:c0 + 256])
            after_chunk()
        while pending:
            pending.pop(0)()

    @pl.when(i == n_used)
    def _():
        wait_block(xbuf, sem_x, i)
        wait_block(ybuf, sem_y, i)
        for g in range(n_groups):
            scatter_group(i - 1, g)

    @pl.when(i == n_used + 1)
    def _():
        wait_block(ybuf, sem_y, i)


def _experts(bexp, seg_meta, inv, h2, w1, w3, w2, n_blocks):
    M, D = h2.shape
    FF = w1.shape[2]
    n_assign = 2 * M
    any_spec = pl.BlockSpec(memory_space=pl.ANY)
    gs = pltpu.PrefetchScalarGridSpec(
        num_scalar_prefetch=3,
        grid=(n_blocks + 2,),
        in_specs=[any_spec, any_spec, any_spec, any_spec],
        out_specs=any_spec,
        scratch_shapes=[pltpu.VMEM((2, MOE_BLOCK, D), F32), pltpu.VMEM((2, MOE_BLOCK, D), F32),
                        pltpu.VMEM((2, D, FF), F32), pltpu.VMEM((2, D, FF), F32),
                        pltpu.VMEM((2, FF, D), F32),
                        pltpu.SemaphoreType.DMA((2,)), pltpu.SemaphoreType.DMA((2,)),
                        pltpu.SemaphoreType.DMA((2,)), pltpu.SMEM((1,), I32)],
    )
    vm = 2 * 3 * D * FF * 4 + 4 * MOE_BLOCK * D * 4 + (12 << 20)
    return pl.pallas_call(
        functools.partial(_experts_kernel, n_assign=n_assign),
        grid_spec=gs,
        out_shape=jax.ShapeDtypeStruct((n_assign + 2 * MOE_BLOCK, D), F32),
        compiler_params=pltpu.CompilerParams(
            dimension_semantics=("arbitrary",), vmem_limit_bytes=_vmem_limit(vm),
            has_side_effects=True),
        name="experts",
    )(bexp, seg_meta, inv, h2, w1, w3, w2)


def _combine_kernel(x1_ref, info_ref, gt_ref, fg_ref, ya_ref, yb_ref, o_ref):
    info = info_ref[...]
    moe = info[:, 4:5] * ya_ref[...] + info[:, 5:6] * yb_ref[...]
    x = x1_ref[...] + gt_ref[0] * moe
    ms = jnp.mean(x * x, axis=-1, keepdims=True)
    o_ref[...] = x * lax.rsqrt(ms + NORM_EPS) * fg_ref[...]


def _combine(x1, info, gt, final_g, y2, T):
    M, D = x1.shape
    tm = 256
    tps = T // tm
    nt = M // tm
    vm = 2 * 4 * tm * D * 4 + tm * D * 16 + (8 << 20)
    return pl.pallas_call(
        _combine_kernel,
        grid=(nt,),
        in_specs=[pl.BlockSpec((tm, D), lambda i: (i, 0)),
                  pl.BlockSpec((tm, LANES), lambda i: (i, 0)),
                  pl.BlockSpec((1, 1, D), lambda i: (i // tps, 0, 0)),
                  pl.BlockSpec((1, D), lambda i: (0, 0)),
                  pl.BlockSpec((tm, D), lambda i: (i, 0)),
                  pl.BlockSpec((tm, D), lambda i: (nt + i, 0))],
        out_specs=pl.BlockSpec((tm, D), lambda i: (i, 0)),
        out_shape=jax.ShapeDtypeStruct((M, D), F32),
        compiler_params=pltpu.CompilerParams(
            dimension_semantics=("arbitrary",), vmem_limit_bytes=_vmem_limit(vm)),
        name="combine",
    )(x1, info, gt, final_g.reshape(1, D), y2, y2)


def _rope_tables(T):
    inv_freq = ROPE_THETA ** (-np.arange(0, HEAD_DIM, 2, dtype=np.float64) / HEAD_DIM)
    ang = np.arange(T, dtype=np.float64)[:, None] * inv_freq[None, :]
    cos, sin = np.cos(ang), np.sin(ang)
    cos_h = np.concatenate([cos, cos], axis=1)
    sin_h = np.concatenate([-sin, sin], axis=1)
    return (jnp.asarray(np.tile(cos_h, (1, 2)), F32), jnp.asarray(np.tile(sin_h, (1, 2)), F32))


def _layer(x, c, w_ada, b_ada, norm1_g, w_in, mu_shift, sinks, w0, w_decay_up, a0, w_a_up, w_g_up,
           k_k, k_a, r_k, ln_x_w, ln_x_b, w_o, norm2_g, w_rg, b_rg, w_re, b_re, w1, w3, w2, final_g,
           apply_final):
    B, T, D = x.shape
    M = B * T
    mod = _ada(c, w_ada, b_ada)
    sh1, sc1, gt1, sh2, sc2, gt2 = [m.reshape(B, 1, D) for m in jnp.split(mod, 6, axis=-1)]

    pad = RWKV_PAD - RWKV_COLS
    w_bf = jnp.concatenate([w_in.astype(BF16), jnp.zeros((D, pad), BF16)], axis=1)
    mu_pad = jnp.pad(mu_shift, (0, pad)).reshape(1, RWKV_PAD)
    cos_t, sin_t = _rope_tables(T)
    x2 = x.reshape(M, D)
    za, zr = _inproj(x2, norm1_g.reshape(1, D), sh1, sc1, w_bf, mu_pad, cos_t, sin_t, T)

    o_attn = _attention(za.reshape(B, T, ATTN_COLS), sinks)
    o_rwkv = _rwkv(zr.reshape(B, T, RWKV_PAD), w0, w_decay_up, a0, w_a_up, w_g_up, k_k, k_a,
                   r_k.reshape(-1), ln_x_w, ln_x_b)

    wr = jnp.concatenate([w_rg, w_re], axis=1)
    wr = jnp.pad(wr, ((0, 0), (0, LANES - wr.shape[1])))
    wr_hi = wr.astype(BF16)
    wr_lo = (wr - wr_hi.astype(F32)).astype(BF16)
    br = jnp.pad(jnp.concatenate([b_rg, b_re]), (0, LANES - N_GROUPS - N_EXPERTS)).reshape(1, LANES)
    x1, h2, lg = _outproj(o_attn.reshape(M, Q_COLS), o_rwkv.reshape(M, RWKV_W), x2, w_o.astype(BF16),
                          gt1, norm2_g.reshape(1, D), sh2, sc2, wr_hi, wr_lo, br, T)

    n_blocks = -(-(2 * M) // MOE_BLOCK) + N_EXPERTS
    info, cnt = _route(lg)
    slots, meta = _plan(info, cnt)
    slots_km = slots[:, :2].T.reshape(-1)
    seg_meta = jnp.concatenate([meta[2, :N_EXPERTS], meta[1, :1]])
    inv = _slot_table(slots_km, n_blocks)
    y2 = _experts(meta[0, :n_blocks + 2], seg_meta, inv, h2, w1, w3, w2, n_blocks)
    out = _combine(x1, info, gt2, final_g, y2, T)
    del apply_final
    return out.reshape(B, T, D)


def kernel(x, c, w_ada, b_ada, norm1_g, w_in, mu_shift, sinks, w0, w_decay_up, a0, w_a_up, w_g_up, k_k, k_a, r_k, ln_x_w, ln_x_b, w_o, norm2_g, w_router_group, b_router_group, w_router_expert, b_router_expert, w1, w3, w2, final_g):
    depth = w_ada.shape[0]
    assert depth == 1, "single-layer stack"
    l = 0
    return _layer(x, c, w_ada[l], b_ada[l], norm1_g[l], w_in[l], mu_shift[l], sinks[l], w0[l],
                  w_decay_up[l], a0[l], w_a_up[l], w_g_up[l], k_k[l], k_a[l], r_k[l], ln_x_w[l],
                  ln_x_b[l], w_o[l], norm2_g[l], w_router_group[l], b_router_group[l],
                  w_router_expert[l], b_router_expert[l], w1[l], w3[l], w2[l], final_g, True)
```

```python
import functools
import math

import jax
import jax.numpy as jnp
import numpy as np
from jax import lax
from jax.experimental import pallas as pl
from jax.experimental.pallas import tpu as pltpu

F32 = jnp.float32
BF16 = jnp.bfloat16
I32 = jnp.int32

LANES = 128
VMEM_BYTES_V7X = 64 * 1024 * 1024

HEAD_DIM = 64
ATTN_HEADS = 16
ATTN_KV_HEADS = 2
ATTN_GROUP = ATTN_HEADS // ATTN_KV_HEADS
WINDOW = 128
ROPE_THETA = 10000.0
RWKV_HEADS = 16
RWKV_N = 64
DECAY_LORA = 64
A_LORA = 64
GATE_LORA = 160
RWKV_LN_EPS = 64e-5
N_GROUPS = 8
EXPERTS_PER_GROUP = 8
N_EXPERTS = N_GROUPS * EXPERTS_PER_GROUP
EXPERT_FF = 512
MOE_BLOCK = 128
ROW_SLABS = 8
NORM_EPS = 1e-6

Q_COLS = ATTN_HEADS * HEAD_DIM
KV_COLS = ATTN_KV_HEADS * HEAD_DIM
ATTN_COLS = Q_COLS + 2 * KV_COLS
RWKV_W = RWKV_HEADS * RWKV_N
LORA_COLS = DECAY_LORA + A_LORA + GATE_LORA
LORA_PAD = 384
RWKV_COLS = 3 * RWKV_W + LORA_COLS
RWKV_PAD = 3 * RWKV_W + LORA_PAD
CHUNK = 64
RWKV_GROUP = 4
NEG_BIG = -1e30


def _vmem_limit(nbytes):
    return int(min(nbytes, VMEM_BYTES_V7X - 4 * 1024 * 1024))


def _dot(a, b):
    return jnp.dot(a, b, preferred_element_type=F32)


def _dot_nt(a, b):
    return lax.dot_general(a, b, (((1,), (1,)), ((), ())), preferred_element_type=F32)


def _dot_tn(a, b):
    return lax.dot_general(a, b, (((0,), (0,)), ((), ())), preferred_element_type=F32)


def _ada_kernel(c_ref, w_ref, b_ref, o_ref):
    c = c_ref[...]
    s = c * (1.0 / (1.0 + jnp.exp(-c)))
    o_ref[...] = _dot(s, w_ref[...]) + b_ref[...]


def _ada(c, w_ada, b_ada):
    B, D = c.shape
    N = w_ada.shape[1]
    tn = 1024
    cp = jnp.zeros((8, D), F32).at[:B].set(c)
    out = pl.pallas_call(
        _ada_kernel,
        grid=(N // tn,),
        in_specs=[pl.BlockSpec((8, D), lambda j: (0, 0)),
                  pl.BlockSpec((D, tn), lambda j: (0, j)),
                  pl.BlockSpec((1, tn), lambda j: (0, j))],
        out_specs=pl.BlockSpec((8, tn), lambda j: (0, j)),
        out_shape=jax.ShapeDtypeStruct((8, N), F32),
        compiler_params=pltpu.CompilerParams(
            dimension_semantics=("arbitrary",),
            vmem_limit_bytes=_vmem_limit(2 * D * tn * 4 + (8 << 20))),
        name="ada",
    )(cp, w_ada, b_ada.reshape(1, N))
    return out[:B]


def _rope(z, cos, sin):
    w = z.shape[-1]
    lane = lax.broadcasted_iota(I32, z.shape, 1)
    first_half = (lane % HEAD_DIM) < (HEAD_DIM // 2)
    partner = jnp.where(first_half, pltpu.roll(z, w - HEAD_DIM // 2, 1), pltpu.roll(z, HEAD_DIM // 2, 1))
    return z * cos + partner * sin


def _inproj_kernel(x_ref, g_ref, sh_ref, sc_ref, w_ref, mu_ref, cos_ref, sin_ref,
                   za_ref, zr_ref, carry_ref, *, tiles_per_seq):
    i = pl.program_id(0)
    tm = x_ref.shape[0]
    x = x_ref[...]
    ms = jnp.mean(x * x, axis=-1, keepdims=True)
    y = x * lax.rsqrt(ms + NORM_EPS) * g_ref[...]
    h = (y * (1.0 + sc_ref[0]) + sh_ref[0]).astype(BF16)

    cos2 = jnp.concatenate([cos_ref[...], cos_ref[...]], axis=1)
    sin2 = jnp.concatenate([sin_ref[...], sin_ref[...]], axis=1)
    for c0 in range(0, ATTN_COLS, 256):
        z = _dot(h, w_ref[:, c0:c0 + 256])
        if c0 < Q_COLS:
            z = _rope(z, cos2, sin2)
        else:
            z = jnp.concatenate([_rope(z[:, :KV_COLS], cos_ref[...], sin_ref[...]), z[:, KV_COLS:]], axis=1)
        za_ref[:, c0:c0 + 256] = z.astype(za_ref.dtype)

    first = (i % tiles_per_seq) == 0
    row = lax.broadcasted_iota(I32, (tm, 1), 0)
    c0 = 0
    while c0 < RWKV_PAD:
        n = min(512, RWKV_PAD - c0)
        z = _dot(h, w_ref[:, ATTN_COLS + c0:ATTN_COLS + c0 + n])
        prev_last = jnp.where(first, 0.0, carry_ref[7:8, c0:c0 + n])
        z_prev = jnp.where(row == 0, prev_last, pltpu.roll(z, 1, 0))
        carry_ref[:, c0:c0 + n] = z[tm - 8:tm, :]
        zr_ref[:, c0:c0 + n] = z + (z_prev - z) * mu_ref[:, c0:c0 + n]
        c0 += n


def _inproj(x2, g, sh, sc, w_bf, mu_pad, cos_t, sin_t, T):
    M, D = x2.shape
    tm = 256
    tps = T // tm
    NW = w_bf.shape[1]
    kern = functools.partial(_inproj_kernel, tiles_per_seq=tps)
    vm = (D * NW * 2 + 2 * tm * D * 4 + 2 * tm * ATTN_COLS * 2 + 2 * tm * RWKV_PAD * 4
          + 4 * tm * 512 * 4 + tm * D * 8 + (8 << 20))
    return pl.pallas_call(
        kern,
        grid=(M // tm,),
        in_specs=[pl.BlockSpec((tm, D), lambda i: (i, 0)),
                  pl.BlockSpec((1, D), lambda i: (0, 0)),
                  pl.BlockSpec((1, 1, D), lambda i: (i // tps, 0, 0)),
                  pl.BlockSpec((1, 1, D), lambda i: (i // tps, 0, 0)),
                  pl.BlockSpec((D, NW), lambda i: (0, 0), pipeline_mode=pl.Buffered(1)),
                  pl.BlockSpec((1, RWKV_PAD), lambda i: (0, 0)),
                  pl.BlockSpec((tm, 2 * HEAD_DIM), lambda i: (i % tps, 0)),
                  pl.BlockSpec((tm, 2 * HEAD_DIM), lambda i: (i % tps, 0))],
        out_specs=[pl.BlockSpec((tm, ATTN_COLS), lambda i: (i, 0)),
                   pl.BlockSpec((tm, RWKV_PAD), lambda i: (i, 0))],
        out_shape=[jax.ShapeDtypeStruct((M, ATTN_COLS), BF16),
                   jax.ShapeDtypeStruct((M, RWKV_PAD), F32)],
        scratch_shapes=[pltpu.VMEM((8, RWKV_PAD), F32)],
        compiler_params=pltpu.CompilerParams(
            dimension_semantics=("arbitrary",), vmem_limit_bytes=_vmem_limit(vm)),
        name="inproj",
    )(x2, g, sh, sc, w_bf, mu_pad, cos_t, sin_t)


def _attn_kernel(sink_ref, q_ref, kc_ref, kp_ref, vc_ref, vp_ref, o_ref):
    n = pl.program_id(1)
    blk = q_ref.shape[1]
    row = lax.broadcasted_iota(I32, (blk, blk), 0)
    col = lax.broadcasted_iota(I32, (blk, blk), 1)
    mask = jnp.concatenate([(col > row) & (n > 0), col <= row], axis=1)
    scale = 1.0 / math.sqrt(HEAD_DIM)
    outs = []
    for kvh in range(ATTN_KV_HEADS):
        ks = slice(kvh * HEAD_DIM, (kvh + 1) * HEAD_DIM)
        kmat = jnp.concatenate([kp_ref[0, :, ks], kc_ref[0, :, ks]], axis=0)
        vmat = jnp.concatenate([vp_ref[0, :, ks], vc_ref[0, :, ks]], axis=0)
        for g in range(ATTN_GROUP):
            hd = kvh * ATTN_GROUP + g
            qh = q_ref[0, :, hd * HEAD_DIM:(hd + 1) * HEAD_DIM]
            s = _dot_nt(qh, kmat) * scale
            s = jnp.where(mask, s, NEG_BIG)
            sink = sink_ref[hd]
            m = jnp.maximum(jnp.max(s, axis=-1, keepdims=True), sink)
            p = jnp.exp(s - m)
            denom = jnp.sum(p, axis=-1, keepdims=True) + jnp.exp(sink - m)
            o = _dot(p.astype(BF16), vmat)
            outs.append(o / denom)
    o_ref[0] = jnp.concatenate(outs, axis=1).astype(o_ref.dtype)


def _attention(za3, sinks):
    B, T, _ = za3.shape
    nb = T // WINDOW
    kcol = Q_COLS // KV_COLS
    prev = lambda b, n, s: (b, jnp.maximum(n - 1, 0), kcol)
    prev_v = lambda b, n, s: (b, jnp.maximum(n - 1, 0), kcol + 1)
    gs = pltpu.PrefetchScalarGridSpec(
        num_scalar_prefetch=1,
        grid=(B, nb),
        in_specs=[pl.BlockSpec((1, WINDOW, Q_COLS), lambda b, n, s: (b, n, 0)),
                  pl.BlockSpec((1, WINDOW, KV_COLS), lambda b, n, s: (b, n, kcol)),
                  pl.BlockSpec((1, WINDOW, KV_COLS), prev),
                  pl.BlockSpec((1, WINDOW, KV_COLS), lambda b, n, s: (b, n, kcol + 1)),
                  pl.BlockSpec((1, WINDOW, KV_COLS), prev_v)],
        out_specs=pl.BlockSpec((1, WINDOW, Q_COLS), lambda b, n, s: (b, n, 0)),
    )
    return pl.pallas_call(
        _attn_kernel,
        grid_spec=gs,
        out_shape=jax.ShapeDtypeStruct((B, T, Q_COLS), BF16),
        compiler_params=pltpu.CompilerParams(dimension_semantics=("arbitrary", "arbitrary")),
        name="attn",
    )(sinks, za3, za3, za3, za3, za3)


def _rwkv_kernel(r_ref, k_ref, v_ref, lora_ref, w0_ref, wdu_ref, a0_ref, wau_ref, wgu_ref,
                 kk_ref, ka_ref, rk_ref, lnw_ref, lnb_ref, o_ref, s_ref):
    c = pl.program_id(0)
    nseq = r_ref.shape[0]
    C = r_ref.shape[1]
    N = RWKV_N
    G = RWKV_GROUP
    GW = G * N
    ng = RWKV_HEADS // G

    @pl.when(c == 0)
    def _():
        s_ref[...] = jnp.zeros_like(s_ref)

    ti = lax.broadcasted_iota(I32, (C, C), 0)
    si = lax.broadcasted_iota(I32, (C, C), 1)
    tril = jnp.where(si <= ti, 1.0, 0.0).astype(BF16)
    a_seq, g_seq, e_in_seq, e_ex_seq, e_neg_seq = [], [], [], [], []
    for b in range(nseq):
        lora = lora_ref[b]
        wd = lora[:, 0:DECAY_LORA]
        ad = lora[:, DECAY_LORA:DECAY_LORA + A_LORA]
        gd = lora[:, DECAY_LORA + A_LORA:LORA_COLS]
        wlin = w0_ref[...] + _dot(jnp.tanh(wd), wdu_ref[...])
        neg = -wlin
        softplus = jnp.maximum(neg, 0.0) + jnp.log(1.0 + jnp.exp(-jnp.abs(neg)))
        w = -softplus - 0.5
        logdec = -jnp.exp(w)
        a_seq.append(1.0 / (1.0 + jnp.exp(-(a0_ref[...] + _dot(ad, wau_ref[...])))))
        g_seq.append(_dot(1.0 / (1.0 + jnp.exp(-gd)), wgu_ref[...]))
        ld_hi = logdec.astype(BF16)
        ld_lo = (logdec - ld_hi.astype(F32)).astype(BF16)
        cum = _dot(tril, ld_hi) + _dot(tril, ld_lo)
        e_in_seq.append(jnp.exp(cum))
        e_ex_seq.append(jnp.exp(cum - logdec))
        e_neg_seq.append(jnp.exp(-cum))

    lane_head = lax.broadcasted_iota(I32, (1, GW), 1) // N
    t_row = lax.broadcasted_iota(I32, (C, GW), 0)
    s_lane = lax.broadcasted_iota(I32, (C, GW), 1) % N
    strict = s_lane < t_row
    incl = s_lane <= t_row
    bi = lax.broadcasted_iota(I32, (GW, GW), 0)
    bj = lax.broadcasted_iota(I32, (GW, GW), 1)
    same_head = (bi // N) == (bj // N)
    ones_bd = jnp.where(same_head, 1.0, 0.0).astype(BF16)
    eye = jnp.where(s_lane == t_row, 1.0, 0.0)

    def expand(xc):
        return jnp.concatenate([jnp.where(lane_head == h, xc, 0.0) for h in range(G)], axis=0)

    def head_sum(xs):
        s = _dot(jnp.concatenate(xs, axis=0).astype(BF16), ones_bd)
        return [s[i * C:(i + 1) * C] for i in range(len(xs))]

    seq = [b for b in range(nseq) for _ in range(ng)]
    col = [slice(gi * GW, (gi + 1) * GW) for _ in range(nseq) for gi in range(ng)]
    rng = range(nseq * ng)
    r_l = [r_ref[seq[i], :, col[i]] for i in rng]
    k_l = [k_ref[seq[i], :, col[i]] for i in rng]
    v_l = [v_ref[seq[i], :, col[i]] for i in rng]
    a_l = [a_seq[seq[i]][:, col[i]] for i in rng]
    e_in = [e_in_seq[seq[i]][:, col[i]] for i in rng]
    e_neg = [e_neg_seq[seq[i]][:, col[i]] for i in rng]
    kk0 = [k_l[i] * kk_ref[:, col[i]] for i in rng]
    nrm2 = head_sum([kk0[i] * kk0[i] for i in rng])
    kk = [kk0[i] / jnp.maximum(jnp.sqrt(nrm2[i]), 1e-12) for i in rng]
    k2 = [k_l[i] * (1.0 + (a_l[i] - 1.0) * ka_ref[:, col[i]]) for i in rng]
    bt = [kk[i] * a_l[i] * e_neg[i] for i in rng]
    kt = [k2[i] * e_neg[i] for i in rng]
    left = [jnp.concatenate([-kk[i] * e_ex_seq[seq[i]][:, col[i]], r_l[i] * e_in[i]], axis=0).astype(BF16)
            for i in rng]
    right = [jnp.concatenate([bt[i], kt[i]], axis=0).astype(BF16) for i in rng]
    zed = [jnp.concatenate([expand(bt[i]), expand(kt[i])], axis=0).astype(BF16) for i in rng]
    big = [_dot_nt(left[i], zed[i]) for i in rng]
    s_old = [s_ref[i] for i in rng]
    ls = [_dot_nt(left[i], s_old[i].astype(BF16)) for i in rng]
    v_bd = [expand(v_l[i]).astype(BF16) for i in rng]
    pw = [jnp.where(strict, big[i][:C, :G * C], 0.0) for i in rng]
    tinv = [eye + pw[i] for i in rng]
    pw_bd = [expand(pw[i]).astype(BF16) for i in rng]
    x = [ls[i][:C] + _dot(jnp.where(strict, big[i][:C, G * C:], 0.0).astype(BF16), v_bd[i]) for i in rng]
    span = 2
    while span < C:
        pw = [_dot(pw[i].astype(BF16), pw_bd[i]) for i in rng]
        pw_bd = [expand(pw[i]).astype(BF16) for i in rng]
        tinv = [tinv[i] + _dot(tinv[i].astype(BF16), pw_bd[i]) for i in rng]
        span *= 2
    u = [_dot(tinv[i].astype(BF16), expand(x[i]).astype(BF16)) for i in rng]
    a_r = [jnp.concatenate([jnp.where(incl, big[i][C:, :G * C], 0.0),
                            jnp.where(incl, big[i][C:, G * C:], 0.0)], axis=1).astype(BF16) for i in rng]
    y = [ls[i][C:] + _dot(a_r[i], jnp.concatenate([expand(u[i]).astype(BF16), v_bd[i]], axis=0)) for i in rng]
    uv = [jnp.concatenate([u[i], v_l[i]], axis=0).astype(BF16) for i in rng]
    for i in rng:
        s_ref[i] = ((s_old[i] + jnp.where(same_head, _dot_tn(uv[i], right[i]), 0.0))
                    * e_in[i][C - 1:C, :])
    ysum = head_sum(y)
    yc = [y[i] - ysum[i] * (1.0 / N) for i in rng]
    ysq = head_sum([yc[i] * yc[i] for i in rng])
    var = [ysq[i] * (1.0 / N) for i in rng]
    rksum = head_sum([r_l[i] * k2[i] * rk_ref[:, col[i]] for i in rng])
    bonus = [rksum[i] * v_l[i] for i in rng]
    for i in rng:
        yn = yc[i] * lax.rsqrt(var[i] + RWKV_LN_EPS) * lnw_ref[:, col[i]] + lnb_ref[:, col[i]]
        o_ref[seq[i], :, col[i]] = ((yn + bonus[i]) * g_seq[seq[i]][:, col[i]]).astype(o_ref.dtype)


def _rwkv(zr3, w0, wdu, a0, wau, wgu, k_k, k_a, r_k, ln_w, ln_b):
    B, T, _ = zr3.shape
    C = CHUNK
    W = RWKV_W
    GW = RWKV_GROUP * RWKV_N
    vec = lambda v: v.reshape(1, W)
    full = lambda shape: pl.BlockSpec(shape, lambda c: (0,) * len(shape))
    return pl.pallas_call(
        _rwkv_kernel,
        grid=(T // C,),
        in_specs=[pl.BlockSpec((B, C, W), lambda c: (0, c, 0)),
                  pl.BlockSpec((B, C, W), lambda c: (0, c, 1)),
                  pl.BlockSpec((B, C, W), lambda c: (0, c, 2)),
                  pl.BlockSpec((B, C, LORA_PAD), lambda c: (0, c, 3 * W // LORA_PAD)),
                  full((1, W)), full((DECAY_LORA, W)), full((1, W)), full((A_LORA, W)),
                  full((GATE_LORA, W)), full((1, W)), full((1, W)), full((1, W)),
                  full((1, W)), full((1, W))],
        out_specs=pl.BlockSpec((B, C, W), lambda c: (0, c, 0)),
        out_shape=jax.ShapeDtypeStruct((B, T, W), BF16),
        scratch_shapes=[pltpu.VMEM((B * RWKV_HEADS // RWKV_GROUP, GW, GW), F32)],
        compiler_params=pltpu.CompilerParams(dimension_semantics=("arbitrary",)),
        name="rwkv",
    )(zr3, zr3, zr3, zr3, vec(w0), wdu, vec(a0), wau, wgu, vec(k_k), vec(k_a), vec(r_k),
      vec(ln_w), vec(ln_b))


def _outproj_kernel(oa_ref, or_ref, x_ref, wo_ref, gt_ref, g_ref, sh_ref, sc_ref,
                    wrh_ref, wrl_ref, br_ref, x1_ref, h2_ref, lg_ref):
    mixed = _dot(oa_ref[...], wo_ref[0:Q_COLS, :]) + _dot(or_ref[...], wo_ref[Q_COLS:, :])
    x1 = x_ref[...] + gt_ref[0] * mixed
    x1_ref[...] = x1
    ms = jnp.mean(x1 * x1, axis=-1, keepdims=True)
    h2 = x1 * lax.rsqrt(ms + NORM_EPS) * g_ref[...] * (1.0 + sc_ref[0]) + sh_ref[0]
    for j in range(ROW_SLABS):
        w = h2_ref.shape[2]
        h2_ref[:, j, :] = h2[:, j * w:(j + 1) * w]
    hh = h2.astype(BF16)
    hl = (h2 - hh.astype(F32)).astype(BF16)
    lg_ref[...] = (_dot(hh, wrh_ref[...]) + _dot(hl, wrh_ref[...]) + _dot(hh, wrl_ref[...])
                   + br_ref[...])


def _outproj(oa2, or2, x2, wo_bf, gt, g2, sh, sc, wr_hi, wr_lo, br, T):
    M, D = x2.shape
    tm = 256
    tps = T // tm
    bvec = pl.BlockSpec((1, 1, D), lambda i: (i // tps, 0, 0))
    vm = D * D * 2 * 2 + 2 * tm * D * (2 + 4 + 4 + 4) + tm * D * 16 + (8 << 20)
    return pl.pallas_call(
        _outproj_kernel,
        grid=(M // tm,),
        in_specs=[pl.BlockSpec((tm, Q_COLS), lambda i: (i, 0)),
                  pl.BlockSpec((tm, RWKV_W), lambda i: (i, 0)),
                  pl.BlockSpec((tm, D), lambda i: (i, 0)),
                  pl.BlockSpec((D, D), lambda i: (0, 0)),
                  bvec,
                  pl.BlockSpec((1, D), lambda i: (0, 0)),
                  bvec, bvec,
                  pl.BlockSpec((D, LANES), lambda i: (0, 0)),
                  pl.BlockSpec((D, LANES), lambda i: (0, 0)),
                  pl.BlockSpec((1, LANES), lambda i: (0, 0))],
        out_specs=[pl.BlockSpec((tm, D), lambda i: (i, 0)),
                   pl.BlockSpec((tm, ROW_SLABS, D // ROW_SLABS), lambda i: (i, 0, 0)),
                   pl.BlockSpec((tm, LANES), lambda i: (i, 0))],
        out_shape=[jax.ShapeDtypeStruct((M, D), F32),
                   jax.ShapeDtypeStruct((M, ROW_SLABS, D // ROW_SLABS), F32),
                   jax.ShapeDtypeStruct((M, LANES), F32)],
        compiler_params=pltpu.CompilerParams(
            dimension_semantics=("arbitrary",), vmem_limit_bytes=_vmem_limit(vm)),
        name="outproj",
    )(oa2, or2, x2, wo_bf, gt, g2, sh, sc, wr_hi, wr_lo, br)


def _route_kernel(lg_ref, info_ref, cnt_ref, run_ref):
    i = pl.program_id(0)
    tm = lg_ref.shape[0]

    @pl.when(i == 0)
    def _():
        run_ref[...] = jnp.zeros_like(run_ref)

    lg = lg_ref[...]
    lane = lax.broadcasted_iota(I32, lg.shape, 1)
    gl = jnp.where(lane < N_GROUPS, lg, NEG_BIG)
    gmax = jnp.max(gl, axis=-1, keepdims=True)
    gsum = jnp.sum(jnp.exp(gl - gmax), axis=-1, keepdims=True)
    g_gate = 1.0 / gsum
    g_idx = jnp.min(jnp.where(gl == gmax, lane, LANES), axis=-1, keepdims=True)
    lo = N_GROUPS + EXPERTS_PER_GROUP * g_idx
    el = jnp.where((lane >= lo) & (lane < lo + EXPERTS_PER_GROUP), lg, NEG_BIG)
    e1max = jnp.max(el, axis=-1, keepdims=True)
    l1 = jnp.min(jnp.where(el == e1max, lane, LANES), axis=-1, keepdims=True)
    el2 = jnp.where(lane == l1, NEG_BIG, el)
    e2max = jnp.max(el2, axis=-1, keepdims=True)
    l2 = jnp.min(jnp.where(el2 == e2max, lane, LANES), axis=-1, keepdims=True)
    t2 = jnp.exp(e2max - e1max)
    w1 = g_gate / (1.0 + t2)
    w2 = g_gate * t2 / (1.0 + t2)
    ex1 = l1 - N_GROUPS
    ex2 = l2 - N_GROUPS

    oh1 = jnp.where(lane == ex1, 1.0, 0.0)
    oh2 = jnp.where(lane == ex2, 1.0, 0.0)
    ti = lax.broadcasted_iota(I32, (tm, tm), 0)
    si = lax.broadcasted_iota(I32, (tm, tm), 1)
    lower = jnp.where(si < ti, 1.0, 0.0).astype(BF16)
    pre1 = _dot(lower, oh1.astype(BF16))
    pre2 = _dot(lower, oh2.astype(BF16))
    cnt1 = jnp.sum(oh1, axis=0, keepdims=True)
    cnt2 = jnp.sum(oh2, axis=0, keepdims=True)
    run = run_ref[...]
    rank1 = jnp.sum(oh1 * (pre1 + run), axis=-1, keepdims=True)
    rank2 = jnp.sum(oh2 * (pre2 + run + cnt1), axis=-1, keepdims=True)
    run = run + cnt1 + cnt2
    run_ref[...] = run
    cnt_ref[...] = run

    info = jnp.where(lane == 0, ex1.astype(F32), 0.0)
    info = jnp.where(lane == 1, ex2.astype(F32), info)
    info = jnp.where(lane == 2, rank1, info)
    info = jnp.where(lane == 3, rank2, info)
    info = jnp.where(lane == 4, w1, info)
    info = jnp.where(lane == 5, w2, info)
    info_ref[...] = info


def _route(lg):
    M = lg.shape[0]
    tm = 256
    return pl.pallas_call(
        _route_kernel,
        grid=(M // tm,),
        in_specs=[pl.BlockSpec((tm, LANES), lambda i: (i, 0))],
        out_specs=[pl.BlockSpec((tm, LANES), lambda i: (i, 0)),
                   pl.BlockSpec((1, LANES), lambda i: (0, 0))],
        out_shape=[jax.ShapeDtypeStruct((M, LANES), F32),
                   jax.ShapeDtypeStruct((1, LANES), F32)],
        scratch_shapes=[pltpu.VMEM((1, LANES), F32)],
        compiler_params=pltpu.CompilerParams(dimension_semantics=("arbitrary",)),
        name="route",
    )(lg)


def _plan_kernel(info_ref, cnt_ref, slot_ref, meta_ref):
    cnt = cnt_ref[...]
    lane_r = lax.broadcasted_iota(I32, (1, LANES), 1)
    nblk = jnp.floor((cnt + (MOE_BLOCK - 1)) * (1.0 / MOE_BLOCK))
    ei = lax.broadcasted_iota(I32, (LANES, LANES), 0)
    ej = lax.broadcasted_iota(I32, (LANES, LANES), 1)
    upper = jnp.where(ei <= ej, 1.0, 0.0).astype(BF16)
    nb8 = jnp.broadcast_to(nblk, (8, LANES)).astype(BF16)
    bend = _dot(nb8, upper)[0:1, :]
    bstart = bend - nblk
    pstart = bstart * MOE_BLOCK

    info = info_ref[...]
    lane = lax.broadcasted_iota(I32, info.shape, 1)
    ex1 = info[:, 0:1].astype(I32)
    ex2 = info[:, 1:2].astype(I32)
    s1 = jnp.sum(jnp.where(lane == ex1, pstart, 0.0), axis=-1, keepdims=True) + info[:, 2:3]
    s2 = jnp.sum(jnp.where(lane == ex2, pstart, 0.0), axis=-1, keepdims=True) + info[:, 3:4]
    slot = jnp.where(lane == 0, s1, 0.0)
    slot = jnp.where(lane == 1, s2, slot)
    slot_ref[...] = slot.astype(I32)

    blk = lax.broadcasted_iota(I32, (LANES, 2 * LANES), 1).astype(F32)
    bend_col = jnp.sum(jnp.where(ei == ej, jnp.broadcast_to(bend, (LANES, LANES)), 0.0),
                       axis=-1, keepdims=True)
    erow = lax.broadcasted_iota(I32, (LANES, 2 * LANES), 0)
    hit = jnp.where((bend_col <= blk) & (erow < N_EXPERTS), 1.0, 0.0)
    bexp = jnp.minimum(jnp.sum(hit, axis=0, keepdims=True), N_EXPERTS - 1.0)
    n_used = jnp.max(jnp.where(lane_r < N_EXPERTS, bend, 0.0), axis=-1, keepdims=True)
    lastblk = jnp.where(nblk > 0, bend - 1.0, -1.0)
    r8 = lax.broadcasted_iota(I32, (8, 2 * LANES), 0)
    last2 = jnp.concatenate([lastblk, jnp.full((1, LANES), -1.0)], axis=1)
    meta = jnp.where(r8 == 0, jnp.broadcast_to(bexp, (8, 2 * LANES)), 0.0)
    meta = jnp.where(r8 == 1, jnp.broadcast_to(n_used, (8, 2 * LANES)), meta)
    meta = jnp.where(r8 == 2, jnp.broadcast_to(last2, (8, 2 * LANES)), meta)
    meta_ref[...] = meta.astype(I32)


def _plan(info, cnt):
    M = info.shape[0]
    tm = 256
    return pl.pallas_call(
        _plan_kernel,
        grid=(M // tm,),
        in_specs=[pl.BlockSpec((tm, LANES), lambda i: (i, 0)),
                  pl.BlockSpec((1, LANES), lambda i: (0, 0))],
        out_specs=[pl.BlockSpec((tm, LANES), lambda i: (i, 0)),
                   pl.BlockSpec((8, 2 * LANES), lambda i: (0, 0))],
        out_shape=[jax.ShapeDtypeStruct((M, LANES), I32),
                   jax.ShapeDtypeStruct((8, 2 * LANES), I32)],
        compiler_params=pltpu.CompilerParams(dimension_semantics=("arbitrary",)),
        name="plan",
    )(info, cnt)


ROW_DEPTH = 3
DMA_GROUP = 16


def _slot_table_kernel(slots_ref, inv_ref, *, n_assign, n_slots):
    def init(b, c):
        base = n_assign + lax.rem(b + ROW_DEPTH - 1, ROW_DEPTH) * MOE_BLOCK
        for j in range(MOE_BLOCK):
            inv_ref[b * MOE_BLOCK + j] = base + j
        return c

    def put(a, c):
        inv_ref[slots_ref[a] + MOE_BLOCK] = a
        return c

    lax.fori_loop(0, n_slots // MOE_BLOCK, init, 0)
    lax.fori_loop(0, n_assign, put, 0, unroll=16)


def _slot_table(slots_km, n_blocks):
    n_assign = slots_km.shape[0]
    n_slots = (n_blocks + ROW_DEPTH) * MOE_BLOCK
    gs = pltpu.PrefetchScalarGridSpec(
        num_scalar_prefetch=1, grid=(1,), in_specs=[],
        out_specs=pl.BlockSpec(memory_space=pltpu.SMEM))
    return pl.pallas_call(
        functools.partial(_slot_table_kernel, n_assign=n_assign, n_slots=n_slots),
        grid_spec=gs,
        out_shape=jax.ShapeDtypeStruct((n_slots,), I32),
        compiler_params=pltpu.CompilerParams(dimension_semantics=("arbitrary",)),
        name="slot_table",
    )(slots_km)


def _experts_kernel(bexp_ref, seg_ref, inv_ref, h2_ref, w1_ref, w3_ref, w2_ref, y2_ref,
                    xbuf, ybuf, w1buf, w3buf, w2buf, sem_x, sem_y, sem_w, par_ref, *, n_assign):
    i = pl.program_id(0)
    n_used = seg_ref[N_EXPERTS]
    n_tok = n_assign // 2
    B = MOE_BLOCK
    FF = w1buf.shape[2]
    D = w1buf.shape[1]
    ND = ROW_DEPTH

    def src_row(a):
        if n_tok & (n_tok - 1) == 0:
            return a & (n_tok - 1)
        return lax.rem(a, n_tok)

    def weight_copies(e, slot):
        return (pltpu.make_async_copy(w1_ref.at[e], w1buf.at[slot], sem_w.at[slot]),
                pltpu.make_async_copy(w3_ref.at[e], w3buf.at[slot], sem_w.at[slot]),
                pltpu.make_async_copy(w2_ref.at[e], w2buf.at[slot], sem_w.at[slot]))

    def start_weights(e, slot):
        for cp in weight_copies(e, slot):
            cp.start(priority=1)

    def prefetch_next_expert(e, slot):
        nb = seg_ref[e] + 1

        @pl.when(nb < n_used)
        def _():
            start_weights(bexp_ref[nb], slot)

    def gather_group(b, g):
        slot = lax.rem(b + ND, ND)
        for r in range(g * DMA_GROUP, (g + 1) * DMA_GROUP):
            a = inv_ref[(b + 1) * B + r]
            pltpu.make_async_copy(h2_ref.at[pl.ds(src_row(a), 1)], xbuf.at[slot, pl.ds(r, 1)],
                                  sem_x.at[slot]).start()

    def scatter_group(b, g):
        slot = lax.rem(b + ND, ND)
        for r in range(g * DMA_GROUP, (g + 1) * DMA_GROUP):
            a = inv_ref[(b + 1) * B + r]
            pltpu.make_async_copy(ybuf.at[slot, pl.ds(r, 1)], y2_ref.at[pl.ds(a, 1)],
                                  sem_y.at[slot]).start()

    def wait_block(buf, sem, b):
        slot = lax.rem(b + ND, ND)
        pltpu.make_async_copy(buf.at[slot], buf.at[slot], sem.at[slot]).wait()

    n_groups = B // DMA_GROUP

    @pl.when(i == 0)
    def _():
        ybuf[...] = jnp.zeros_like(ybuf)
        for s in range(ND - 1):
            pltpu.make_async_copy(ybuf.at[s], y2_ref.at[pl.ds(n_assign + s * B, B)], sem_y.at[s]).start()
        par_ref[0] = 0
        e0 = bexp_ref[0]
        start_weights(e0, 0)
        prefetch_next_expert(e0, 1)
        for b in range(ND - 1):
            for g in range(n_groups):
                gather_group(b, g)

    active = i < n_used

    @pl.when(active)
    def _():
        e = bexp_ref[i]
        first = jnp.logical_or(i == 0, e != bexp_ref[jnp.maximum(i - 1, 0)])

        @pl.when(jnp.logical_and(first, i > 0))
        def _():
            par_ref[0] = 1 - par_ref[0]

        @pl.when(first)
        def _():
            p = par_ref[0]
            for cp in weight_copies(e, p):
                cp.wait()

            @pl.when(i > 0)
            def _():
                prefetch_next_expert(e, 1 - p)

        p = par_ref[0]
        slot = lax.rem(i, ND)
        wait_block(xbuf, sem_x, i)
        wait_block(ybuf, sem_y, i)
        pending = [functools.partial(scatter_group, i - 1, g) for g in range(n_groups)]
        pending += [functools.partial(gather_group, i + ND - 1, g) for g in range(n_groups)]
        n_chunks = 2 * (FF // 256) + ROW_SLABS
        for _ in range(max(len(pending) - n_chunks, 0)):
            pending.pop(0)()

        def after_chunk():
            if pending:
                pending.pop(0)()

        x = jnp.concatenate([xbuf[slot, :, j, :] for j in range(ROW_SLABS)], axis=1).astype(BF16)
        sw = D // ROW_SLABS
        h1c, h3c = [], []
        for c0 in range(0, FF, 256):
            h1c.append(_dot(x, w1buf[p, :, c0:c0 + 256]))
            after_chunk()
            h3c.append(_dot(x, w3buf[p, :, c0:c0 + 256]))
            after_chunk()
        h1 = jnp.concatenate(h1c, axis=1)
        h3 = jnp.concatenate(h3c, axis=1)
        hid = (h1 * (1.0 / (1.0 + jnp.exp(-h1))) * h3).astype(BF16)
        for j in range(ROW_SLABS):
            ybuf[slot, :, j, :] = _dot(hid, w2buf[p, :, j * sw:(j + 1) * sw])
            after_chunk()
        while pending:
            pending.pop(0)()

    @pl.when(i == n_used)
    def _():
        wait_block(xbuf, sem_x, i)
        wait_block(ybuf, sem_y, i)
        for g in range(n_groups):
            scatter_group(i - 1, g)

    @pl.when(jnp.logical_and(i > n_used, i < n_used + ND - 1))
    def _():
        wait_block(xbuf, sem_x, i)
        wait_block(ybuf, sem_y, i)

    @pl.when(i == n_used + ND - 1)
    def _():
        wait_block(ybuf, sem_y, i)


def _experts(bexp, seg_meta, inv, h2, w1, w3, w2, n_blocks):
    M, DJ, SW = h2.shape
    D = DJ * SW
    FF = w1.shape[2]
    n_assign = 2 * M
    any_spec = pl.BlockSpec(memory_space=pl.ANY)
    row_buf = pltpu.VMEM((ROW_DEPTH, MOE_BLOCK, DJ, SW), F32)
    gs = pltpu.PrefetchScalarGridSpec(
        num_scalar_prefetch=3,
        grid=(n_blocks + ROW_DEPTH,),
        in_specs=[any_spec, any_spec, any_spec, any_spec],
        out_specs=any_spec,
        scratch_shapes=[row_buf, row_buf,
                        pltpu.VMEM((2, D, FF), F32), pltpu.VMEM((2, D, FF), F32),
                        pltpu.VMEM((2, FF, D), F32),
                        pltpu.SemaphoreType.DMA((ROW_DEPTH,)), pltpu.SemaphoreType.DMA((ROW_DEPTH,)),
                        pltpu.SemaphoreType.DMA((2,)), pltpu.SMEM((1,), I32)],
    )
    vm = 2 * 3 * D * FF * 4 + 2 * ROW_DEPTH * MOE_BLOCK * D * 4 + (12 << 20)
    return pl.pallas_call(
        functools.partial(_experts_kernel, n_assign=n_assign),
        grid_spec=gs,
        out_shape=jax.ShapeDtypeStruct((n_assign + ROW_DEPTH * MOE_BLOCK, DJ, SW), F32),
        compiler_params=pltpu.CompilerParams(
            dimension_semantics=("arbitrary",), vmem_limit_bytes=_vmem_limit(vm),
            has_side_effects=True),
        name="experts",
    )(bexp, seg_meta, inv, h2, w1, w3, w2)


def _combine_kernel(x1_ref, info_ref, gt_ref, fg_ref, ya_ref, yb_ref, o_ref):
    info = info_ref[...]
    nj = ya_ref.shape[1]
    ya = jnp.concatenate([ya_ref[:, j, :] for j in range(nj)], axis=1)
    yb = jnp.concatenate([yb_ref[:, j, :] for j in range(nj)], axis=1)
    moe = info[:, 4:5] * ya + info[:, 5:6] * yb
    x = x1_ref[...] + gt_ref[0] * moe
    ms = jnp.mean(x * x, axis=-1, keepdims=True)
    o_ref[...] = x * lax.rsqrt(ms + NORM_EPS) * fg_ref[...]


def _combine(x1, info, gt, final_g, y2, T):
    M, D = x1.shape
    tm = 256
    tps = T // tm
    nt = M // tm
    vm = 2 * 4 * tm * D * 4 + tm * D * 16 + (8 << 20)
    return pl.pallas_call(
        _combine_kernel,
        grid=(nt,),
        in_specs=[pl.BlockSpec((tm, D), lambda i: (i, 0)),
                  pl.BlockSpec((tm, LANES), lambda i: (i, 0)),
                  pl.BlockSpec((1, 1, D), lambda i: (i // tps, 0, 0)),
                  pl.BlockSpec((1, D), lambda i: (0, 0)),
                  pl.BlockSpec((tm, ROW_SLABS, D // ROW_SLABS), lambda i: (i, 0, 0)),
                  pl.BlockSpec((tm, ROW_SLABS, D // ROW_SLABS), lambda i: (nt + i, 0, 0))],
        out_specs=pl.BlockSpec((tm, D), lambda i: (i, 0)),
        out_shape=jax.ShapeDtypeStruct((M, D), F32),
        compiler_params=pltpu.CompilerParams(
            dimension_semantics=("arbitrary",), vmem_limit_bytes=_vmem_limit(vm)),
        name="combine",
    )(x1, info, gt, final_g.reshape(1, D), y2, y2)


def _rope_tables(T):
    inv_freq = ROPE_THETA ** (-np.arange(0, HEAD_DIM, 2, dtype=np.float64) / HEAD_DIM)
    ang = np.arange(T, dtype=np.float64)[:, None] * inv_freq[None, :]
    cos, sin = np.cos(ang), np.sin(ang)
    cos_h = np.concatenate([cos, cos], axis=1)
    sin_h = np.concatenate([-sin, sin], axis=1)
    return (jnp.asarray(np.tile(cos_h, (1, 2)), F32), jnp.asarray(np.tile(sin_h, (1, 2)), F32))


def _layer(x, c, w_ada, b_ada, norm1_g, w_in, mu_shift, sinks, w0, w_decay_up, a0, w_a_up, w_g_up,
           k_k, k_a, r_k, ln_x_w, ln_x_b, w_o, norm2_g, w_rg, b_rg, w_re, b_re, w1, w3, w2, final_g,
           apply_final):
    B, T, D = x.shape
    M = B * T
    mod = _ada(c, w_ada, b_ada)
    sh1, sc1, gt1, sh2, sc2, gt2 = [m.reshape(B, 1, D) for m in jnp.split(mod, 6, axis=-1)]

    pad = RWKV_PAD - RWKV_COLS
    w_bf = jnp.concatenate([w_in.astype(BF16), jnp.zeros((D, pad), BF16)], axis=1)
    mu_pad = jnp.pad(mu_shift, (0, pad)).reshape(1, RWKV_PAD)
    cos_t, sin_t = _rope_tables(T)
    x2 = x.reshape(M, D)
    za, zr = _inproj(x2, norm1_g.reshape(1, D), sh1, sc1, w_bf, mu_pad, cos_t, sin_t, T)

    o_attn = _attention(za.reshape(B, T, ATTN_COLS), sinks)
    o_rwkv = _rwkv(zr.reshape(B, T, RWKV_PAD), w0, w_decay_up, a0, w_a_up, w_g_up, k_k, k_a,
                   r_k.reshape(-1), ln_x_w, ln_x_b)

    wr = jnp.concatenate([w_rg, w_re], axis=1)
    wr = jnp.pad(wr, ((0, 0), (0, LANES - wr.shape[1])))
    wr_hi = wr.astype(BF16)
    wr_lo = (wr - wr_hi.astype(F32)).astype(BF16)
    br = jnp.pad(jnp.concatenate([b_rg, b_re]), (0, LANES - N_GROUPS - N_EXPERTS)).reshape(1, LANES)
    x1, h2, lg = _outproj(o_attn.reshape(M, Q_COLS), o_rwkv.reshape(M, RWKV_W), x2, w_o.astype(BF16),
                          gt1, norm2_g.reshape(1, D), sh2, sc2, wr_hi, wr_lo, br, T)

    n_blocks = -(-(2 * M) // MOE_BLOCK) + N_EXPERTS
    info, cnt = _route(lg)
    slots, meta = _plan(info, cnt)
    slots_km = slots[:, :2].T.reshape(-1)
    seg_meta = jnp.concatenate([meta[2, :N_EXPERTS], meta[1, :1]])
    inv = _slot_table(slots_km, n_blocks)
    y2 = _experts(meta[0, :n_blocks + ROW_DEPTH], seg_meta, inv, h2, w1, w3, w2, n_blocks)
    out = _combine(x1, info, gt2, final_g, y2, T)
    del apply_final
    return out.reshape(B, T, D)


def kernel(x, c, w_ada, b_ada, norm1_g, w_in, mu_shift, sinks, w0, w_decay_up, a0, w_a_up, w_g_up, k_k, k_a, r_k, ln_x_w, ln_x_b, w_o, norm2_g, w_router_group, b_router_group, w_router_expert, b_router_expert, w1, w3, w2, final_g):
    depth = w_ada.shape[0]
    assert depth == 1, "single-layer stack"
    l = 0
    return _layer(x, c, w_ada[l], b_ada[l], norm1_g[l], w_in[l], mu_shift[l], sinks[l], w0[l],
                  w_decay_up[l], a0[l], w_a_up[l], w_g_up[l], k_k[l], k_a[l], r_k[l], ln_x_w[l],
                  ln_x_b[l], w_o[l], norm2_g[l], w_router_group[l], b_router_group[l],
                  w_router_expert[l], b_router_expert[l], w1[l], w3[l], w2[l], final_g, True)
```

```python
import functools
import math

import jax
import jax.numpy as jnp
import numpy as np
from jax import lax
from jax.experimental import pallas as pl
from jax.experimental.pallas import tpu as pltpu

F32 = jnp.float32
BF16 = jnp.bfloat16
I32 = jnp.int32

LANES = 128
VMEM_BYTES_V7X = 64 * 1024 * 1024

HEAD_DIM = 64
ATTN_HEADS = 16
ATTN_KV_HEADS = 2
ATTN_GROUP = ATTN_HEADS // ATTN_KV_HEADS
WINDOW = 128
ROPE_THETA = 10000.0
RWKV_HEADS = 16
RWKV_N = 64
DECAY_LORA = 64
A_LORA = 64
GATE_LORA = 160
RWKV_LN_EPS = 64e-5
N_GROUPS = 8
EXPERTS_PER_GROUP = 8
N_EXPERTS = N_GROUPS * EXPERTS_PER_GROUP
EXPERT_FF = 512
MOE_BLOCK = 128
ROW_SLABS = 8
NORM_EPS = 1e-6

Q_COLS = ATTN_HEADS * HEAD_DIM
KV_COLS = ATTN_KV_HEADS * HEAD_DIM
ATTN_COLS = Q_COLS + 2 * KV_COLS
RWKV_W = RWKV_HEADS * RWKV_N
LORA_COLS = DECAY_LORA + A_LORA + GATE_LORA
LORA_PAD = 384
RWKV_COLS = 3 * RWKV_W + LORA_COLS
RWKV_PAD = 3 * RWKV_W + LORA_PAD
CHUNK = 64
RWKV_GROUP = 4
NEG_BIG = -1e30


def _vmem_limit(nbytes):
    return int(min(nbytes, VMEM_BYTES_V7X - 4 * 1024 * 1024))


def _dot(a, b):
    return jnp.dot(a, b, preferred_element_type=F32)


def _dot_nt(a, b):
    return lax.dot_general(a, b, (((1,), (1,)), ((), ())), preferred_element_type=F32)


def _dot_tn(a, b):
    return lax.dot_general(a, b, (((0,), (0,)), ((), ())), preferred_element_type=F32)


def _ada_kernel(c_ref, w_ref, b_ref, o_ref):
    c = c_ref[...]
    s = c * (1.0 / (1.0 + jnp.exp(-c)))
    o_ref[...] = _dot(s, w_ref[...]) + b_ref[...]


def _ada(c, w_ada, b_ada):
    B, D = c.shape
    N = w_ada.shape[1]
    tn = 1024
    cp = jnp.zeros((8, D), F32).at[:B].set(c)
    out = pl.pallas_call(
        _ada_kernel,
        grid=(N // tn,),
        in_specs=[pl.BlockSpec((8, D), lambda j: (0, 0)),
                  pl.BlockSpec((D, tn), lambda j: (0, j)),
                  pl.BlockSpec((1, tn), lambda j: (0, j))],
        out_specs=pl.BlockSpec((8, tn), lambda j: (0, j)),
        out_shape=jax.ShapeDtypeStruct((8, N), F32),
        compiler_params=pltpu.CompilerParams(
            dimension_semantics=("arbitrary",),
            vmem_limit_bytes=_vmem_limit(2 * D * tn * 4 + (8 << 20))),
        name="ada",
    )(cp, w_ada, b_ada.reshape(1, N))
    return out[:B]


def _rope(z, cos, sin):
    w = z.shape[-1]
    lane = lax.broadcasted_iota(I32, z.shape, 1)
    first_half = (lane % HEAD_DIM) < (HEAD_DIM // 2)
    partner = jnp.where(first_half, pltpu.roll(z, w - HEAD_DIM // 2, 1), pltpu.roll(z, HEAD_DIM // 2, 1))
    return z * cos + partner * sin


def _inproj_kernel(x_ref, g_ref, sh_ref, sc_ref, w_ref, mu_ref, cos_ref, sin_ref,
                   za_ref, zr_ref, carry_ref, *, tiles_per_seq):
    i = pl.program_id(0)
    tm = x_ref.shape[0]
    x = x_ref[...]
    ms = jnp.mean(x * x, axis=-1, keepdims=True)
    y = x * lax.rsqrt(ms + NORM_EPS) * g_ref[...]
    h = (y * (1.0 + sc_ref[0]) + sh_ref[0]).astype(BF16)

    cos2 = jnp.concatenate([cos_ref[...], cos_ref[...]], axis=1)
    sin2 = jnp.concatenate([sin_ref[...], sin_ref[...]], axis=1)
    for c0 in range(0, ATTN_COLS, 256):
        z = _dot(h, w_ref[:, c0:c0 + 256])
        if c0 < Q_COLS:
            z = _rope(z, cos2, sin2)
        else:
            z = jnp.concatenate([_rope(z[:, :KV_COLS], cos_ref[...], sin_ref[...]), z[:, KV_COLS:]], axis=1)
        za_ref[:, c0:c0 + 256] = z.astype(za_ref.dtype)

    first = (i % tiles_per_seq) == 0
    row = lax.broadcasted_iota(I32, (tm, 1), 0)
    c0 = 0
    while c0 < RWKV_PAD:
        n = min(512, RWKV_PAD - c0)
        z = _dot(h, w_ref[:, ATTN_COLS + c0:ATTN_COLS + c0 + n])
        prev_last = jnp.where(first, 0.0, carry_ref[7:8, c0:c0 + n])
        z_prev = jnp.where(row == 0, prev_last, pltpu.roll(z, 1, 0))
        carry_ref[:, c0:c0 + n] = z[tm - 8:tm, :]
        zr_ref[:, c0:c0 + n] = z + (z_prev - z) * mu_ref[:, c0:c0 + n]
        c0 += n


def _inproj(x2, g, sh, sc, w_bf, mu_pad, cos_t, sin_t, T):
    M, D = x2.shape
    tm = 256
    tps = T // tm
    NW = w_bf.shape[1]
    kern = functools.partial(_inproj_kernel, tiles_per_seq=tps)
    vm = (D * NW * 2 + 2 * tm * D * 4 + 2 * tm * ATTN_COLS * 2 + 2 * tm * RWKV_PAD * 4
          + 4 * tm * 512 * 4 + tm * D * 8 + (8 << 20))
    return pl.pallas_call(
        kern,
        grid=(M // tm,),
        in_specs=[pl.BlockSpec((tm, D), lambda i: (i, 0)),
                  pl.BlockSpec((1, D), lambda i: (0, 0)),
                  pl.BlockSpec((1, 1, D), lambda i: (i // tps, 0, 0)),
                  pl.BlockSpec((1, 1, D), lambda i: (i // tps, 0, 0)),
                  pl.BlockSpec((D, NW), lambda i: (0, 0), pipeline_mode=pl.Buffered(1)),
                  pl.BlockSpec((1, RWKV_PAD), lambda i: (0, 0)),
                  pl.BlockSpec((tm, 2 * HEAD_DIM), lambda i: (i % tps, 0)),
                  pl.BlockSpec((tm, 2 * HEAD_DIM), lambda i: (i % tps, 0))],
        out_specs=[pl.BlockSpec((tm, ATTN_COLS), lambda i: (i, 0)),
                   pl.BlockSpec((tm, RWKV_PAD), lambda i: (i, 0))],
        out_shape=[jax.ShapeDtypeStruct((M, ATTN_COLS), BF16),
                   jax.ShapeDtypeStruct((M, RWKV_PAD), F32)],
        scratch_shapes=[pltpu.VMEM((8, RWKV_PAD), F32)],
        compiler_params=pltpu.CompilerParams(
            dimension_semantics=("arbitrary",), vmem_limit_bytes=_vmem_limit(vm)),
        name="inproj",
    )(x2, g, sh, sc, w_bf, mu_pad, cos_t, sin_t)


def _attn_kernel(sink_ref, q_ref, kc_ref, kp_ref, vc_ref, vp_ref, o_ref):
    n = pl.program_id(1)
    blk = q_ref.shape[1]
    row = lax.broadcasted_iota(I32, (blk, blk), 0)
    col = lax.broadcasted_iota(I32, (blk, blk), 1)
    mask = jnp.concatenate([(col > row) & (n > 0), col <= row], axis=1)
    scale = 1.0 / math.sqrt(HEAD_DIM)
    outs = []
    for kvh in range(ATTN_KV_HEADS):
        ks = slice(kvh * HEAD_DIM, (kvh + 1) * HEAD_DIM)
        kmat = jnp.concatenate([kp_ref[0, :, ks], kc_ref[0, :, ks]], axis=0)
        vmat = jnp.concatenate([vp_ref[0, :, ks], vc_ref[0, :, ks]], axis=0)
        for g in range(ATTN_GROUP):
            hd = kvh * ATTN_GROUP + g
            qh = q_ref[0, :, hd * HEAD_DIM:(hd + 1) * HEAD_DIM]
            s = _dot_nt(qh, kmat) * scale
            s = jnp.where(mask, s, NEG_BIG)
            sink = sink_ref[hd]
            m = jnp.maximum(jnp.max(s, axis=-1, keepdims=True), sink)
            p = jnp.exp(s - m)
            denom = jnp.sum(p, axis=-1, keepdims=True) + jnp.exp(sink - m)
            o = _dot(p.astype(BF16), vmat)
            outs.append(o / denom)
    o_ref[0] = jnp.concatenate(outs, axis=1).astype(o_ref.dtype)


def _attention(za3, sinks):
    B, T, _ = za3.shape
    nb = T // WINDOW
    kcol = Q_COLS // KV_COLS
    prev = lambda b, n, s: (b, jnp.maximum(n - 1, 0), kcol)
    prev_v = lambda b, n, s: (b, jnp.maximum(n - 1, 0), kcol + 1)
    gs = pltpu.PrefetchScalarGridSpec(
        num_scalar_prefetch=1,
        grid=(B, nb),
        in_specs=[pl.BlockSpec((1, WINDOW, Q_COLS), lambda b, n, s: (b, n, 0)),
                  pl.BlockSpec((1, WINDOW, KV_COLS), lambda b, n, s: (b, n, kcol)),
                  pl.BlockSpec((1, WINDOW, KV_COLS), prev),
                  pl.BlockSpec((1, WINDOW, KV_COLS), lambda b, n, s: (b, n, kcol + 1)),
                  pl.BlockSpec((1, WINDOW, KV_COLS), prev_v)],
        out_specs=pl.BlockSpec((1, WINDOW, Q_COLS), lambda b, n, s: (b, n, 0)),
    )
    return pl.pallas_call(
        _attn_kernel,
        grid_spec=gs,
        out_shape=jax.ShapeDtypeStruct((B, T, Q_COLS), BF16),
        compiler_params=pltpu.CompilerParams(dimension_semantics=("arbitrary", "arbitrary")),
        name="attn",
    )(sinks, za3, za3, za3, za3, za3)


def _rwkv_kernel(r_ref, k_ref, v_ref, lora_ref, w0_ref, wdu_ref, a0_ref, wau_ref, wgu_ref,
                 kk_ref, ka_ref, rk_ref, lnw_ref, lnb_ref, o_ref, s_ref):
    c = pl.program_id(0)
    nseq = r_ref.shape[0]
    C = r_ref.shape[1]
    N = RWKV_N
    G = RWKV_GROUP
    GW = G * N
    ng = RWKV_HEADS // G

    @pl.when(c == 0)
    def _():
        s_ref[...] = jnp.zeros_like(s_ref)

    ti = lax.broadcasted_iota(I32, (C, C), 0)
    si = lax.broadcasted_iota(I32, (C, C), 1)
    tril = jnp.where(si <= ti, 1.0, 0.0).astype(BF16)
    a_seq, g_seq, e_in_seq, e_ex_seq, e_neg_seq = [], [], [], [], []
    for b in range(nseq):
        lora = lora_ref[b]
        wd = lora[:, 0:DECAY_LORA]
        ad = lora[:, DECAY_LORA:DECAY_LORA + A_LORA]
        gd = lora[:, DECAY_LORA + A_LORA:LORA_COLS]
        wlin = w0_ref[...] + _dot(jnp.tanh(wd), wdu_ref[...])
        neg = -wlin
        softplus = jnp.maximum(neg, 0.0) + jnp.log(1.0 + jnp.exp(-jnp.abs(neg)))
        w = -softplus - 0.5
        logdec = -jnp.exp(w)
        a_seq.append(1.0 / (1.0 + jnp.exp(-(a0_ref[...] + _dot(ad, wau_ref[...])))))
        g_seq.append(_dot(1.0 / (1.0 + jnp.exp(-gd)), wgu_ref[...]))
        ld_hi = logdec.astype(BF16)
        ld_lo = (logdec - ld_hi.astype(F32)).astype(BF16)
        cum = _dot(tril, ld_hi) + _dot(tril, ld_lo)
        e_in_seq.append(jnp.exp(cum))
        e_ex_seq.append(jnp.exp(cum - logdec))
        e_neg_seq.append(jnp.exp(-cum))

    lane_head = lax.broadcasted_iota(I32, (1, GW), 1) // N
    t_row = lax.broadcasted_iota(I32, (C, GW), 0)
    s_lane = lax.broadcasted_iota(I32, (C, GW), 1) % N
    strict = s_lane < t_row
    incl = s_lane <= t_row
    bi = lax.broadcasted_iota(I32, (GW, GW), 0)
    bj = lax.broadcasted_iota(I32, (GW, GW), 1)
    same_head = (bi // N) == (bj // N)
    ones_bd = jnp.where(same_head, 1.0, 0.0).astype(BF16)
    eye = jnp.where(s_lane == t_row, 1.0, 0.0)

    def expand(xc):
        return jnp.concatenate([jnp.where(lane_head == h, xc, 0.0) for h in range(G)], axis=0)

    def head_sum(xs):
        s = _dot(jnp.concatenate(xs, axis=0).astype(BF16), ones_bd)
        return [s[i * C:(i + 1) * C] for i in range(len(xs))]

    seq = [b for b in range(nseq) for _ in range(ng)]
    col = [slice(gi * GW, (gi + 1) * GW) for _ in range(nseq) for gi in range(ng)]
    rng = range(nseq * ng)
    r_l = [r_ref[seq[i], :, col[i]] for i in rng]
    k_l = [k_ref[seq[i], :, col[i]] for i in rng]
    v_l = [v_ref[seq[i], :, col[i]] for i in rng]
    a_l = [a_seq[seq[i]][:, col[i]] for i in rng]
    e_in = [e_in_seq[seq[i]][:, col[i]] for i in rng]
    e_neg = [e_neg_seq[seq[i]][:, col[i]] for i in rng]
    kk0 = [k_l[i] * kk_ref[:, col[i]] for i in rng]
    nrm2 = head_sum([kk0[i] * kk0[i] for i in rng])
    kk = [kk0[i] / jnp.maximum(jnp.sqrt(nrm2[i]), 1e-12) for i in rng]
    k2 = [k_l[i] * (1.0 + (a_l[i] - 1.0) * ka_ref[:, col[i]]) for i in rng]
    bt = [kk[i] * a_l[i] * e_neg[i] for i in rng]
    kt = [k2[i] * e_neg[i] for i in rng]
    left = [jnp.concatenate([-kk[i] * e_ex_seq[seq[i]][:, col[i]], r_l[i] * e_in[i]], axis=0).astype(BF16)
            for i in rng]
    right = [jnp.concatenate([bt[i], kt[i]], axis=0).astype(BF16) for i in rng]
    zed = [jnp.concatenate([expand(bt[i]), expand(kt[i])], axis=0).astype(BF16) for i in rng]
    big = [_dot_nt(left[i], zed[i]) for i in rng]
    s_old = [s_ref[i] for i in rng]
    ls = [_dot_nt(left[i], s_old[i].astype(BF16)) for i in rng]
    v_bd = [expand(v_l[i]).astype(BF16) for i in rng]
    pw = [jnp.where(strict, big[i][:C, :G * C], 0.0) for i in rng]
    tinv = [eye + pw[i] for i in rng]
    pw_bd = [expand(pw[i]).astype(BF16) for i in rng]
    x = [ls[i][:C] + _dot(jnp.where(strict, big[i][:C, G * C:], 0.0).astype(BF16), v_bd[i]) for i in rng]
    span = 2
    while span < C:
        pw = [_dot(pw[i].astype(BF16), pw_bd[i]) for i in rng]
        pw_bd = [expand(pw[i]).astype(BF16) for i in rng]
        tinv = [tinv[i] + _dot(tinv[i].astype(BF16), pw_bd[i]) for i in rng]
        span *= 2
    u = [_dot(tinv[i].astype(BF16), expand(x[i]).astype(BF16)) for i in rng]
    a_r = [jnp.concatenate([jnp.where(incl, big[i][C:, :G * C], 0.0),
                            jnp.where(incl, big[i][C:, G * C:], 0.0)], axis=1).astype(BF16) for i in rng]
    y = [ls[i][C:] + _dot(a_r[i], jnp.concatenate([expand(u[i]).astype(BF16), v_bd[i]], axis=0)) for i in rng]
    uv = [jnp.concatenate([u[i], v_l[i]], axis=0).astype(BF16) for i in rng]
    for i in rng:
        s_ref[i] = ((s_old[i] + jnp.where(same_head, _dot_tn(uv[i], right[i]), 0.0))
                    * e_in[i][C - 1:C, :])
    ysum = head_sum(y)
    yc = [y[i] - ysum[i] * (1.0 / N) for i in rng]
    ysq = head_sum([yc[i] * yc[i] for i in rng])
    var = [ysq[i] * (1.0 / N) for i in rng]
    rksum = head_sum([r_l[i] * k2[i] * rk_ref[:, col[i]] for i in rng])
    bonus = [rksum[i] * v_l[i] for i in rng]
    for i in rng:
        yn = yc[i] * lax.rsqrt(var[i] + RWKV_LN_EPS) * lnw_ref[:, col[i]] + lnb_ref[:, col[i]]
        o_ref[seq[i], :, col[i]] = ((yn + bonus[i]) * g_seq[seq[i]][:, col[i]]).astype(o_ref.dtype)


def _rwkv(zr3, w0, wdu, a0, wau, wgu, k_k, k_a, r_k, ln_w, ln_b):
    B, T, _ = zr3.shape
    C = CHUNK
    W = RWKV_W
    GW = RWKV_GROUP * RWKV_N
    vec = lambda v: v.reshape(1, W)
    full = lambda shape: pl.BlockSpec(shape, lambda c: (0,) * len(shape))
    return pl.pallas_call(
        _rwkv_kernel,
        grid=(T // C,),
        in_specs=[pl.BlockSpec((B, C, W), lambda c: (0, c, 0)),
                  pl.BlockSpec((B, C, W), lambda c: (0, c, 1)),
                  pl.BlockSpec((B, C, W), lambda c: (0, c, 2)),
                  pl.BlockSpec((B, C, LORA_PAD), lambda c: (0, c, 3 * W // LORA_PAD)),
                  full((1, W)), full((DECAY_LORA, W)), full((1, W)), full((A_LORA, W)),
                  full((GATE_LORA, W)), full((1, W)), full((1, W)), full((1, W)),
                  full((1, W)), full((1, W))],
        out_specs=pl.BlockSpec((B, C, W), lambda c: (0, c, 0)),
        out_shape=jax.ShapeDtypeStruct((B, T, W), BF16),
        scratch_shapes=[pltpu.VMEM((B * RWKV_HEADS // RWKV_GROUP, GW, GW), F32)],
        compiler_params=pltpu.CompilerParams(dimension_semantics=("arbitrary",)),
        name="rwkv",
    )(zr3, zr3, zr3, zr3, vec(w0), wdu, vec(a0), wau, wgu, vec(k_k), vec(k_a), vec(r_k),
      vec(ln_w), vec(ln_b))


def _outproj_kernel(oa_ref, or_ref, x_ref, wo_ref, gt_ref, g_ref, sh_ref, sc_ref,
                    wrh_ref, wrl_ref, br_ref, x1_ref, h2_ref, lg_ref):
    mixed = _dot(oa_ref[...], wo_ref[0:Q_COLS, :]) + _dot(or_ref[...], wo_ref[Q_COLS:, :])
    x1 = x_ref[...] + gt_ref[0] * mixed
    x1_ref[...] = x1
    ms = jnp.mean(x1 * x1, axis=-1, keepdims=True)
    h2 = x1 * lax.rsqrt(ms + NORM_EPS) * g_ref[...] * (1.0 + sc_ref[0]) + sh_ref[0]
    for j in range(ROW_SLABS):
        w = h2_ref.shape[2]
        h2_ref[:, j, :] = h2[:, j * w:(j + 1) * w]
    hh = h2.astype(BF16)
    hl = (h2 - hh.astype(F32)).astype(BF16)
    lg_ref[...] = (_dot(hh, wrh_ref[...]) + _dot(hl, wrh_ref[...]) + _dot(hh, wrl_ref[...])
                   + br_ref[...])


def _outproj(oa2, or2, x2, wo_bf, gt, g2, sh, sc, wr_hi, wr_lo, br, T):
    M, D = x2.shape
    tm = 256
    tps = T // tm
    bvec = pl.BlockSpec((1, 1, D), lambda i: (i // tps, 0, 0))
    vm = D * D * 2 + 2 * tm * D * (2 + 4 + 4 + 4) + tm * D * 16 + (8 << 20)
    return pl.pallas_call(
        _outproj_kernel,
        grid=(M // tm,),
        in_specs=[pl.BlockSpec((tm, Q_COLS), lambda i: (i, 0)),
                  pl.BlockSpec((tm, RWKV_W), lambda i: (i, 0)),
                  pl.BlockSpec((tm, D), lambda i: (i, 0)),
                  pl.BlockSpec((D, D), lambda i: (0, 0), pipeline_mode=pl.Buffered(1)),
                  bvec,
                  pl.BlockSpec((1, D), lambda i: (0, 0)),
                  bvec, bvec,
                  pl.BlockSpec((D, LANES), lambda i: (0, 0)),
                  pl.BlockSpec((D, LANES), lambda i: (0, 0)),
                  pl.BlockSpec((1, LANES), lambda i: (0, 0))],
        out_specs=[pl.BlockSpec((tm, D), lambda i: (i, 0)),
                   pl.BlockSpec((tm, ROW_SLABS, D // ROW_SLABS), lambda i: (i, 0, 0)),
                   pl.BlockSpec((tm, LANES), lambda i: (i, 0))],
        out_shape=[jax.ShapeDtypeStruct((M, D), F32),
                   jax.ShapeDtypeStruct((M, ROW_SLABS, D // ROW_SLABS), F32),
                   jax.ShapeDtypeStruct((M, LANES), F32)],
        compiler_params=pltpu.CompilerParams(
            dimension_semantics=("arbitrary",), vmem_limit_bytes=_vmem_limit(vm)),
        name="outproj",
    )(oa2, or2, x2, wo_bf, gt, g2, sh, sc, wr_hi, wr_lo, br)


def _route_kernel(lg_ref, info_ref, cnt_ref, run_ref):
    i = pl.program_id(0)
    tm = lg_ref.shape[0]

    @pl.when(i == 0)
    def _():
        run_ref[...] = jnp.zeros_like(run_ref)

    lg = lg_ref[...]
    lane = lax.broadcasted_iota(I32, lg.shape, 1)
    gl = jnp.where(lane < N_GROUPS, lg, NEG_BIG)
    gmax = jnp.max(gl, axis=-1, keepdims=True)
    gsum = jnp.sum(jnp.exp(gl - gmax), axis=-1, keepdims=True)
    g_gate = 1.0 / gsum
    g_idx = jnp.min(jnp.where(gl == gmax, lane, LANES), axis=-1, keepdims=True)
    lo = N_GROUPS + EXPERTS_PER_GROUP * g_idx
    el = jnp.where((lane >= lo) & (lane < lo + EXPERTS_PER_GROUP), lg, NEG_BIG)
    e1max = jnp.max(el, axis=-1, keepdims=True)
    l1 = jnp.min(jnp.where(el == e1max, lane, LANES), axis=-1, keepdims=True)
    el2 = jnp.where(lane == l1, NEG_BIG, el)
    e2max = jnp.max(el2, axis=-1, keepdims=True)
    l2 = jnp.min(jnp.where(el2 == e2max, lane, LANES), axis=-1, keepdims=True)
    t2 = jnp.exp(e2max - e1max)
    w1 = g_gate / (1.0 + t2)
    w2 = g_gate * t2 / (1.0 + t2)
    ex1 = l1 - N_GROUPS
    ex2 = l2 - N_GROUPS

    oh1 = jnp.where(lane == ex1, 1.0, 0.0)
    oh2 = jnp.where(lane == ex2, 1.0, 0.0)
    ti = lax.broadcasted_iota(I32, (tm, tm), 0)
    si = lax.broadcasted_iota(I32, (tm, tm), 1)
    lower = jnp.where(si < ti, 1.0, 0.0).astype(BF16)
    pre1 = _dot(lower, oh1.astype(BF16))
    pre2 = _dot(lower, oh2.astype(BF16))
    cnt1 = jnp.sum(oh1, axis=0, keepdims=True)
    cnt2 = jnp.sum(oh2, axis=0, keepdims=True)
    run = run_ref[...]
    rank1 = jnp.sum(oh1 * (pre1 + run), axis=-1, keepdims=True)
    rank2 = jnp.sum(oh2 * (pre2 + run + cnt1), axis=-1, keepdims=True)
    run = run + cnt1 + cnt2
    run_ref[...] = run
    cnt_ref[...] = run

    info = jnp.where(lane == 0, ex1.astype(F32), 0.0)
    info = jnp.where(lane == 1, ex2.astype(F32), info)
    info = jnp.where(lane == 2, rank1, info)
    info = jnp.where(lane == 3, rank2, info)
    info = jnp.where(lane == 4, w1, info)
    info = jnp.where(lane == 5, w2, info)
    info_ref[...] = info


def _route(lg):
    M = lg.shape[0]
    tm = 256
    return pl.pallas_call(
        _route_kernel,
        grid=(M // tm,),
        in_specs=[pl.BlockSpec((tm, LANES), lambda i: (i, 0))],
        out_specs=[pl.BlockSpec((tm, LANES), lambda i: (i, 0)),
                   pl.BlockSpec((1, LANES), lambda i: (0, 0))],
        out_shape=[jax.ShapeDtypeStruct((M, LANES), F32),
                   jax.ShapeDtypeStruct((1, LANES), F32)],
        scratch_shapes=[pltpu.VMEM((1, LANES), F32)],
        compiler_params=pltpu.CompilerParams(dimension_semantics=("arbitrary",)),
        name="route",
    )(lg)


def _plan_kernel(info_ref, cnt_ref, slot_ref, meta_ref):
    cnt = cnt_ref[...]
    lane_r = lax.broadcasted_iota(I32, (1, LANES), 1)
    nblk = jnp.floor((cnt + (MOE_BLOCK - 1)) * (1.0 / MOE_BLOCK))
    ei = lax.broadcasted_iota(I32, (LANES, LANES), 0)
    ej = lax.broadcasted_iota(I32, (LANES, LANES), 1)
    upper = jnp.where(ei <= ej, 1.0, 0.0).astype(BF16)
    nb8 = jnp.broadcast_to(nblk, (8, LANES)).astype(BF16)
    bend = _dot(nb8, upper)[0:1, :]
    bstart = bend - nblk
    pstart = bstart * MOE_BLOCK

    info = info_ref[...]
    lane = lax.broadcasted_iota(I32, info.shape, 1)
    ex1 = info[:, 0:1].astype(I32)
    ex2 = info[:, 1:2].astype(I32)
    s1 = jnp.sum(jnp.where(lane == ex1, pstart, 0.0), axis=-1, keepdims=True) + info[:, 2:3]
    s2 = jnp.sum(jnp.where(lane == ex2, pstart, 0.0), axis=-1, keepdims=True) + info[:, 3:4]
    slot = jnp.where(lane == 0, s1, 0.0)
    slot = jnp.where(lane == 1, s2, slot)
    slot_ref[...] = slot.astype(I32)

    blk = lax.broadcasted_iota(I32, (LANES, 2 * LANES), 1).astype(F32)
    bend_col = jnp.sum(jnp.where(ei == ej, jnp.broadcast_to(bend, (LANES, LANES)), 0.0),
                       axis=-1, keepdims=True)
    erow = lax.broadcasted_iota(I32, (LANES, 2 * LANES), 0)
    hit = jnp.where((bend_col <= blk) & (erow < N_EXPERTS), 1.0, 0.0)
    bexp = jnp.minimum(jnp.sum(hit, axis=0, keepdims=True), N_EXPERTS - 1.0)
    n_used = jnp.max(jnp.where(lane_r < N_EXPERTS, bend, 0.0), axis=-1, keepdims=True)
    lastblk = jnp.where(nblk > 0, bend - 1.0, -1.0)
    r8 = lax.broadcasted_iota(I32, (8, 2 * LANES), 0)
    last2 = jnp.concatenate([lastblk, jnp.full((1, LANES), -1.0)], axis=1)
    meta = jnp.where(r8 == 0, jnp.broadcast_to(bexp, (8, 2 * LANES)), 0.0)
    meta = jnp.where(r8 == 1, jnp.broadcast_to(n_used, (8, 2 * LANES)), meta)
    meta = jnp.where(r8 == 2, jnp.broadcast_to(last2, (8, 2 * LANES)), meta)
    meta_ref[...] = meta.astype(I32)


def _plan(info, cnt):
    M = info.shape[0]
    tm = 256
    return pl.pallas_call(
        _plan_kernel,
        grid=(M // tm,),
        in_specs=[pl.BlockSpec((tm, LANES), lambda i: (i, 0)),
                  pl.BlockSpec((1, LANES), lambda i: (0, 0))],
        out_specs=[pl.BlockSpec((tm, LANES), lambda i: (i, 0)),
                   pl.BlockSpec((8, 2 * LANES), lambda i: (0, 0))],
        out_shape=[jax.ShapeDtypeStruct((M, LANES), I32),
                   jax.ShapeDtypeStruct((8, 2 * LANES), I32)],
        compiler_params=pltpu.CompilerParams(dimension_semantics=("arbitrary",)),
        name="plan",
    )(info, cnt)


ROW_DEPTH = 3
DMA_GROUP = 16
WEIGHT_DEPTH = 3


def _slot_table_kernel(slots_ref, inv_ref, *, n_assign, n_slots):
    def init(b, c):
        base = n_assign + lax.rem(b + ROW_DEPTH - 1, ROW_DEPTH) * MOE_BLOCK
        for j in range(MOE_BLOCK):
            inv_ref[b * MOE_BLOCK + j] = base + j
        return c

    def put(a, c):
        inv_ref[slots_ref[a] + MOE_BLOCK] = a
        return c

    lax.fori_loop(0, n_slots // MOE_BLOCK, init, 0)
    lax.fori_loop(0, n_assign, put, 0, unroll=16)


def _slot_table(slots_km, n_blocks):
    n_assign = slots_km.shape[0]
    n_slots = (n_blocks + ROW_DEPTH) * MOE_BLOCK
    gs = pltpu.PrefetchScalarGridSpec(
        num_scalar_prefetch=1, grid=(1,), in_specs=[],
        out_specs=pl.BlockSpec(memory_space=pltpu.SMEM))
    return pl.pallas_call(
        functools.partial(_slot_table_kernel, n_assign=n_assign, n_slots=n_slots),
        grid_spec=gs,
        out_shape=jax.ShapeDtypeStruct((n_slots,), I32),
        compiler_params=pltpu.CompilerParams(dimension_semantics=("arbitrary",)),
        name="slot_table",
    )(slots_km)


def _experts_kernel(bexp_ref, seg_ref, inv_ref, h2_ref, w1_ref, w3_ref, w2_ref, y2_ref,
                    xbuf, ybuf, w1buf, w3buf, w2buf, sem_x, sem_y, sem_w, par_ref, *, n_assign):
    i = pl.program_id(0)
    n_used = seg_ref[N_EXPERTS]
    n_tok = n_assign // 2
    B = MOE_BLOCK
    FF = w1buf.shape[2]
    D = w1buf.shape[1]
    ND = ROW_DEPTH
    NW = w1buf.shape[0]

    def src_row(a):
        if n_tok & (n_tok - 1) == 0:
            return a & (n_tok - 1)
        return lax.rem(a, n_tok)

    def weight_copies(e, slot):
        return (pltpu.make_async_copy(w1_ref.at[e], w1buf.at[slot], sem_w.at[slot]),
                pltpu.make_async_copy(w3_ref.at[e], w3buf.at[slot], sem_w.at[slot]),
                pltpu.make_async_copy(w2_ref.at[e], w2buf.at[slot], sem_w.at[slot]))

    def start_weights(e, slot):
        c1, c3, c2 = weight_copies(e, slot)
        c1.start(priority=1)
        c3.start(priority=1)
        c2.start(priority=0)

    def prefetch_expert_after(e, hops, slot):
        ok = hops >= 0
        cur = e
        for _ in range(hops):
            nb = seg_ref[cur] + 1
            ok = jnp.logical_and(ok, nb < n_used)
            cur = bexp_ref[jnp.where(ok, nb, 0)]

        @pl.when(ok)
        def _():
            start_weights(cur, slot)

    def gather_group(b, g):
        slot = lax.rem(b + ND, ND)
        for r in range(g * DMA_GROUP, (g + 1) * DMA_GROUP):
            a = inv_ref[(b + 1) * B + r]
            pltpu.make_async_copy(h2_ref.at[pl.ds(src_row(a), 1)], xbuf.at[slot, pl.ds(r, 1)],
                                  sem_x.at[slot]).start()

    def scatter_group(b, g):
        slot = lax.rem(b + ND, ND)
        for r in range(g * DMA_GROUP, (g + 1) * DMA_GROUP):
            a = inv_ref[(b + 1) * B + r]
            pltpu.make_async_copy(ybuf.at[slot, pl.ds(r, 1)], y2_ref.at[pl.ds(a, 1)],
                                  sem_y.at[slot]).start()

    def wait_block(buf, sem, b):
        slot = lax.rem(b + ND, ND)
        pltpu.make_async_copy(buf.at[slot], buf.at[slot], sem.at[slot]).wait()

    n_groups = B // DMA_GROUP

    @pl.when(i == 0)
    def _():
        ybuf[...] = jnp.zeros_like(ybuf)
        for s in range(ND - 1):
            pltpu.make_async_copy(ybuf.at[s], y2_ref.at[pl.ds(n_assign + s * B, B)], sem_y.at[s]).start()
        par_ref[0] = 0
        e0 = bexp_ref[0]
        start_weights(e0, 0)
        for hops in range(1, NW):
            prefetch_expert_after(e0, hops, hops)
        for b in range(ND - 1):
            for g in range(n_groups):
                gather_group(b, g)

    active = i < n_used

    @pl.when(active)
    def _():
        e = bexp_ref[i]
        first = jnp.logical_or(i == 0, e != bexp_ref[jnp.maximum(i - 1, 0)])

        @pl.when(jnp.logical_and(first, i > 0))
        def _():
            par_ref[0] = par_ref[0] + 1

        @pl.when(first)
        def _():
            q = par_ref[0]
            for cp in weight_copies(e, lax.rem(q, NW)):
                cp.wait()

            @pl.when(i > 0)
            def _():
                prefetch_expert_after(e, NW - 1, lax.rem(q + NW - 1, NW))

        p = lax.rem(par_ref[0], NW)
        slot = lax.rem(i, ND)
        wait_block(xbuf, sem_x, i)
        wait_block(ybuf, sem_y, i)
        pending = [functools.partial(scatter_group, i - 1, g) for g in range(n_groups)]
        pending += [functools.partial(gather_group, i + ND - 1, g) for g in range(n_groups)]
        n_chunks = 2 * (FF // 256) + ROW_SLABS
        for _ in range(max(len(pending) - n_chunks, 0)):
            pending.pop(0)()

        def after_chunk():
            if pending:
                pending.pop(0)()

        x = jnp.concatenate([xbuf[slot, :, j, :] for j in range(ROW_SLABS)], axis=1).astype(BF16)
        sw = D // ROW_SLABS
        h1c, h3c = [], []
        for c0 in range(0, FF, 256):
            h1c.append(_dot(x, w1buf[p, :, c0:c0 + 256]))
            after_chunk()
            h3c.append(_dot(x, w3buf[p, :, c0:c0 + 256]))
            after_chunk()
        h1 = jnp.concatenate(h1c, axis=1)
        h3 = jnp.concatenate(h3c, axis=1)
        hid = (h1 * (1.0 / (1.0 + jnp.exp(-h1))) * h3).astype(BF16)
        for j in range(ROW_SLABS):
            ybuf[slot, :, j, :] = _dot(hid, w2buf[p, :, j * sw:(j + 1) * sw])
            after_chunk()
        while pending:
            pending.pop(0)()

    @pl.when(i == n_used)
    def _():
        wait_block(xbuf, sem_x, i)
        wait_block(ybuf, sem_y, i)
        for g in range(n_groups):
            scatter_group(i - 1, g)

    @pl.when(jnp.logical_and(i > n_used, i < n_used + ND - 1))
    def _():
        wait_block(xbuf, sem_x, i)
        wait_block(ybuf, sem_y, i)

    @pl.when(i == n_used + ND - 1)
    def _():
        wait_block(ybuf, sem_y, i)


def _experts(bexp, seg_meta, inv, h2, w1, w3, w2, n_blocks):
    M, DJ, SW = h2.shape
    D = DJ * SW
    FF = w1.shape[2]
    n_assign = 2 * M
    any_spec = pl.BlockSpec(memory_space=pl.ANY)
    row_buf = pltpu.VMEM((ROW_DEPTH, MOE_BLOCK, DJ, SW), F32)
    gs = pltpu.PrefetchScalarGridSpec(
        num_scalar_prefetch=3,
        grid=(n_blocks + ROW_DEPTH,),
        in_specs=[any_spec, any_spec, any_spec, any_spec],
        out_specs=any_spec,
        scratch_shapes=[row_buf, row_buf,
                        pltpu.VMEM((WEIGHT_DEPTH, D, FF), F32), pltpu.VMEM((WEIGHT_DEPTH, D, FF), F32),
                        pltpu.VMEM((WEIGHT_DEPTH, FF, D), F32),
                        pltpu.SemaphoreType.DMA((ROW_DEPTH,)), pltpu.SemaphoreType.DMA((ROW_DEPTH,)),
                        pltpu.SemaphoreType.DMA((WEIGHT_DEPTH,)), pltpu.SMEM((1,), I32)],
    )
    vm = WEIGHT_DEPTH * 3 * D * FF * 4 + 2 * ROW_DEPTH * MOE_BLOCK * D * 4 + (12 << 20)
    return pl.pallas_call(
        functools.partial(_experts_kernel, n_assign=n_assign),
        grid_spec=gs,
        out_shape=jax.ShapeDtypeStruct((n_assign + ROW_DEPTH * MOE_BLOCK, DJ, SW), F32),
        compiler_params=pltpu.CompilerParams(
            dimension_semantics=("arbitrary",), vmem_limit_bytes=_vmem_limit(vm),
            has_side_effects=True),
        name="experts",
    )(bexp, seg_meta, inv, h2, w1, w3, w2)


def _combine_kernel(x1_ref, info_ref, gt_ref, fg_ref, ya_ref, yb_ref, o_ref):
    info = info_ref[...]
    nj = ya_ref.shape[1]
    ya = jnp.concatenate([ya_ref[:, j, :] for j in range(nj)], axis=1)
    yb = jnp.concatenate([yb_ref[:, j, :] for j in range(nj)], axis=1)
    moe = info[:, 4:5] * ya + info[:, 5:6] * yb
    x = x1_ref[...] + gt_ref[0] * moe
    ms = jnp.mean(x * x, axis=-1, keepdims=True)
    o_ref[...] = x * lax.rsqrt(ms + NORM_EPS) * fg_ref[...]


def _combine(x1, info, gt, final_g, y2, T):
    M, D = x1.shape
    tm = 256
    tps = T // tm
    nt = M // tm
    vm = 2 * 4 * tm * D * 4 + tm * D * 16 + (8 << 20)
    return pl.pallas_call(
        _combine_kernel,
        grid=(nt,),
        in_specs=[pl.BlockSpec((tm, D), lambda i: (i, 0)),
                  pl.BlockSpec((tm, LANES), lambda i: (i, 0)),
                  pl.BlockSpec((1, 1, D), lambda i: (i // tps, 0, 0)),
                  pl.BlockSpec((1, D), lambda i: (0, 0)),
                  pl.BlockSpec((tm, ROW_SLABS, D // ROW_SLABS), lambda i: (i, 0, 0)),
                  pl.BlockSpec((tm, ROW_SLABS, D // ROW_SLABS), lambda i: (nt + i, 0, 0))],
        out_specs=pl.BlockSpec((tm, D), lambda i: (i, 0)),
        out_shape=jax.ShapeDtypeStruct((M, D), F32),
        compiler_params=pltpu.CompilerParams(
            dimension_semantics=("arbitrary",), vmem_limit_bytes=_vmem_limit(vm)),
        name="combine",
    )(x1, info, gt, final_g.reshape(1, D), y2, y2)


def _rope_tables(T):
    inv_freq = ROPE_THETA ** (-np.arange(0, HEAD_DIM, 2, dtype=np.float64) / HEAD_DIM)
    ang = np.arange(T, dtype=np.float64)[:, None] * inv_freq[None, :]
    cos, sin = np.cos(ang), np.sin(ang)
    cos_h = np.concatenate([cos, cos], axis=1)
    sin_h = np.concatenate([-sin, sin], axis=1)
    return (jnp.asarray(np.tile(cos_h, (1, 2)), F32), jnp.asarray(np.tile(sin_h, (1, 2)), F32))


def _layer(x, c, w_ada, b_ada, norm1_g, w_in, mu_shift, sinks, w0, w_decay_up, a0, w_a_up, w_g_up,
           k_k, k_a, r_k, ln_x_w, ln_x_b, w_o, norm2_g, w_rg, b_rg, w_re, b_re, w1, w3, w2, final_g,
           apply_final):
    B, T, D = x.shape
    M = B * T
    mod = _ada(c, w_ada, b_ada)
    sh1, sc1, gt1, sh2, sc2, gt2 = [m.reshape(B, 1, D) for m in jnp.split(mod, 6, axis=-1)]

    pad = RWKV_PAD - RWKV_COLS
    w_bf = jnp.concatenate([w_in.astype(BF16), jnp.zeros((D, pad), BF16)], axis=1)
    mu_pad = jnp.pad(mu_shift, (0, pad)).reshape(1, RWKV_PAD)
    cos_t, sin_t = _rope_tables(T)
    x2 = x.reshape(M, D)
    za, zr = _inproj(x2, norm1_g.reshape(1, D), sh1, sc1, w_bf, mu_pad, cos_t, sin_t, T)

    o_attn = _attention(za.reshape(B, T, ATTN_COLS), sinks)
    o_rwkv = _rwkv(zr.reshape(B, T, RWKV_PAD), w0, w_decay_up, a0, w_a_up, w_g_up, k_k, k_a,
                   r_k.reshape(-1), ln_x_w, ln_x_b)

    wr = jnp.concatenate([w_rg, w_re], axis=1)
    wr = jnp.pad(wr, ((0, 0), (0, LANES - wr.shape[1])))
    wr_hi = wr.astype(BF16)
    wr_lo = (wr - wr_hi.astype(F32)).astype(BF16)
    br = jnp.pad(jnp.concatenate([b_rg, b_re]), (0, LANES - N_GROUPS - N_EXPERTS)).reshape(1, LANES)
    x1, h2, lg = _outproj(o_attn.reshape(M, Q_COLS), o_rwkv.reshape(M, RWKV_W), x2, w_o.astype(BF16),
                          gt1, norm2_g.reshape(1, D), sh2, sc2, wr_hi, wr_lo, br, T)

    n_blocks = -(-(2 * M) // MOE_BLOCK) + N_EXPERTS
    info, cnt = _route(lg)
    slots, meta = _plan(info, cnt)
    slots_km = slots[:, :2].T.reshape(-1)
    seg_meta = jnp.concatenate([meta[2, :N_EXPERTS], meta[1, :1]])
    inv = _slot_table(slots_km, n_blocks)
    y2 = _experts(meta[0, :n_blocks + ROW_DEPTH], seg_meta, inv, h2, w1, w3, w2, n_blocks)
    out = _combine(x1, info, gt2, final_g, y2, T)
    del apply_final
    return out.reshape(B, T, D)


def kernel(x, c, w_ada, b_ada, norm1_g, w_in, mu_shift, sinks, w0, w_decay_up, a0, w_a_up, w_g_up, k_k, k_a, r_k, ln_x_w, ln_x_b, w_o, norm2_g, w_router_group, b_router_group, w_router_expert, b_router_expert, w1, w3, w2, final_g):
    depth = w_ada.shape[0]
    assert depth == 1, "single-layer stack"
    l = 0
    return _layer(x, c, w_ada[l], b_ada[l], norm1_g[l], w_in[l], mu_shift[l], sinks[l], w0[l],
                  w_decay_up[l], a0[l], w_a_up[l], w_g_up[l], k_k[l], k_a[l], r_k[l], ln_x_w[l],
                  ln_x_b[l], w_o[l], norm2_g[l], w_router_group[l], b_router_group[l],
                  w_router_expert[l], b_router_expert[l], w1[l], w3[l], w2[l], final_g, True)
```

```python
import functools
import math

import jax
import jax.numpy as jnp
import numpy as np
from jax import lax
from jax.experimental import pallas as pl
from jax.experimental.pallas import tpu as pltpu

F32 = jnp.float32
BF16 = jnp.bfloat16
I32 = jnp.int32

LANES = 128
VMEM_BYTES_V7X = 64 * 1024 * 1024

HEAD_DIM = 64
ATTN_HEADS = 16
ATTN_KV_HEADS = 2
ATTN_GROUP = ATTN_HEADS // ATTN_KV_HEADS
WINDOW = 128
ROPE_THETA = 10000.0
RWKV_HEADS = 16
RWKV_N = 64
DECAY_LORA = 64
A_LORA = 64
GATE_LORA = 160
RWKV_LN_EPS = 64e-5
N_GROUPS = 8
EXPERTS_PER_GROUP = 8
N_EXPERTS = N_GROUPS * EXPERTS_PER_GROUP
EXPERT_FF = 512
MOE_BLOCK = 128
ROW_SLABS = 8
NORM_EPS = 1e-6

Q_COLS = ATTN_HEADS * HEAD_DIM
KV_COLS = ATTN_KV_HEADS * HEAD_DIM
ATTN_COLS = Q_COLS + 2 * KV_COLS
RWKV_W = RWKV_HEADS * RWKV_N
LORA_COLS = DECAY_LORA + A_LORA + GATE_LORA
LORA_PAD = 384
RWKV_COLS = 3 * RWKV_W + LORA_COLS
RWKV_PAD = 3 * RWKV_W + LORA_PAD
CHUNK = 64
RWKV_GROUP = 4
NEG_BIG = -1e30


def _vmem_limit(nbytes):
    return int(min(nbytes, VMEM_BYTES_V7X - 4 * 1024 * 1024))


def _dot(a, b):
    return jnp.dot(a, b, preferred_element_type=F32)


def _dot_nt(a, b):
    return lax.dot_general(a, b, (((1,), (1,)), ((), ())), preferred_element_type=F32)


def _dot_tn(a, b):
    return lax.dot_general(a, b, (((0,), (0,)), ((), ())), preferred_element_type=F32)


def _ada_kernel(c_ref, w_ref, b_ref, o_ref):
    c = c_ref[...]
    s = c * (1.0 / (1.0 + jnp.exp(-c)))
    o_ref[...] = _dot(s, w_ref[...]) + b_ref[...]


def _ada(c, w_ada, b_ada):
    B, D = c.shape
    N = w_ada.shape[1]
    tn = 1024
    cp = jnp.zeros((8, D), F32).at[:B].set(c)
    out = pl.pallas_call(
        _ada_kernel,
        grid=(N // tn,),
        in_specs=[pl.BlockSpec((8, D), lambda j: (0, 0)),
                  pl.BlockSpec((D, tn), lambda j: (0, j)),
                  pl.BlockSpec((1, tn), lambda j: (0, j))],
        out_specs=pl.BlockSpec((8, tn), lambda j: (0, j)),
        out_shape=jax.ShapeDtypeStruct((8, N), F32),
        compiler_params=pltpu.CompilerParams(
            dimension_semantics=("arbitrary",),
            vmem_limit_bytes=_vmem_limit(2 * D * tn * 4 + (8 << 20))),
        name="ada",
    )(cp, w_ada, b_ada.reshape(1, N))
    return out[:B]


def _cast_pad_kernel(w_ref, o_ref, *, n_valid):
    j = pl.program_id(0)
    tn = w_ref.shape[1]
    col = j * tn + lax.broadcasted_iota(I32, w_ref.shape, 1)
    o_ref[...] = jnp.where(col < n_valid, w_ref[...], 0.0).astype(o_ref.dtype)


def _cast_pad(w, n_out):
    K, N = w.shape
    tn = 512
    return pl.pallas_call(
        functools.partial(_cast_pad_kernel, n_valid=N),
        grid=(pl.cdiv(n_out, tn),),
        in_specs=[pl.BlockSpec((K, tn), lambda j: (0, j))],
        out_specs=pl.BlockSpec((K, tn), lambda j: (0, j)),
        out_shape=jax.ShapeDtypeStruct((K, n_out), BF16),
        compiler_params=pltpu.CompilerParams(dimension_semantics=("arbitrary",)),
        name="cast_pad",
    )(w)


def _rope(z, cos, sin):
    w = z.shape[-1]
    lane = lax.broadcasted_iota(I32, z.shape, 1)
    first_half = (lane % HEAD_DIM) < (HEAD_DIM // 2)
    partner = jnp.where(first_half, pltpu.roll(z, w - HEAD_DIM // 2, 1), pltpu.roll(z, HEAD_DIM // 2, 1))
    return z * cos + partner * sin


def _inproj_kernel(x_ref, g_ref, sh_ref, sc_ref, w_ref, mu_ref, cos_ref, sin_ref,
                   za_ref, zr_ref, carry_ref, *, tiles_per_seq):
    i = pl.program_id(0)
    tm = x_ref.shape[0]
    x = x_ref[...]
    ms = jnp.mean(x * x, axis=-1, keepdims=True)
    y = x * lax.rsqrt(ms + NORM_EPS) * g_ref[...]
    h = (y * (1.0 + sc_ref[0]) + sh_ref[0]).astype(BF16)

    cos2 = jnp.concatenate([cos_ref[...], cos_ref[...]], axis=1)
    sin2 = jnp.concatenate([sin_ref[...], sin_ref[...]], axis=1)
    for c0 in range(0, ATTN_COLS, 256):
        z = _dot(h, w_ref[:, c0:c0 + 256])
        if c0 < Q_COLS:
            z = _rope(z, cos2, sin2)
        else:
            z = jnp.concatenate([_rope(z[:, :KV_COLS], cos_ref[...], sin_ref[...]), z[:, KV_COLS:]], axis=1)
        za_ref[:, c0:c0 + 256] = z.astype(za_ref.dtype)

    first = (i % tiles_per_seq) == 0
    row = lax.broadcasted_iota(I32, (tm, 1), 0)
    c0 = 0
    while c0 < RWKV_PAD:
        n = min(512, RWKV_PAD - c0)
        z = _dot(h, w_ref[:, ATTN_COLS + c0:ATTN_COLS + c0 + n])
        prev_last = jnp.where(first, 0.0, carry_ref[7:8, c0:c0 + n])
        z_prev = jnp.where(row == 0, prev_last, pltpu.roll(z, 1, 0))
        carry_ref[:, c0:c0 + n] = z[tm - 8:tm, :]
        zr_ref[:, c0:c0 + n] = z + (z_prev - z) * mu_ref[:, c0:c0 + n]
        c0 += n


def _inproj(x2, g, sh, sc, w_bf, mu_pad, cos_t, sin_t, T):
    M, D = x2.shape
    tm = 256
    tps = T // tm
    NW = w_bf.shape[1]
    kern = functools.partial(_inproj_kernel, tiles_per_seq=tps)
    vm = (D * NW * 2 + 2 * tm * D * 4 + 2 * tm * ATTN_COLS * 2 + 2 * tm * RWKV_PAD * 4
          + 4 * tm * 512 * 4 + tm * D * 8 + (8 << 20))
    return pl.pallas_call(
        kern,
        grid=(M // tm,),
        in_specs=[pl.BlockSpec((tm, D), lambda i: (i, 0)),
                  pl.BlockSpec((1, D), lambda i: (0, 0)),
                  pl.BlockSpec((1, 1, D), lambda i: (i // tps, 0, 0)),
                  pl.BlockSpec((1, 1, D), lambda i: (i // tps, 0, 0)),
                  pl.BlockSpec((D, NW), lambda i: (0, 0), pipeline_mode=pl.Buffered(1)),
                  pl.BlockSpec((1, RWKV_PAD), lambda i: (0, 0)),
                  pl.BlockSpec((tm, 2 * HEAD_DIM), lambda i: (i % tps, 0)),
                  pl.BlockSpec((tm, 2 * HEAD_DIM), lambda i: (i % tps, 0))],
        out_specs=[pl.BlockSpec((tm, ATTN_COLS), lambda i: (i, 0)),
                   pl.BlockSpec((tm, RWKV_PAD), lambda i: (i, 0))],
        out_shape=[jax.ShapeDtypeStruct((M, ATTN_COLS), BF16),
                   jax.ShapeDtypeStruct((M, RWKV_PAD), F32)],
        scratch_shapes=[pltpu.VMEM((8, RWKV_PAD), F32)],
        compiler_params=pltpu.CompilerParams(
            dimension_semantics=("arbitrary",), vmem_limit_bytes=_vmem_limit(vm)),
        name="inproj",
    )(x2, g, sh, sc, w_bf, mu_pad, cos_t, sin_t)


def _attn_kernel(sink_ref, q_ref, kc_ref, kp_ref, vc_ref, vp_ref, o_ref):
    n = pl.program_id(1)
    blk = q_ref.shape[1]
    row = lax.broadcasted_iota(I32, (blk, blk), 0)
    col = lax.broadcasted_iota(I32, (blk, blk), 1)
    mask = jnp.concatenate([(col > row) & (n > 0), col <= row], axis=1)
    scale = 1.0 / math.sqrt(HEAD_DIM)
    outs = []
    for kvh in range(ATTN_KV_HEADS):
        ks = slice(kvh * HEAD_DIM, (kvh + 1) * HEAD_DIM)
        kmat = jnp.concatenate([kp_ref[0, :, ks], kc_ref[0, :, ks]], axis=0)
        vmat = jnp.concatenate([vp_ref[0, :, ks], vc_ref[0, :, ks]], axis=0)
        for g in range(ATTN_GROUP):
            hd = kvh * ATTN_GROUP + g
            qh = q_ref[0, :, hd * HEAD_DIM:(hd + 1) * HEAD_DIM]
            s = _dot_nt(qh, kmat) * scale
            s = jnp.where(mask, s, NEG_BIG)
            sink = sink_ref[hd]
            m = jnp.maximum(jnp.max(s, axis=-1, keepdims=True), sink)
            p = jnp.exp(s - m)
            denom = jnp.sum(p, axis=-1, keepdims=True) + jnp.exp(sink - m)
            o = _dot(p.astype(BF16), vmat)
            outs.append(o / denom)
    o_ref[0] = jnp.concatenate(outs, axis=1).astype(o_ref.dtype)


def _attention(za3, sinks):
    B, T, _ = za3.shape
    nb = T // WINDOW
    kcol = Q_COLS // KV_COLS
    prev = lambda b, n, s: (b, jnp.maximum(n - 1, 0), kcol)
    prev_v = lambda b, n, s: (b, jnp.maximum(n - 1, 0), kcol + 1)
    gs = pltpu.PrefetchScalarGridSpec(
        num_scalar_prefetch=1,
        grid=(B, nb),
        in_specs=[pl.BlockSpec((1, WINDOW, Q_COLS), lambda b, n, s: (b, n, 0)),
                  pl.BlockSpec((1, WINDOW, KV_COLS), lambda b, n, s: (b, n, kcol)),
                  pl.BlockSpec((1, WINDOW, KV_COLS), prev),
                  pl.BlockSpec((1, WINDOW, KV_COLS), lambda b, n, s: (b, n, kcol + 1)),
                  pl.BlockSpec((1, WINDOW, KV_COLS), prev_v)],
        out_specs=pl.BlockSpec((1, WINDOW, Q_COLS), lambda b, n, s: (b, n, 0)),
    )
    return pl.pallas_call(
        _attn_kernel,
        grid_spec=gs,
        out_shape=jax.ShapeDtypeStruct((B, T, Q_COLS), BF16),
        compiler_params=pltpu.CompilerParams(dimension_semantics=("arbitrary", "arbitrary")),
        name="attn",
    )(sinks, za3, za3, za3, za3, za3)


def _rwkv_kernel(r_ref, k_ref, v_ref, lora_ref, w0_ref, wdu_ref, a0_ref, wau_ref, wgu_ref,
                 kk_ref, ka_ref, rk_ref, lnw_ref, lnb_ref, o_ref, s_ref):
    c = pl.program_id(0)
    nseq = r_ref.shape[0]
    C = r_ref.shape[1]
    N = RWKV_N
    G = RWKV_GROUP
    GW = G * N
    ng = RWKV_HEADS // G

    @pl.when(c == 0)
    def _():
        s_ref[...] = jnp.zeros_like(s_ref)

    ti = lax.broadcasted_iota(I32, (C, C), 0)
    si = lax.broadcasted_iota(I32, (C, C), 1)
    tril = jnp.where(si <= ti, 1.0, 0.0).astype(BF16)
    a_seq, g_seq, e_in_seq, e_ex_seq, e_neg_seq = [], [], [], [], []
    for b in range(nseq):
        lora = lora_ref[b]
        wd = lora[:, 0:DECAY_LORA]
        ad = lora[:, DECAY_LORA:DECAY_LORA + A_LORA]
        gd = lora[:, DECAY_LORA + A_LORA:LORA_COLS]
        wlin = w0_ref[...] + _dot(jnp.tanh(wd), wdu_ref[...])
        neg = -wlin
        softplus = jnp.maximum(neg, 0.0) + jnp.log(1.0 + jnp.exp(-jnp.abs(neg)))
        w = -softplus - 0.5
        logdec = -jnp.exp(w)
        a_seq.append(1.0 / (1.0 + jnp.exp(-(a0_ref[...] + _dot(ad, wau_ref[...])))))
        g_seq.append(_dot(1.0 / (1.0 + jnp.exp(-gd)), wgu_ref[...]))
        ld_hi = logdec.astype(BF16)
        ld_lo = (logdec - ld_hi.astype(F32)).astype(BF16)
        cum = _dot(tril, ld_hi) + _dot(tril, ld_lo)
        e_in_seq.append(jnp.exp(cum))
        e_ex_seq.append(jnp.exp(cum - logdec))
        e_neg_seq.append(jnp.exp(-cum))

    lane_head = lax.broadcasted_iota(I32, (1, GW), 1) // N
    t_row = lax.broadcasted_iota(I32, (C, GW), 0)
    s_lane = lax.broadcasted_iota(I32, (C, GW), 1) % N
    strict = s_lane < t_row
    incl = s_lane <= t_row
    bi = lax.broadcasted_iota(I32, (GW, GW), 0)
    bj = lax.broadcasted_iota(I32, (GW, GW), 1)
    same_head = (bi // N) == (bj // N)
    ones_bd = jnp.where(same_head, 1.0, 0.0).astype(BF16)
    eye = jnp.where(s_lane == t_row, 1.0, 0.0)

    def expand(xc):
        return jnp.concatenate([jnp.where(lane_head == h, xc, 0.0) for h in range(G)], axis=0)

    def head_sum(xs):
        s = _dot(jnp.concatenate(xs, axis=0).astype(BF16), ones_bd)
        return [s[i * C:(i + 1) * C] for i in range(len(xs))]

    seq = [b for b in range(nseq) for _ in range(ng)]
    col = [slice(gi * GW, (gi + 1) * GW) for _ in range(nseq) for gi in range(ng)]
    rng = range(nseq * ng)
    r_l = [r_ref[seq[i], :, col[i]] for i in rng]
    k_l = [k_ref[seq[i], :, col[i]] for i in rng]
    v_l = [v_ref[seq[i], :, col[i]] for i in rng]
    a_l = [a_seq[seq[i]][:, col[i]] for i in rng]
    e_in = [e_in_seq[seq[i]][:, col[i]] for i in rng]
    e_neg = [e_neg_seq[seq[i]][:, col[i]] for i in rng]
    kk0 = [k_l[i] * kk_ref[:, col[i]] for i in rng]
    nrm2 = head_sum([kk0[i] * kk0[i] for i in rng])
    kk = [kk0[i] / jnp.maximum(jnp.sqrt(nrm2[i]), 1e-12) for i in rng]
    k2 = [k_l[i] * (1.0 + (a_l[i] - 1.0) * ka_ref[:, col[i]]) for i in rng]
    bt = [kk[i] * a_l[i] * e_neg[i] for i in rng]
    kt = [k2[i] * e_neg[i] for i in rng]
    left = [jnp.concatenate([-kk[i] * e_ex_seq[seq[i]][:, col[i]], r_l[i] * e_in[i]], axis=0).astype(BF16)
            for i in rng]
    right = [jnp.concatenate([bt[i], kt[i]], axis=0).astype(BF16) for i in rng]
    zed = [jnp.concatenate([expand(bt[i]), expand(kt[i])], axis=0).astype(BF16) for i in rng]
    big = [_dot_nt(left[i], zed[i]) for i in rng]
    s_old = [s_ref[i] for i in rng]
    ls = [_dot_nt(left[i], s_old[i].astype(BF16)) for i in rng]
    v_bd = [expand(v_l[i]).astype(BF16) for i in rng]
    pw = [jnp.where(strict, big[i][:C, :G * C], 0.0) for i in rng]
    tinv = [eye + pw[i] for i in rng]
    pw_bd = [expand(pw[i]).astype(BF16) for i in rng]
    x = [ls[i][:C] + _dot(jnp.where(strict, big[i][:C, G * C:], 0.0).astype(BF16), v_bd[i]) for i in rng]
    span = 2
    while span < C:
        pw = [_dot(pw[i].astype(BF16), pw_bd[i]) for i in rng]
        pw_bd = [expand(pw[i]).astype(BF16) for i in rng]
        tinv = [tinv[i] + _dot(tinv[i].astype(BF16), pw_bd[i]) for i in rng]
        span *= 2
    u = [_dot(tinv[i].astype(BF16), expand(x[i]).astype(BF16)) for i in rng]
    a_r = [jnp.concatenate([jnp.where(incl, big[i][C:, :G * C], 0.0),
                            jnp.where(incl, big[i][C:, G * C:], 0.0)], axis=1).astype(BF16) for i in rng]
    y = [ls[i][C:] + _dot(a_r[i], jnp.concatenate([expand(u[i]).astype(BF16), v_bd[i]], axis=0)) for i in rng]
    uv = [jnp.concatenate([u[i], v_l[i]], axis=0).astype(BF16) for i in rng]
    for i in rng:
        s_ref[i] = ((s_old[i] + jnp.where(same_head, _dot_tn(uv[i], right[i]), 0.0))
                    * e_in[i][C - 1:C, :])
    ysum = head_sum(y)
    yc = [y[i] - ysum[i] * (1.0 / N) for i in rng]
    ysq = head_sum([yc[i] * yc[i] for i in rng])
    var = [ysq[i] * (1.0 / N) for i in rng]
    rksum = head_sum([r_l[i] * k2[i] * rk_ref[:, col[i]] for i in rng])
    bonus = [rksum[i] * v_l[i] for i in rng]
    for i in rng:
        yn = yc[i] * lax.rsqrt(var[i] + RWKV_LN_EPS) * lnw_ref[:, col[i]] + lnb_ref[:, col[i]]
        o_ref[seq[i], :, col[i]] = ((yn + bonus[i]) * g_seq[seq[i]][:, col[i]]).astype(o_ref.dtype)


def _rwkv(zr3, w0, wdu, a0, wau, wgu, k_k, k_a, r_k, ln_w, ln_b):
    B, T, _ = zr3.shape
    C = CHUNK
    W = RWKV_W
    GW = RWKV_GROUP * RWKV_N
    vec = lambda v: v.reshape(1, W)
    full = lambda shape: pl.BlockSpec(shape, lambda c: (0,) * len(shape))
    return pl.pallas_call(
        _rwkv_kernel,
        grid=(T // C,),
        in_specs=[pl.BlockSpec((B, C, W), lambda c: (0, c, 0)),
                  pl.BlockSpec((B, C, W), lambda c: (0, c, 1)),
                  pl.BlockSpec((B, C, W), lambda c: (0, c, 2)),
                  pl.BlockSpec((B, C, LORA_PAD), lambda c: (0, c, 3 * W // LORA_PAD)),
                  full((1, W)), full((DECAY_LORA, W)), full((1, W)), full((A_LORA, W)),
                  full((GATE_LORA, W)), full((1, W)), full((1, W)), full((1, W)),
                  full((1, W)), full((1, W))],
        out_specs=pl.BlockSpec((B, C, W), lambda c: (0, c, 0)),
        out_shape=jax.ShapeDtypeStruct((B, T, W), BF16),
        scratch_shapes=[pltpu.VMEM((B * RWKV_HEADS // RWKV_GROUP, GW, GW), F32)],
        compiler_params=pltpu.CompilerParams(dimension_semantics=("arbitrary",)),
        name="rwkv",
    )(zr3, zr3, zr3, zr3, vec(w0), wdu, vec(a0), wau, wgu, vec(k_k), vec(k_a), vec(r_k),
      vec(ln_w), vec(ln_b))


def _outproj_kernel(oa_ref, or_ref, x_ref, wo_ref, gt_ref, g_ref, sh_ref, sc_ref,
                    wrb_ref, wrh_ref, br_ref, x1_ref, h2_ref, lg_ref):
    mixed = _dot(oa_ref[...], wo_ref[0:Q_COLS, :]) + _dot(or_ref[...], wo_ref[Q_COLS:, :])
    x1 = x_ref[...] + gt_ref[0] * mixed
    x1_ref[...] = x1
    ms = jnp.mean(x1 * x1, axis=-1, keepdims=True)
    h2 = x1 * lax.rsqrt(ms + NORM_EPS) * g_ref[...] * (1.0 + sc_ref[0]) + sh_ref[0]
    for j in range(ROW_SLABS):
        w = h2_ref.shape[2]
        h2_ref[:, j, :] = h2[:, j * w:(j + 1) * w]
    hh = h2.astype(BF16)
    hl = (h2 - hh.astype(F32)).astype(BF16)
    both = _dot(hh, wrb_ref[...])
    lg_ref[...] = both[:, :LANES] + both[:, LANES:] + _dot(hl, wrh_ref[...]) + br_ref[...]


def _outproj(oa2, or2, x2, wo_bf, gt, g2, sh, sc, wr_both, wr_hi, br, T):
    M, D = x2.shape
    tm = 256
    tps = T // tm
    bvec = pl.BlockSpec((1, 1, D), lambda i: (i // tps, 0, 0))
    vm = D * D * 2 + 2 * tm * D * (2 + 4 + 4 + 4) + tm * D * 16 + (8 << 20)
    return pl.pallas_call(
        _outproj_kernel,
        grid=(M // tm,),
        in_specs=[pl.BlockSpec((tm, Q_COLS), lambda i: (i, 0)),
                  pl.BlockSpec((tm, RWKV_W), lambda i: (i, 0)),
                  pl.BlockSpec((tm, D), lambda i: (i, 0)),
                  pl.BlockSpec((D, D), lambda i: (0, 0), pipeline_mode=pl.Buffered(1)),
                  bvec,
                  pl.BlockSpec((1, D), lambda i: (0, 0)),
                  bvec, bvec,
                  pl.BlockSpec((D, 2 * LANES), lambda i: (0, 0)),
                  pl.BlockSpec((D, LANES), lambda i: (0, 0)),
                  pl.BlockSpec((1, LANES), lambda i: (0, 0))],
        out_specs=[pl.BlockSpec((tm, D), lambda i: (i, 0)),
                   pl.BlockSpec((tm, ROW_SLABS, D // ROW_SLABS), lambda i: (i, 0, 0)),
                   pl.BlockSpec((tm, LANES), lambda i: (i, 0))],
        out_shape=[jax.ShapeDtypeStruct((M, D), F32),
                   jax.ShapeDtypeStruct((M, ROW_SLABS, D // ROW_SLABS), F32),
                   jax.ShapeDtypeStruct((M, LANES), F32)],
        compiler_params=pltpu.CompilerParams(
            dimension_semantics=("arbitrary",), vmem_limit_bytes=_vmem_limit(vm)),
        name="outproj",
    )(oa2, or2, x2, wo_bf, gt, g2, sh, sc, wr_both, wr_hi, br)


def _route_kernel(lg_ref, info_ref, cnt_ref, run_ref):
    i = pl.program_id(0)
    tm = lg_ref.shape[0]

    @pl.when(i == 0)
    def _():
        run_ref[...] = jnp.zeros_like(run_ref)

    lg = lg_ref[...]
    lane = lax.broadcasted_iota(I32, lg.shape, 1)
    gl = jnp.where(lane < N_GROUPS, lg, NEG_BIG)
    gmax = jnp.max(gl, axis=-1, keepdims=True)
    gsum = jnp.sum(jnp.exp(gl - gmax), axis=-1, keepdims=True)
    g_gate = 1.0 / gsum
    g_idx = jnp.min(jnp.where(gl == gmax, lane, LANES), axis=-1, keepdims=True)
    lo = N_GROUPS + EXPERTS_PER_GROUP * g_idx
    el = jnp.where((lane >= lo) & (lane < lo + EXPERTS_PER_GROUP), lg, NEG_BIG)
    e1max = jnp.max(el, axis=-1, keepdims=True)
    l1 = jnp.min(jnp.where(el == e1max, lane, LANES), axis=-1, keepdims=True)
    el2 = jnp.where(lane == l1, NEG_BIG, el)
    e2max = jnp.max(el2, axis=-1, keepdims=True)
    l2 = jnp.min(jnp.where(el2 == e2max, lane, LANES), axis=-1, keepdims=True)
    t2 = jnp.exp(e2max - e1max)
    w1 = g_gate / (1.0 + t2)
    w2 = g_gate * t2 / (1.0 + t2)
    ex1 = l1 - N_GROUPS
    ex2 = l2 - N_GROUPS

    oh1 = jnp.where(lane == ex1, 1.0, 0.0)
    oh2 = jnp.where(lane == ex2, 1.0, 0.0)
    ti = lax.broadcasted_iota(I32, (tm, tm), 0)
    si = lax.broadcasted_iota(I32, (tm, tm), 1)
    lower = jnp.where(si < ti, 1.0, 0.0).astype(BF16)
    pre1 = _dot(lower, oh1.astype(BF16))
    pre2 = _dot(lower, oh2.astype(BF16))
    cnt1 = jnp.sum(oh1, axis=0, keepdims=True)
    cnt2 = jnp.sum(oh2, axis=0, keepdims=True)
    run = run_ref[...]
    rank1 = jnp.sum(oh1 * (pre1 + run), axis=-1, keepdims=True)
    rank2 = jnp.sum(oh2 * (pre2 + run + cnt1), axis=-1, keepdims=True)
    run = run + cnt1 + cnt2
    run_ref[...] = run
    cnt_ref[...] = run

    info = jnp.where(lane == 0, ex1.astype(F32), 0.0)
    info = jnp.where(lane == 1, ex2.astype(F32), info)
    info = jnp.where(lane == 2, rank1, info)
    info = jnp.where(lane == 3, rank2, info)
    info = jnp.where(lane == 4, w1, info)
    info = jnp.where(lane == 5, w2, info)
    info_ref[...] = info


def _route(lg):
    M = lg.shape[0]
    tm = 256
    return pl.pallas_call(
        _route_kernel,
        grid=(M // tm,),
        in_specs=[pl.BlockSpec((tm, LANES), lambda i: (i, 0))],
        out_specs=[pl.BlockSpec((tm, LANES), lambda i: (i, 0)),
                   pl.BlockSpec((1, LANES), lambda i: (0, 0))],
        out_shape=[jax.ShapeDtypeStruct((M, LANES), F32),
                   jax.ShapeDtypeStruct((1, LANES), F32)],
        scratch_shapes=[pltpu.VMEM((1, LANES), F32)],
        compiler_params=pltpu.CompilerParams(dimension_semantics=("arbitrary",)),
        name="route",
    )(lg)


def _plan_kernel(info_ref, cnt_ref, slot_ref, meta_ref):
    cnt = cnt_ref[...]
    lane_r = lax.broadcasted_iota(I32, (1, LANES), 1)
    nblk = jnp.floor((cnt + (MOE_BLOCK - 1)) * (1.0 / MOE_BLOCK))
    ei = lax.broadcasted_iota(I32, (LANES, LANES), 0)
    ej = lax.broadcasted_iota(I32, (LANES, LANES), 1)
    upper = jnp.where(ei <= ej, 1.0, 0.0).astype(BF16)
    nb8 = jnp.broadcast_to(nblk, (8, LANES)).astype(BF16)
    bend = _dot(nb8, upper)[0:1, :]
    bstart = bend - nblk
    pstart = bstart * MOE_BLOCK

    info = info_ref[...]
    lane = lax.broadcasted_iota(I32, info.shape, 1)
    ex1 = info[:, 0:1].astype(I32)
    ex2 = info[:, 1:2].astype(I32)
    s1 = jnp.sum(jnp.where(lane == ex1, pstart, 0.0), axis=-1, keepdims=True) + info[:, 2:3]
    s2 = jnp.sum(jnp.where(lane == ex2, pstart, 0.0), axis=-1, keepdims=True) + info[:, 3:4]
    slot = jnp.where(lane == 0, s1, 0.0)
    slot = jnp.where(lane == 1, s2, slot)
    slot_ref[...] = slot.astype(I32)

    blk = lax.broadcasted_iota(I32, (LANES, 2 * LANES), 1).astype(F32)
    bend_col = jnp.sum(jnp.where(ei == ej, jnp.broadcast_to(bend, (LANES, LANES)), 0.0),
                       axis=-1, keepdims=True)
    erow = lax.broadcasted_iota(I32, (LANES, 2 * LANES), 0)
    hit = jnp.where((bend_col <= blk) & (erow < N_EXPERTS), 1.0, 0.0)
    bexp = jnp.minimum(jnp.sum(hit, axis=0, keepdims=True), N_EXPERTS - 1.0)
    n_used = jnp.max(jnp.where(lane_r < N_EXPERTS, bend, 0.0), axis=-1, keepdims=True)
    lastblk = jnp.where(nblk > 0, bend - 1.0, -1.0)
    r8 = lax.broadcasted_iota(I32, (8, 2 * LANES), 0)
    last2 = jnp.concatenate([lastblk, jnp.full((1, LANES), -1.0)], axis=1)
    meta = jnp.where(r8 == 0, jnp.broadcast_to(bexp, (8, 2 * LANES)), 0.0)
    meta = jnp.where(r8 == 1, jnp.broadcast_to(n_used, (8, 2 * LANES)), meta)
    meta = jnp.where(r8 == 2, jnp.broadcast_to(last2, (8, 2 * LANES)), meta)
    meta_ref[...] = meta.astype(I32)


def _plan(info, cnt):
    M = info.shape[0]
    tm = 256
    return pl.pallas_call(
        _plan_kernel,
        grid=(M // tm,),
        in_specs=[pl.BlockSpec((tm, LANES), lambda i: (i, 0)),
                  pl.BlockSpec((1, LANES), lambda i: (0, 0))],
        out_specs=[pl.BlockSpec((tm, LANES), lambda i: (i, 0)),
                   pl.BlockSpec((8, 2 * LANES), lambda i: (0, 0))],
        out_shape=[jax.ShapeDtypeStruct((M, LANES), I32),
                   jax.ShapeDtypeStruct((8, 2 * LANES), I32)],
        compiler_params=pltpu.CompilerParams(dimension_semantics=("arbitrary",)),
        name="plan",
    )(info, cnt)


ROW_DEPTH = 3
DMA_GROUP = 16
WEIGHT_DEPTH = 3


def _slot_table_kernel(slots_ref, inv_ref, *, n_assign, n_slots):
    def init(b, c):
        base = n_assign + lax.rem(b + ROW_DEPTH - 1, ROW_DEPTH) * MOE_BLOCK
        for j in range(MOE_BLOCK):
            inv_ref[b * MOE_BLOCK + j] = base + j
        return c

    def put(a, c):
        inv_ref[slots_ref[a] + MOE_BLOCK] = a
        return c

    lax.fori_loop(0, n_slots // MOE_BLOCK, init, 0)
    lax.fori_loop(0, n_assign, put, 0, unroll=16)


def _slot_table(slots_km, n_blocks):
    n_assign = slots_km.shape[0]
    n_slots = (n_blocks + ROW_DEPTH) * MOE_BLOCK
    gs = pltpu.PrefetchScalarGridSpec(
        num_scalar_prefetch=1, grid=(1,), in_specs=[],
        out_specs=pl.BlockSpec(memory_space=pltpu.SMEM))
    return pl.pallas_call(
        functools.partial(_slot_table_kernel, n_assign=n_assign, n_slots=n_slots),
        grid_spec=gs,
        out_shape=jax.ShapeDtypeStruct((n_slots,), I32),
        compiler_params=pltpu.CompilerParams(dimension_semantics=("arbitrary",)),
        name="slot_table",
    )(slots_km)


def _experts_kernel(bexp_ref, seg_ref, inv_ref, h2_ref, w1_ref, w3_ref, w2_ref, y2_ref,
                    xbuf, ybuf, w1buf, w3buf, w2buf, sem_x, sem_y, sem_w, par_ref, *, n_assign):
    i = pl.program_id(0)
    n_used = seg_ref[N_EXPERTS]
    n_tok = n_assign // 2
    B = MOE_BLOCK
    FF = w1buf.shape[2]
    D = w1buf.shape[1]
    ND = ROW_DEPTH
    NW = w1buf.shape[0]

    def src_row(a):
        if n_tok & (n_tok - 1) == 0:
            return a & (n_tok - 1)
        return lax.rem(a, n_tok)

    def weight_copies(e, slot):
        return (pltpu.make_async_copy(w1_ref.at[e], w1buf.at[slot], sem_w.at[slot]),
                pltpu.make_async_copy(w3_ref.at[e], w3buf.at[slot], sem_w.at[slot]),
                pltpu.make_async_copy(w2_ref.at[e], w2buf.at[slot], sem_w.at[slot]))

    def start_weights(e, slot):
        c1, c3, c2 = weight_copies(e, slot)
        c1.start(priority=1)
        c3.start(priority=1)
        c2.start(priority=0)

    def prefetch_expert_after(e, hops, slot):
        ok = hops >= 0
        cur = e
        for _ in range(hops):
            nb = seg_ref[cur] + 1
            ok = jnp.logical_and(ok, nb < n_used)
            cur = bexp_ref[jnp.where(ok, nb, 0)]

        @pl.when(ok)
        def _():
            start_weights(cur, slot)

    def gather_group(b, g):
        slot = lax.rem(b + ND, ND)
        for r in range(g * DMA_GROUP, (g + 1) * DMA_GROUP):
            a = inv_ref[(b + 1) * B + r]
            pltpu.make_async_copy(h2_ref.at[pl.ds(src_row(a), 1)], xbuf.at[slot, pl.ds(r, 1)],
                                  sem_x.at[slot]).start()

    def scatter_group(b, g):
        slot = lax.rem(b + ND, ND)
        for r in range(g * DMA_GROUP, (g + 1) * DMA_GROUP):
            a = inv_ref[(b + 1) * B + r]
            pltpu.make_async_copy(ybuf.at[slot, pl.ds(r, 1)], y2_ref.at[pl.ds(a, 1)],
                                  sem_y.at[slot]).start()

    def wait_block(buf, sem, b):
        slot = lax.rem(b + ND, ND)
        pltpu.make_async_copy(buf.at[slot], buf.at[slot], sem.at[slot]).wait()

    n_groups = B // DMA_GROUP

    @pl.when(i == 0)
    def _():
        ybuf[...] = jnp.zeros_like(ybuf)
        for s in range(ND - 1):
            pltpu.make_async_copy(ybuf.at[s], y2_ref.at[pl.ds(n_assign + s * B, B)], sem_y.at[s]).start()
        par_ref[0] = 0
        e0 = bexp_ref[0]
        start_weights(e0, 0)
        for hops in range(1, NW):
            prefetch_expert_after(e0, hops, hops)
        for b in range(ND - 1):
            for g in range(n_groups):
                gather_group(b, g)

    active = i < n_used

    @pl.when(active)
    def _():
        e = bexp_ref[i]
        first = jnp.logical_or(i == 0, e != bexp_ref[jnp.maximum(i - 1, 0)])

        @pl.when(jnp.logical_and(first, i > 0))
        def _():
            par_ref[0] = par_ref[0] + 1

        @pl.when(first)
        def _():
            q = par_ref[0]
            for cp in weight_copies(e, lax.rem(q, NW)):
                cp.wait()

            @pl.when(i > 0)
            def _():
                prefetch_expert_after(e, NW - 1, lax.rem(q + NW - 1, NW))

        p = lax.rem(par_ref[0], NW)
        slot = lax.rem(i, ND)
        wait_block(xbuf, sem_x, i)
        wait_block(ybuf, sem_y, i)
        pending = [functools.partial(scatter_group, i - 1, g) for g in range(n_groups)]
        pending += [functools.partial(gather_group, i + ND - 1, g) for g in range(n_groups)]
        n_chunks = 2 * (FF // 256) + ROW_SLABS
        for _ in range(max(len(pending) - n_chunks, 0)):
            pending.pop(0)()

        def after_chunk():
            if pending:
                pending.pop(0)()

        x = jnp.concatenate([xbuf[slot, :, j, :] for j in range(ROW_SLABS)], axis=1).astype(BF16)
        sw = D // ROW_SLABS
        h1c, h3c = [], []
        for c0 in range(0, FF, 256):
            h1c.append(_dot(x, w1buf[p, :, c0:c0 + 256]))
            after_chunk()
            h3c.append(_dot(x, w3buf[p, :, c0:c0 + 256]))
            after_chunk()
        h1 = jnp.concatenate(h1c, axis=1)
        h3 = jnp.concatenate(h3c, axis=1)
        hid = (h1 * (1.0 / (1.0 + jnp.exp(-h1))) * h3).astype(BF16)
        for j in range(ROW_SLABS):
            ybuf[slot, :, j, :] = _dot(hid, w2buf[p, :, j * sw:(j + 1) * sw])
            after_chunk()
        while pending:
            pending.pop(0)()

    @pl.when(i == n_used)
    def _():
        wait_block(xbuf, sem_x, i)
        wait_block(ybuf, sem_y, i)
        for g in range(n_groups):
            scatter_group(i - 1, g)

    @pl.when(jnp.logical_and(i > n_used, i < n_used + ND - 1))
    def _():
        wait_block(xbuf, sem_x, i)
        wait_block(ybuf, sem_y, i)

    @pl.when(i == n_used + ND - 1)
    def _():
        wait_block(ybuf, sem_y, i)


def _experts(bexp, seg_meta, inv, h2, w1, w3, w2, n_blocks):
    M, DJ, SW = h2.shape
    D = DJ * SW
    FF = w1.shape[2]
    n_assign = 2 * M
    any_spec = pl.BlockSpec(memory_space=pl.ANY)
    row_buf = pltpu.VMEM((ROW_DEPTH, MOE_BLOCK, DJ, SW), F32)
    gs = pltpu.PrefetchScalarGridSpec(
        num_scalar_prefetch=3,
        grid=(n_blocks + ROW_DEPTH,),
        in_specs=[any_spec, any_spec, any_spec, any_spec],
        out_specs=any_spec,
        scratch_shapes=[row_buf, row_buf,
                        pltpu.VMEM((WEIGHT_DEPTH, D, FF), F32), pltpu.VMEM((WEIGHT_DEPTH, D, FF), F32),
                        pltpu.VMEM((WEIGHT_DEPTH, FF, D), F32),
                        pltpu.SemaphoreType.DMA((ROW_DEPTH,)), pltpu.SemaphoreType.DMA((ROW_DEPTH,)),
                        pltpu.SemaphoreType.DMA((WEIGHT_DEPTH,)), pltpu.SMEM((1,), I32)],
    )
    vm = WEIGHT_DEPTH * 3 * D * FF * 4 + 2 * ROW_DEPTH * MOE_BLOCK * D * 4 + (12 << 20)
    return pl.pallas_call(
        functools.partial(_experts_kernel, n_assign=n_assign),
        grid_spec=gs,
        out_shape=jax.ShapeDtypeStruct((n_assign + ROW_DEPTH * MOE_BLOCK, DJ, SW), F32),
        compiler_params=pltpu.CompilerParams(
            dimension_semantics=("arbitrary",), vmem_limit_bytes=_vmem_limit(vm),
            has_side_effects=True),
        name="experts",
    )(bexp, seg_meta, inv, h2, w1, w3, w2)


def _combine_kernel(x1_ref, info_ref, gt_ref, fg_ref, ya_ref, yb_ref, o_ref):
    info = info_ref[...]
    nj = ya_ref.shape[1]
    ya = jnp.concatenate([ya_ref[:, j, :] for j in range(nj)], axis=1)
    yb = jnp.concatenate([yb_ref[:, j, :] for j in range(nj)], axis=1)
    moe = info[:, 4:5] * ya + info[:, 5:6] * yb
    x = x1_ref[...] + gt_ref[0] * moe
    ms = jnp.mean(x * x, axis=-1, keepdims=True)
    o_ref[...] = x * lax.rsqrt(ms + NORM_EPS) * fg_ref[...]


def _combine(x1, info, gt, final_g, y2, T):
    M, D = x1.shape
    tm = 256
    tps = T // tm
    nt = M // tm
    vm = 2 * 4 * tm * D * 4 + tm * D * 16 + (8 << 20)
    return pl.pallas_call(
        _combine_kernel,
        grid=(nt,),
        in_specs=[pl.BlockSpec((tm, D), lambda i: (i, 0)),
                  pl.BlockSpec((tm, LANES), lambda i: (i, 0)),
                  pl.BlockSpec((1, 1, D), lambda i: (i // tps, 0, 0)),
                  pl.BlockSpec((1, D), lambda i: (0, 0)),
                  pl.BlockSpec((tm, ROW_SLABS, D // ROW_SLABS), lambda i: (i, 0, 0)),
                  pl.BlockSpec((tm, ROW_SLABS, D // ROW_SLABS), lambda i: (nt + i, 0, 0))],
        out_specs=pl.BlockSpec((tm, D), lambda i: (i, 0)),
        out_shape=jax.ShapeDtypeStruct((M, D), F32),
        compiler_params=pltpu.CompilerParams(
            dimension_semantics=("arbitrary",), vmem_limit_bytes=_vmem_limit(vm)),
        name="combine",
    )(x1, info, gt, final_g.reshape(1, D), y2, y2)


def _rope_tables(T):
    inv_freq = ROPE_THETA ** (-np.arange(0, HEAD_DIM, 2, dtype=np.float64) / HEAD_DIM)
    ang = np.arange(T, dtype=np.float64)[:, None] * inv_freq[None, :]
    cos, sin = np.cos(ang), np.sin(ang)
    cos_h = np.concatenate([cos, cos], axis=1)
    sin_h = np.concatenate([-sin, sin], axis=1)
    return (jnp.asarray(np.tile(cos_h, (1, 2)), F32), jnp.asarray(np.tile(sin_h, (1, 2)), F32))


def _layer(x, c, w_ada, b_ada, norm1_g, w_in, mu_shift, sinks, w0, w_decay_up, a0, w_a_up, w_g_up,
           k_k, k_a, r_k, ln_x_w, ln_x_b, w_o, norm2_g, w_rg, b_rg, w_re, b_re, w1, w3, w2, final_g,
           apply_final):
    B, T, D = x.shape
    M = B * T
    mod = _ada(c, w_ada, b_ada)
    sh1, sc1, gt1, sh2, sc2, gt2 = [m.reshape(B, 1, D) for m in jnp.split(mod, 6, axis=-1)]

    pad = RWKV_PAD - RWKV_COLS
    w_bf = _cast_pad(w_in, ATTN_COLS + RWKV_PAD)
    mu_pad = jnp.pad(mu_shift, (0, pad)).reshape(1, RWKV_PAD)
    cos_t, sin_t = _rope_tables(T)
    x2 = x.reshape(M, D)
    za, zr = _inproj(x2, norm1_g.reshape(1, D), sh1, sc1, w_bf, mu_pad, cos_t, sin_t, T)

    o_attn = _attention(za.reshape(B, T, ATTN_COLS), sinks)
    o_rwkv = _rwkv(zr.reshape(B, T, RWKV_PAD), w0, w_decay_up, a0, w_a_up, w_g_up, k_k, k_a,
                   r_k.reshape(-1), ln_x_w, ln_x_b)

    wr = jnp.concatenate([w_rg, w_re], axis=1)
    wr = jnp.pad(wr, ((0, 0), (0, LANES - wr.shape[1])))
    wr_hi = wr.astype(BF16)
    wr_lo = (wr - wr_hi.astype(F32)).astype(BF16)
    br = jnp.pad(jnp.concatenate([b_rg, b_re]), (0, LANES - N_GROUPS - N_EXPERTS)).reshape(1, LANES)
    x1, h2, lg = _outproj(o_attn.reshape(M, Q_COLS), o_rwkv.reshape(M, RWKV_W), x2, w_o.astype(BF16),
                          gt1, norm2_g.reshape(1, D), sh2, sc2, jnp.concatenate([wr_hi, wr_lo], axis=1),
                          wr_hi, br, T)

    n_blocks = -(-(2 * M) // MOE_BLOCK) + N_EXPERTS
    info, cnt = _route(lg)
    slots, meta = _plan(info, cnt)
    slots_km = slots[:, :2].T.reshape(-1)
    seg_meta = jnp.concatenate([meta[2, :N_EXPERTS], meta[1, :1]])
    inv = _slot_table(slots_km, n_blocks)
    y2 = _experts(meta[0, :n_blocks + ROW_DEPTH], seg_meta, inv, h2, w1, w3, w2, n_blocks)
    out = _combine(x1, info, gt2, final_g, y2, T)
    del apply_final
    return out.reshape(B, T, D)


def kernel(x, c, w_ada, b_ada, norm1_g, w_in, mu_shift, sinks, w0, w_decay_up, a0, w_a_up, w_g_up, k_k, k_a, r_k, ln_x_w, ln_x_b, w_o, norm2_g, w_router_group, b_router_group, w_router_expert, b_router_expert, w1, w3, w2, final_g):
    depth = w_ada.shape[0]
    assert depth == 1, "single-layer stack"
    l = 0
    return _layer(x, c, w_ada[l], b_ada[l], norm1_g[l], w_in[l], mu_shift[l], sinks[l], w0[l],
                  w_decay_up[l], a0[l], w_a_up[l], w_g_up[l], k_k[l], k_a[l], r_k[l], ln_x_w[l],
                  ln_x_b[l], w_o[l], norm2_g[l], w_router_group[l], b_router_group[l],
                  w_router_expert[l], b_router_expert[l], w1[l], w3[l], w2[l], final_g, True)
```

```python
import functools
import math

import jax
import jax.numpy as jnp
import numpy as np
from jax import lax
from jax.experimental import pallas as pl
from jax.experimental.pallas import tpu as pltpu

F32 = jnp.float32
BF16 = jnp.bfloat16
I32 = jnp.int32

LANES = 128
VMEM_BYTES_V7X = 64 * 1024 * 1024

HEAD_DIM = 64
ATTN_HEADS = 16
ATTN_KV_HEADS = 2
ATTN_GROUP = ATTN_HEADS // ATTN_KV_HEADS
WINDOW = 128
ROPE_THETA = 10000.0
RWKV_HEADS = 16
RWKV_N = 64
DECAY_LORA = 64
A_LORA = 64
GATE_LORA = 160
RWKV_LN_EPS = 64e-5
N_GROUPS = 8
EXPERTS_PER_GROUP = 8
N_EXPERTS = N_GROUPS * EXPERTS_PER_GROUP
EXPERT_FF = 512
MOE_BLOCK = 128
ROW_SLABS = 8
NORM_EPS = 1e-6

Q_COLS = ATTN_HEADS * HEAD_DIM
KV_COLS = ATTN_KV_HEADS * HEAD_DIM
ATTN_COLS = Q_COLS + 2 * KV_COLS
RWKV_W = RWKV_HEADS * RWKV_N
LORA_COLS = DECAY_LORA + A_LORA + GATE_LORA
LORA_PAD = 384
RWKV_COLS = 3 * RWKV_W + LORA_COLS
RWKV_PAD = 3 * RWKV_W + LORA_PAD
CHUNK = 64
RWKV_GROUP = 4
NEG_BIG = -1e30


def _vmem_limit(nbytes):
    return int(min(nbytes, VMEM_BYTES_V7X - 4 * 1024 * 1024))


def _dot(a, b):
    return jnp.dot(a, b, preferred_element_type=F32)


def _dot_nt(a, b):
    return lax.dot_general(a, b, (((1,), (1,)), ((), ())), preferred_element_type=F32)


def _dot_tn(a, b):
    return lax.dot_general(a, b, (((0,), (0,)), ((), ())), preferred_element_type=F32)


def _ada_kernel(c_ref, w_ref, b_ref, o_ref):
    c = c_ref[...]
    s = c * (1.0 / (1.0 + jnp.exp(-c)))
    o_ref[...] = _dot(s, w_ref[...]) + b_ref[...]


def _ada(c, w_ada, b_ada):
    B, D = c.shape
    N = w_ada.shape[1]
    tn = 1024
    cp = jnp.zeros((8, D), F32).at[:B].set(c)
    out = pl.pallas_call(
        _ada_kernel,
        grid=(N // tn,),
        in_specs=[pl.BlockSpec((8, D), lambda j: (0, 0)),
                  pl.BlockSpec((D, tn), lambda j: (0, j)),
                  pl.BlockSpec((1, tn), lambda j: (0, j))],
        out_specs=pl.BlockSpec((8, tn), lambda j: (0, j)),
        out_shape=jax.ShapeDtypeStruct((8, N), F32),
        compiler_params=pltpu.CompilerParams(
            dimension_semantics=("arbitrary",),
            vmem_limit_bytes=_vmem_limit(2 * D * tn * 4 + (8 << 20))),
        name="ada",
    )(cp, w_ada, b_ada.reshape(1, N))
    return out[:B]


def _cast_pad_kernel(wt_ref, o_ref, *, n_valid):
    j = pl.program_id(0)
    tn = wt_ref.shape[0]
    row = j * tn + lax.broadcasted_iota(I32, wt_ref.shape, 0)
    wt = jnp.where(row < n_valid, wt_ref[...], 0.0)
    o_ref[...] = wt.T.astype(o_ref.dtype)


def _cast_pad(w_t, n_out):
    N, K = w_t.shape
    tn = 512
    return pl.pallas_call(
        functools.partial(_cast_pad_kernel, n_valid=N),
        grid=(pl.cdiv(n_out, tn),),
        in_specs=[pl.BlockSpec((tn, K), lambda j: (j, 0))],
        out_specs=pl.BlockSpec((K, tn), lambda j: (0, j)),
        out_shape=jax.ShapeDtypeStruct((K, n_out), BF16),
        compiler_params=pltpu.CompilerParams(dimension_semantics=("arbitrary",)),
        name="cast_pad",
    )(w_t)


def _rope(z, cos, sin):
    w = z.shape[-1]
    lane = lax.broadcasted_iota(I32, z.shape, 1)
    first_half = (lane % HEAD_DIM) < (HEAD_DIM // 2)
    partner = jnp.where(first_half, pltpu.roll(z, w - HEAD_DIM // 2, 1), pltpu.roll(z, HEAD_DIM // 2, 1))
    return z * cos + partner * sin


def _inproj_kernel(x_ref, g_ref, sh_ref, sc_ref, w_ref, mu_ref, cos_ref, sin_ref,
                   za_ref, zr_ref, carry_ref, *, tiles_per_seq):
    i = pl.program_id(0)
    tm = x_ref.shape[0]
    x = x_ref[...]
    ms = jnp.mean(x * x, axis=-1, keepdims=True)
    y = x * lax.rsqrt(ms + NORM_EPS) * g_ref[...]
    h = (y * (1.0 + sc_ref[0]) + sh_ref[0]).astype(BF16)

    cos2 = jnp.concatenate([cos_ref[...], cos_ref[...]], axis=1)
    sin2 = jnp.concatenate([sin_ref[...], sin_ref[...]], axis=1)
    for c0 in range(0, ATTN_COLS, 256):
        z = _dot(h, w_ref[:, c0:c0 + 256])
        if c0 < Q_COLS:
            z = _rope(z, cos2, sin2)
        else:
            z = jnp.concatenate([_rope(z[:, :KV_COLS], cos_ref[...], sin_ref[...]), z[:, KV_COLS:]], axis=1)
        za_ref[:, c0:c0 + 256] = z.astype(za_ref.dtype)

    first = (i % tiles_per_seq) == 0
    row = lax.broadcasted_iota(I32, (tm, 1), 0)
    c0 = 0
    while c0 < RWKV_PAD:
        n = min(512, RWKV_PAD - c0)
        z = _dot(h, w_ref[:, ATTN_COLS + c0:ATTN_COLS + c0 + n])
        prev_last = jnp.where(first, 0.0, carry_ref[7:8, c0:c0 + n])
        z_prev = jnp.where(row == 0, prev_last, pltpu.roll(z, 1, 0))
        carry_ref[:, c0:c0 + n] = z[tm - 8:tm, :]
        zr_ref[:, c0:c0 + n] = z + (z_prev - z) * mu_ref[:, c0:c0 + n]
        c0 += n


def _inproj(x2, g, sh, sc, w_bf, mu_pad, cos_t, sin_t, T):
    M, D = x2.shape
    tm = 256
    tps = T // tm
    NW = w_bf.shape[1]
    kern = functools.partial(_inproj_kernel, tiles_per_seq=tps)
    vm = (D * NW * 2 + 2 * tm * D * 4 + 2 * tm * ATTN_COLS * 2 + 2 * tm * RWKV_PAD * 4
          + 4 * tm * 512 * 4 + tm * D * 8 + (8 << 20))
    return pl.pallas_call(
        kern,
        grid=(M // tm,),
        in_specs=[pl.BlockSpec((tm, D), lambda i: (i, 0)),
                  pl.BlockSpec((1, D), lambda i: (0, 0)),
                  pl.BlockSpec((1, 1, D), lambda i: (i // tps, 0, 0)),
                  pl.BlockSpec((1, 1, D), lambda i: (i // tps, 0, 0)),
                  pl.BlockSpec((D, NW), lambda i: (0, 0), pipeline_mode=pl.Buffered(1)),
                  pl.BlockSpec((1, RWKV_PAD), lambda i: (0, 0)),
                  pl.BlockSpec((tm, 2 * HEAD_DIM), lambda i: (i % tps, 0)),
                  pl.BlockSpec((tm, 2 * HEAD_DIM), lambda i: (i % tps, 0))],
        out_specs=[pl.BlockSpec((tm, ATTN_COLS), lambda i: (i, 0)),
                   pl.BlockSpec((tm, RWKV_PAD), lambda i: (i, 0))],
        out_shape=[jax.ShapeDtypeStruct((M, ATTN_COLS), BF16),
                   jax.ShapeDtypeStruct((M, RWKV_PAD), F32)],
        scratch_shapes=[pltpu.VMEM((8, RWKV_PAD), F32)],
        compiler_params=pltpu.CompilerParams(
            dimension_semantics=("arbitrary",), vmem_limit_bytes=_vmem_limit(vm)),
        name="inproj",
    )(x2, g, sh, sc, w_bf, mu_pad, cos_t, sin_t)


def _attn_kernel(sink_ref, q_ref, kc_ref, kp_ref, vc_ref, vp_ref, o_ref):
    n = pl.program_id(1)
    blk = q_ref.shape[1]
    row = lax.broadcasted_iota(I32, (blk, blk), 0)
    col = lax.broadcasted_iota(I32, (blk, blk), 1)
    mask = jnp.concatenate([(col > row) & (n > 0), col <= row], axis=1)
    scale = 1.0 / math.sqrt(HEAD_DIM)
    outs = []
    for kvh in range(ATTN_KV_HEADS):
        ks = slice(kvh * HEAD_DIM, (kvh + 1) * HEAD_DIM)
        kmat = jnp.concatenate([kp_ref[0, :, ks], kc_ref[0, :, ks]], axis=0)
        vmat = jnp.concatenate([vp_ref[0, :, ks], vc_ref[0, :, ks]], axis=0)
        for g in range(ATTN_GROUP):
            hd = kvh * ATTN_GROUP + g
            qh = q_ref[0, :, hd * HEAD_DIM:(hd + 1) * HEAD_DIM]
            s = _dot_nt(qh, kmat) * scale
            s = jnp.where(mask, s, NEG_BIG)
            sink = sink_ref[hd]
            m = jnp.maximum(jnp.max(s, axis=-1, keepdims=True), sink)
            p = jnp.exp(s - m)
            denom = jnp.sum(p, axis=-1, keepdims=True) + jnp.exp(sink - m)
            o = _dot(p.astype(BF16), vmat)
            outs.append(o / denom)
    o_ref[0] = jnp.concatenate(outs, axis=1).astype(o_ref.dtype)


def _attention(za3, sinks):
    B, T, _ = za3.shape
    nb = T // WINDOW
    kcol = Q_COLS // KV_COLS
    prev = lambda b, n, s: (b, jnp.maximum(n - 1, 0), kcol)
    prev_v = lambda b, n, s: (b, jnp.maximum(n - 1, 0), kcol + 1)
    gs = pltpu.PrefetchScalarGridSpec(
        num_scalar_prefetch=1,
        grid=(B, nb),
        in_specs=[pl.BlockSpec((1, WINDOW, Q_COLS), lambda b, n, s: (b, n, 0)),
                  pl.BlockSpec((1, WINDOW, KV_COLS), lambda b, n, s: (b, n, kcol)),
                  pl.BlockSpec((1, WINDOW, KV_COLS), prev),
                  pl.BlockSpec((1, WINDOW, KV_COLS), lambda b, n, s: (b, n, kcol + 1)),
                  pl.BlockSpec((1, WINDOW, KV_COLS), prev_v)],
        out_specs=pl.BlockSpec((1, WINDOW, Q_COLS), lambda b, n, s: (b, n, 0)),
    )
    return pl.pallas_call(
        _attn_kernel,
        grid_spec=gs,
        out_shape=jax.ShapeDtypeStruct((B, T, Q_COLS), BF16),
        compiler_params=pltpu.CompilerParams(dimension_semantics=("arbitrary", "arbitrary")),
        name="attn",
    )(sinks, za3, za3, za3, za3, za3)


def _rwkv_kernel(r_ref, k_ref, v_ref, lora_ref, w0_ref, wdu_ref, a0_ref, wau_ref, wgu_ref,
                 kk_ref, ka_ref, rk_ref, lnw_ref, lnb_ref, o_ref, s_ref):
    c = pl.program_id(0)
    nseq = r_ref.shape[0]
    C = r_ref.shape[1]
    N = RWKV_N
    G = RWKV_GROUP
    GW = G * N
    ng = RWKV_HEADS // G

    @pl.when(c == 0)
    def _():
        s_ref[...] = jnp.zeros_like(s_ref)

    ti = lax.broadcasted_iota(I32, (C, C), 0)
    si = lax.broadcasted_iota(I32, (C, C), 1)
    tril = jnp.where(si <= ti, 1.0, 0.0).astype(BF16)
    a_seq, g_seq, e_in_seq, e_ex_seq, e_neg_seq = [], [], [], [], []
    for b in range(nseq):
        lora = lora_ref[b]
        wd = lora[:, 0:DECAY_LORA]
        ad = lora[:, DECAY_LORA:DECAY_LORA + A_LORA]
        gd = lora[:, DECAY_LORA + A_LORA:LORA_COLS]
        wlin = w0_ref[...] + _dot(jnp.tanh(wd), wdu_ref[...])
        neg = -wlin
        softplus = jnp.maximum(neg, 0.0) + jnp.log(1.0 + jnp.exp(-jnp.abs(neg)))
        w = -softplus - 0.5
        logdec = -jnp.exp(w)
        a_seq.append(1.0 / (1.0 + jnp.exp(-(a0_ref[...] + _dot(ad, wau_ref[...])))))
        g_seq.append(_dot(1.0 / (1.0 + jnp.exp(-gd)), wgu_ref[...]))
        ld_hi = logdec.astype(BF16)
        ld_lo = (logdec - ld_hi.astype(F32)).astype(BF16)
        cum = _dot(tril, ld_hi) + _dot(tril, ld_lo)
        e_in_seq.append(jnp.exp(cum))
        e_ex_seq.append(jnp.exp(cum - logdec))
        e_neg_seq.append(jnp.exp(-cum))

    lane_head = lax.broadcasted_iota(I32, (1, GW), 1) // N
    t_row = lax.broadcasted_iota(I32, (C, GW), 0)
    s_lane = lax.broadcasted_iota(I32, (C, GW), 1) % N
    strict = s_lane < t_row
    incl = s_lane <= t_row
    bi = lax.broadcasted_iota(I32, (GW, GW), 0)
    bj = lax.broadcasted_iota(I32, (GW, GW), 1)
    same_head = (bi // N) == (bj // N)
    ones_bd = jnp.where(same_head, 1.0, 0.0).astype(BF16)
    eye = jnp.where(s_lane == t_row, 1.0, 0.0)

    def expand(xc):
        return jnp.concatenate([jnp.where(lane_head == h, xc, 0.0) for h in range(G)], axis=0)

    def head_sum(xs):
        s = _dot(jnp.concatenate(xs, axis=0).astype(BF16), ones_bd)
        return [s[i * C:(i + 1) * C] for i in range(len(xs))]

    seq = [b for b in range(nseq) for _ in range(ng)]
    col = [slice(gi * GW, (gi + 1) * GW) for _ in range(nseq) for gi in range(ng)]
    rng = range(nseq * ng)
    r_l = [r_ref[seq[i], :, col[i]] for i in rng]
    k_l = [k_ref[seq[i], :, col[i]] for i in rng]
    v_l = [v_ref[seq[i], :, col[i]] for i in rng]
    a_l = [a_seq[seq[i]][:, col[i]] for i in rng]
    e_in = [e_in_seq[seq[i]][:, col[i]] for i in rng]
    e_neg = [e_neg_seq[seq[i]][:, col[i]] for i in rng]
    kk0 = [k_l[i] * kk_ref[:, col[i]] for i in rng]
    nrm2 = head_sum([kk0[i] * kk0[i] for i in rng])
    kk = [kk0[i] / jnp.maximum(jnp.sqrt(nrm2[i]), 1e-12) for i in rng]
    k2 = [k_l[i] * (1.0 + (a_l[i] - 1.0) * ka_ref[:, col[i]]) for i in rng]
    bt = [kk[i] * a_l[i] * e_neg[i] for i in rng]
    kt = [k2[i] * e_neg[i] for i in rng]
    left = [jnp.concatenate([-kk[i] * e_ex_seq[seq[i]][:, col[i]], r_l[i] * e_in[i]], axis=0).astype(BF16)
            for i in rng]
    right = [jnp.concatenate([bt[i], kt[i]], axis=0).astype(BF16) for i in rng]
    zed = [jnp.concatenate([expand(bt[i]), expand(kt[i])], axis=0).astype(BF16) for i in rng]
    big = [_dot_nt(left[i], zed[i]) for i in rng]
    s_old = [s_ref[i] for i in rng]
    ls = [_dot_nt(left[i], s_old[i].astype(BF16)) for i in rng]
    v_bd = [expand(v_l[i]).astype(BF16) for i in rng]
    pw = [jnp.where(strict, big[i][:C, :G * C], 0.0) for i in rng]
    tinv = [eye + pw[i] for i in rng]
    pw_bd = [expand(pw[i]).astype(BF16) for i in rng]
    x = [ls[i][:C] + _dot(jnp.where(strict, big[i][:C, G * C:], 0.0).astype(BF16), v_bd[i]) for i in rng]
    span = 2
    while span < C:
        pw = [_dot(pw[i].astype(BF16), pw_bd[i]) for i in rng]
        pw_bd = [expand(pw[i]).astype(BF16) for i in rng]
        tinv = [tinv[i] + _dot(tinv[i].astype(BF16), pw_bd[i]) for i in rng]
        span *= 2
    u = [_dot(tinv[i].astype(BF16), expand(x[i]).astype(BF16)) for i in rng]
    a_r = [jnp.concatenate([jnp.where(incl, big[i][C:, :G * C], 0.0),
                            jnp.where(incl, big[i][C:, G * C:], 0.0)], axis=1).astype(BF16) for i in rng]
    y = [ls[i][C:] + _dot(a_r[i], jnp.concatenate([expand(u[i]).astype(BF16), v_bd[i]], axis=0)) for i in rng]
    uv = [jnp.concatenate([u[i], v_l[i]], axis=0).astype(BF16) for i in rng]
    for i in rng:
        s_ref[i] = ((s_old[i] + jnp.where(same_head, _dot_tn(uv[i], right[i]), 0.0))
                    * e_in[i][C - 1:C, :])
    ysum = head_sum(y)
    yc = [y[i] - ysum[i] * (1.0 / N) for i in rng]
    ysq = head_sum([yc[i] * yc[i] for i in rng])
    var = [ysq[i] * (1.0 / N) for i in rng]
    rksum = head_sum([r_l[i] * k2[i] * rk_ref[:, col[i]] for i in rng])
    bonus = [rksum[i] * v_l[i] for i in rng]
    for i in rng:
        yn = yc[i] * lax.rsqrt(var[i] + RWKV_LN_EPS) * lnw_ref[:, col[i]] + lnb_ref[:, col[i]]
        o_ref[seq[i], :, col[i]] = ((yn + bonus[i]) * g_seq[seq[i]][:, col[i]]).astype(o_ref.dtype)


def _rwkv(zr3, w0, wdu, a0, wau, wgu, k_k, k_a, r_k, ln_w, ln_b):
    B, T, _ = zr3.shape
    C = CHUNK
    W = RWKV_W
    GW = RWKV_GROUP * RWKV_N
    vec = lambda v: v.reshape(1, W)
    full = lambda shape: pl.BlockSpec(shape, lambda c: (0,) * len(shape))
    return pl.pallas_call(
        _rwkv_kernel,
        grid=(T // C,),
        in_specs=[pl.BlockSpec((B, C, W), lambda c: (0, c, 0)),
                  pl.BlockSpec((B, C, W), lambda c: (0, c, 1)),
                  pl.BlockSpec((B, C, W), lambda c: (0, c, 2)),
                  pl.BlockSpec((B, C, LORA_PAD), lambda c: (0, c, 3 * W // LORA_PAD)),
                  full((1, W)), full((DECAY_LORA, W)), full((1, W)), full((A_LORA, W)),
                  full((GATE_LORA, W)), full((1, W)), full((1, W)), full((1, W)),
                  full((1, W)), full((1, W))],
        out_specs=pl.BlockSpec((B, C, W), lambda c: (0, c, 0)),
        out_shape=jax.ShapeDtypeStruct((B, T, W), BF16),
        scratch_shapes=[pltpu.VMEM((B * RWKV_HEADS // RWKV_GROUP, GW, GW), F32)],
        compiler_params=pltpu.CompilerParams(dimension_semantics=("arbitrary",)),
        name="rwkv",
    )(zr3, zr3, zr3, zr3, vec(w0), wdu, vec(a0), wau, wgu, vec(k_k), vec(k_a), vec(r_k),
      vec(ln_w), vec(ln_b))


def _outproj_kernel(oa_ref, or_ref, x_ref, wo_ref, gt_ref, g_ref, sh_ref, sc_ref,
                    wrb_ref, wrh_ref, br_ref, x1_ref, h2_ref, lg_ref):
    mixed = _dot(oa_ref[...], wo_ref[0:Q_COLS, :]) + _dot(or_ref[...], wo_ref[Q_COLS:, :])
    x1 = x_ref[...] + gt_ref[0] * mixed
    x1_ref[...] = x1
    ms = jnp.mean(x1 * x1, axis=-1, keepdims=True)
    h2 = x1 * lax.rsqrt(ms + NORM_EPS) * g_ref[...] * (1.0 + sc_ref[0]) + sh_ref[0]
    for j in range(ROW_SLABS):
        w = h2_ref.shape[2]
        h2_ref[:, j, :] = h2[:, j * w:(j + 1) * w]
    hh = h2.astype(BF16)
    hl = (h2 - hh.astype(F32)).astype(BF16)
    both = _dot(hh, wrb_ref[...])
    lg_ref[...] = both[:, :LANES] + both[:, LANES:] + _dot(hl, wrh_ref[...]) + br_ref[...]


def _outproj(oa2, or2, x2, wo_bf, gt, g2, sh, sc, wr_both, wr_hi, br, T):
    M, D = x2.shape
    tm = 256
    tps = T // tm
    bvec = pl.BlockSpec((1, 1, D), lambda i: (i // tps, 0, 0))
    vm = D * D * 2 + 2 * tm * D * (2 + 4 + 4 + 4) + tm * D * 16 + (8 << 20)
    return pl.pallas_call(
        _outproj_kernel,
        grid=(M // tm,),
        in_specs=[pl.BlockSpec((tm, Q_COLS), lambda i: (i, 0)),
                  pl.BlockSpec((tm, RWKV_W), lambda i: (i, 0)),
                  pl.BlockSpec((tm, D), lambda i: (i, 0)),
                  pl.BlockSpec((D, D), lambda i: (0, 0), pipeline_mode=pl.Buffered(1)),
                  bvec,
                  pl.BlockSpec((1, D), lambda i: (0, 0)),
                  bvec, bvec,
                  pl.BlockSpec((D, 2 * LANES), lambda i: (0, 0)),
                  pl.BlockSpec((D, LANES), lambda i: (0, 0)),
                  pl.BlockSpec((1, LANES), lambda i: (0, 0))],
        out_specs=[pl.BlockSpec((tm, D), lambda i: (i, 0)),
                   pl.BlockSpec((tm, ROW_SLABS, D // ROW_SLABS), lambda i: (i, 0, 0)),
                   pl.BlockSpec((tm, LANES), lambda i: (i, 0))],
        out_shape=[jax.ShapeDtypeStruct((M, D), F32),
                   jax.ShapeDtypeStruct((M, ROW_SLABS, D // ROW_SLABS), F32),
                   jax.ShapeDtypeStruct((M, LANES), F32)],
        compiler_params=pltpu.CompilerParams(
            dimension_semantics=("arbitrary",), vmem_limit_bytes=_vmem_limit(vm)),
        name="outproj",
    )(oa2, or2, x2, wo_bf, gt, g2, sh, sc, wr_both, wr_hi, br)


def _route_kernel(lg_ref, info_ref, cnt_ref, run_ref):
    i = pl.program_id(0)
    tm = lg_ref.shape[0]

    @pl.when(i == 0)
    def _():
        run_ref[...] = jnp.zeros_like(run_ref)

    lg = lg_ref[...]
    lane = lax.broadcasted_iota(I32, lg.shape, 1)
    gl = jnp.where(lane < N_GROUPS, lg, NEG_BIG)
    gmax = jnp.max(gl, axis=-1, keepdims=True)
    gsum = jnp.sum(jnp.exp(gl - gmax), axis=-1, keepdims=True)
    g_gate = 1.0 / gsum
    g_idx = jnp.min(jnp.where(gl == gmax, lane, LANES), axis=-1, keepdims=True)
    lo = N_GROUPS + EXPERTS_PER_GROUP * g_idx
    el = jnp.where((lane >= lo) & (lane < lo + EXPERTS_PER_GROUP), lg, NEG_BIG)
    e1max = jnp.max(el, axis=-1, keepdims=True)
    l1 = jnp.min(jnp.where(el == e1max, lane, LANES), axis=-1, keepdims=True)
    el2 = jnp.where(lane == l1, NEG_BIG, el)
    e2max = jnp.max(el2, axis=-1, keepdims=True)
    l2 = jnp.min(jnp.where(el2 == e2max, lane, LANES), axis=-1, keepdims=True)
    t2 = jnp.exp(e2max - e1max)
    w1 = g_gate / (1.0 + t2)
    w2 = g_gate * t2 / (1.0 + t2)
    ex1 = l1 - N_GROUPS
    ex2 = l2 - N_GROUPS

    oh1 = jnp.where(lane == ex1, 1.0, 0.0)
    oh2 = jnp.where(lane == ex2, 1.0, 0.0)
    ti = lax.broadcasted_iota(I32, (tm, tm), 0)
    si = lax.broadcasted_iota(I32, (tm, tm), 1)
    lower = jnp.where(si < ti, 1.0, 0.0).astype(BF16)
    pre1 = _dot(lower, oh1.astype(BF16))
    pre2 = _dot(lower, oh2.astype(BF16))
    cnt1 = jnp.sum(oh1, axis=0, keepdims=True)
    cnt2 = jnp.sum(oh2, axis=0, keepdims=True)
    run = run_ref[...]
    rank1 = jnp.sum(oh1 * (pre1 + run), axis=-1, keepdims=True)
    rank2 = jnp.sum(oh2 * (pre2 + run + cnt1), axis=-1, keepdims=True)
    run = run + cnt1 + cnt2
    run_ref[...] = run
    cnt_ref[...] = run

    info = jnp.where(lane == 0, ex1.astype(F32), 0.0)
    info = jnp.where(lane == 1, ex2.astype(F32), info)
    info = jnp.where(lane == 2, rank1, info)
    info = jnp.where(lane == 3, rank2, info)
    info = jnp.where(lane == 4, w1, info)
    info = jnp.where(lane == 5, w2, info)
    info_ref[...] = info


def _route(lg):
    M = lg.shape[0]
    tm = 256
    return pl.pallas_call(
        _route_kernel,
        grid=(M // tm,),
        in_specs=[pl.BlockSpec((tm, LANES), lambda i: (i, 0))],
        out_specs=[pl.BlockSpec((tm, LANES), lambda i: (i, 0)),
                   pl.BlockSpec((1, LANES), lambda i: (0, 0))],
        out_shape=[jax.ShapeDtypeStruct((M, LANES), F32),
                   jax.ShapeDtypeStruct((1, LANES), F32)],
        scratch_shapes=[pltpu.VMEM((1, LANES), F32)],
        compiler_params=pltpu.CompilerParams(dimension_semantics=("arbitrary",)),
        name="route",
    )(lg)


def _plan_kernel(info_ref, cnt_ref, slot_ref, meta_ref):
    cnt = cnt_ref[...]
    lane_r = lax.broadcasted_iota(I32, (1, LANES), 1)
    nblk = jnp.floor((cnt + (MOE_BLOCK - 1)) * (1.0 / MOE_BLOCK))
    ei = lax.broadcasted_iota(I32, (LANES, LANES), 0)
    ej = lax.broadcasted_iota(I32, (LANES, LANES), 1)
    upper = jnp.where(ei <= ej, 1.0, 0.0).astype(BF16)
    nb8 = jnp.broadcast_to(nblk, (8, LANES)).astype(BF16)
    bend = _dot(nb8, upper)[0:1, :]
    bstart = bend - nblk
    pstart = bstart * MOE_BLOCK

    info = info_ref[...]
    lane = lax.broadcasted_iota(I32, info.shape, 1)
    ex1 = info[:, 0:1].astype(I32)
    ex2 = info[:, 1:2].astype(I32)
    s1 = jnp.sum(jnp.where(lane == ex1, pstart, 0.0), axis=-1, keepdims=True) + info[:, 2:3]
    s2 = jnp.sum(jnp.where(lane == ex2, pstart, 0.0), axis=-1, keepdims=True) + info[:, 3:4]
    slot = jnp.where(lane == 0, s1, 0.0)
    slot = jnp.where(lane == 1, s2, slot)
    slot_ref[...] = slot.astype(I32)

    blk = lax.broadcasted_iota(I32, (LANES, 2 * LANES), 1).astype(F32)
    bend_col = jnp.sum(jnp.where(ei == ej, jnp.broadcast_to(bend, (LANES, LANES)), 0.0),
                       axis=-1, keepdims=True)
    erow = lax.broadcasted_iota(I32, (LANES, 2 * LANES), 0)
    hit = jnp.where((bend_col <= blk) & (erow < N_EXPERTS), 1.0, 0.0)
    bexp = jnp.minimum(jnp.sum(hit, axis=0, keepdims=True), N_EXPERTS - 1.0)
    n_used = jnp.max(jnp.where(lane_r < N_EXPERTS, bend, 0.0), axis=-1, keepdims=True)
    lastblk = jnp.where(nblk > 0, bend - 1.0, -1.0)
    r8 = lax.broadcasted_iota(I32, (8, 2 * LANES), 0)
    last2 = jnp.concatenate([lastblk, jnp.full((1, LANES), -1.0)], axis=1)
    meta = jnp.where(r8 == 0, jnp.broadcast_to(bexp, (8, 2 * LANES)), 0.0)
    meta = jnp.where(r8 == 1, jnp.broadcast_to(n_used, (8, 2 * LANES)), meta)
    meta = jnp.where(r8 == 2, jnp.broadcast_to(last2, (8, 2 * LANES)), meta)
    meta_ref[...] = meta.astype(I32)


def _plan(info, cnt):
    M = info.shape[0]
    tm = 256
    return pl.pallas_call(
        _plan_kernel,
        grid=(M // tm,),
        in_specs=[pl.BlockSpec((tm, LANES), lambda i: (i, 0)),
                  pl.BlockSpec((1, LANES), lambda i: (0, 0))],
        out_specs=[pl.BlockSpec((tm, LANES), lambda i: (i, 0)),
                   pl.BlockSpec((8, 2 * LANES), lambda i: (0, 0))],
        out_shape=[jax.ShapeDtypeStruct((M, LANES), I32),
                   jax.ShapeDtypeStruct((8, 2 * LANES), I32)],
        compiler_params=pltpu.CompilerParams(dimension_semantics=("arbitrary",)),
        name="plan",
    )(info, cnt)


ROW_DEPTH = 3
DMA_GROUP = 16
WEIGHT_DEPTH = 3


def _slot_table_kernel(slots_ref, inv_ref, *, n_assign, n_slots):
    def init(b, c):
        base = n_assign + lax.rem(b + ROW_DEPTH - 1, ROW_DEPTH) * MOE_BLOCK
        for j in range(MOE_BLOCK):
            inv_ref[b * MOE_BLOCK + j] = base + j
        return c

    def put(a, c):
        inv_ref[slots_ref[a] + MOE_BLOCK] = a
        return c

    lax.fori_loop(0, n_slots // MOE_BLOCK, init, 0)
    lax.fori_loop(0, n_assign, put, 0, unroll=16)


def _slot_table(slots_km, n_blocks):
    n_assign = slots_km.shape[0]
    n_slots = (n_blocks + ROW_DEPTH) * MOE_BLOCK
    gs = pltpu.PrefetchScalarGridSpec(
        num_scalar_prefetch=1, grid=(1,), in_specs=[],
        out_specs=pl.BlockSpec(memory_space=pltpu.SMEM))
    return pl.pallas_call(
        functools.partial(_slot_table_kernel, n_assign=n_assign, n_slots=n_slots),
        grid_spec=gs,
        out_shape=jax.ShapeDtypeStruct((n_slots,), I32),
        compiler_params=pltpu.CompilerParams(dimension_semantics=("arbitrary",)),
        name="slot_table",
    )(slots_km)


def _experts_kernel(bexp_ref, seg_ref, inv_ref, h2_ref, w1_ref, w3_ref, w2_ref, y2_ref,
                    xbuf, ybuf, w1buf, w3buf, w2buf, sem_x, sem_y, sem_w, par_ref, *, n_assign):
    i = pl.program_id(0)
    n_used = seg_ref[N_EXPERTS]
    n_tok = n_assign // 2
    B = MOE_BLOCK
    FF = w1buf.shape[2]
    D = w1buf.shape[1]
    ND = ROW_DEPTH
    NW = w1buf.shape[0]

    def src_row(a):
        if n_tok & (n_tok - 1) == 0:
            return a & (n_tok - 1)
        return lax.rem(a, n_tok)

    def weight_copies(e, slot):
        return (pltpu.make_async_copy(w1_ref.at[e], w1buf.at[slot], sem_w.at[slot]),
                pltpu.make_async_copy(w3_ref.at[e], w3buf.at[slot], sem_w.at[slot]),
                pltpu.make_async_copy(w2_ref.at[e], w2buf.at[slot], sem_w.at[slot]))

    def start_weights(e, slot):
        c1, c3, c2 = weight_copies(e, slot)
        c1.start(priority=1)
        c3.start(priority=1)
        c2.start(priority=0)

    def prefetch_expert_after(e, hops, slot):
        ok = hops >= 0
        cur = e
        for _ in range(hops):
            nb = seg_ref[cur] + 1
            ok = jnp.logical_and(ok, nb < n_used)
            cur = bexp_ref[jnp.where(ok, nb, 0)]

        @pl.when(ok)
        def _():
            start_weights(cur, slot)

    def gather_group(b, g):
        slot = lax.rem(b + ND, ND)
        for r in range(g * DMA_GROUP, (g + 1) * DMA_GROUP):
            a = inv_ref[(b + 1) * B + r]
            pltpu.make_async_copy(h2_ref.at[pl.ds(src_row(a), 1)], xbuf.at[slot, pl.ds(r, 1)],
                                  sem_x.at[slot]).start()

    def scatter_group(b, g):
        slot = lax.rem(b + ND, ND)
        for r in range(g * DMA_GROUP, (g + 1) * DMA_GROUP):
            a = inv_ref[(b + 1) * B + r]
            pltpu.make_async_copy(ybuf.at[slot, pl.ds(r, 1)], y2_ref.at[pl.ds(a, 1)],
                                  sem_y.at[slot]).start()

    def wait_block(buf, sem, b):
        slot = lax.rem(b + ND, ND)
        pltpu.make_async_copy(buf.at[slot], buf.at[slot], sem.at[slot]).wait()

    n_groups = B // DMA_GROUP

    @pl.when(i == 0)
    def _():
        ybuf[...] = jnp.zeros_like(ybuf)
        for s in range(ND - 1):
            pltpu.make_async_copy(ybuf.at[s], y2_ref.at[pl.ds(n_assign + s * B, B)], sem_y.at[s]).start()
        par_ref[0] = 0
        e0 = bexp_ref[0]
        start_weights(e0, 0)
        for hops in range(1, NW):
            prefetch_expert_after(e0, hops, hops)
        for b in range(ND - 1):
            for g in range(n_groups):
                gather_group(b, g)

    active = i < n_used

    @pl.when(active)
    def _():
        e = bexp_ref[i]
        first = jnp.logical_or(i == 0, e != bexp_ref[jnp.maximum(i - 1, 0)])

        @pl.when(jnp.logical_and(first, i > 0))
        def _():
            par_ref[0] = par_ref[0] + 1

        @pl.when(first)
        def _():
            q = par_ref[0]
            for cp in weight_copies(e, lax.rem(q, NW)):
                cp.wait()

            @pl.when(i > 0)
            def _():
                prefetch_expert_after(e, NW - 1, lax.rem(q + NW - 1, NW))

        p = lax.rem(par_ref[0], NW)
        slot = lax.rem(i, ND)
        wait_block(xbuf, sem_x, i)
        wait_block(ybuf, sem_y, i)
        pending = [functools.partial(scatter_group, i - 1, g) for g in range(n_groups)]
        pending += [functools.partial(gather_group, i + ND - 1, g) for g in range(n_groups)]
        n_chunks = 2 * (FF // 256) + ROW_SLABS
        for _ in range(max(len(pending) - n_chunks, 0)):
            pending.pop(0)()

        def after_chunk():
            if pending:
                pending.pop(0)()

        x = jnp.concatenate([xbuf[slot, :, j, :] for j in range(ROW_SLABS)], axis=1).astype(BF16)
        sw = D // ROW_SLABS
        h1c, h3c = [], []
        for c0 in range(0, FF, 256):
            h1c.append(_dot(x, w1buf[p, :, c0:c0 + 256]))
            after_chunk()
            h3c.append(_dot(x, w3buf[p, :, c0:c0 + 256]))
            after_chunk()
        h1 = jnp.concatenate(h1c, axis=1)
        h3 = jnp.concatenate(h3c, axis=1)
        hid = (h1 * (1.0 / (1.0 + jnp.exp(-h1))) * h3).astype(BF16)
        for j in range(ROW_SLABS):
            ybuf[slot, :, j, :] = _dot(hid, w2buf[p, :, j * sw:(j + 1) * sw])
            after_chunk()
        while pending:
            pending.pop(0)()

    @pl.when(i == n_used)
    def _():
        wait_block(xbuf, sem_x, i)
        wait_block(ybuf, sem_y, i)
        for g in range(n_groups):
            scatter_group(i - 1, g)

    @pl.when(jnp.logical_and(i > n_used, i < n_used + ND - 1))
    def _():
        wait_block(xbuf, sem_x, i)
        wait_block(ybuf, sem_y, i)

    @pl.when(i == n_used + ND - 1)
    def _():
        wait_block(ybuf, sem_y, i)


def _experts(bexp, seg_meta, inv, h2, w1, w3, w2, n_blocks):
    M, DJ, SW = h2.shape
    D = DJ * SW
    FF = w1.shape[2]
    n_assign = 2 * M
    any_spec = pl.BlockSpec(memory_space=pl.ANY)
    row_buf = pltpu.VMEM((ROW_DEPTH, MOE_BLOCK, DJ, SW), F32)
    gs = pltpu.PrefetchScalarGridSpec(
        num_scalar_prefetch=3,
        grid=(n_blocks + ROW_DEPTH,),
        in_specs=[any_spec, any_spec, any_spec, any_spec],
        out_specs=any_spec,
        scratch_shapes=[row_buf, row_buf,
                        pltpu.VMEM((WEIGHT_DEPTH, D, FF), F32), pltpu.VMEM((WEIGHT_DEPTH, D, FF), F32),
                        pltpu.VMEM((WEIGHT_DEPTH, FF, D), F32),
                        pltpu.SemaphoreType.DMA((ROW_DEPTH,)), pltpu.SemaphoreType.DMA((ROW_DEPTH,)),
                        pltpu.SemaphoreType.DMA((WEIGHT_DEPTH,)), pltpu.SMEM((1,), I32)],
    )
    vm = WEIGHT_DEPTH * 3 * D * FF * 4 + 2 * ROW_DEPTH * MOE_BLOCK * D * 4 + (12 << 20)
    return pl.pallas_call(
        functools.partial(_experts_kernel, n_assign=n_assign),
        grid_spec=gs,
        out_shape=jax.ShapeDtypeStruct((n_assign + ROW_DEPTH * MOE_BLOCK, DJ, SW), F32),
        compiler_params=pltpu.CompilerParams(
            dimension_semantics=("arbitrary",), vmem_limit_bytes=_vmem_limit(vm),
            has_side_effects=True),
        name="experts",
    )(bexp, seg_meta, inv, h2, w1, w3, w2)


def _combine_kernel(x1_ref, info_ref, gt_ref, fg_ref, ya_ref, yb_ref, o_ref):
    info = info_ref[...]
    nj = ya_ref.shape[1]
    ya = jnp.concatenate([ya_ref[:, j, :] for j in range(nj)], axis=1)
    yb = jnp.concatenate([yb_ref[:, j, :] for j in range(nj)], axis=1)
    moe = info[:, 4:5] * ya + info[:, 5:6] * yb
    x = x1_ref[...] + gt_ref[0] * moe
    ms = jnp.mean(x * x, axis=-1, keepdims=True)
    o_ref[...] = x * lax.rsqrt(ms + NORM_EPS) * fg_ref[...]


def _combine(x1, info, gt, final_g, y2, T):
    M, D = x1.shape
    tm = 256
    tps = T // tm
    nt = M // tm
    vm = 2 * 4 * tm * D * 4 + tm * D * 16 + (8 << 20)
    return pl.pallas_call(
        _combine_kernel,
        grid=(nt,),
        in_specs=[pl.BlockSpec((tm, D), lambda i: (i, 0)),
                  pl.BlockSpec((tm, LANES), lambda i: (i, 0)),
                  pl.BlockSpec((1, 1, D), lambda i: (i // tps, 0, 0)),
                  pl.BlockSpec((1, D), lambda i: (0, 0)),
                  pl.BlockSpec((tm, ROW_SLABS, D // ROW_SLABS), lambda i: (i, 0, 0)),
                  pl.BlockSpec((tm, ROW_SLABS, D // ROW_SLABS), lambda i: (nt + i, 0, 0))],
        out_specs=pl.BlockSpec((tm, D), lambda i: (i, 0)),
        out_shape=jax.ShapeDtypeStruct((M, D), F32),
        compiler_params=pltpu.CompilerParams(
            dimension_semantics=("arbitrary",), vmem_limit_bytes=_vmem_limit(vm)),
        name="combine",
    )(x1, info, gt, final_g.reshape(1, D), y2, y2)


def _rope_tables(T):
    inv_freq = ROPE_THETA ** (-np.arange(0, HEAD_DIM, 2, dtype=np.float64) / HEAD_DIM)
    ang = np.arange(T, dtype=np.float64)[:, None] * inv_freq[None, :]
    cos, sin = np.cos(ang), np.sin(ang)
    cos_h = np.concatenate([cos, cos], axis=1)
    sin_h = np.concatenate([-sin, sin], axis=1)
    return (jnp.asarray(np.tile(cos_h, (1, 2)), F32), jnp.asarray(np.tile(sin_h, (1, 2)), F32))


def _layer(x, c, w_ada, b_ada, norm1_g, w_in, mu_shift, sinks, w0, w_decay_up, a0, w_a_up, w_g_up,
           k_k, k_a, r_k, ln_x_w, ln_x_b, w_o, norm2_g, w_rg, b_rg, w_re, b_re, w1, w3, w2, final_g,
           apply_final):
    B, T, D = x.shape
    M = B * T
    mod = _ada(c, w_ada, b_ada)
    sh1, sc1, gt1, sh2, sc2, gt2 = [m.reshape(B, 1, D) for m in jnp.split(mod, 6, axis=-1)]

    pad = RWKV_PAD - RWKV_COLS
    w_bf = _cast_pad(jnp.swapaxes(w_in, 0, 1), ATTN_COLS + RWKV_PAD)
    mu_pad = jnp.pad(mu_shift, (0, pad)).reshape(1, RWKV_PAD)
    cos_t, sin_t = _rope_tables(T)
    x2 = x.reshape(M, D)
    za, zr = _inproj(x2, norm1_g.reshape(1, D), sh1, sc1, w_bf, mu_pad, cos_t, sin_t, T)

    o_attn = _attention(za.reshape(B, T, ATTN_COLS), sinks)
    o_rwkv = _rwkv(zr.reshape(B, T, RWKV_PAD), w0, w_decay_up, a0, w_a_up, w_g_up, k_k, k_a,
                   r_k.reshape(-1), ln_x_w, ln_x_b)

    wr = jnp.concatenate([w_rg, w_re], axis=1)
    wr = jnp.pad(wr, ((0, 0), (0, LANES - wr.shape[1])))
    wr_hi = wr.astype(BF16)
    wr_lo = (wr - wr_hi.astype(F32)).astype(BF16)
    br = jnp.pad(jnp.concatenate([b_rg, b_re]), (0, LANES - N_GROUPS - N_EXPERTS)).reshape(1, LANES)
    x1, h2, lg = _outproj(o_attn.reshape(M, Q_COLS), o_rwkv.reshape(M, RWKV_W), x2, w_o.astype(BF16),
                          gt1, norm2_g.reshape(1, D), sh2, sc2, jnp.concatenate([wr_hi, wr_lo], axis=1),
                          wr_hi, br, T)

    n_blocks = -(-(2 * M) // MOE_BLOCK) + N_EXPERTS
    info, cnt = _route(lg)
    slots, meta = _plan(info, cnt)
    slots_km = slots[:, :2].T.reshape(-1)
    seg_meta = jnp.concatenate([meta[2, :N_EXPERTS], meta[1, :1]])
    inv = _slot_table(slots_km, n_blocks)
    y2 = _experts(meta[0, :n_blocks + ROW_DEPTH], seg_meta, inv, h2, w1, w3, w2, n_blocks)
    out = _combine(x1, info, gt2, final_g, y2, T)
    del apply_final
    return out.reshape(B, T, D)


def kernel(x, c, w_ada, b_ada, norm1_g, w_in, mu_shift, sinks, w0, w_decay_up, a0, w_a_up, w_g_up, k_k, k_a, r_k, ln_x_w, ln_x_b, w_o, norm2_g, w_router_group, b_router_group, w_router_expert, b_router_expert, w1, w3, w2, final_g):
    depth = w_ada.shape[0]
    assert depth == 1, "single-layer stack"
    l = 0
    return _layer(x, c, w_ada[l], b_ada[l], norm1_g[l], w_in[l], mu_shift[l], sinks[l], w0[l],
                  w_decay_up[l], a0[l], w_a_up[l], w_g_up[l], k_k[l], k_a[l], r_k[l], ln_x_w[l],
                  ln_x_b[l], w_o[l], norm2_g[l], w_router_group[l], b_router_group[l],
                  w_router_expert[l], b_router_expert[l], w1[l], w3[l], w2[l], final_g, True)
```

```python
import functools
import math

import jax
import jax.numpy as jnp
import numpy as np
from jax import lax
from jax.experimental import pallas as pl
from jax.experimental.pallas import tpu as pltpu

F32 = jnp.float32
BF16 = jnp.bfloat16
I32 = jnp.int32

LANES = 128
VMEM_BYTES_V7X = 64 * 1024 * 1024

HEAD_DIM = 64
ATTN_HEADS = 16
ATTN_KV_HEADS = 2
ATTN_GROUP = ATTN_HEADS // ATTN_KV_HEADS
WINDOW = 128
ROPE_THETA = 10000.0
RWKV_HEADS = 16
RWKV_N = 64
DECAY_LORA = 64
A_LORA = 64
GATE_LORA = 160
RWKV_LN_EPS = 64e-5
N_GROUPS = 8
EXPERTS_PER_GROUP = 8
N_EXPERTS = N_GROUPS * EXPERTS_PER_GROUP
EXPERT_FF = 512
MOE_BLOCK = 128
ROW_SLABS = 8
NORM_EPS = 1e-6

Q_COLS = ATTN_HEADS * HEAD_DIM
KV_COLS = ATTN_KV_HEADS * HEAD_DIM
ATTN_COLS = Q_COLS + 2 * KV_COLS
RWKV_W = RWKV_HEADS * RWKV_N
LORA_COLS = DECAY_LORA + A_LORA + GATE_LORA
LORA_PAD = 384
RWKV_COLS = 3 * RWKV_W + LORA_COLS
RWKV_PAD = 3 * RWKV_W + LORA_PAD
CHUNK = 64
RWKV_GROUP = 4
NEG_BIG = -1e30


def _vmem_limit(nbytes):
    return int(min(nbytes, VMEM_BYTES_V7X - 4 * 1024 * 1024))


def _dot(a, b):
    return jnp.dot(a, b, preferred_element_type=F32)


def _dot_nt(a, b):
    return lax.dot_general(a, b, (((1,), (1,)), ((), ())), preferred_element_type=F32)


def _dot_tn(a, b):
    return lax.dot_general(a, b, (((0,), (0,)), ((), ())), preferred_element_type=F32)


def _bf16_pair_pack(a, b):
    def hi16(x):
        u = pltpu.bitcast(x, jnp.uint32)
        r = u + jnp.uint32(0x7FFF) + ((u >> 16) & jnp.uint32(1))
        return jnp.where(x != x, u | jnp.uint32(0x00400000), r)
    return (hi16(a) & jnp.uint32(0xFFFF0000)) | (hi16(b) >> 16)


def _bf16_pair_unpack(w):
    return (pltpu.bitcast(w & jnp.uint32(0xFFFF0000), F32), pltpu.bitcast(w << 16, F32))


def _ada_kernel(c_ref, w_ref, b_ref, o_ref):
    c = c_ref[...]
    s = c * (1.0 / (1.0 + jnp.exp(-c)))
    o_ref[...] = _dot(s, w_ref[...]) + b_ref[...]


def _ada(c, w_ada, b_ada):
    B, D = c.shape
    N = w_ada.shape[1]
    tn = 1024
    cp = jnp.zeros((8, D), F32).at[:B].set(c)
    out = pl.pallas_call(
        _ada_kernel,
        grid=(N // tn,),
        in_specs=[pl.BlockSpec((8, D), lambda j: (0, 0)),
                  pl.BlockSpec((D, tn), lambda j: (0, j)),
                  pl.BlockSpec((1, tn), lambda j: (0, j))],
        out_specs=pl.BlockSpec((8, tn), lambda j: (0, j)),
        out_shape=jax.ShapeDtypeStruct((8, N), F32),
        compiler_params=pltpu.CompilerParams(
            dimension_semantics=("arbitrary",),
            vmem_limit_bytes=_vmem_limit(2 * D * tn * 4 + (8 << 20))),
        name="ada",
    )(cp, w_ada, b_ada.reshape(1, N))
    return out[:B]


def _cast_pad_kernel(wt_ref, o_ref, *, n_valid):
    j = pl.program_id(0)
    tn = wt_ref.shape[0]
    row = j * tn + lax.broadcasted_iota(I32, wt_ref.shape, 0)
    wt = jnp.where(row < n_valid, wt_ref[...], 0.0)
    o_ref[...] = wt.T.astype(o_ref.dtype)


def _cast_pad(w_t, n_out):
    N, K = w_t.shape
    tn = 512
    return pl.pallas_call(
        functools.partial(_cast_pad_kernel, n_valid=N),
        grid=(pl.cdiv(n_out, tn),),
        in_specs=[pl.BlockSpec((tn, K), lambda j: (j, 0))],
        out_specs=pl.BlockSpec((K, tn), lambda j: (0, j)),
        out_shape=jax.ShapeDtypeStruct((K, n_out), BF16),
        compiler_params=pltpu.CompilerParams(dimension_semantics=("arbitrary",)),
        name="cast_pad",
    )(w_t)


def _rope(z, cos, sin):
    w = z.shape[-1]
    lane = lax.broadcasted_iota(I32, z.shape, 1)
    first_half = (lane % HEAD_DIM) < (HEAD_DIM // 2)
    partner = jnp.where(first_half, pltpu.roll(z, w - HEAD_DIM // 2, 1), pltpu.roll(z, HEAD_DIM // 2, 1))
    return z * cos + partner * sin


def _inproj_kernel(x_ref, g_ref, sh_ref, sc_ref, w_ref, mu_ref, cos_ref, sin_ref,
                   za_ref, zr_ref, carry_ref, *, tiles_per_seq):
    i = pl.program_id(0)
    tm = x_ref.shape[0]
    x = x_ref[...]
    ms = jnp.mean(x * x, axis=-1, keepdims=True)
    y = x * lax.rsqrt(ms + NORM_EPS) * g_ref[...]
    h = (y * (1.0 + sc_ref[0]) + sh_ref[0]).astype(BF16)

    cos2 = jnp.concatenate([cos_ref[...], cos_ref[...]], axis=1)
    sin2 = jnp.concatenate([sin_ref[...], sin_ref[...]], axis=1)
    for c0 in range(0, ATTN_COLS, 256):
        z = _dot(h, w_ref[:, c0:c0 + 256])
        if c0 < Q_COLS:
            z = _rope(z, cos2, sin2)
        else:
            z = jnp.concatenate([_rope(z[:, :KV_COLS], cos_ref[...], sin_ref[...]), z[:, KV_COLS:]], axis=1)
        za_ref[:, c0:c0 + 256] = z.astype(za_ref.dtype)

    first = (i % tiles_per_seq) == 0
    row = lax.broadcasted_iota(I32, (tm, 1), 0)
    c0 = 0
    while c0 < RWKV_PAD:
        n = min(512, RWKV_PAD - c0)
        z = _dot(h, w_ref[:, ATTN_COLS + c0:ATTN_COLS + c0 + n])
        prev_last = jnp.where(first, 0.0, carry_ref[7:8, c0:c0 + n])
        z_prev = jnp.where(row == 0, prev_last, pltpu.roll(z, 1, 0))
        carry_ref[:, c0:c0 + n] = z[tm - 8:tm, :]
        zr_ref[:, c0:c0 + n] = z + (z_prev - z) * mu_ref[:, c0:c0 + n]
        c0 += n


def _inproj(x2, g, sh, sc, w_bf, mu_pad, cos_t, sin_t, T):
    M, D = x2.shape
    tm = 256
    tps = T // tm
    NW = w_bf.shape[1]
    kern = functools.partial(_inproj_kernel, tiles_per_seq=tps)
    vm = (D * NW * 2 + 2 * tm * D * 4 + 2 * tm * ATTN_COLS * 2 + 2 * tm * RWKV_PAD * 4
          + 4 * tm * 512 * 4 + tm * D * 8 + (8 << 20))
    return pl.pallas_call(
        kern,
        grid=(M // tm,),
        in_specs=[pl.BlockSpec((tm, D), lambda i: (i, 0)),
                  pl.BlockSpec((1, D), lambda i: (0, 0)),
                  pl.BlockSpec((1, 1, D), lambda i: (i // tps, 0, 0)),
                  pl.BlockSpec((1, 1, D), lambda i: (i // tps, 0, 0)),
                  pl.BlockSpec((D, NW), lambda i: (0, 0), pipeline_mode=pl.Buffered(1)),
                  pl.BlockSpec((1, RWKV_PAD), lambda i: (0, 0)),
                  pl.BlockSpec((tm, 2 * HEAD_DIM), lambda i: (i % tps, 0)),
                  pl.BlockSpec((tm, 2 * HEAD_DIM), lambda i: (i % tps, 0))],
        out_specs=[pl.BlockSpec((tm, ATTN_COLS), lambda i: (i, 0)),
                   pl.BlockSpec((tm, RWKV_PAD), lambda i: (i, 0))],
        out_shape=[jax.ShapeDtypeStruct((M, ATTN_COLS), BF16),
                   jax.ShapeDtypeStruct((M, RWKV_PAD), F32)],
        scratch_shapes=[pltpu.VMEM((8, RWKV_PAD), F32)],
        compiler_params=pltpu.CompilerParams(
            dimension_semantics=("arbitrary",), vmem_limit_bytes=_vmem_limit(vm)),
        name="inproj",
    )(x2, g, sh, sc, w_bf, mu_pad, cos_t, sin_t)


def _attn_kernel(sink_ref, q_ref, kc_ref, kp_ref, vc_ref, vp_ref, o_ref):
    n = pl.program_id(1)
    blk = q_ref.shape[1]
    row = lax.broadcasted_iota(I32, (blk, blk), 0)
    col = lax.broadcasted_iota(I32, (blk, blk), 1)
    mask = jnp.concatenate([(col > row) & (n > 0), col <= row], axis=1)
    scale = 1.0 / math.sqrt(HEAD_DIM)
    outs = []
    for kvh in range(ATTN_KV_HEADS):
        ks = slice(kvh * HEAD_DIM, (kvh + 1) * HEAD_DIM)
        kmat = jnp.concatenate([kp_ref[0, :, ks], kc_ref[0, :, ks]], axis=0)
        vmat = jnp.concatenate([vp_ref[0, :, ks], vc_ref[0, :, ks]], axis=0)
        for g in range(ATTN_GROUP):
            hd = kvh * ATTN_GROUP + g
            qh = q_ref[0, :, hd * HEAD_DIM:(hd + 1) * HEAD_DIM]
            s = _dot_nt(qh, kmat) * scale
            s = jnp.where(mask, s, NEG_BIG)
            sink = sink_ref[hd]
            m = jnp.maximum(jnp.max(s, axis=-1, keepdims=True), sink)
            p = jnp.exp(s - m)
            denom = jnp.sum(p, axis=-1, keepdims=True) + jnp.exp(sink - m)
            o = _dot(p.astype(BF16), vmat)
            outs.append(o / denom)
    o_ref[0] = jnp.concatenate(outs, axis=1).astype(o_ref.dtype)


def _attention(za3, sinks):
    B, T, _ = za3.shape
    nb = T // WINDOW
    kcol = Q_COLS // KV_COLS
    prev = lambda b, n, s: (b, jnp.maximum(n - 1, 0), kcol)
    prev_v = lambda b, n, s: (b, jnp.maximum(n - 1, 0), kcol + 1)
    gs = pltpu.PrefetchScalarGridSpec(
        num_scalar_prefetch=1,
        grid=(B, nb),
        in_specs=[pl.BlockSpec((1, WINDOW, Q_COLS), lambda b, n, s: (b, n, 0)),
                  pl.BlockSpec((1, WINDOW, KV_COLS), lambda b, n, s: (b, n, kcol)),
                  pl.BlockSpec((1, WINDOW, KV_COLS), prev),
                  pl.BlockSpec((1, WINDOW, KV_COLS), lambda b, n, s: (b, n, kcol + 1)),
                  pl.BlockSpec((1, WINDOW, KV_COLS), prev_v)],
        out_specs=pl.BlockSpec((1, WINDOW, Q_COLS), lambda b, n, s: (b, n, 0)),
    )
    return pl.pallas_call(
        _attn_kernel,
        grid_spec=gs,
        out_shape=jax.ShapeDtypeStruct((B, T, Q_COLS), BF16),
        compiler_params=pltpu.CompilerParams(dimension_semantics=("arbitrary", "arbitrary")),
        name="attn",
    )(sinks, za3, za3, za3, za3, za3)


def _rwkv_kernel(r_ref, k_ref, v_ref, lora_ref, w0_ref, wdu_ref, a0_ref, wau_ref, wgu_ref,
                 kk_ref, ka_ref, rk_ref, lnw_ref, lnb_ref, o_ref, s_ref):
    c = pl.program_id(0)
    nseq = r_ref.shape[0]
    C = r_ref.shape[1]
    N = RWKV_N
    G = RWKV_GROUP
    GW = G * N
    ng = RWKV_HEADS // G

    @pl.when(c == 0)
    def _():
        s_ref[...] = jnp.zeros_like(s_ref)

    ti = lax.broadcasted_iota(I32, (C, C), 0)
    si = lax.broadcasted_iota(I32, (C, C), 1)
    tril = jnp.where(si <= ti, 1.0, 0.0).astype(BF16)
    a_seq, g_seq, e_in_seq, e_ex_seq, e_neg_seq = [], [], [], [], []
    for b in range(nseq):
        lora = lora_ref[b]
        wd = lora[:, 0:DECAY_LORA]
        ad = lora[:, DECAY_LORA:DECAY_LORA + A_LORA]
        gd = lora[:, DECAY_LORA + A_LORA:LORA_COLS]
        wlin = w0_ref[...] + _dot(jnp.tanh(wd), wdu_ref[...])
        neg = -wlin
        softplus = jnp.maximum(neg, 0.0) + jnp.log(1.0 + jnp.exp(-jnp.abs(neg)))
        w = -softplus - 0.5
        logdec = -jnp.exp(w)
        a_seq.append(1.0 / (1.0 + jnp.exp(-(a0_ref[...] + _dot(ad, wau_ref[...])))))
        g_seq.append(_dot(1.0 / (1.0 + jnp.exp(-gd)), wgu_ref[...]))
        ld_hi = logdec.astype(BF16)
        ld_lo = (logdec - ld_hi.astype(F32)).astype(BF16)
        cum = _dot(tril, ld_hi) + _dot(tril, ld_lo)
        e_in_seq.append(jnp.exp(cum))
        e_ex_seq.append(jnp.exp(cum - logdec))
        e_neg_seq.append(jnp.exp(-cum))

    lane_head = lax.broadcasted_iota(I32, (1, GW), 1) // N
    t_row = lax.broadcasted_iota(I32, (C, GW), 0)
    s_lane = lax.broadcasted_iota(I32, (C, GW), 1) % N
    strict = s_lane < t_row
    incl = s_lane <= t_row
    bi = lax.broadcasted_iota(I32, (GW, GW), 0)
    bj = lax.broadcasted_iota(I32, (GW, GW), 1)
    same_head = (bi // N) == (bj // N)
    ones_bd = jnp.where(same_head, 1.0, 0.0).astype(BF16)
    eye = jnp.where(s_lane == t_row, 1.0, 0.0)

    def expand(xc):
        return jnp.concatenate([jnp.where(lane_head == h, xc, 0.0) for h in range(G)], axis=0)

    def head_sum(xs):
        s = _dot(jnp.concatenate(xs, axis=0).astype(BF16), ones_bd)
        return [s[i * C:(i + 1) * C] for i in range(len(xs))]

    seq = [b for b in range(nseq) for _ in range(ng)]
    col = [slice(gi * GW, (gi + 1) * GW) for _ in range(nseq) for gi in range(ng)]
    rng = range(nseq * ng)
    r_l = [r_ref[seq[i], :, col[i]] for i in rng]
    k_l = [k_ref[seq[i], :, col[i]] for i in rng]
    v_l = [v_ref[seq[i], :, col[i]] for i in rng]
    a_l = [a_seq[seq[i]][:, col[i]] for i in rng]
    e_in = [e_in_seq[seq[i]][:, col[i]] for i in rng]
    e_neg = [e_neg_seq[seq[i]][:, col[i]] for i in rng]
    kk0 = [k_l[i] * kk_ref[:, col[i]] for i in rng]
    nrm2 = head_sum([kk0[i] * kk0[i] for i in rng])
    kk = [kk0[i] / jnp.maximum(jnp.sqrt(nrm2[i]), 1e-12) for i in rng]
    k2 = [k_l[i] * (1.0 + (a_l[i] - 1.0) * ka_ref[:, col[i]]) for i in rng]
    bt = [kk[i] * a_l[i] * e_neg[i] for i in rng]
    kt = [k2[i] * e_neg[i] for i in rng]
    left = [jnp.concatenate([-kk[i] * e_ex_seq[seq[i]][:, col[i]], r_l[i] * e_in[i]], axis=0).astype(BF16)
            for i in rng]
    right = [jnp.concatenate([bt[i], kt[i]], axis=0).astype(BF16) for i in rng]
    zed = [jnp.concatenate([expand(bt[i]), expand(kt[i])], axis=0).astype(BF16) for i in rng]
    big = [_dot_nt(left[i], zed[i]) for i in rng]
    s_old = [s_ref[i] for i in rng]
    ls = [_dot_nt(left[i], s_old[i].astype(BF16)) for i in rng]
    v_bd = [expand(v_l[i]).astype(BF16) for i in rng]
    pw = [jnp.where(strict, big[i][:C, :G * C], 0.0) for i in rng]
    tinv = [eye + pw[i] for i in rng]
    pw_bd = [expand(pw[i]).astype(BF16) for i in rng]
    x = [ls[i][:C] + _dot(jnp.where(strict, big[i][:C, G * C:], 0.0).astype(BF16), v_bd[i]) for i in rng]
    span = 2
    while span < C:
        pw = [_dot(pw[i].astype(BF16), pw_bd[i]) for i in rng]
        pw_bd = [expand(pw[i]).astype(BF16) for i in rng]
        tinv = [tinv[i] + _dot(tinv[i].astype(BF16), pw_bd[i]) for i in rng]
        span *= 2
    u = [_dot(tinv[i].astype(BF16), expand(x[i]).astype(BF16)) for i in rng]
    a_r = [jnp.concatenate([jnp.where(incl, big[i][C:, :G * C], 0.0),
                            jnp.where(incl, big[i][C:, G * C:], 0.0)], axis=1).astype(BF16) for i in rng]
    y = [ls[i][C:] + _dot(a_r[i], jnp.concatenate([expand(u[i]).astype(BF16), v_bd[i]], axis=0)) for i in rng]
    uv = [jnp.concatenate([u[i], v_l[i]], axis=0).astype(BF16) for i in rng]
    for i in rng:
        s_ref[i] = ((s_old[i] + jnp.where(same_head, _dot_tn(uv[i], right[i]), 0.0))
                    * e_in[i][C - 1:C, :])
    ysum = head_sum(y)
    yc = [y[i] - ysum[i] * (1.0 / N) for i in rng]
    ysq = head_sum([yc[i] * yc[i] for i in rng])
    var = [ysq[i] * (1.0 / N) for i in rng]
    rksum = head_sum([r_l[i] * k2[i] * rk_ref[:, col[i]] for i in rng])
    bonus = [rksum[i] * v_l[i] for i in rng]
    for i in rng:
        yn = yc[i] * lax.rsqrt(var[i] + RWKV_LN_EPS) * lnw_ref[:, col[i]] + lnb_ref[:, col[i]]
        o_ref[seq[i], :, col[i]] = ((yn + bonus[i]) * g_seq[seq[i]][:, col[i]]).astype(o_ref.dtype)


def _rwkv(zr3, w0, wdu, a0, wau, wgu, k_k, k_a, r_k, ln_w, ln_b):
    B, T, _ = zr3.shape
    C = CHUNK
    W = RWKV_W
    GW = RWKV_GROUP * RWKV_N
    vec = lambda v: v.reshape(1, W)
    full = lambda shape: pl.BlockSpec(shape, lambda c: (0,) * len(shape))
    return pl.pallas_call(
        _rwkv_kernel,
        grid=(T // C,),
        in_specs=[pl.BlockSpec((B, C, W), lambda c: (0, c, 0)),
                  pl.BlockSpec((B, C, W), lambda c: (0, c, 1)),
                  pl.BlockSpec((B, C, W), lambda c: (0, c, 2)),
                  pl.BlockSpec((B, C, LORA_PAD), lambda c: (0, c, 3 * W // LORA_PAD)),
                  full((1, W)), full((DECAY_LORA, W)), full((1, W)), full((A_LORA, W)),
                  full((GATE_LORA, W)), full((1, W)), full((1, W)), full((1, W)),
                  full((1, W)), full((1, W))],
        out_specs=pl.BlockSpec((B, C, W), lambda c: (0, c, 0)),
        out_shape=jax.ShapeDtypeStruct((B, T, W), BF16),
        scratch_shapes=[pltpu.VMEM((B * RWKV_HEADS // RWKV_GROUP, GW, GW), F32)],
        compiler_params=pltpu.CompilerParams(dimension_semantics=("arbitrary",)),
        name="rwkv",
    )(zr3, zr3, zr3, zr3, vec(w0), wdu, vec(a0), wau, wgu, vec(k_k), vec(k_a), vec(r_k),
      vec(ln_w), vec(ln_b))


def _outproj_kernel(oa_ref, or_ref, x_ref, wo_ref, gt_ref, g_ref, sh_ref, sc_ref,
                    wrb_ref, wrh_ref, br_ref, x1_ref, h2_ref, lg_ref):
    mixed = _dot(oa_ref[...], wo_ref[0:Q_COLS, :]) + _dot(or_ref[...], wo_ref[Q_COLS:, :])
    x1 = x_ref[...] + gt_ref[0] * mixed
    x1_ref[...] = x1
    ms = jnp.mean(x1 * x1, axis=-1, keepdims=True)
    h2 = x1 * lax.rsqrt(ms + NORM_EPS) * g_ref[...] * (1.0 + sc_ref[0]) + sh_ref[0]
    half = h2.shape[1] // 2
    h2p = _bf16_pair_pack(h2[:, :half], h2[:, half:])
    for j in range(ROW_SLABS):
        w = h2_ref.shape[2]
        h2_ref[:, j, :] = h2p[:, j * w:(j + 1) * w]
    hh = h2.astype(BF16)
    hl = (h2 - hh.astype(F32)).astype(BF16)
    both = _dot(hh, wrb_ref[...])
    lg_ref[...] = both[:, :LANES] + both[:, LANES:] + _dot(hl, wrh_ref[...]) + br_ref[...]


def _outproj(oa2, or2, x2, wo_bf, gt, g2, sh, sc, wr_both, wr_hi, br, T):
    M, D = x2.shape
    tm = 256
    tps = T // tm
    bvec = pl.BlockSpec((1, 1, D), lambda i: (i // tps, 0, 0))
    vm = D * D * 2 + 2 * tm * D * (2 + 4 + 4 + 4) + tm * D * 16 + (8 << 20)
    return pl.pallas_call(
        _outproj_kernel,
        grid=(M // tm,),
        in_specs=[pl.BlockSpec((tm, Q_COLS), lambda i: (i, 0)),
                  pl.BlockSpec((tm, RWKV_W), lambda i: (i, 0)),
                  pl.BlockSpec((tm, D), lambda i: (i, 0)),
                  pl.BlockSpec((D, D), lambda i: (0, 0), pipeline_mode=pl.Buffered(1)),
                  bvec,
                  pl.BlockSpec((1, D), lambda i: (0, 0)),
                  bvec, bvec,
                  pl.BlockSpec((D, 2 * LANES), lambda i: (0, 0)),
                  pl.BlockSpec((D, LANES), lambda i: (0, 0)),
                  pl.BlockSpec((1, LANES), lambda i: (0, 0))],
        out_specs=[pl.BlockSpec((tm, D), lambda i: (i, 0)),
                   pl.BlockSpec((tm, ROW_SLABS, D // (2 * ROW_SLABS)), lambda i: (i, 0, 0)),
                   pl.BlockSpec((tm, LANES), lambda i: (i, 0))],
        out_shape=[jax.ShapeDtypeStruct((M, D), F32),
                   jax.ShapeDtypeStruct((M, ROW_SLABS, D // (2 * ROW_SLABS)), jnp.uint32),
                   jax.ShapeDtypeStruct((M, LANES), F32)],
        compiler_params=pltpu.CompilerParams(
            dimension_semantics=("arbitrary",), vmem_limit_bytes=_vmem_limit(vm)),
        name="outproj",
    )(oa2, or2, x2, wo_bf, gt, g2, sh, sc, wr_both, wr_hi, br)


def _route_kernel(lg_ref, info_ref, cnt_ref, run_ref):
    i = pl.program_id(0)
    tm = lg_ref.shape[0]

    @pl.when(i == 0)
    def _():
        run_ref[...] = jnp.zeros_like(run_ref)

    lg = lg_ref[...]
    lane = lax.broadcasted_iota(I32, lg.shape, 1)
    gl = jnp.where(lane < N_GROUPS, lg, NEG_BIG)
    gmax = jnp.max(gl, axis=-1, keepdims=True)
    gsum = jnp.sum(jnp.exp(gl - gmax), axis=-1, keepdims=True)
    g_gate = 1.0 / gsum
    g_idx = jnp.min(jnp.where(gl == gmax, lane, LANES), axis=-1, keepdims=True)
    lo = N_GROUPS + EXPERTS_PER_GROUP * g_idx
    el = jnp.where((lane >= lo) & (lane < lo + EXPERTS_PER_GROUP), lg, NEG_BIG)
    e1max = jnp.max(el, axis=-1, keepdims=True)
    l1 = jnp.min(jnp.where(el == e1max, lane, LANES), axis=-1, keepdims=True)
    el2 = jnp.where(lane == l1, NEG_BIG, el)
    e2max = jnp.max(el2, axis=-1, keepdims=True)
    l2 = jnp.min(jnp.where(el2 == e2max, lane, LANES), axis=-1, keepdims=True)
    t2 = jnp.exp(e2max - e1max)
    w1 = g_gate / (1.0 + t2)
    w2 = g_gate * t2 / (1.0 + t2)
    ex1 = l1 - N_GROUPS
    ex2 = l2 - N_GROUPS

    oh1 = jnp.where(lane == ex1, 1.0, 0.0)
    oh2 = jnp.where(lane == ex2, 1.0, 0.0)
    ti = lax.broadcasted_iota(I32, (tm, tm), 0)
    si = lax.broadcasted_iota(I32, (tm, tm), 1)
    lower = jnp.where(si < ti, 1.0, 0.0).astype(BF16)
    pre1 = _dot(lower, oh1.astype(BF16))
    pre2 = _dot(lower, oh2.astype(BF16))
    cnt1 = jnp.sum(oh1, axis=0, keepdims=True)
    cnt2 = jnp.sum(oh2, axis=0, keepdims=True)
    run = run_ref[...]
    rank1 = jnp.sum(oh1 * (pre1 + run), axis=-1, keepdims=True)
    rank2 = jnp.sum(oh2 * (pre2 + run + cnt1), axis=-1, keepdims=True)
    run = run + cnt1 + cnt2
    run_ref[...] = run
    cnt_ref[...] = run

    info = jnp.where(lane == 0, ex1.astype(F32), 0.0)
    info = jnp.where(lane == 1, ex2.astype(F32), info)
    info = jnp.where(lane == 2, rank1, info)
    info = jnp.where(lane == 3, rank2, info)
    info = jnp.where(lane == 4, w1, info)
    info = jnp.where(lane == 5, w2, info)
    info_ref[...] = info


def _route(lg):
    M = lg.shape[0]
    tm = 256
    return pl.pallas_call(
        _route_kernel,
        grid=(M // tm,),
        in_specs=[pl.BlockSpec((tm, LANES), lambda i: (i, 0))],
        out_specs=[pl.BlockSpec((tm, LANES), lambda i: (i, 0)),
                   pl.BlockSpec((1, LANES), lambda i: (0, 0))],
        out_shape=[jax.ShapeDtypeStruct((M, LANES), F32),
                   jax.ShapeDtypeStruct((1, LANES), F32)],
        scratch_shapes=[pltpu.VMEM((1, LANES), F32)],
        compiler_params=pltpu.CompilerParams(dimension_semantics=("arbitrary",)),
        name="route",
    )(lg)


def _plan_kernel(info_ref, cnt_ref, slot_ref, meta_ref):
    cnt = cnt_ref[...]
    lane_r = lax.broadcasted_iota(I32, (1, LANES), 1)
    nblk = jnp.floor((cnt + (MOE_BLOCK - 1)) * (1.0 / MOE_BLOCK))
    ei = lax.broadcasted_iota(I32, (LANES, LANES), 0)
    ej = lax.broadcasted_iota(I32, (LANES, LANES), 1)
    upper = jnp.where(ei <= ej, 1.0, 0.0).astype(BF16)
    nb8 = jnp.broadcast_to(nblk, (8, LANES)).astype(BF16)
    bend = _dot(nb8, upper)[0:1, :]
    bstart = bend - nblk
    pstart = bstart * MOE_BLOCK

    info = info_ref[...]
    lane = lax.broadcasted_iota(I32, info.shape, 1)
    ex1 = info[:, 0:1].astype(I32)
    ex2 = info[:, 1:2].astype(I32)
    s1 = jnp.sum(jnp.where(lane == ex1, pstart, 0.0), axis=-1, keepdims=True) + info[:, 2:3]
    s2 = jnp.sum(jnp.where(lane == ex2, pstart, 0.0), axis=-1, keepdims=True) + info[:, 3:4]
    slot = jnp.where(lane == 0, s1, 0.0)
    slot = jnp.where(lane == 1, s2, slot)
    slot_ref[...] = slot.astype(I32)

    blk = lax.broadcasted_iota(I32, (LANES, 2 * LANES), 1).astype(F32)
    bend_col = jnp.sum(jnp.where(ei == ej, jnp.broadcast_to(bend, (LANES, LANES)), 0.0),
                       axis=-1, keepdims=True)
    erow = lax.broadcasted_iota(I32, (LANES, 2 * LANES), 0)
    hit = jnp.where((bend_col <= blk) & (erow < N_EXPERTS), 1.0, 0.0)
    bexp = jnp.minimum(jnp.sum(hit, axis=0, keepdims=True), N_EXPERTS - 1.0)
    n_used = jnp.max(jnp.where(lane_r < N_EXPERTS, bend, 0.0), axis=-1, keepdims=True)
    lastblk = jnp.where(nblk > 0, bend - 1.0, -1.0)
    r8 = lax.broadcasted_iota(I32, (8, 2 * LANES), 0)
    last2 = jnp.concatenate([lastblk, jnp.full((1, LANES), -1.0)], axis=1)
    meta = jnp.where(r8 == 0, jnp.broadcast_to(bexp, (8, 2 * LANES)), 0.0)
    meta = jnp.where(r8 == 1, jnp.broadcast_to(n_used, (8, 2 * LANES)), meta)
    meta = jnp.where(r8 == 2, jnp.broadcast_to(last2, (8, 2 * LANES)), meta)
    meta_ref[...] = meta.astype(I32)


def _plan(info, cnt):
    M = info.shape[0]
    tm = 256
    return pl.pallas_call(
        _plan_kernel,
        grid=(M // tm,),
        in_specs=[pl.BlockSpec((tm, LANES), lambda i: (i, 0)),
                  pl.BlockSpec((1, LANES), lambda i: (0, 0))],
        out_specs=[pl.BlockSpec((tm, LANES), lambda i: (i, 0)),
                   pl.BlockSpec((8, 2 * LANES), lambda i: (0, 0))],
        out_shape=[jax.ShapeDtypeStruct((M, LANES), I32),
                   jax.ShapeDtypeStruct((8, 2 * LANES), I32)],
        compiler_params=pltpu.CompilerParams(dimension_semantics=("arbitrary",)),
        name="plan",
    )(info, cnt)


ROW_DEPTH = 3
DMA_GROUP = 16
WEIGHT_DEPTH = 3


def _slot_table_kernel(slots_ref, inv_ref, *, n_assign, n_slots):
    def init(b, c):
        base = n_assign + lax.rem(b + ROW_DEPTH - 1, ROW_DEPTH) * MOE_BLOCK
        for j in range(MOE_BLOCK):
            inv_ref[b * MOE_BLOCK + j] = base + j
        return c

    def put(a, c):
        inv_ref[slots_ref[a] + MOE_BLOCK] = a
        return c

    lax.fori_loop(0, n_slots // MOE_BLOCK, init, 0)
    lax.fori_loop(0, n_assign, put, 0, unroll=16)


def _slot_table(slots_km, n_blocks):
    n_assign = slots_km.shape[0]
    n_slots = (n_blocks + ROW_DEPTH) * MOE_BLOCK
    gs = pltpu.PrefetchScalarGridSpec(
        num_scalar_prefetch=1, grid=(1,), in_specs=[],
        out_specs=pl.BlockSpec(memory_space=pltpu.SMEM))
    return pl.pallas_call(
        functools.partial(_slot_table_kernel, n_assign=n_assign, n_slots=n_slots),
        grid_spec=gs,
        out_shape=jax.ShapeDtypeStruct((n_slots,), I32),
        compiler_params=pltpu.CompilerParams(dimension_semantics=("arbitrary",)),
        name="slot_table",
    )(slots_km)


def _experts_kernel(bexp_ref, seg_ref, inv_ref, h2_ref, w1_ref, w3_ref, w2_ref, y2_ref,
                    xbuf, ybuf, w1buf, w3buf, w2buf, sem_x, sem_y, sem_w, par_ref, *, n_assign):
    i = pl.program_id(0)
    n_used = seg_ref[N_EXPERTS]
    n_tok = n_assign // 2
    B = MOE_BLOCK
    FF = w1buf.shape[2]
    D = w1buf.shape[1]
    ND = ROW_DEPTH
    NW = w1buf.shape[0]

    def src_row(a):
        if n_tok & (n_tok - 1) == 0:
            return a & (n_tok - 1)
        return lax.rem(a, n_tok)

    def weight_copies(e, slot):
        return (pltpu.make_async_copy(w1_ref.at[e], w1buf.at[slot], sem_w.at[slot]),
                pltpu.make_async_copy(w3_ref.at[e], w3buf.at[slot], sem_w.at[slot]),
                pltpu.make_async_copy(w2_ref.at[e], w2buf.at[slot], sem_w.at[slot]))

    def start_weights(e, slot):
        c1, c3, c2 = weight_copies(e, slot)
        c1.start(priority=1)
        c3.start(priority=1)
        c2.start(priority=0)

    def prefetch_expert_after(e, hops, slot):
        ok = hops >= 0
        cur = e
        for _ in range(hops):
            nb = seg_ref[cur] + 1
            ok = jnp.logical_and(ok, nb < n_used)
            cur = bexp_ref[jnp.where(ok, nb, 0)]

        @pl.when(ok)
        def _():
            start_weights(cur, slot)

    def gather_group(b, g):
        slot = lax.rem(b + ND, ND)
        for r in range(g * DMA_GROUP, (g + 1) * DMA_GROUP):
            a = inv_ref[(b + 1) * B + r]
            pltpu.make_async_copy(h2_ref.at[pl.ds(src_row(a), 1)], xbuf.at[slot, pl.ds(r, 1)],
                                  sem_x.at[slot]).start()

    def scatter_group(b, g):
        slot = lax.rem(b + ND, ND)
        for r in range(g * DMA_GROUP, (g + 1) * DMA_GROUP):
            a = inv_ref[(b + 1) * B + r]
            pltpu.make_async_copy(ybuf.at[slot, pl.ds(r, 1)], y2_ref.at[pl.ds(a, 1)],
                                  sem_y.at[slot]).start()

    def wait_block(buf, sem, b):
        slot = lax.rem(b + ND, ND)
        pltpu.make_async_copy(buf.at[slot], buf.at[slot], sem.at[slot]).wait()

    n_groups = B // DMA_GROUP

    @pl.when(i == 0)
    def _():
        ybuf[...] = jnp.zeros_like(ybuf)
        for s in range(ND - 1):
            pltpu.make_async_copy(ybuf.at[s], y2_ref.at[pl.ds(n_assign + s * B, B)], sem_y.at[s]).start()
        par_ref[0] = 0
        e0 = bexp_ref[0]
        start_weights(e0, 0)
        for hops in range(1, NW):
            prefetch_expert_after(e0, hops, hops)
        for b in range(ND - 1):
            for g in range(n_groups):
                gather_group(b, g)

    active = i < n_used

    @pl.when(active)
    def _():
        e = bexp_ref[i]
        first = jnp.logical_or(i == 0, e != bexp_ref[jnp.maximum(i - 1, 0)])

        @pl.when(jnp.logical_and(first, i > 0))
        def _():
            par_ref[0] = par_ref[0] + 1

        @pl.when(first)
        def _():
            q = par_ref[0]
            for cp in weight_copies(e, lax.rem(q, NW)):
                cp.wait()

            @pl.when(i > 0)
            def _():
                prefetch_expert_after(e, NW - 1, lax.rem(q + NW - 1, NW))

        p = lax.rem(par_ref[0], NW)
        slot = lax.rem(i, ND)
        wait_block(xbuf, sem_x, i)
        wait_block(ybuf, sem_y, i)
        pending = [functools.partial(scatter_group, i - 1, g) for g in range(n_groups)]
        pending += [functools.partial(gather_group, i + ND - 1, g) for g in range(n_groups)]
        n_chunks = 2 * (FF // 256) + ROW_SLABS
        for _ in range(max(len(pending) - n_chunks, 0)):
            pending.pop(0)()

        def after_chunk():
            if pending:
                pending.pop(0)()

        xa, xb = _bf16_pair_unpack(jnp.concatenate([xbuf[slot, :, j, :] for j in range(ROW_SLABS)], axis=1))
        x = jnp.concatenate([xa, xb], axis=1).astype(BF16)
        sw = ybuf.shape[3]
        h1c, h3c = [], []
        for c0 in range(0, FF, 256):
            h1c.append(_dot(x, w1buf[p, :, c0:c0 + 256]))
            after_chunk()
            h3c.append(_dot(x, w3buf[p, :, c0:c0 + 256]))
            after_chunk()
        h1 = jnp.concatenate(h1c, axis=1)
        h3 = jnp.concatenate(h3c, axis=1)
        hid = (h1 * (1.0 / (1.0 + jnp.exp(-h1))) * h3).astype(BF16)
        for j in range(0, ROW_SLABS, 2):
            c0 = j * sw
            ya = _dot(hid, w2buf[p, :, c0:c0 + 2 * sw])
            after_chunk()
            yb = _dot(hid, w2buf[p, :, D // 2 + c0:D // 2 + c0 + 2 * sw])
            yp = _bf16_pair_pack(ya, yb)
            ybuf[slot, :, j, :] = yp[:, :sw]
            ybuf[slot, :, j + 1, :] = yp[:, sw:]
            after_chunk()
        while pending:
            pending.pop(0)()

    @pl.when(i == n_used)
    def _():
        wait_block(xbuf, sem_x, i)
        wait_block(ybuf, sem_y, i)
        for g in range(n_groups):
            scatter_group(i - 1, g)

    @pl.when(jnp.logical_and(i > n_used, i < n_used + ND - 1))
    def _():
        wait_block(xbuf, sem_x, i)
        wait_block(ybuf, sem_y, i)

    @pl.when(i == n_used + ND - 1)
    def _():
        wait_block(ybuf, sem_y, i)


def _experts(bexp, seg_meta, inv, h2, w1, w3, w2, n_blocks):
    M, DJ, SW = h2.shape
    D = 2 * DJ * SW
    FF = w1.shape[2]
    n_assign = 2 * M
    any_spec = pl.BlockSpec(memory_space=pl.ANY)
    row_buf = pltpu.VMEM((ROW_DEPTH, MOE_BLOCK, DJ, SW), jnp.uint32)
    gs = pltpu.PrefetchScalarGridSpec(
        num_scalar_prefetch=3,
        grid=(n_blocks + ROW_DEPTH,),
        in_specs=[any_spec, any_spec, any_spec, any_spec],
        out_specs=any_spec,
        scratch_shapes=[row_buf, row_buf,
                        pltpu.VMEM((WEIGHT_DEPTH, D, FF), F32), pltpu.VMEM((WEIGHT_DEPTH, D, FF), F32),
                        pltpu.VMEM((WEIGHT_DEPTH, FF, D), F32),
                        pltpu.SemaphoreType.DMA((ROW_DEPTH,)), pltpu.SemaphoreType.DMA((ROW_DEPTH,)),
                        pltpu.SemaphoreType.DMA((WEIGHT_DEPTH,)), pltpu.SMEM((1,), I32)],
    )
    vm = WEIGHT_DEPTH * 3 * D * FF * 4 + 2 * ROW_DEPTH * MOE_BLOCK * D * 4 + (12 << 20)
    return pl.pallas_call(
        functools.partial(_experts_kernel, n_assign=n_assign),
        grid_spec=gs,
        out_shape=jax.ShapeDtypeStruct((n_assign + ROW_DEPTH * MOE_BLOCK, DJ, SW), jnp.uint32),
        compiler_params=pltpu.CompilerParams(
            dimension_semantics=("arbitrary",), vmem_limit_bytes=_vmem_limit(vm),
            has_side_effects=True),
        name="experts",
    )(bexp, seg_meta, inv, h2, w1, w3, w2)


def _combine_kernel(x1_ref, info_ref, gt_ref, fg_ref, ya_ref, yb_ref, o_ref):
    info = info_ref[...]
    nj = ya_ref.shape[1]
    ya = jnp.concatenate(_bf16_pair_unpack(jnp.concatenate([ya_ref[:, j, :] for j in range(nj)], axis=1)), axis=1)
    yb = jnp.concatenate(_bf16_pair_unpack(jnp.concatenate([yb_ref[:, j, :] for j in range(nj)], axis=1)), axis=1)
    moe = info[:, 4:5] * ya + info[:, 5:6] * yb
    x = x1_ref[...] + gt_ref[0] * moe
    ms = jnp.mean(x * x, axis=-1, keepdims=True)
    o_ref[...] = x * lax.rsqrt(ms + NORM_EPS) * fg_ref[...]


def _combine(x1, info, gt, final_g, y2, T):
    M, D = x1.shape
    tm = 256
    tps = T // tm
    nt = M // tm
    vm = 2 * 4 * tm * D * 4 + tm * D * 16 + (8 << 20)
    return pl.pallas_call(
        _combine_kernel,
        grid=(nt,),
        in_specs=[pl.BlockSpec((tm, D), lambda i: (i, 0)),
                  pl.BlockSpec((tm, LANES), lambda i: (i, 0)),
                  pl.BlockSpec((1, 1, D), lambda i: (i // tps, 0, 0)),
                  pl.BlockSpec((1, D), lambda i: (0, 0)),
                  pl.BlockSpec((tm, ROW_SLABS, D // (2 * ROW_SLABS)), lambda i: (i, 0, 0)),
                  pl.BlockSpec((tm, ROW_SLABS, D // (2 * ROW_SLABS)), lambda i: (nt + i, 0, 0))],
        out_specs=pl.BlockSpec((tm, D), lambda i: (i, 0)),
        out_shape=jax.ShapeDtypeStruct((M, D), F32),
        compiler_params=pltpu.CompilerParams(
            dimension_semantics=("arbitrary",), vmem_limit_bytes=_vmem_limit(vm)),
        name="combine",
    )(x1, info, gt, final_g.reshape(1, D), y2, y2)


def _rope_tables(T):
    inv_freq = ROPE_THETA ** (-np.arange(0, HEAD_DIM, 2, dtype=np.float64) / HEAD_DIM)
    ang = np.arange(T, dtype=np.float64)[:, None] * inv_freq[None, :]
    cos, sin = np.cos(ang), np.sin(ang)
    cos_h = np.concatenate([cos, cos], axis=1)
    sin_h = np.concatenate([-sin, sin], axis=1)
    return (jnp.asarray(np.tile(cos_h, (1, 2)), F32), jnp.asarray(np.tile(sin_h, (1, 2)), F32))


def _layer(x, c, w_ada, b_ada, norm1_g, w_in, mu_shift, sinks, w0, w_decay_up, a0, w_a_up, w_g_up,
           k_k, k_a, r_k, ln_x_w, ln_x_b, w_o, norm2_g, w_rg, b_rg, w_re, b_re, w1, w3, w2, final_g):
    B, T, D = x.shape
    M = B * T
    mod = _ada(c, w_ada, b_ada)
    sh1, sc1, gt1, sh2, sc2, gt2 = [m.reshape(B, 1, D) for m in jnp.split(mod, 6, axis=-1)]

    pad = RWKV_PAD - RWKV_COLS
    w_bf = _cast_pad(jnp.swapaxes(w_in, 0, 1), ATTN_COLS + RWKV_PAD)
    mu_pad = jnp.pad(mu_shift, (0, pad)).reshape(1, RWKV_PAD)
    cos_t, sin_t = _rope_tables(T)
    x2 = x.reshape(M, D)
    za, zr = _inproj(x2, norm1_g.reshape(1, D), sh1, sc1, w_bf, mu_pad, cos_t, sin_t, T)

    o_attn = _attention(za.reshape(B, T, ATTN_COLS), sinks)
    o_rwkv = _rwkv(zr.reshape(B, T, RWKV_PAD), w0, w_decay_up, a0, w_a_up, w_g_up, k_k, k_a,
                   r_k.reshape(-1), ln_x_w, ln_x_b)

    wr = jnp.concatenate([w_rg, w_re], axis=1)
    wr = jnp.pad(wr, ((0, 0), (0, LANES - wr.shape[1])))
    wr_hi = wr.astype(BF16)
    wr_lo = (wr - wr_hi.astype(F32)).astype(BF16)
    br = jnp.pad(jnp.concatenate([b_rg, b_re]), (0, LANES - N_GROUPS - N_EXPERTS)).reshape(1, LANES)
    x1, h2, lg = _outproj(o_attn.reshape(M, Q_COLS), o_rwkv.reshape(M, RWKV_W), x2, w_o.astype(BF16),
                          gt1, norm2_g.reshape(1, D), sh2, sc2, jnp.concatenate([wr_hi, wr_lo], axis=1),
                          wr_hi, br, T)

    n_blocks = -(-(2 * M) // MOE_BLOCK) + N_EXPERTS
    info, cnt = _route(lg)
    slots, meta = _plan(info, cnt)
    slots_km = slots[:, :2].T.reshape(-1)
    seg_meta = jnp.concatenate([meta[2, :N_EXPERTS], meta[1, :1]])
    inv = _slot_table(slots_km, n_blocks)
    y2 = _experts(meta[0, :n_blocks + ROW_DEPTH], seg_meta, inv, h2, w1, w3, w2, n_blocks)
    out = _combine(x1, info, gt2, final_g, y2, T)
    return out.reshape(B, T, D)


def kernel(x, c, w_ada, b_ada, norm1_g, w_in, mu_shift, sinks, w0, w_decay_up, a0, w_a_up, w_g_up, k_k, k_a, r_k, ln_x_w, ln_x_b, w_o, norm2_g, w_router_group, b_router_group, w_router_expert, b_router_expert, w1, w3, w2, final_g):
    depth = w_ada.shape[0]
    assert depth == 1, "single-layer stack"
    l = 0
    return _layer(x, c, w_ada[l], b_ada[l], norm1_g[l], w_in[l], mu_shift[l], sinks[l], w0[l],
                  w_decay_up[l], a0[l], w_a_up[l], w_g_up[l], k_k[l], k_a[l], r_k[l], ln_x_w[l],
                  ln_x_b[l], w_o[l], norm2_g[l], w_router_group[l], b_router_group[l],
                  w_router_expert[l], b_router_expert[l], w1[l], w3[l], w2[l], final_g)
```

```python
import functools
import math

import jax
import jax.numpy as jnp
import numpy as np
from jax import lax
from jax.experimental import pallas as pl
from jax.experimental.pallas import tpu as pltpu

F32 = jnp.float32
BF16 = jnp.bfloat16
I32 = jnp.int32

LANES = 128
VMEM_BYTES_V7X = 64 * 1024 * 1024

HEAD_DIM = 64
ATTN_HEADS = 16
ATTN_KV_HEADS = 2
ATTN_GROUP = ATTN_HEADS // ATTN_KV_HEADS
WINDOW = 128
ROPE_THETA = 10000.0
RWKV_HEADS = 16
RWKV_N = 64
DECAY_LORA = 64
A_LORA = 64
GATE_LORA = 160
RWKV_LN_EPS = 64e-5
N_GROUPS = 8
EXPERTS_PER_GROUP = 8
N_EXPERTS = N_GROUPS * EXPERTS_PER_GROUP
EXPERT_FF = 512
MOE_BLOCK = 128
ROW_SLABS = 8
NORM_EPS = 1e-6

Q_COLS = ATTN_HEADS * HEAD_DIM
KV_COLS = ATTN_KV_HEADS * HEAD_DIM
ATTN_COLS = Q_COLS + 2 * KV_COLS
RWKV_W = RWKV_HEADS * RWKV_N
LORA_COLS = DECAY_LORA + A_LORA + GATE_LORA
LORA_PAD = 384
RWKV_COLS = 3 * RWKV_W + LORA_COLS
RWKV_PAD = 3 * RWKV_W + LORA_PAD
CHUNK = 64
RWKV_GROUP = 4
NEG_BIG = -1e30


def _vmem_limit(nbytes):
    return int(min(nbytes, VMEM_BYTES_V7X - 4 * 1024 * 1024))


def _dot(a, b):
    return jnp.dot(a, b, preferred_element_type=F32)


def _dot_nt(a, b):
    return lax.dot_general(a, b, (((1,), (1,)), ((), ())), preferred_element_type=F32)


def _dot_tn(a, b):
    return lax.dot_general(a, b, (((0,), (0,)), ((), ())), preferred_element_type=F32)


def _bf16_pair_pack(a, b):
    def hi16(x):
        u = pltpu.bitcast(x, jnp.uint32)
        r = u + jnp.uint32(0x7FFF) + ((u >> 16) & jnp.uint32(1))
        return jnp.where(x != x, u | jnp.uint32(0x00400000), r)
    return (hi16(a) & jnp.uint32(0xFFFF0000)) | (hi16(b) >> 16)


def _bf16_pair_unpack(w):
    return (pltpu.bitcast(w & jnp.uint32(0xFFFF0000), F32), pltpu.bitcast(w << 16, F32))


def _ada_kernel(c_ref, w_ref, b_ref, o_ref):
    c = c_ref[...]
    s = c * (1.0 / (1.0 + jnp.exp(-c)))
    o_ref[...] = _dot(s, w_ref[...]) + b_ref[...]


def _ada(c, w_ada, b_ada):
    B, D = c.shape
    N = w_ada.shape[1]
    tn = 1024
    cp = jnp.zeros((8, D), F32).at[:B].set(c)
    out = pl.pallas_call(
        _ada_kernel,
        grid=(N // tn,),
        in_specs=[pl.BlockSpec((8, D), lambda j: (0, 0)),
                  pl.BlockSpec((D, tn), lambda j: (0, j)),
                  pl.BlockSpec((1, tn), lambda j: (0, j))],
        out_specs=pl.BlockSpec((8, tn), lambda j: (0, j)),
        out_shape=jax.ShapeDtypeStruct((8, N), F32),
        compiler_params=pltpu.CompilerParams(
            dimension_semantics=("arbitrary",),
            vmem_limit_bytes=_vmem_limit(2 * D * tn * 4 + (8 << 20))),
        name="ada",
    )(cp, w_ada, b_ada.reshape(1, N))
    return out[:B]


def _cast_pad_kernel(wt_ref, o_ref, *, n_valid):
    j = pl.program_id(0)
    tn = wt_ref.shape[0]
    row = j * tn + lax.broadcasted_iota(I32, wt_ref.shape, 0)
    wt = jnp.where(row < n_valid, wt_ref[...], 0.0)
    o_ref[...] = wt.T.astype(o_ref.dtype)


def _cast_pad(w_t, n_out):
    N, K = w_t.shape
    tn = 512
    return pl.pallas_call(
        functools.partial(_cast_pad_kernel, n_valid=N),
        grid=(pl.cdiv(n_out, tn),),
        in_specs=[pl.BlockSpec((tn, K), lambda j: (j, 0))],
        out_specs=pl.BlockSpec((K, tn), lambda j: (0, j)),
        out_shape=jax.ShapeDtypeStruct((K, n_out), BF16),
        compiler_params=pltpu.CompilerParams(dimension_semantics=("arbitrary",)),
        name="cast_pad",
    )(w_t)


def _rope(z, cos, sin):
    w = z.shape[-1]
    lane = lax.broadcasted_iota(I32, z.shape, 1)
    first_half = (lane % HEAD_DIM) < (HEAD_DIM // 2)
    partner = jnp.where(first_half, pltpu.roll(z, w - HEAD_DIM // 2, 1), pltpu.roll(z, HEAD_DIM // 2, 1))
    return z * cos + partner * sin


def _inproj_kernel(x_ref, g_ref, sh_ref, sc_ref, w_ref, mu_ref, cos_ref, sin_ref,
                   za_ref, zr_ref, carry_ref, *, tiles_per_seq):
    i = pl.program_id(0)
    tm = x_ref.shape[0]
    x = x_ref[...]
    ms = jnp.mean(x * x, axis=-1, keepdims=True)
    y = x * lax.rsqrt(ms + NORM_EPS) * g_ref[...]
    h = (y * (1.0 + sc_ref[0]) + sh_ref[0]).astype(BF16)

    cos2 = jnp.concatenate([cos_ref[...], cos_ref[...]], axis=1)
    sin2 = jnp.concatenate([sin_ref[...], sin_ref[...]], axis=1)
    for c0 in range(0, ATTN_COLS, 256):
        z = _dot(h, w_ref[:, c0:c0 + 256])
        if c0 < Q_COLS:
            z = _rope(z, cos2, sin2)
        else:
            z = jnp.concatenate([_rope(z[:, :KV_COLS], cos_ref[...], sin_ref[...]), z[:, KV_COLS:]], axis=1)
        za_ref[:, c0:c0 + 256] = z.astype(za_ref.dtype)

    first = (i % tiles_per_seq) == 0
    row = lax.broadcasted_iota(I32, (tm, 1), 0)
    c0 = 0
    while c0 < RWKV_PAD:
        n = min(512, RWKV_PAD - c0)
        z = _dot(h, w_ref[:, ATTN_COLS + c0:ATTN_COLS + c0 + n])
        prev_last = jnp.where(first, 0.0, carry_ref[7:8, c0:c0 + n])
        z_prev = jnp.where(row == 0, prev_last, pltpu.roll(z, 1, 0))
        carry_ref[:, c0:c0 + n] = z[tm - 8:tm, :]
        zr_ref[:, c0:c0 + n] = z + (z_prev - z) * mu_ref[:, c0:c0 + n]
        c0 += n


def _inproj(x2, g, sh, sc, w_bf, mu_pad, cos_t, sin_t, T):
    M, D = x2.shape
    tm = 256
    tps = T // tm
    NW = w_bf.shape[1]
    kern = functools.partial(_inproj_kernel, tiles_per_seq=tps)
    vm = (D * NW * 2 + 2 * tm * D * 4 + 2 * tm * ATTN_COLS * 2 + 2 * tm * RWKV_PAD * 4
          + 4 * tm * 512 * 4 + tm * D * 8 + (8 << 20))
    return pl.pallas_call(
        kern,
        grid=(M // tm,),
        in_specs=[pl.BlockSpec((tm, D), lambda i: (i, 0)),
                  pl.BlockSpec((1, D), lambda i: (0, 0)),
                  pl.BlockSpec((1, 1, D), lambda i: (i // tps, 0, 0)),
                  pl.BlockSpec((1, 1, D), lambda i: (i // tps, 0, 0)),
                  pl.BlockSpec((D, NW), lambda i: (0, 0), pipeline_mode=pl.Buffered(1)),
                  pl.BlockSpec((1, RWKV_PAD), lambda i: (0, 0)),
                  pl.BlockSpec((tm, 2 * HEAD_DIM), lambda i: (i % tps, 0)),
                  pl.BlockSpec((tm, 2 * HEAD_DIM), lambda i: (i % tps, 0))],
        out_specs=[pl.BlockSpec((tm, ATTN_COLS), lambda i: (i, 0)),
                   pl.BlockSpec((tm, RWKV_PAD), lambda i: (i, 0))],
        out_shape=[jax.ShapeDtypeStruct((M, ATTN_COLS), BF16),
                   jax.ShapeDtypeStruct((M, RWKV_PAD), F32)],
        scratch_shapes=[pltpu.VMEM((8, RWKV_PAD), F32)],
        compiler_params=pltpu.CompilerParams(
            dimension_semantics=("arbitrary",), vmem_limit_bytes=_vmem_limit(vm)),
        name="inproj",
    )(x2, g, sh, sc, w_bf, mu_pad, cos_t, sin_t)


def _attn_kernel(sink_ref, q_ref, kc_ref, kp_ref, vc_ref, vp_ref, o_ref):
    n = pl.program_id(1)
    blk = q_ref.shape[1]
    row = lax.broadcasted_iota(I32, (blk, blk), 0)
    col = lax.broadcasted_iota(I32, (blk, blk), 1)
    mask = jnp.concatenate([(col > row) & (n > 0), col <= row], axis=1)
    scale = 1.0 / math.sqrt(HEAD_DIM)
    outs = []
    for kvh in range(ATTN_KV_HEADS):
        ks = slice(kvh * HEAD_DIM, (kvh + 1) * HEAD_DIM)
        kmat = jnp.concatenate([kp_ref[0, :, ks], kc_ref[0, :, ks]], axis=0)
        vmat = jnp.concatenate([vp_ref[0, :, ks], vc_ref[0, :, ks]], axis=0)
        for g in range(ATTN_GROUP):
            hd = kvh * ATTN_GROUP + g
            qh = q_ref[0, :, hd * HEAD_DIM:(hd + 1) * HEAD_DIM]
            s = _dot_nt(qh, kmat) * scale
            s = jnp.where(mask, s, NEG_BIG)
            sink = sink_ref[hd]
            m = jnp.maximum(jnp.max(s, axis=-1, keepdims=True), sink)
            p = jnp.exp(s - m)
            denom = jnp.sum(p, axis=-1, keepdims=True) + jnp.exp(sink - m)
            o = _dot(p.astype(BF16), vmat)
            outs.append(o / denom)
    o_ref[0] = jnp.concatenate(outs, axis=1).astype(o_ref.dtype)


def _attention(za3, sinks):
    B, T, _ = za3.shape
    nb = T // WINDOW
    kcol = Q_COLS // KV_COLS
    prev = lambda b, n, s: (b, jnp.maximum(n - 1, 0), kcol)
    prev_v = lambda b, n, s: (b, jnp.maximum(n - 1, 0), kcol + 1)
    gs = pltpu.PrefetchScalarGridSpec(
        num_scalar_prefetch=1,
        grid=(B, nb),
        in_specs=[pl.BlockSpec((1, WINDOW, Q_COLS), lambda b, n, s: (b, n, 0)),
                  pl.BlockSpec((1, WINDOW, KV_COLS), lambda b, n, s: (b, n, kcol)),
                  pl.BlockSpec((1, WINDOW, KV_COLS), prev),
                  pl.BlockSpec((1, WINDOW, KV_COLS), lambda b, n, s: (b, n, kcol + 1)),
                  pl.BlockSpec((1, WINDOW, KV_COLS), prev_v)],
        out_specs=pl.BlockSpec((1, WINDOW, Q_COLS), lambda b, n, s: (b, n, 0)),
    )
    return pl.pallas_call(
        _attn_kernel,
        grid_spec=gs,
        out_shape=jax.ShapeDtypeStruct((B, T, Q_COLS), BF16),
        compiler_params=pltpu.CompilerParams(dimension_semantics=("arbitrary", "arbitrary")),
        name="attn",
    )(sinks, za3, za3, za3, za3, za3)


def _rwkv_kernel(r_ref, k_ref, v_ref, lora_ref, w0_ref, wdu_ref, a0_ref, wau_ref, wgu_ref,
                 kk_ref, ka_ref, rk_ref, lnw_ref, lnb_ref, o_ref, s_ref):
    c = pl.program_id(0)
    nseq = r_ref.shape[0]
    C = r_ref.shape[1]
    N = RWKV_N
    G = RWKV_GROUP
    GW = G * N
    ng = RWKV_HEADS // G

    @pl.when(c == 0)
    def _():
        s_ref[...] = jnp.zeros_like(s_ref)

    ti = lax.broadcasted_iota(I32, (C, C), 0)
    si = lax.broadcasted_iota(I32, (C, C), 1)
    tril = jnp.where(si <= ti, 1.0, 0.0).astype(BF16)
    a_seq, g_seq, e_in_seq, e_ex_seq, e_neg_seq = [], [], [], [], []
    for b in range(nseq):
        lora = lora_ref[b]
        wd = lora[:, 0:DECAY_LORA]
        ad = lora[:, DECAY_LORA:DECAY_LORA + A_LORA]
        gd = lora[:, DECAY_LORA + A_LORA:LORA_COLS]
        wlin = w0_ref[...] + _dot(jnp.tanh(wd), wdu_ref[...])
        neg = -wlin
        softplus = jnp.maximum(neg, 0.0) + jnp.log(1.0 + jnp.exp(-jnp.abs(neg)))
        w = -softplus - 0.5
        logdec = -jnp.exp(w)
        a_seq.append(1.0 / (1.0 + jnp.exp(-(a0_ref[...] + _dot(ad, wau_ref[...])))))
        g_seq.append(_dot(1.0 / (1.0 + jnp.exp(-gd)), wgu_ref[...]))
        ld_hi = logdec.astype(BF16)
        ld_lo = (logdec - ld_hi.astype(F32)).astype(BF16)
        cum = _dot(tril, ld_hi) + _dot(tril, ld_lo)
        e_in_seq.append(jnp.exp(cum))
        e_ex_seq.append(jnp.exp(cum - logdec))
        e_neg_seq.append(jnp.exp(-cum))

    lane_head = lax.broadcasted_iota(I32, (1, GW), 1) // N
    t_row = lax.broadcasted_iota(I32, (C, GW), 0)
    s_lane = lax.broadcasted_iota(I32, (C, GW), 1) % N
    strict = s_lane < t_row
    incl = s_lane <= t_row
    bi = lax.broadcasted_iota(I32, (GW, GW), 0)
    bj = lax.broadcasted_iota(I32, (GW, GW), 1)
    same_head = (bi // N) == (bj // N)
    ones_bd = jnp.where(same_head, 1.0, 0.0).astype(BF16)
    eye = jnp.where(s_lane == t_row, 1.0, 0.0)

    def expand(xc):
        return jnp.concatenate([jnp.where(lane_head == h, xc, 0.0) for h in range(G)], axis=0).astype(BF16)

    def head_sum(xs):
        s = _dot(jnp.concatenate(xs, axis=0).astype(BF16), ones_bd)
        return [s[i * C:(i + 1) * C] for i in range(len(xs))]

    seq = [b for b in range(nseq) for _ in range(ng)]
    col = [slice(gi * GW, (gi + 1) * GW) for _ in range(nseq) for gi in range(ng)]
    rng = range(nseq * ng)
    r_l = [r_ref[seq[i], :, col[i]] for i in rng]
    k_l = [k_ref[seq[i], :, col[i]] for i in rng]
    v_l = [v_ref[seq[i], :, col[i]] for i in rng]
    a_l = [a_seq[seq[i]][:, col[i]] for i in rng]
    e_in = [e_in_seq[seq[i]][:, col[i]] for i in rng]
    e_neg = [e_neg_seq[seq[i]][:, col[i]] for i in rng]
    kk0 = [k_l[i] * kk_ref[:, col[i]] for i in rng]
    nrm2 = head_sum([kk0[i] * kk0[i] for i in rng])
    kk = [kk0[i] / jnp.maximum(jnp.sqrt(nrm2[i]), 1e-12) for i in rng]
    k2 = [k_l[i] * (1.0 + (a_l[i] - 1.0) * ka_ref[:, col[i]]) for i in rng]
    bt = [kk[i] * a_l[i] * e_neg[i] for i in rng]
    kt = [k2[i] * e_neg[i] for i in rng]
    left = [jnp.concatenate([-kk[i] * e_ex_seq[seq[i]][:, col[i]], r_l[i] * e_in[i]], axis=0).astype(BF16)
            for i in rng]
    right = [jnp.concatenate([bt[i], kt[i]], axis=0).astype(BF16) for i in rng]
    zed = [jnp.concatenate([expand(bt[i]), expand(kt[i])], axis=0) for i in rng]
    big = [_dot_nt(left[i], zed[i]) for i in rng]
    s_old = [s_ref[i] for i in rng]
    ls = [_dot_nt(left[i], s_old[i].astype(BF16)) for i in rng]
    v_bd = [expand(v_l[i]) for i in rng]
    pw = [jnp.where(strict, big[i][:C, :G * C], 0.0) for i in rng]
    tinv = [eye + pw[i] for i in rng]
    pw_bd = [expand(pw[i]) for i in rng]
    x = [ls[i][:C] + _dot(jnp.where(strict, big[i][:C, G * C:], 0.0).astype(BF16), v_bd[i]) for i in rng]
    span = 2
    while span < C:
        pw = [_dot(pw[i].astype(BF16), pw_bd[i]) for i in rng]
        pw_bd = [expand(pw[i]) for i in rng]
        tinv = [tinv[i] + _dot(tinv[i].astype(BF16), pw_bd[i]) for i in rng]
        span *= 2
    u = [_dot(tinv[i].astype(BF16), expand(x[i])) for i in rng]
    a_r = [jnp.concatenate([jnp.where(incl, big[i][C:, :G * C], 0.0),
                            jnp.where(incl, big[i][C:, G * C:], 0.0)], axis=1).astype(BF16) for i in rng]
    y = [ls[i][C:] + _dot(a_r[i], jnp.concatenate([expand(u[i]), v_bd[i]], axis=0)) for i in rng]
    uv = [jnp.concatenate([u[i], v_l[i]], axis=0).astype(BF16) for i in rng]
    for i in rng:
        s_ref[i] = ((s_old[i] + jnp.where(same_head, _dot_tn(uv[i], right[i]), 0.0))
                    * e_in[i][C - 1:C, :])
    ysum = head_sum(y)
    yc = [y[i] - ysum[i] * (1.0 / N) for i in rng]
    ysq = head_sum([yc[i] * yc[i] for i in rng])
    var = [ysq[i] * (1.0 / N) for i in rng]
    rksum = head_sum([r_l[i] * k2[i] * rk_ref[:, col[i]] for i in rng])
    bonus = [rksum[i] * v_l[i] for i in rng]
    for i in rng:
        yn = yc[i] * lax.rsqrt(var[i] + RWKV_LN_EPS) * lnw_ref[:, col[i]] + lnb_ref[:, col[i]]
        o_ref[seq[i], :, col[i]] = ((yn + bonus[i]) * g_seq[seq[i]][:, col[i]]).astype(o_ref.dtype)


def _rwkv(zr3, w0, wdu, a0, wau, wgu, k_k, k_a, r_k, ln_w, ln_b):
    B, T, _ = zr3.shape
    C = CHUNK
    W = RWKV_W
    GW = RWKV_GROUP * RWKV_N
    vec = lambda v: v.reshape(1, W)
    full = lambda shape: pl.BlockSpec(shape, lambda c: (0,) * len(shape))
    return pl.pallas_call(
        _rwkv_kernel,
        grid=(T // C,),
        in_specs=[pl.BlockSpec((B, C, W), lambda c: (0, c, 0)),
                  pl.BlockSpec((B, C, W), lambda c: (0, c, 1)),
                  pl.BlockSpec((B, C, W), lambda c: (0, c, 2)),
                  pl.BlockSpec((B, C, LORA_PAD), lambda c: (0, c, 3 * W // LORA_PAD)),
                  full((1, W)), full((DECAY_LORA, W)), full((1, W)), full((A_LORA, W)),
                  full((GATE_LORA, W)), full((1, W)), full((1, W)), full((1, W)),
                  full((1, W)), full((1, W))],
        out_specs=pl.BlockSpec((B, C, W), lambda c: (0, c, 0)),
        out_shape=jax.ShapeDtypeStruct((B, T, W), BF16),
        scratch_shapes=[pltpu.VMEM((B * RWKV_HEADS // RWKV_GROUP, GW, GW), F32)],
        compiler_params=pltpu.CompilerParams(dimension_semantics=("arbitrary",)),
        name="rwkv",
    )(zr3, zr3, zr3, zr3, vec(w0), wdu, vec(a0), wau, wgu, vec(k_k), vec(k_a), vec(r_k),
      vec(ln_w), vec(ln_b))


def _outproj_kernel(oa_ref, or_ref, x_ref, wo_ref, gt_ref, g_ref, sh_ref, sc_ref,
                    wrb_ref, wrh_ref, br_ref, x1_ref, h2_ref, lg_ref):
    mixed = _dot(oa_ref[...], wo_ref[0:Q_COLS, :]) + _dot(or_ref[...], wo_ref[Q_COLS:, :])
    x1 = x_ref[...] + gt_ref[0] * mixed
    x1_ref[...] = x1
    ms = jnp.mean(x1 * x1, axis=-1, keepdims=True)
    h2 = x1 * lax.rsqrt(ms + NORM_EPS) * g_ref[...] * (1.0 + sc_ref[0]) + sh_ref[0]
    half = h2.shape[1] // 2
    h2p = _bf16_pair_pack(h2[:, :half], h2[:, half:])
    for j in range(ROW_SLABS):
        w = h2_ref.shape[2]
        h2_ref[:, j, :] = h2p[:, j * w:(j + 1) * w]
    hh = h2.astype(BF16)
    hl = (h2 - hh.astype(F32)).astype(BF16)
    both = _dot(hh, wrb_ref[...])
    lg_ref[...] = both[:, :LANES] + both[:, LANES:] + _dot(hl, wrh_ref[...]) + br_ref[...]


def _outproj(oa2, or2, x2, wo_bf, gt, g2, sh, sc, wr_both, wr_hi, br, T):
    M, D = x2.shape
    tm = 256
    tps = T // tm
    bvec = pl.BlockSpec((1, 1, D), lambda i: (i // tps, 0, 0))
    vm = D * D * 2 + 2 * tm * D * (2 + 4 + 4 + 4) + tm * D * 16 + (8 << 20)
    return pl.pallas_call(
        _outproj_kernel,
        grid=(M // tm,),
        in_specs=[pl.BlockSpec((tm, Q_COLS), lambda i: (i, 0)),
                  pl.BlockSpec((tm, RWKV_W), lambda i: (i, 0)),
                  pl.BlockSpec((tm, D), lambda i: (i, 0)),
                  pl.BlockSpec((D, D), lambda i: (0, 0), pipeline_mode=pl.Buffered(1)),
                  bvec,
                  pl.BlockSpec((1, D), lambda i: (0, 0)),
                  bvec, bvec,
                  pl.BlockSpec((D, 2 * LANES), lambda i: (0, 0)),
                  pl.BlockSpec((D, LANES), lambda i: (0, 0)),
                  pl.BlockSpec((1, LANES), lambda i: (0, 0))],
        out_specs=[pl.BlockSpec((tm, D), lambda i: (i, 0)),
                   pl.BlockSpec((tm, ROW_SLABS, D // (2 * ROW_SLABS)), lambda i: (i, 0, 0)),
                   pl.BlockSpec((tm, LANES), lambda i: (i, 0))],
        out_shape=[jax.ShapeDtypeStruct((M, D), F32),
                   jax.ShapeDtypeStruct((M, ROW_SLABS, D // (2 * ROW_SLABS)), jnp.uint32),
                   jax.ShapeDtypeStruct((M, LANES), F32)],
        compiler_params=pltpu.CompilerParams(
            dimension_semantics=("arbitrary",), vmem_limit_bytes=_vmem_limit(vm)),
        name="outproj",
    )(oa2, or2, x2, wo_bf, gt, g2, sh, sc, wr_both, wr_hi, br)


def _route_kernel(lg_ref, info_ref, cnt_ref, run_ref):
    i = pl.program_id(0)
    tm = lg_ref.shape[0]

    @pl.when(i == 0)
    def _():
        run_ref[...] = jnp.zeros_like(run_ref)

    lg = lg_ref[...]
    lane = lax.broadcasted_iota(I32, lg.shape, 1)
    gl = jnp.where(lane < N_GROUPS, lg, NEG_BIG)
    gmax = jnp.max(gl, axis=-1, keepdims=True)
    gsum = jnp.sum(jnp.exp(gl - gmax), axis=-1, keepdims=True)
    g_gate = 1.0 / gsum
    g_idx = jnp.min(jnp.where(gl == gmax, lane, LANES), axis=-1, keepdims=True)
    lo = N_GROUPS + EXPERTS_PER_GROUP * g_idx
    el = jnp.where((lane >= lo) & (lane < lo + EXPERTS_PER_GROUP), lg, NEG_BIG)
    e1max = jnp.max(el, axis=-1, keepdims=True)
    l1 = jnp.min(jnp.where(el == e1max, lane, LANES), axis=-1, keepdims=True)
    el2 = jnp.where(lane == l1, NEG_BIG, el)
    e2max = jnp.max(el2, axis=-1, keepdims=True)
    l2 = jnp.min(jnp.where(el2 == e2max, lane, LANES), axis=-1, keepdims=True)
    t2 = jnp.exp(e2max - e1max)
    w1 = g_gate / (1.0 + t2)
    w2 = g_gate * t2 / (1.0 + t2)
    ex1 = l1 - N_GROUPS
    ex2 = l2 - N_GROUPS

    oh1 = jnp.where(lane == ex1, 1.0, 0.0)
    oh2 = jnp.where(lane == ex2, 1.0, 0.0)
    ti = lax.broadcasted_iota(I32, (tm, tm), 0)
    si = lax.broadcasted_iota(I32, (tm, tm), 1)
    lower = jnp.where(si < ti, 1.0, 0.0).astype(BF16)
    pre1 = _dot(lower, oh1.astype(BF16))
    pre2 = _dot(lower, oh2.astype(BF16))
    cnt1 = jnp.sum(oh1, axis=0, keepdims=True)
    cnt2 = jnp.sum(oh2, axis=0, keepdims=True)
    run = run_ref[...]
    rank1 = jnp.sum(oh1 * (pre1 + run), axis=-1, keepdims=True)
    rank2 = jnp.sum(oh2 * (pre2 + run + cnt1), axis=-1, keepdims=True)
    run = run + cnt1 + cnt2
    run_ref[...] = run
    cnt_ref[...] = run

    info = jnp.where(lane == 0, ex1.astype(F32), 0.0)
    info = jnp.where(lane == 1, ex2.astype(F32), info)
    info = jnp.where(lane == 2, rank1, info)
    info = jnp.where(lane == 3, rank2, info)
    info = jnp.where(lane == 4, w1, info)
    info = jnp.where(lane == 5, w2, info)
    info_ref[...] = info


def _route(lg):
    M = lg.shape[0]
    tm = 512
    return pl.pallas_call(
        _route_kernel,
        grid=(M // tm,),
        in_specs=[pl.BlockSpec((tm, LANES), lambda i: (i, 0))],
        out_specs=[pl.BlockSpec((tm, LANES), lambda i: (i, 0)),
                   pl.BlockSpec((1, LANES), lambda i: (0, 0))],
        out_shape=[jax.ShapeDtypeStruct((M, LANES), F32),
                   jax.ShapeDtypeStruct((1, LANES), F32)],
        scratch_shapes=[pltpu.VMEM((1, LANES), F32)],
        compiler_params=pltpu.CompilerParams(dimension_semantics=("arbitrary",)),
        name="route",
    )(lg)


def _plan_kernel(info_ref, cnt_ref, slot_ref, meta_ref):
    cnt = cnt_ref[...]
    lane_r = lax.broadcasted_iota(I32, (1, LANES), 1)
    nblk = jnp.floor((cnt + (MOE_BLOCK - 1)) * (1.0 / MOE_BLOCK))
    ei = lax.broadcasted_iota(I32, (LANES, LANES), 0)
    ej = lax.broadcasted_iota(I32, (LANES, LANES), 1)
    upper = jnp.where(ei <= ej, 1.0, 0.0).astype(BF16)
    nb8 = jnp.broadcast_to(nblk, (8, LANES)).astype(BF16)
    bend = _dot(nb8, upper)[0:1, :]
    bstart = bend - nblk
    pstart = bstart * MOE_BLOCK

    info = info_ref[...]
    lane = lax.broadcasted_iota(I32, info.shape, 1)
    ex1 = info[:, 0:1].astype(I32)
    ex2 = info[:, 1:2].astype(I32)
    s1 = jnp.sum(jnp.where(lane == ex1, pstart, 0.0), axis=-1, keepdims=True) + info[:, 2:3]
    s2 = jnp.sum(jnp.where(lane == ex2, pstart, 0.0), axis=-1, keepdims=True) + info[:, 3:4]
    slot = jnp.where(lane == 0, s1, 0.0)
    slot = jnp.where(lane == 1, s2, slot)
    slot_ref[...] = slot.astype(I32)

    blk = lax.broadcasted_iota(I32, (LANES, 2 * LANES), 1).astype(F32)
    bend_col = jnp.sum(jnp.where(ei == ej, jnp.broadcast_to(bend, (LANES, LANES)), 0.0),
                       axis=-1, keepdims=True)
    erow = lax.broadcasted_iota(I32, (LANES, 2 * LANES), 0)
    hit = jnp.where((bend_col <= blk) & (erow < N_EXPERTS), 1.0, 0.0)
    bexp = jnp.minimum(jnp.sum(hit, axis=0, keepdims=True), N_EXPERTS - 1.0)
    n_used = jnp.max(jnp.where(lane_r < N_EXPERTS, bend, 0.0), axis=-1, keepdims=True)
    lastblk = jnp.where(nblk > 0, bend - 1.0, -1.0)
    r8 = lax.broadcasted_iota(I32, (8, 2 * LANES), 0)
    last2 = jnp.concatenate([lastblk, jnp.full((1, LANES), -1.0)], axis=1)
    meta = jnp.where(r8 == 0, jnp.broadcast_to(bexp, (8, 2 * LANES)), 0.0)
    meta = jnp.where(r8 == 1, jnp.broadcast_to(n_used, (8, 2 * LANES)), meta)
    meta = jnp.where(r8 == 2, jnp.broadcast_to(last2, (8, 2 * LANES)), meta)
    meta_ref[...] = meta.astype(I32)


def _plan(info, cnt):
    M = info.shape[0]
    tm = 1024
    return pl.pallas_call(
        _plan_kernel,
        grid=(M // tm,),
        in_specs=[pl.BlockSpec((tm, LANES), lambda i: (i, 0)),
                  pl.BlockSpec((1, LANES), lambda i: (0, 0))],
        out_specs=[pl.BlockSpec((tm, LANES), lambda i: (i, 0)),
                   pl.BlockSpec((8, 2 * LANES), lambda i: (0, 0))],
        out_shape=[jax.ShapeDtypeStruct((M, LANES), I32),
                   jax.ShapeDtypeStruct((8, 2 * LANES), I32)],
        compiler_params=pltpu.CompilerParams(dimension_semantics=("arbitrary",)),
        name="plan",
    )(info, cnt)


ROW_DEPTH = 3
DMA_GROUP = 16
WEIGHT_DEPTH = 3


def _slot_table_kernel(slots_ref, inv_ref, *, n_assign, n_slots):
    def init(b, c):
        base = n_assign + lax.rem(b + ROW_DEPTH - 1, ROW_DEPTH) * MOE_BLOCK
        for j in range(MOE_BLOCK):
            inv_ref[b * MOE_BLOCK + j] = base + j
        return c

    def put(a, c):
        inv_ref[slots_ref[a] + MOE_BLOCK] = a
        return c

    lax.fori_loop(0, n_slots // MOE_BLOCK, init, 0)
    lax.fori_loop(0, n_assign, put, 0, unroll=16)


def _slot_table(slots_km, n_blocks):
    n_assign = slots_km.shape[0]
    n_slots = (n_blocks + ROW_DEPTH) * MOE_BLOCK
    gs = pltpu.PrefetchScalarGridSpec(
        num_scalar_prefetch=1, grid=(1,), in_specs=[],
        out_specs=pl.BlockSpec(memory_space=pltpu.SMEM))
    return pl.pallas_call(
        functools.partial(_slot_table_kernel, n_assign=n_assign, n_slots=n_slots),
        grid_spec=gs,
        out_shape=jax.ShapeDtypeStruct((n_slots,), I32),
        compiler_params=pltpu.CompilerParams(dimension_semantics=("arbitrary",)),
        name="slot_table",
    )(slots_km)


def _experts_kernel(bexp_ref, seg_ref, inv_ref, h2_ref, w1_ref, w3_ref, w2_ref, y2_ref,
                    xbuf, ybuf, w1buf, w3buf, w2buf, sem_x, sem_y, sem_w, par_ref, *, n_assign):
    i = pl.program_id(0)
    n_used = seg_ref[N_EXPERTS]
    n_tok = n_assign // 2
    B = MOE_BLOCK
    FF = w1buf.shape[2]
    D = w1buf.shape[1]
    ND = ROW_DEPTH
    NW = w1buf.shape[0]

    def src_row(a):
        if n_tok & (n_tok - 1) == 0:
            return a & (n_tok - 1)
        return lax.rem(a, n_tok)

    def weight_copies(e, slot):
        return (pltpu.make_async_copy(w1_ref.at[e], w1buf.at[slot], sem_w.at[slot]),
                pltpu.make_async_copy(w3_ref.at[e], w3buf.at[slot], sem_w.at[slot]),
                pltpu.make_async_copy(w2_ref.at[e], w2buf.at[slot], sem_w.at[slot]))

    def start_weights(e, slot):
        c1, c3, c2 = weight_copies(e, slot)
        c1.start(priority=1)
        c3.start(priority=1)
        c2.start(priority=0)

    def prefetch_expert_after(e, hops, slot):
        ok = hops >= 0
        cur = e
        for _ in range(hops):
            nb = seg_ref[cur] + 1
            ok = jnp.logical_and(ok, nb < n_used)
            cur = bexp_ref[jnp.where(ok, nb, 0)]

        @pl.when(ok)
        def _():
            start_weights(cur, slot)

    def gather_group(b, g):
        slot = lax.rem(b + ND, ND)
        for r in range(g * DMA_GROUP, (g + 1) * DMA_GROUP):
            a = inv_ref[(b + 1) * B + r]
            pltpu.make_async_copy(h2_ref.at[pl.ds(src_row(a), 1)], xbuf.at[slot, pl.ds(r, 1)],
                                  sem_x.at[slot]).start()

    def scatter_group(b, g):
        slot = lax.rem(b + ND, ND)
        for r in range(g * DMA_GROUP, (g + 1) * DMA_GROUP):
            a = inv_ref[(b + 1) * B + r]
            pltpu.make_async_copy(ybuf.at[slot, pl.ds(r, 1)], y2_ref.at[pl.ds(a, 1)],
                                  sem_y.at[slot]).start()

    def wait_block(buf, sem, b):
        slot = lax.rem(b + ND, ND)
        pltpu.make_async_copy(buf.at[slot], buf.at[slot], sem.at[slot]).wait()

    n_groups = B // DMA_GROUP

    @pl.when(i == 0)
    def _():
        ybuf[...] = jnp.zeros_like(ybuf)
        for s in range(ND - 1):
            pltpu.make_async_copy(ybuf.at[s], y2_ref.at[pl.ds(n_assign + s * B, B)], sem_y.at[s]).start()
        par_ref[0] = 0
        e0 = bexp_ref[0]
        start_weights(e0, 0)
        for hops in range(1, NW):
            prefetch_expert_after(e0, hops, hops)
        for b in range(ND - 1):
            for g in range(n_groups):
                gather_group(b, g)

    active = i < n_used

    @pl.when(active)
    def _():
        e = bexp_ref[i]
        first = jnp.logical_or(i == 0, e != bexp_ref[jnp.maximum(i - 1, 0)])

        @pl.when(jnp.logical_and(first, i > 0))
        def _():
            par_ref[0] = par_ref[0] + 1

        @pl.when(first)
        def _():
            q = par_ref[0]
            for cp in weight_copies(e, lax.rem(q, NW)):
                cp.wait()

            @pl.when(i > 0)
            def _():
                prefetch_expert_after(e, NW - 1, lax.rem(q + NW - 1, NW))

        p = lax.rem(par_ref[0], NW)
        slot = lax.rem(i, ND)
        wait_block(xbuf, sem_x, i)
        wait_block(ybuf, sem_y, i)
        pending = [functools.partial(scatter_group, i - 1, g) for g in range(n_groups)]
        pending += [functools.partial(gather_group, i + ND - 1, g) for g in range(n_groups)]
        n_chunks = 2 * (FF // 256) + ROW_SLABS
        for _ in range(max(len(pending) - n_chunks, 0)):
            pending.pop(0)()

        def after_chunk():
            if pending:
                pending.pop(0)()

        xa, xb = _bf16_pair_unpack(jnp.concatenate([xbuf[slot, :, j, :] for j in range(ROW_SLABS)], axis=1))
        x = jnp.concatenate([xa, xb], axis=1).astype(BF16)
        sw = ybuf.shape[3]
        h1c, h3c = [], []
        for c0 in range(0, FF, 256):
            h1c.append(_dot(x, w1buf[p, :, c0:c0 + 256]))
            after_chunk()
            h3c.append(_dot(x, w3buf[p, :, c0:c0 + 256]))
            after_chunk()
        h1 = jnp.concatenate(h1c, axis=1)
        h3 = jnp.concatenate(h3c, axis=1)
        hid = (h1 * (1.0 / (1.0 + jnp.exp(-h1))) * h3).astype(BF16)
        for j in range(0, ROW_SLABS, 2):
            c0 = j * sw
            ya = _dot(hid, w2buf[p, :, c0:c0 + 2 * sw])
            after_chunk()
            yb = _dot(hid, w2buf[p, :, D // 2 + c0:D // 2 + c0 + 2 * sw])
            yp = _bf16_pair_pack(ya, yb)
            ybuf[slot, :, j, :] = yp[:, :sw]
            ybuf[slot, :, j + 1, :] = yp[:, sw:]
            after_chunk()
        while pending:
            pending.pop(0)()

    @pl.when(i == n_used)
    def _():
        wait_block(xbuf, sem_x, i)
        wait_block(ybuf, sem_y, i)
        for g in range(n_groups):
            scatter_group(i - 1, g)

    @pl.when(jnp.logical_and(i > n_used, i < n_used + ND - 1))
    def _():
        wait_block(xbuf, sem_x, i)
        wait_block(ybuf, sem_y, i)

    @pl.when(i == n_used + ND - 1)
    def _():
        wait_block(ybuf, sem_y, i)


def _experts(bexp, seg_meta, inv, h2, w1, w3, w2, n_blocks):
    M, DJ, SW = h2.shape
    D = 2 * DJ * SW
    FF = w1.shape[2]
    n_assign = 2 * M
    any_spec = pl.BlockSpec(memory_space=pl.ANY)
    row_buf = pltpu.VMEM((ROW_DEPTH, MOE_BLOCK, DJ, SW), jnp.uint32)
    gs = pltpu.PrefetchScalarGridSpec(
        num_scalar_prefetch=3,
        grid=(n_blocks + ROW_DEPTH,),
        in_specs=[any_spec, any_spec, any_spec, any_spec],
        out_specs=any_spec,
        scratch_shapes=[row_buf, row_buf,
                        pltpu.VMEM((WEIGHT_DEPTH, D, FF), F32), pltpu.VMEM((WEIGHT_DEPTH, D, FF), F32),
                        pltpu.VMEM((WEIGHT_DEPTH, FF, D), F32),
                        pltpu.SemaphoreType.DMA((ROW_DEPTH,)), pltpu.SemaphoreType.DMA((ROW_DEPTH,)),
                        pltpu.SemaphoreType.DMA((WEIGHT_DEPTH,)), pltpu.SMEM((1,), I32)],
    )
    vm = WEIGHT_DEPTH * 3 * D * FF * 4 + 2 * ROW_DEPTH * MOE_BLOCK * D * 4 + (12 << 20)
    return pl.pallas_call(
        functools.partial(_experts_kernel, n_assign=n_assign),
        grid_spec=gs,
        out_shape=jax.ShapeDtypeStruct((n_assign + ROW_DEPTH * MOE_BLOCK, DJ, SW), jnp.uint32),
        compiler_params=pltpu.CompilerParams(
            dimension_semantics=("arbitrary",), vmem_limit_bytes=_vmem_limit(vm),
            has_side_effects=True),
        name="experts",
    )(bexp, seg_meta, inv, h2, w1, w3, w2)


def _combine_kernel(x1_ref, info_ref, gt_ref, fg_ref, ya_ref, yb_ref, o_ref):
    info = info_ref[...]
    nj = ya_ref.shape[1]
    ya = jnp.concatenate(_bf16_pair_unpack(jnp.concatenate([ya_ref[:, j, :] for j in range(nj)], axis=1)), axis=1)
    yb = jnp.concatenate(_bf16_pair_unpack(jnp.concatenate([yb_ref[:, j, :] for j in range(nj)], axis=1)), axis=1)
    moe = info[:, 4:5] * ya + info[:, 5:6] * yb
    x = x1_ref[...] + gt_ref[0] * moe
    ms = jnp.mean(x * x, axis=-1, keepdims=True)
    o_ref[...] = x * lax.rsqrt(ms + NORM_EPS) * fg_ref[...]


def _combine(x1, info, gt, final_g, y2, T):
    M, D = x1.shape
    tm = 256
    tps = T // tm
    nt = M // tm
    vm = 2 * 4 * tm * D * 4 + tm * D * 16 + (8 << 20)
    return pl.pallas_call(
        _combine_kernel,
        grid=(nt,),
        in_specs=[pl.BlockSpec((tm, D), lambda i: (i, 0)),
                  pl.BlockSpec((tm, LANES), lambda i: (i, 0)),
                  pl.BlockSpec((1, 1, D), lambda i: (i // tps, 0, 0)),
                  pl.BlockSpec((1, D), lambda i: (0, 0)),
                  pl.BlockSpec((tm, ROW_SLABS, D // (2 * ROW_SLABS)), lambda i: (i, 0, 0)),
                  pl.BlockSpec((tm, ROW_SLABS, D // (2 * ROW_SLABS)), lambda i: (nt + i, 0, 0))],
        out_specs=pl.BlockSpec((tm, D), lambda i: (i, 0)),
        out_shape=jax.ShapeDtypeStruct((M, D), F32),
        compiler_params=pltpu.CompilerParams(
            dimension_semantics=("arbitrary",), vmem_limit_bytes=_vmem_limit(vm)),
        name="combine",
    )(x1, info, gt, final_g.reshape(1, D), y2, y2)


def _rope_tables(T):
    inv_freq = ROPE_THETA ** (-np.arange(0, HEAD_DIM, 2, dtype=np.float64) / HEAD_DIM)
    ang = np.arange(T, dtype=np.float64)[:, None] * inv_freq[None, :]
    cos, sin = np.cos(ang), np.sin(ang)
    cos_h = np.concatenate([cos, cos], axis=1)
    sin_h = np.concatenate([-sin, sin], axis=1)
    return (jnp.asarray(np.tile(cos_h, (1, 2)), F32), jnp.asarray(np.tile(sin_h, (1, 2)), F32))


def _layer(x, c, w_ada, b_ada, norm1_g, w_in, mu_shift, sinks, w0, w_decay_up, a0, w_a_up, w_g_up,
           k_k, k_a, r_k, ln_x_w, ln_x_b, w_o, norm2_g, w_rg, b_rg, w_re, b_re, w1, w3, w2, final_g):
    B, T, D = x.shape
    M = B * T
    mod = _ada(c, w_ada, b_ada)
    sh1, sc1, gt1, sh2, sc2, gt2 = [m.reshape(B, 1, D) for m in jnp.split(mod, 6, axis=-1)]

    pad = RWKV_PAD - RWKV_COLS
    w_bf = _cast_pad(jnp.swapaxes(w_in, 0, 1), ATTN_COLS + RWKV_PAD)
    mu_pad = jnp.pad(mu_shift, (0, pad)).reshape(1, RWKV_PAD)
    cos_t, sin_t = _rope_tables(T)
    x2 = x.reshape(M, D)
    za, zr = _inproj(x2, norm1_g.reshape(1, D), sh1, sc1, w_bf, mu_pad, cos_t, sin_t, T)

    o_attn = _attention(za.reshape(B, T, ATTN_COLS), sinks)
    o_rwkv = _rwkv(zr.reshape(B, T, RWKV_PAD), w0, w_decay_up, a0, w_a_up, w_g_up, k_k, k_a,
                   r_k.reshape(-1), ln_x_w, ln_x_b)

    wr = jnp.concatenate([w_rg, w_re], axis=1)
    wr = jnp.pad(wr, ((0, 0), (0, LANES - wr.shape[1])))
    wr_hi = wr.astype(BF16)
    wr_lo = (wr - wr_hi.astype(F32)).astype(BF16)
    br = jnp.pad(jnp.concatenate([b_rg, b_re]), (0, LANES - N_GROUPS - N_EXPERTS)).reshape(1, LANES)
    x1, h2, lg = _outproj(o_attn.reshape(M, Q_COLS), o_rwkv.reshape(M, RWKV_W), x2, w_o.astype(BF16),
                          gt1, norm2_g.reshape(1, D), sh2, sc2, jnp.concatenate([wr_hi, wr_lo], axis=1),
                          wr_hi, br, T)

    n_blocks = -(-(2 * M) // MOE_BLOCK) + N_EXPERTS
    info, cnt = _route(lg)
    slots, meta = _plan(info, cnt)
    slots_km = slots[:, :2].T.reshape(-1)
    seg_meta = jnp.concatenate([meta[2, :N_EXPERTS], meta[1, :1]])
    inv = _slot_table(slots_km, n_blocks)
    y2 = _experts(meta[0, :n_blocks + ROW_DEPTH], seg_meta, inv, h2, w1, w3, w2, n_blocks)
    out = _combine(x1, info, gt2, final_g, y2, T)
    return out.reshape(B, T, D)


def kernel(x, c, w_ada, b_ada, norm1_g, w_in, mu_shift, sinks, w0, w_decay_up, a0, w_a_up, w_g_up, k_k, k_a, r_k, ln_x_w, ln_x_b, w_o, norm2_g, w_router_group, b_router_group, w_router_expert, b_router_expert, w1, w3, w2, final_g):
    depth = w_ada.shape[0]
    assert depth == 1, "single-layer stack"
    l = 0
    return _layer(x, c, w_ada[l], b_ada[l], norm1_g[l], w_in[l], mu_shift[l], sinks[l], w0[l],
                  w_decay_up[l], a0[l], w_a_up[l], w_g_up[l], k_k[l], k_a[l], r_k[l], ln_x_w[l],
                  ln_x_b[l], w_o[l], norm2_g[l], w_router_group[l], b_router_group[l],
                  w_router_expert[l], b_router_expert[l], w1[l], w3[l], w2[l], final_g)
```

```python
import functools
import math

import jax
import jax.numpy as jnp
import numpy as np
from jax import lax
from jax.experimental import pallas as pl
from jax.experimental.pallas import tpu as pltpu

F32 = jnp.float32
BF16 = jnp.bfloat16
I32 = jnp.int32

LANES = 128
VMEM_BYTES_V7X = 64 * 1024 * 1024

HEAD_DIM = 64
ATTN_HEADS = 16
ATTN_KV_HEADS = 2
ATTN_GROUP = ATTN_HEADS // ATTN_KV_HEADS
WINDOW = 128
ROPE_THETA = 10000.0
RWKV_HEADS = 16
RWKV_N = 64
DECAY_LORA = 64
A_LORA = 64
GATE_LORA = 160
RWKV_LN_EPS = 64e-5
N_GROUPS = 8
EXPERTS_PER_GROUP = 8
N_EXPERTS = N_GROUPS * EXPERTS_PER_GROUP
EXPERT_FF = 512
MOE_BLOCK = 128
ROW_SLABS = 8
NORM_EPS = 1e-6

Q_COLS = ATTN_HEADS * HEAD_DIM
KV_COLS = ATTN_KV_HEADS * HEAD_DIM
ATTN_COLS = Q_COLS + 2 * KV_COLS
RWKV_W = RWKV_HEADS * RWKV_N
LORA_COLS = DECAY_LORA + A_LORA + GATE_LORA
LORA_PAD = 384
RWKV_COLS = 3 * RWKV_W + LORA_COLS
RWKV_PAD = 3 * RWKV_W + LORA_PAD
CHUNK = 64
RWKV_GROUP = 4
NEG_BIG = -1e30


def _vmem_limit(nbytes):
    return int(min(nbytes, VMEM_BYTES_V7X - 4 * 1024 * 1024))


def _dot(a, b):
    return jnp.dot(a, b, preferred_element_type=F32)


def _dot_nt(a, b):
    return lax.dot_general(a, b, (((1,), (1,)), ((), ())), preferred_element_type=F32)


def _dot_tn(a, b):
    return lax.dot_general(a, b, (((0,), (0,)), ((), ())), preferred_element_type=F32)


def _bf16_pair_pack(a, b):
    def hi16(x):
        u = pltpu.bitcast(x, jnp.uint32)
        r = u + jnp.uint32(0x7FFF) + ((u >> 16) & jnp.uint32(1))
        return jnp.where(x != x, u | jnp.uint32(0x00400000), r)
    return (hi16(a) & jnp.uint32(0xFFFF0000)) | (hi16(b) >> 16)


def _bf16_pair_unpack(w):
    return (pltpu.bitcast(w & jnp.uint32(0xFFFF0000), F32), pltpu.bitcast(w << 16, F32))


def _ada_kernel(c_ref, w_ref, b_ref, o_ref):
    c = c_ref[...]
    s = c * (1.0 / (1.0 + jnp.exp(-c)))
    o_ref[...] = _dot(s, w_ref[...]) + b_ref[...]


def _ada(c, w_ada, b_ada):
    B, D = c.shape
    N = w_ada.shape[1]
    tn = 1024
    cp = jnp.zeros((8, D), F32).at[:B].set(c)
    out = pl.pallas_call(
        _ada_kernel,
        grid=(N // tn,),
        in_specs=[pl.BlockSpec((8, D), lambda j: (0, 0)),
                  pl.BlockSpec((D, tn), lambda j: (0, j)),
                  pl.BlockSpec((1, tn), lambda j: (0, j))],
        out_specs=pl.BlockSpec((8, tn), lambda j: (0, j)),
        out_shape=jax.ShapeDtypeStruct((8, N), F32),
        compiler_params=pltpu.CompilerParams(
            dimension_semantics=("arbitrary",),
            vmem_limit_bytes=_vmem_limit(2 * D * tn * 4 + (8 << 20))),
        name="ada",
    )(cp, w_ada, b_ada.reshape(1, N))
    return out[:B]


def _cast_pad_kernel(wt_ref, o_ref, *, n_valid):
    j = pl.program_id(0)
    tn = wt_ref.shape[0]
    row = j * tn + lax.broadcasted_iota(I32, wt_ref.shape, 0)
    wt = jnp.where(row < n_valid, wt_ref[...], 0.0)
    o_ref[...] = wt.T.astype(o_ref.dtype)


def _cast_pad(w_t, n_out):
    N, K = w_t.shape
    tn = 512
    return pl.pallas_call(
        functools.partial(_cast_pad_kernel, n_valid=N),
        grid=(pl.cdiv(n_out, tn),),
        in_specs=[pl.BlockSpec((tn, K), lambda j: (j, 0))],
        out_specs=pl.BlockSpec((K, tn), lambda j: (0, j)),
        out_shape=jax.ShapeDtypeStruct((K, n_out), BF16),
        compiler_params=pltpu.CompilerParams(dimension_semantics=("arbitrary",)),
        name="cast_pad",
    )(w_t)


def _rope(z, cos, sin):
    w = z.shape[-1]
    lane = lax.broadcasted_iota(I32, z.shape, 1)
    first_half = (lane % HEAD_DIM) < (HEAD_DIM // 2)
    partner = jnp.where(first_half, pltpu.roll(z, w - HEAD_DIM // 2, 1), pltpu.roll(z, HEAD_DIM // 2, 1))
    return z * cos + partner * sin


def _inproj_kernel(x_ref, g_ref, sh_ref, sc_ref, w_ref, mu_ref, cos_ref, sin_ref,
                   za_ref, zr_ref, carry_ref, *, tiles_per_seq):
    i = pl.program_id(0)
    tm = x_ref.shape[0]
    x = x_ref[...]
    ms = jnp.mean(x * x, axis=-1, keepdims=True)
    y = x * lax.rsqrt(ms + NORM_EPS) * g_ref[...]
    h = (y * (1.0 + sc_ref[0]) + sh_ref[0]).astype(BF16)

    cos2 = jnp.concatenate([cos_ref[...], cos_ref[...]], axis=1)
    sin2 = jnp.concatenate([sin_ref[...], sin_ref[...]], axis=1)
    for c0 in range(0, ATTN_COLS, 256):
        z = _dot(h, w_ref[:, c0:c0 + 256])
        if c0 < Q_COLS:
            z = _rope(z, cos2, sin2)
        else:
            z = jnp.concatenate([_rope(z[:, :KV_COLS], cos_ref[...], sin_ref[...]), z[:, KV_COLS:]], axis=1)
        za_ref[:, c0:c0 + 256] = z.astype(za_ref.dtype)

    first = (i % tiles_per_seq) == 0
    row = lax.broadcasted_iota(I32, (tm, 1), 0)
    c0 = 0
    while c0 < RWKV_PAD:
        n = min(512, RWKV_PAD - c0)
        z = _dot(h, w_ref[:, ATTN_COLS + c0:ATTN_COLS + c0 + n])
        prev_last = jnp.where(first, 0.0, carry_ref[7:8, c0:c0 + n])
        z_prev = jnp.where(row == 0, prev_last, pltpu.roll(z, 1, 0))
        carry_ref[:, c0:c0 + n] = z[tm - 8:tm, :]
        zr_ref[:, c0:c0 + n] = z + (z_prev - z) * mu_ref[:, c0:c0 + n]
        c0 += n


def _inproj(x2, g, sh, sc, w_bf, mu_pad, cos_t, sin_t, T):
    M, D = x2.shape
    tm = 256
    tps = T // tm
    NW = w_bf.shape[1]
    kern = functools.partial(_inproj_kernel, tiles_per_seq=tps)
    vm = (D * NW * 2 + 2 * tm * D * 4 + 2 * tm * ATTN_COLS * 2 + 2 * tm * RWKV_PAD * 4
          + 4 * tm * 512 * 4 + tm * D * 8 + (8 << 20))
    return pl.pallas_call(
        kern,
        grid=(M // tm,),
        in_specs=[pl.BlockSpec((tm, D), lambda i: (i, 0)),
                  pl.BlockSpec((1, D), lambda i: (0, 0)),
                  pl.BlockSpec((1, 1, D), lambda i: (i // tps, 0, 0)),
                  pl.BlockSpec((1, 1, D), lambda i: (i // tps, 0, 0)),
                  pl.BlockSpec((D, NW), lambda i: (0, 0), pipeline_mode=pl.Buffered(1)),
                  pl.BlockSpec((1, RWKV_PAD), lambda i: (0, 0)),
                  pl.BlockSpec((tm, 2 * HEAD_DIM), lambda i: (i % tps, 0)),
                  pl.BlockSpec((tm, 2 * HEAD_DIM), lambda i: (i % tps, 0))],
        out_specs=[pl.BlockSpec((tm, ATTN_COLS), lambda i: (i, 0)),
                   pl.BlockSpec((tm, RWKV_PAD), lambda i: (i, 0))],
        out_shape=[jax.ShapeDtypeStruct((M, ATTN_COLS), BF16),
                   jax.ShapeDtypeStruct((M, RWKV_PAD), F32)],
        scratch_shapes=[pltpu.VMEM((8, RWKV_PAD), F32)],
        compiler_params=pltpu.CompilerParams(
            dimension_semantics=("arbitrary",), vmem_limit_bytes=_vmem_limit(vm)),
        name="inproj",
    )(x2, g, sh, sc, w_bf, mu_pad, cos_t, sin_t)


def _attn_kernel(sink_ref, q_ref, kc_ref, kp_ref, vc_ref, vp_ref, o_ref):
    n = pl.program_id(1)
    blk = q_ref.shape[1]
    row = lax.broadcasted_iota(I32, (blk, blk), 0)
    col = lax.broadcasted_iota(I32, (blk, blk), 1)
    mask = jnp.concatenate([(col > row) & (n > 0), col <= row], axis=1)
    scale = 1.0 / math.sqrt(HEAD_DIM)
    outs = []
    for kvh in range(ATTN_KV_HEADS):
        ks = slice(kvh * HEAD_DIM, (kvh + 1) * HEAD_DIM)
        kmat = jnp.concatenate([kp_ref[0, :, ks], kc_ref[0, :, ks]], axis=0)
        vmat = jnp.concatenate([vp_ref[0, :, ks], vc_ref[0, :, ks]], axis=0)
        for g in range(ATTN_GROUP):
            hd = kvh * ATTN_GROUP + g
            qh = q_ref[0, :, hd * HEAD_DIM:(hd + 1) * HEAD_DIM]
            s = _dot_nt(qh, kmat) * scale
            s = jnp.where(mask, s, NEG_BIG)
            sink = sink_ref[hd]
            m = jnp.maximum(jnp.max(s, axis=-1, keepdims=True), sink)
            p = jnp.exp(s - m)
            denom = jnp.sum(p, axis=-1, keepdims=True) + jnp.exp(sink - m)
            o = _dot(p.astype(BF16), vmat)
            outs.append(o / denom)
    o_ref[0] = jnp.concatenate(outs, axis=1).astype(o_ref.dtype)


def _attention(za3, sinks):
    B, T, _ = za3.shape
    nb = T // WINDOW
    kcol = Q_COLS // KV_COLS
    prev = lambda b, n, s: (b, jnp.maximum(n - 1, 0), kcol)
    prev_v = lambda b, n, s: (b, jnp.maximum(n - 1, 0), kcol + 1)
    gs = pltpu.PrefetchScalarGridSpec(
        num_scalar_prefetch=1,
        grid=(B, nb),
        in_specs=[pl.BlockSpec((1, WINDOW, Q_COLS), lambda b, n, s: (b, n, 0)),
                  pl.BlockSpec((1, WINDOW, KV_COLS), lambda b, n, s: (b, n, kcol)),
                  pl.BlockSpec((1, WINDOW, KV_COLS), prev),
                  pl.BlockSpec((1, WINDOW, KV_COLS), lambda b, n, s: (b, n, kcol + 1)),
                  pl.BlockSpec((1, WINDOW, KV_COLS), prev_v)],
        out_specs=pl.BlockSpec((1, WINDOW, Q_COLS), lambda b, n, s: (b, n, 0)),
    )
    return pl.pallas_call(
        _attn_kernel,
        grid_spec=gs,
        out_shape=jax.ShapeDtypeStruct((B, T, Q_COLS), BF16),
        compiler_params=pltpu.CompilerParams(dimension_semantics=("arbitrary", "arbitrary")),
        name="attn",
    )(sinks, za3, za3, za3, za3, za3)


def _rwkv_kernel(r_ref, k_ref, v_ref, lora_ref, w0_ref, wdu_ref, a0_ref, wau_ref, wgu_ref,
                 kk_ref, ka_ref, rk_ref, lnw_ref, lnb_ref, o_ref, s_ref):
    c = pl.program_id(0)
    nseq = r_ref.shape[0]
    C = r_ref.shape[1]
    N = RWKV_N
    G = RWKV_GROUP
    GW = G * N
    ng = RWKV_HEADS // G

    @pl.when(c == 0)
    def _():
        s_ref[...] = jnp.zeros_like(s_ref)

    ti = lax.broadcasted_iota(I32, (C, C), 0)
    si = lax.broadcasted_iota(I32, (C, C), 1)
    tril = jnp.where(si <= ti, 1.0, 0.0).astype(BF16)
    a_seq, g_seq, e_in_seq, e_ex_seq, e_neg_seq = [], [], [], [], []
    for b in range(nseq):
        lora = lora_ref[b]
        wd = lora[:, 0:DECAY_LORA]
        ad = lora[:, DECAY_LORA:DECAY_LORA + A_LORA]
        gd = lora[:, DECAY_LORA + A_LORA:LORA_COLS]
        wlin = w0_ref[...] + _dot(jnp.tanh(wd), wdu_ref[...])
        neg = -wlin
        softplus = jnp.maximum(neg, 0.0) + jnp.log(1.0 + jnp.exp(-jnp.abs(neg)))
        w = -softplus - 0.5
        logdec = -jnp.exp(w)
        a_seq.append(1.0 / (1.0 + jnp.exp(-(a0_ref[...] + _dot(ad, wau_ref[...])))))
        g_seq.append(_dot(1.0 / (1.0 + jnp.exp(-gd)), wgu_ref[...]))
        ld_hi = logdec.astype(BF16)
        ld_lo = (logdec - ld_hi.astype(F32)).astype(BF16)
        cum = _dot(tril, ld_hi) + _dot(tril, ld_lo)
        e_in_seq.append(jnp.exp(cum))
        e_ex_seq.append(jnp.exp(cum - logdec))
        e_neg_seq.append(jnp.exp(-cum))

    lane_head = lax.broadcasted_iota(I32, (1, GW), 1) // N
    t_row = lax.broadcasted_iota(I32, (C, GW), 0)
    s_lane = lax.broadcasted_iota(I32, (C, GW), 1) % N
    strict = s_lane < t_row
    incl = s_lane <= t_row
    bi = lax.broadcasted_iota(I32, (GW, GW), 0)
    bj = lax.broadcasted_iota(I32, (GW, GW), 1)
    same_head = (bi // N) == (bj // N)
    ones_bd = jnp.where(same_head, 1.0, 0.0).astype(BF16)
    eye = jnp.where(s_lane == t_row, 1.0, 0.0)

    def expand(xc):
        return jnp.concatenate([jnp.where(lane_head == h, xc, 0.0) for h in range(G)], axis=0).astype(BF16)

    def head_sum(xs):
        s = _dot(jnp.concatenate(xs, axis=0).astype(BF16), ones_bd)
        return [s[i * C:(i + 1) * C] for i in range(len(xs))]

    seq = [b for b in range(nseq) for _ in range(ng)]
    col = [slice(gi * GW, (gi + 1) * GW) for _ in range(nseq) for gi in range(ng)]
    rng = range(nseq * ng)
    r_l = [r_ref[seq[i], :, col[i]] for i in rng]
    k_l = [k_ref[seq[i], :, col[i]] for i in rng]
    v_l = [v_ref[seq[i], :, col[i]] for i in rng]
    a_l = [a_seq[seq[i]][:, col[i]] for i in rng]
    e_in = [e_in_seq[seq[i]][:, col[i]] for i in rng]
    e_neg = [e_neg_seq[seq[i]][:, col[i]] for i in rng]
    kk0 = [k_l[i] * kk_ref[:, col[i]] for i in rng]
    nrm2 = head_sum([kk0[i] * kk0[i] for i in rng])
    kk = [kk0[i] / jnp.maximum(jnp.sqrt(nrm2[i]), 1e-12) for i in rng]
    k2 = [k_l[i] * (1.0 + (a_l[i] - 1.0) * ka_ref[:, col[i]]) for i in rng]
    bt = [kk[i] * a_l[i] * e_neg[i] for i in rng]
    kt = [k2[i] * e_neg[i] for i in rng]
    left = [jnp.concatenate([-kk[i] * e_ex_seq[seq[i]][:, col[i]], r_l[i] * e_in[i]], axis=0).astype(BF16)
            for i in rng]
    right = [jnp.concatenate([bt[i], kt[i]], axis=0).astype(BF16) for i in rng]
    zed = [jnp.concatenate([expand(bt[i]), expand(kt[i])], axis=0) for i in rng]
    big = [_dot_nt(left[i], zed[i]) for i in rng]
    s_old = [s_ref[i] for i in rng]
    ls = [_dot_nt(left[i], s_old[i].astype(BF16)) for i in rng]
    v_bd = [expand(v_l[i]) for i in rng]
    pw = [jnp.where(strict, big[i][:C, :G * C], 0.0) for i in rng]
    tinv = [eye + pw[i] for i in rng]
    pw_bd = [expand(pw[i]) for i in rng]
    x = [ls[i][:C] + _dot(jnp.where(strict, big[i][:C, G * C:], 0.0).astype(BF16), v_bd[i]) for i in rng]
    span = 2
    while span < C:
        pw = [_dot(pw[i].astype(BF16), pw_bd[i]) for i in rng]
        pw_bd = [expand(pw[i]) for i in rng]
        tinv = [tinv[i] + _dot(tinv[i].astype(BF16), pw_bd[i]) for i in rng]
        span *= 2
    u = [_dot(tinv[i].astype(BF16), expand(x[i])) for i in rng]
    a_r = [jnp.concatenate([jnp.where(incl, big[i][C:, :G * C], 0.0),
                            jnp.where(incl, big[i][C:, G * C:], 0.0)], axis=1).astype(BF16) for i in rng]
    y = [ls[i][C:] + _dot(a_r[i], jnp.concatenate([expand(u[i]), v_bd[i]], axis=0)) for i in rng]
    uv = [jnp.concatenate([u[i], v_l[i]], axis=0).astype(BF16) for i in rng]
    for i in rng:
        s_ref[i] = ((s_old[i] + jnp.where(same_head, _dot_tn(uv[i], right[i]), 0.0))
                    * e_in[i][C - 1:C, :])
    ysum = head_sum(y)
    yc = [y[i] - ysum[i] * (1.0 / N) for i in rng]
    ysq = head_sum([yc[i] * yc[i] for i in rng])
    var = [ysq[i] * (1.0 / N) for i in rng]
    rksum = head_sum([r_l[i] * k2[i] * rk_ref[:, col[i]] for i in rng])
    bonus = [rksum[i] * v_l[i] for i in rng]
    for i in rng:
        yn = yc[i] * lax.rsqrt(var[i] + RWKV_LN_EPS) * lnw_ref[:, col[i]] + lnb_ref[:, col[i]]
        o_ref[seq[i], :, col[i]] = ((yn + bonus[i]) * g_seq[seq[i]][:, col[i]]).astype(o_ref.dtype)


def _rwkv(zr3, w0, wdu, a0, wau, wgu, k_k, k_a, r_k, ln_w, ln_b):
    B, T, _ = zr3.shape
    C = CHUNK
    W = RWKV_W
    GW = RWKV_GROUP * RWKV_N
    vec = lambda v: v.reshape(1, W)
    full = lambda shape: pl.BlockSpec(shape, lambda c: (0,) * len(shape))
    return pl.pallas_call(
        _rwkv_kernel,
        grid=(T // C,),
        in_specs=[pl.BlockSpec((B, C, W), lambda c: (0, c, 0)),
                  pl.BlockSpec((B, C, W), lambda c: (0, c, 1)),
                  pl.BlockSpec((B, C, W), lambda c: (0, c, 2)),
                  pl.BlockSpec((B, C, LORA_PAD), lambda c: (0, c, 3 * W // LORA_PAD)),
                  full((1, W)), full((DECAY_LORA, W)), full((1, W)), full((A_LORA, W)),
                  full((GATE_LORA, W)), full((1, W)), full((1, W)), full((1, W)),
                  full((1, W)), full((1, W))],
        out_specs=pl.BlockSpec((B, C, W), lambda c: (0, c, 0)),
        out_shape=jax.ShapeDtypeStruct((B, T, W), BF16),
        scratch_shapes=[pltpu.VMEM((B * RWKV_HEADS // RWKV_GROUP, GW, GW), F32)],
        compiler_params=pltpu.CompilerParams(dimension_semantics=("arbitrary",)),
        name="rwkv",
    )(zr3, zr3, zr3, zr3, vec(w0), wdu, vec(a0), wau, wgu, vec(k_k), vec(k_a), vec(r_k),
      vec(ln_w), vec(ln_b))


def _outproj_kernel(oa_ref, or_ref, x_ref, wo_ref, gt_ref, g_ref, sh_ref, sc_ref,
                    wrb_ref, wrh_ref, br_ref, x1_ref, h2_ref, lg_ref):
    mixed = _dot(oa_ref[...], wo_ref[0:Q_COLS, :]) + _dot(or_ref[...], wo_ref[Q_COLS:, :])
    x1 = x_ref[...] + gt_ref[0] * mixed
    x1_ref[...] = x1
    ms = jnp.mean(x1 * x1, axis=-1, keepdims=True)
    h2 = x1 * lax.rsqrt(ms + NORM_EPS) * g_ref[...] * (1.0 + sc_ref[0]) + sh_ref[0]
    half = h2.shape[1] // 2
    h2p = _bf16_pair_pack(h2[:, :half], h2[:, half:])
    for j in range(ROW_SLABS):
        w = h2_ref.shape[2]
        h2_ref[:, j, :] = h2p[:, j * w:(j + 1) * w]
    hh = h2.astype(BF16)
    hl = (h2 - hh.astype(F32)).astype(BF16)
    both = _dot(hh, wrb_ref[...])
    lg_ref[...] = both[:, :LANES] + both[:, LANES:] + _dot(hl, wrh_ref[...]) + br_ref[...]


def _outproj(oa2, or2, x2, wo_bf, gt, g2, sh, sc, wr_both, wr_hi, br, T):
    M, D = x2.shape
    tm = 256
    tps = T // tm
    bvec = pl.BlockSpec((1, 1, D), lambda i: (i // tps, 0, 0))
    vm = D * D * 2 + 2 * tm * D * (2 + 4 + 4 + 4) + tm * D * 16 + (8 << 20)
    return pl.pallas_call(
        _outproj_kernel,
        grid=(M // tm,),
        in_specs=[pl.BlockSpec((tm, Q_COLS), lambda i: (i, 0)),
                  pl.BlockSpec((tm, RWKV_W), lambda i: (i, 0)),
                  pl.BlockSpec((tm, D), lambda i: (i, 0)),
                  pl.BlockSpec((D, D), lambda i: (0, 0), pipeline_mode=pl.Buffered(1)),
                  bvec,
                  pl.BlockSpec((1, D), lambda i: (0, 0)),
                  bvec, bvec,
                  pl.BlockSpec((D, 2 * LANES), lambda i: (0, 0)),
                  pl.BlockSpec((D, LANES), lambda i: (0, 0)),
                  pl.BlockSpec((1, LANES), lambda i: (0, 0))],
        out_specs=[pl.BlockSpec((tm, D), lambda i: (i, 0)),
                   pl.BlockSpec((tm, ROW_SLABS, D // (2 * ROW_SLABS)), lambda i: (i, 0, 0)),
                   pl.BlockSpec((tm, LANES), lambda i: (i, 0))],
        out_shape=[jax.ShapeDtypeStruct((M, D), F32),
                   jax.ShapeDtypeStruct((M, ROW_SLABS, D // (2 * ROW_SLABS)), jnp.uint32),
                   jax.ShapeDtypeStruct((M, LANES), F32)],
        compiler_params=pltpu.CompilerParams(
            dimension_semantics=("arbitrary",), vmem_limit_bytes=_vmem_limit(vm)),
        name="outproj",
    )(oa2, or2, x2, wo_bf, gt, g2, sh, sc, wr_both, wr_hi, br)


def _route_kernel(lg_ref, info_ref, cnt_ref, run_ref):
    i = pl.program_id(0)
    tm = lg_ref.shape[0]

    @pl.when(i == 0)
    def _():
        run_ref[...] = jnp.zeros_like(run_ref)

    lg = lg_ref[...]
    lane = lax.broadcasted_iota(I32, lg.shape, 1)
    gl = jnp.where(lane < N_GROUPS, lg, NEG_BIG)
    gmax = jnp.max(gl, axis=-1, keepdims=True)
    gsum = jnp.sum(jnp.exp(gl - gmax), axis=-1, keepdims=True)
    g_gate = 1.0 / gsum
    g_idx = jnp.min(jnp.where(gl == gmax, lane, LANES), axis=-1, keepdims=True)
    lo = N_GROUPS + EXPERTS_PER_GROUP * g_idx
    el = jnp.where((lane >= lo) & (lane < lo + EXPERTS_PER_GROUP), lg, NEG_BIG)
    e1max = jnp.max(el, axis=-1, keepdims=True)
    l1 = jnp.min(jnp.where(el == e1max, lane, LANES), axis=-1, keepdims=True)
    el2 = jnp.where(lane == l1, NEG_BIG, el)
    e2max = jnp.max(el2, axis=-1, keepdims=True)
    l2 = jnp.min(jnp.where(el2 == e2max, lane, LANES), axis=-1, keepdims=True)
    t2 = jnp.exp(e2max - e1max)
    w1 = g_gate / (1.0 + t2)
    w2 = g_gate * t2 / (1.0 + t2)
    ex1 = l1 - N_GROUPS
    ex2 = l2 - N_GROUPS

    oh1 = jnp.where(lane == ex1, 1.0, 0.0)
    oh2 = jnp.where(lane == ex2, 1.0, 0.0)
    ti = lax.broadcasted_iota(I32, (tm, tm), 0)
    si = lax.broadcasted_iota(I32, (tm, tm), 1)
    lower = jnp.where(si < ti, 1.0, 0.0).astype(BF16)
    pre1 = _dot(lower, oh1.astype(BF16))
    pre2 = _dot(lower, oh2.astype(BF16))
    cnt1 = jnp.sum(oh1, axis=0, keepdims=True)
    cnt2 = jnp.sum(oh2, axis=0, keepdims=True)
    run = run_ref[...]
    rank1 = jnp.sum(oh1 * (pre1 + run), axis=-1, keepdims=True)
    rank2 = jnp.sum(oh2 * (pre2 + run + cnt1), axis=-1, keepdims=True)
    run = run + cnt1 + cnt2
    run_ref[...] = run
    cnt_ref[...] = run

    info = jnp.where(lane == 0, ex1.astype(F32), 0.0)
    info = jnp.where(lane == 1, ex2.astype(F32), info)
    info = jnp.where(lane == 2, rank1, info)
    info = jnp.where(lane == 3, rank2, info)
    info = jnp.where(lane == 4, w1, info)
    info = jnp.where(lane == 5, w2, info)
    info_ref[...] = info


def _route(lg):
    M = lg.shape[0]
    tm = 512
    return pl.pallas_call(
        _route_kernel,
        grid=(M // tm,),
        in_specs=[pl.BlockSpec((tm, LANES), lambda i: (i, 0))],
        out_specs=[pl.BlockSpec((tm, LANES), lambda i: (i, 0)),
                   pl.BlockSpec((1, LANES), lambda i: (0, 0))],
        out_shape=[jax.ShapeDtypeStruct((M, LANES), F32),
                   jax.ShapeDtypeStruct((1, LANES), F32)],
        scratch_shapes=[pltpu.VMEM((1, LANES), F32)],
        compiler_params=pltpu.CompilerParams(dimension_semantics=("arbitrary",)),
        name="route",
    )(lg)


def _plan_kernel(info_ref, cnt_ref, slot_ref, meta_ref):
    cnt = cnt_ref[...]
    lane_r = lax.broadcasted_iota(I32, (1, LANES), 1)
    nblk = jnp.floor((cnt + (MOE_BLOCK - 1)) * (1.0 / MOE_BLOCK))
    ei = lax.broadcasted_iota(I32, (LANES, LANES), 0)
    ej = lax.broadcasted_iota(I32, (LANES, LANES), 1)
    upper = jnp.where(ei <= ej, 1.0, 0.0).astype(BF16)
    nb8 = jnp.broadcast_to(nblk, (8, LANES)).astype(BF16)
    bend = _dot(nb8, upper)[0:1, :]
    bstart = bend - nblk
    pstart = bstart * MOE_BLOCK

    info = info_ref[...]
    lane = lax.broadcasted_iota(I32, info.shape, 1)
    ex1 = info[:, 0:1].astype(I32)
    ex2 = info[:, 1:2].astype(I32)
    s1 = jnp.sum(jnp.where(lane == ex1, pstart, 0.0), axis=-1, keepdims=True) + info[:, 2:3]
    s2 = jnp.sum(jnp.where(lane == ex2, pstart, 0.0), axis=-1, keepdims=True) + info[:, 3:4]
    slot = jnp.where(lane == 0, s1, 0.0)
    slot = jnp.where(lane == 1, s2, slot)
    slot_ref[...] = slot.astype(I32)

    blk = lax.broadcasted_iota(I32, (LANES, 2 * LANES), 1).astype(F32)
    bend_col = jnp.sum(jnp.where(ei == ej, jnp.broadcast_to(bend, (LANES, LANES)), 0.0),
                       axis=-1, keepdims=True)
    erow = lax.broadcasted_iota(I32, (LANES, 2 * LANES), 0)
    hit = jnp.where((bend_col <= blk) & (erow < N_EXPERTS), 1.0, 0.0)
    bexp = jnp.minimum(jnp.sum(hit, axis=0, keepdims=True), N_EXPERTS - 1.0)
    n_used = jnp.max(jnp.where(lane_r < N_EXPERTS, bend, 0.0), axis=-1, keepdims=True)
    lastblk = jnp.where(nblk > 0, bend - 1.0, -1.0)
    r8 = lax.broadcasted_iota(I32, (8, 2 * LANES), 0)
    last2 = jnp.concatenate([lastblk, jnp.full((1, LANES), -1.0)], axis=1)
    meta = jnp.where(r8 == 0, jnp.broadcast_to(bexp, (8, 2 * LANES)), 0.0)
    meta = jnp.where(r8 == 1, jnp.broadcast_to(n_used, (8, 2 * LANES)), meta)
    meta = jnp.where(r8 == 2, jnp.broadcast_to(last2, (8, 2 * LANES)), meta)
    meta_ref[...] = meta.astype(I32)


def _plan(info, cnt):
    M = info.shape[0]
    tm = 1024
    return pl.pallas_call(
        _plan_kernel,
        grid=(M // tm,),
        in_specs=[pl.BlockSpec((tm, LANES), lambda i: (i, 0)),
                  pl.BlockSpec((1, LANES), lambda i: (0, 0))],
        out_specs=[pl.BlockSpec((tm, LANES), lambda i: (i, 0)),
                   pl.BlockSpec((8, 2 * LANES), lambda i: (0, 0))],
        out_shape=[jax.ShapeDtypeStruct((M, LANES), I32),
                   jax.ShapeDtypeStruct((8, 2 * LANES), I32)],
        compiler_params=pltpu.CompilerParams(dimension_semantics=("arbitrary",)),
        name="plan",
    )(info, cnt)


ROW_DEPTH = 3
DMA_GROUP = 16
WEIGHT_DEPTH = 3


def _slot_table_kernel(slots_ref, pad_ref, inv_ref, buf_ref, sem, *, n_assign):
    fill = pltpu.make_async_copy(pad_ref, buf_ref, sem)
    fill.start()
    fill.wait()

    def put(a, c):
        buf_ref[slots_ref[a] + MOE_BLOCK] = a
        return c

    lax.fori_loop(0, n_assign, put, 0, unroll=16)
    out = pltpu.make_async_copy(buf_ref, inv_ref, sem)
    out.start()
    out.wait()


def _slot_table(slots_km, n_blocks):
    n_assign = slots_km.shape[0]
    n_slots = (n_blocks + ROW_DEPTH) * MOE_BLOCK
    row = np.arange(n_slots)
    pad_ids = n_assign + ((row // MOE_BLOCK + ROW_DEPTH - 1) % ROW_DEPTH) * MOE_BLOCK + row % MOE_BLOCK
    gs = pltpu.PrefetchScalarGridSpec(
        num_scalar_prefetch=1, grid=(1,),
        in_specs=[pl.BlockSpec(memory_space=pl.ANY)],
        out_specs=pl.BlockSpec(memory_space=pl.ANY),
        scratch_shapes=[pltpu.SMEM((n_slots,), I32), pltpu.SemaphoreType.DMA(())])
    return pl.pallas_call(
        functools.partial(_slot_table_kernel, n_assign=n_assign),
        grid_spec=gs,
        out_shape=jax.ShapeDtypeStruct((n_slots,), I32),
        compiler_params=pltpu.CompilerParams(dimension_semantics=("arbitrary",)),
        name="slot_table",
    )(slots_km, jnp.asarray(pad_ids, I32))


def _experts_kernel(bexp_ref, seg_ref, inv_ref, h2_ref, w1_ref, w3_ref, w2_ref, y2_ref,
                    xbuf, ybuf, w1buf, w3buf, w2buf, sem_x, sem_y, sem_w, par_ref, *, n_assign):
    i = pl.program_id(0)
    n_used = seg_ref[N_EXPERTS]
    n_tok = n_assign // 2
    B = MOE_BLOCK
    FF = w1buf.shape[2]
    D = w1buf.shape[1]
    ND = ROW_DEPTH
    NW = w1buf.shape[0]

    def src_row(a):
        if n_tok & (n_tok - 1) == 0:
            return a & (n_tok - 1)
        return lax.rem(a, n_tok)

    def weight_copies(e, slot):
        return (pltpu.make_async_copy(w1_ref.at[e], w1buf.at[slot], sem_w.at[slot]),
                pltpu.make_async_copy(w3_ref.at[e], w3buf.at[slot], sem_w.at[slot]),
                pltpu.make_async_copy(w2_ref.at[e], w2buf.at[slot], sem_w.at[slot]))

    def start_weights(e, slot):
        c1, c3, c2 = weight_copies(e, slot)
        c1.start(priority=1)
        c3.start(priority=1)
        c2.start(priority=0)

    def prefetch_expert_after(e, hops, slot):
        ok = hops >= 0
        cur = e
        for _ in range(hops):
            nb = seg_ref[cur] + 1
            ok = jnp.logical_and(ok, nb < n_used)
            cur = bexp_ref[jnp.where(ok, nb, 0)]

        @pl.when(ok)
        def _():
            start_weights(cur, slot)

    def gather_group(b, g):
        slot = lax.rem(b + ND, ND)
        for r in range(g * DMA_GROUP, (g + 1) * DMA_GROUP):
            a = inv_ref[(b + 1) * B + r]
            pltpu.make_async_copy(h2_ref.at[pl.ds(src_row(a), 1)], xbuf.at[slot, pl.ds(r, 1)],
                                  sem_x.at[slot]).start()

    def scatter_group(b, g):
        slot = lax.rem(b + ND, ND)
        for r in range(g * DMA_GROUP, (g + 1) * DMA_GROUP):
            a = inv_ref[(b + 1) * B + r]
            pltpu.make_async_copy(ybuf.at[slot, pl.ds(r, 1)], y2_ref.at[pl.ds(a, 1)],
                                  sem_y.at[slot]).start()

    def wait_block(buf, sem, b):
        slot = lax.rem(b + ND, ND)
        pltpu.make_async_copy(buf.at[slot], buf.at[slot], sem.at[slot]).wait()

    n_groups = B // DMA_GROUP

    @pl.when(i == 0)
    def _():
        ybuf[...] = jnp.zeros_like(ybuf)
        for s in range(ND - 1):
            pltpu.make_async_copy(ybuf.at[s], y2_ref.at[pl.ds(n_assign + s * B, B)], sem_y.at[s]).start()
        par_ref[0] = 0
        e0 = bexp_ref[0]
        start_weights(e0, 0)
        for hops in range(1, NW):
            prefetch_expert_after(e0, hops, hops)
        for b in range(ND - 1):
            for g in range(n_groups):
                gather_group(b, g)

    active = i < n_used

    @pl.when(active)
    def _():
        e = bexp_ref[i]
        first = jnp.logical_or(i == 0, e != bexp_ref[jnp.maximum(i - 1, 0)])

        @pl.when(jnp.logical_and(first, i > 0))
        def _():
            par_ref[0] = par_ref[0] + 1

        @pl.when(first)
        def _():
            q = par_ref[0]
            for cp in weight_copies(e, lax.rem(q, NW)):
                cp.wait()

            @pl.when(i > 0)
            def _():
                prefetch_expert_after(e, NW - 1, lax.rem(q + NW - 1, NW))

        p = lax.rem(par_ref[0], NW)
        slot = lax.rem(i, ND)
        wait_block(xbuf, sem_x, i)
        wait_block(ybuf, sem_y, i)
        pending = [functools.partial(scatter_group, i - 1, g) for g in range(n_groups)]
        pending += [functools.partial(gather_group, i + ND - 1, g) for g in range(n_groups)]
        n_chunks = 2 * (FF // 256) + ROW_SLABS
        for _ in range(max(len(pending) - n_chunks, 0)):
            pending.pop(0)()

        def after_chunk():
            if pending:
                pending.pop(0)()

        xa, xb = _bf16_pair_unpack(jnp.concatenate([xbuf[slot, :, j, :] for j in range(ROW_SLABS)], axis=1))
        x = jnp.concatenate([xa, xb], axis=1).astype(BF16)
        sw = ybuf.shape[3]
        h1c, h3c = [], []
        for c0 in range(0, FF, 256):
            h1c.append(_dot(x, w1buf[p, :, c0:c0 + 256]))
            after_chunk()
            h3c.append(_dot(x, w3buf[p, :, c0:c0 + 256]))
            after_chunk()
        h1 = jnp.concatenate(h1c, axis=1)
        h3 = jnp.concatenate(h3c, axis=1)
        hid = (h1 * (1.0 / (1.0 + jnp.exp(-h1))) * h3).astype(BF16)
        for j in range(0, ROW_SLABS, 2):
            c0 = j * sw
            ya = _dot(hid, w2buf[p, :, c0:c0 + 2 * sw])
            after_chunk()
            yb = _dot(hid, w2buf[p, :, D // 2 + c0:D // 2 + c0 + 2 * sw])
            yp = _bf16_pair_pack(ya, yb)
            ybuf[slot, :, j, :] = yp[:, :sw]
            ybuf[slot, :, j + 1, :] = yp[:, sw:]
            after_chunk()
        while pending:
            pending.pop(0)()

    @pl.when(i == n_used)
    def _():
        wait_block(xbuf, sem_x, i)
        wait_block(ybuf, sem_y, i)
        for g in range(n_groups):
            scatter_group(i - 1, g)

    @pl.when(jnp.logical_and(i > n_used, i < n_used + ND - 1))
    def _():
        wait_block(xbuf, sem_x, i)
        wait_block(ybuf, sem_y, i)

    @pl.when(i == n_used + ND - 1)
    def _():
        wait_block(ybuf, sem_y, i)


def _experts(bexp, seg_meta, inv, h2, w1, w3, w2, n_blocks):
    M, DJ, SW = h2.shape
    D = 2 * DJ * SW
    FF = w1.shape[2]
    n_assign = 2 * M
    any_spec = pl.BlockSpec(memory_space=pl.ANY)
    row_buf = pltpu.VMEM((ROW_DEPTH, MOE_BLOCK, DJ, SW), jnp.uint32)
    gs = pltpu.PrefetchScalarGridSpec(
        num_scalar_prefetch=3,
        grid=(n_blocks + ROW_DEPTH,),
        in_specs=[any_spec, any_spec, any_spec, any_spec],
        out_specs=any_spec,
        scratch_shapes=[row_buf, row_buf,
                        pltpu.VMEM((WEIGHT_DEPTH, D, FF), F32), pltpu.VMEM((WEIGHT_DEPTH, D, FF), F32),
                        pltpu.VMEM((WEIGHT_DEPTH, FF, D), F32),
                        pltpu.SemaphoreType.DMA((ROW_DEPTH,)), pltpu.SemaphoreType.DMA((ROW_DEPTH,)),
                        pltpu.SemaphoreType.DMA((WEIGHT_DEPTH,)), pltpu.SMEM((1,), I32)],
    )
    vm = WEIGHT_DEPTH * 3 * D * FF * 4 + 2 * ROW_DEPTH * MOE_BLOCK * D * 4 + (12 << 20)
    return pl.pallas_call(
        functools.partial(_experts_kernel, n_assign=n_assign),
        grid_spec=gs,
        out_shape=jax.ShapeDtypeStruct((n_assign + ROW_DEPTH * MOE_BLOCK, DJ, SW), jnp.uint32),
        compiler_params=pltpu.CompilerParams(
            dimension_semantics=("arbitrary",), vmem_limit_bytes=_vmem_limit(vm),
            has_side_effects=True),
        name="experts",
    )(bexp, seg_meta, inv, h2, w1, w3, w2)


def _combine_kernel(x1_ref, info_ref, gt_ref, fg_ref, ya_ref, yb_ref, o_ref):
    info = info_ref[...]
    nj = ya_ref.shape[1]
    ya = jnp.concatenate(_bf16_pair_unpack(jnp.concatenate([ya_ref[:, j, :] for j in range(nj)], axis=1)), axis=1)
    yb = jnp.concatenate(_bf16_pair_unpack(jnp.concatenate([yb_ref[:, j, :] for j in range(nj)], axis=1)), axis=1)
    moe = info[:, 4:5] * ya + info[:, 5:6] * yb
    x = x1_ref[...] + gt_ref[0] * moe
    ms = jnp.mean(x * x, axis=-1, keepdims=True)
    o_ref[...] = x * lax.rsqrt(ms + NORM_EPS) * fg_ref[...]


def _combine(x1, info, gt, final_g, y2, T):
    M, D = x1.shape
    tm = 256
    tps = T // tm
    nt = M // tm
    vm = 2 * 4 * tm * D * 4 + tm * D * 16 + (8 << 20)
    return pl.pallas_call(
        _combine_kernel,
        grid=(nt,),
        in_specs=[pl.BlockSpec((tm, D), lambda i: (i, 0)),
                  pl.BlockSpec((tm, LANES), lambda i: (i, 0)),
                  pl.BlockSpec((1, 1, D), lambda i: (i // tps, 0, 0)),
                  pl.BlockSpec((1, D), lambda i: (0, 0)),
                  pl.BlockSpec((tm, ROW_SLABS, D // (2 * ROW_SLABS)), lambda i: (i, 0, 0)),
                  pl.BlockSpec((tm, ROW_SLABS, D // (2 * ROW_SLABS)), lambda i: (nt + i, 0, 0))],
        out_specs=pl.BlockSpec((tm, D), lambda i: (i, 0)),
        out_shape=jax.ShapeDtypeStruct((M, D), F32),
        compiler_params=pltpu.CompilerParams(
            dimension_semantics=("arbitrary",), vmem_limit_bytes=_vmem_limit(vm)),
        name="combine",
    )(x1, info, gt, final_g.reshape(1, D), y2, y2)


def _rope_tables(T):
    inv_freq = ROPE_THETA ** (-np.arange(0, HEAD_DIM, 2, dtype=np.float64) / HEAD_DIM)
    ang = np.arange(T, dtype=np.float64)[:, None] * inv_freq[None, :]
    cos, sin = np.cos(ang), np.sin(ang)
    cos_h = np.concatenate([cos, cos], axis=1)
    sin_h = np.concatenate([-sin, sin], axis=1)
    return (jnp.asarray(np.tile(cos_h, (1, 2)), F32), jnp.asarray(np.tile(sin_h, (1, 2)), F32))


def _layer(x, c, w_ada, b_ada, norm1_g, w_in, mu_shift, sinks, w0, w_decay_up, a0, w_a_up, w_g_up,
           k_k, k_a, r_k, ln_x_w, ln_x_b, w_o, norm2_g, w_rg, b_rg, w_re, b_re, w1, w3, w2, final_g):
    B, T, D = x.shape
    M = B * T
    mod = _ada(c, w_ada, b_ada)
    sh1, sc1, gt1, sh2, sc2, gt2 = [m.reshape(B, 1, D) for m in jnp.split(mod, 6, axis=-1)]

    pad = RWKV_PAD - RWKV_COLS
    w_bf = _cast_pad(jnp.swapaxes(w_in, 0, 1), ATTN_COLS + RWKV_PAD)
    mu_pad = jnp.pad(mu_shift, (0, pad)).reshape(1, RWKV_PAD)
    cos_t, sin_t = _rope_tables(T)
    x2 = x.reshape(M, D)
    za, zr = _inproj(x2, norm1_g.reshape(1, D), sh1, sc1, w_bf, mu_pad, cos_t, sin_t, T)

    o_attn = _attention(za.reshape(B, T, ATTN_COLS), sinks)
    o_rwkv = _rwkv(zr.reshape(B, T, RWKV_PAD), w0, w_decay_up, a0, w_a_up, w_g_up, k_k, k_a,
                   r_k.reshape(-1), ln_x_w, ln_x_b)

    wr = jnp.concatenate([w_rg, w_re], axis=1)
    wr = jnp.pad(wr, ((0, 0), (0, LANES - wr.shape[1])))
    wr_hi = wr.astype(BF16)
    wr_lo = (wr - wr_hi.astype(F32)).astype(BF16)
    br = jnp.pad(jnp.concatenate([b_rg, b_re]), (0, LANES - N_GROUPS - N_EXPERTS)).reshape(1, LANES)
    x1, h2, lg = _outproj(o_attn.reshape(M, Q_COLS), o_rwkv.reshape(M, RWKV_W), x2, w_o.astype(BF16),
                          gt1, norm2_g.reshape(1, D), sh2, sc2, jnp.concatenate([wr_hi, wr_lo], axis=1),
                          wr_hi, br, T)

    n_blocks = -(-(2 * M) // MOE_BLOCK) + N_EXPERTS
    info, cnt = _route(lg)
    slots, meta = _plan(info, cnt)
    slots_km = slots[:, :2].T.reshape(-1)
    seg_meta = jnp.concatenate([meta[2, :N_EXPERTS], meta[1, :1]])
    inv = _slot_table(slots_km, n_blocks)
    y2 = _experts(meta[0, :n_blocks + ROW_DEPTH], seg_meta, inv, h2, w1, w3, w2, n_blocks)
    out = _combine(x1, info, gt2, final_g, y2, T)
    return out.reshape(B, T, D)


def kernel(x, c, w_ada, b_ada, norm1_g, w_in, mu_shift, sinks, w0, w_decay_up, a0, w_a_up, w_g_up, k_k, k_a, r_k, ln_x_w, ln_x_b, w_o, norm2_g, w_router_group, b_router_group, w_router_expert, b_router_expert, w1, w3, w2, final_g):
    depth = w_ada.shape[0]
    assert depth == 1, "single-layer stack"
    l = 0
    return _layer(x, c, w_ada[l], b_ada[l], norm1_g[l], w_in[l], mu_shift[l], sinks[l], w0[l],
                  w_decay_up[l], a0[l], w_a_up[l], w_g_up[l], k_k[l], k_a[l], r_k[l], ln_x_w[l],
                  ln_x_b[l], w_o[l], norm2_g[l], w_router_group[l], b_router_group[l],
                  w_router_expert[l], b_router_expert[l], w1[l], w3[l], w2[l], final_g)
```

```python
import functools
import math

import jax
import jax.numpy as jnp
import numpy as np
from jax import lax
from jax.experimental import pallas as pl
from jax.experimental.pallas import tpu as pltpu

F32 = jnp.float32
BF16 = jnp.bfloat16
I32 = jnp.int32

LANES = 128
VMEM_BYTES_V7X = 64 * 1024 * 1024

HEAD_DIM = 64
ATTN_HEADS = 16
ATTN_KV_HEADS = 2
ATTN_GROUP = ATTN_HEADS // ATTN_KV_HEADS
WINDOW = 128
ROPE_THETA = 10000.0
RWKV_HEADS = 16
RWKV_N = 64
DECAY_LORA = 64
A_LORA = 64
GATE_LORA = 160
RWKV_LN_EPS = 64e-5
N_GROUPS = 8
EXPERTS_PER_GROUP = 8
N_EXPERTS = N_GROUPS * EXPERTS_PER_GROUP
EXPERT_FF = 512
MOE_BLOCK = 128
ROW_SLABS = 8
NORM_EPS = 1e-6

Q_COLS = ATTN_HEADS * HEAD_DIM
KV_COLS = ATTN_KV_HEADS * HEAD_DIM
ATTN_COLS = Q_COLS + 2 * KV_COLS
RWKV_W = RWKV_HEADS * RWKV_N
LORA_COLS = DECAY_LORA + A_LORA + GATE_LORA
LORA_PAD = 384
RWKV_COLS = 3 * RWKV_W + LORA_COLS
RWKV_PAD = 3 * RWKV_W + LORA_PAD
CHUNK = 64
RWKV_GROUP = 4
NEG_BIG = -1e30


def _vmem_limit(nbytes):
    return int(min(nbytes, VMEM_BYTES_V7X - 4 * 1024 * 1024))


def _dot(a, b):
    return jnp.dot(a, b, preferred_element_type=F32)


def _dot_nt(a, b):
    return lax.dot_general(a, b, (((1,), (1,)), ((), ())), preferred_element_type=F32)


def _dot_tn(a, b):
    return lax.dot_general(a, b, (((0,), (0,)), ((), ())), preferred_element_type=F32)


def _bf16_pair_pack(a, b):
    def hi16(x):
        u = pltpu.bitcast(x, jnp.uint32)
        r = u + jnp.uint32(0x7FFF) + ((u >> 16) & jnp.uint32(1))
        return jnp.where(x != x, u | jnp.uint32(0x00400000), r)
    return (hi16(a) & jnp.uint32(0xFFFF0000)) | (hi16(b) >> 16)


def _bf16_pair_unpack(w):
    return (pltpu.bitcast(w & jnp.uint32(0xFFFF0000), F32), pltpu.bitcast(w << 16, F32))


def _ada_kernel(c_ref, w_ref, b_ref, o_ref):
    c = c_ref[...]
    s = c * (1.0 / (1.0 + jnp.exp(-c)))
    o_ref[...] = _dot(s, w_ref[...]) + b_ref[...]


def _ada(c, w_ada, b_ada):
    B, D = c.shape
    N = w_ada.shape[1]
    tn = 1024
    cp = jnp.zeros((8, D), F32).at[:B].set(c)
    out = pl.pallas_call(
        _ada_kernel,
        grid=(N // tn,),
        in_specs=[pl.BlockSpec((8, D), lambda j: (0, 0)),
                  pl.BlockSpec((D, tn), lambda j: (0, j)),
                  pl.BlockSpec((1, tn), lambda j: (0, j))],
        out_specs=pl.BlockSpec((8, tn), lambda j: (0, j)),
        out_shape=jax.ShapeDtypeStruct((8, N), F32),
        compiler_params=pltpu.CompilerParams(
            dimension_semantics=("arbitrary",),
            vmem_limit_bytes=_vmem_limit(2 * D * tn * 4 + (8 << 20))),
        name="ada",
    )(cp, w_ada, b_ada.reshape(1, N))
    return out[:B]


def _cast_pad_kernel(wt_ref, o_ref, *, n_valid):
    j = pl.program_id(0)
    tn = wt_ref.shape[0]
    row = j * tn + lax.broadcasted_iota(I32, wt_ref.shape, 0)
    wt = jnp.where(row < n_valid, wt_ref[...], 0.0)
    o_ref[...] = wt.T.astype(o_ref.dtype)


def _cast_pad(w_t, n_out):
    N, K = w_t.shape
    tn = 512
    return pl.pallas_call(
        functools.partial(_cast_pad_kernel, n_valid=N),
        grid=(pl.cdiv(n_out, tn),),
        in_specs=[pl.BlockSpec((tn, K), lambda j: (j, 0))],
        out_specs=pl.BlockSpec((K, tn), lambda j: (0, j)),
        out_shape=jax.ShapeDtypeStruct((K, n_out), BF16),
        compiler_params=pltpu.CompilerParams(dimension_semantics=("arbitrary",)),
        name="cast_pad",
    )(w_t)


def _rope(z, cos, sin):
    w = z.shape[-1]
    lane = lax.broadcasted_iota(I32, z.shape, 1)
    first_half = (lane % HEAD_DIM) < (HEAD_DIM // 2)
    partner = jnp.where(first_half, pltpu.roll(z, w - HEAD_DIM // 2, 1), pltpu.roll(z, HEAD_DIM // 2, 1))
    return z * cos + partner * sin


def _inproj_kernel(x_ref, g_ref, sh_ref, sc_ref, w_ref, mu_ref, cos_ref, sin_ref,
                   za_ref, zr_ref, carry_ref, *, tiles_per_seq):
    i = pl.program_id(0)
    tm = x_ref.shape[0]
    x = x_ref[...]
    ms = jnp.mean(x * x, axis=-1, keepdims=True)
    y = x * lax.rsqrt(ms + NORM_EPS) * g_ref[...]
    h = (y * (1.0 + sc_ref[0]) + sh_ref[0]).astype(BF16)

    cos2 = jnp.concatenate([cos_ref[...], cos_ref[...]], axis=1)
    sin2 = jnp.concatenate([sin_ref[...], sin_ref[...]], axis=1)
    for c0 in range(0, ATTN_COLS, 256):
        z = _dot(h, w_ref[:, c0:c0 + 256])
        if c0 < Q_COLS:
            z = _rope(z, cos2, sin2)
        else:
            z = jnp.concatenate([_rope(z[:, :KV_COLS], cos_ref[...], sin_ref[...]), z[:, KV_COLS:]], axis=1)
        za_ref[:, c0:c0 + 256] = z.astype(za_ref.dtype)

    first = (i % tiles_per_seq) == 0
    row = lax.broadcasted_iota(I32, (tm, 1), 0)
    c0 = 0
    while c0 < RWKV_PAD:
        n = min(512, RWKV_PAD - c0)
        z = _dot(h, w_ref[:, ATTN_COLS + c0:ATTN_COLS + c0 + n])
        prev_last = jnp.where(first, 0.0, carry_ref[7:8, c0:c0 + n])
        z_prev = jnp.where(row == 0, prev_last, pltpu.roll(z, 1, 0))
        carry_ref[:, c0:c0 + n] = z[tm - 8:tm, :]
        zr_ref[:, c0:c0 + n] = z + (z_prev - z) * mu_ref[:, c0:c0 + n]
        c0 += n


def _inproj(x2, g, sh, sc, w_bf, mu_pad, cos_t, sin_t, T):
    M, D = x2.shape
    tm = 256
    tps = T // tm
    NW = w_bf.shape[1]
    kern = functools.partial(_inproj_kernel, tiles_per_seq=tps)
    vm = (D * NW * 2 + 2 * tm * D * 4 + 2 * tm * ATTN_COLS * 2 + 2 * tm * RWKV_PAD * 4
          + 4 * tm * 512 * 4 + tm * D * 8 + (8 << 20))
    return pl.pallas_call(
        kern,
        grid=(M // tm,),
        in_specs=[pl.BlockSpec((tm, D), lambda i: (i, 0)),
                  pl.BlockSpec((1, D), lambda i: (0, 0)),
                  pl.BlockSpec((1, 1, D), lambda i: (i // tps, 0, 0)),
                  pl.BlockSpec((1, 1, D), lambda i: (i // tps, 0, 0)),
                  pl.BlockSpec((D, NW), lambda i: (0, 0), pipeline_mode=pl.Buffered(1)),
                  pl.BlockSpec((1, RWKV_PAD), lambda i: (0, 0)),
                  pl.BlockSpec((tm, 2 * HEAD_DIM), lambda i: (i % tps, 0)),
                  pl.BlockSpec((tm, 2 * HEAD_DIM), lambda i: (i % tps, 0))],
        out_specs=[pl.BlockSpec((tm, ATTN_COLS), lambda i: (i, 0)),
                   pl.BlockSpec((tm, RWKV_PAD), lambda i: (i, 0))],
        out_shape=[jax.ShapeDtypeStruct((M, ATTN_COLS), BF16),
                   jax.ShapeDtypeStruct((M, RWKV_PAD), F32)],
        scratch_shapes=[pltpu.VMEM((8, RWKV_PAD), F32)],
        compiler_params=pltpu.CompilerParams(
            dimension_semantics=("arbitrary",), vmem_limit_bytes=_vmem_limit(vm)),
        name="inproj",
    )(x2, g, sh, sc, w_bf, mu_pad, cos_t, sin_t)


def _attn_kernel(sink_ref, q_ref, kc_ref, kp_ref, vc_ref, vp_ref, o_ref):
    n = pl.program_id(1)
    blk = q_ref.shape[1]
    row = lax.broadcasted_iota(I32, (blk, blk), 0)
    col = lax.broadcasted_iota(I32, (blk, blk), 1)
    mask = jnp.concatenate([(col > row) & (n > 0), col <= row], axis=1)
    scale = 1.0 / math.sqrt(HEAD_DIM)
    outs = []
    for kvh in range(ATTN_KV_HEADS):
        ks = slice(kvh * HEAD_DIM, (kvh + 1) * HEAD_DIM)
        kmat = jnp.concatenate([kp_ref[0, :, ks], kc_ref[0, :, ks]], axis=0)
        vmat = jnp.concatenate([vp_ref[0, :, ks], vc_ref[0, :, ks]], axis=0)
        for g in range(ATTN_GROUP):
            hd = kvh * ATTN_GROUP + g
            qh = q_ref[0, :, hd * HEAD_DIM:(hd + 1) * HEAD_DIM]
            s = _dot_nt(qh, kmat) * scale
            s = jnp.where(mask, s, NEG_BIG)
            sink = sink_ref[hd]
            m = jnp.maximum(jnp.max(s, axis=-1, keepdims=True), sink)
            p = jnp.exp(s - m)
            denom = jnp.sum(p, axis=-1, keepdims=True) + jnp.exp(sink - m)
            o = _dot(p.astype(BF16), vmat)
            outs.append(o / denom)
    o_ref[0] = jnp.concatenate(outs, axis=1).astype(o_ref.dtype)


def _attention(za3, sinks):
    B, T, _ = za3.shape
    nb = T // WINDOW
    kcol = Q_COLS // KV_COLS
    prev = lambda b, n, s: (b, jnp.maximum(n - 1, 0), kcol)
    prev_v = lambda b, n, s: (b, jnp.maximum(n - 1, 0), kcol + 1)
    gs = pltpu.PrefetchScalarGridSpec(
        num_scalar_prefetch=1,
        grid=(B, nb),
        in_specs=[pl.BlockSpec((1, WINDOW, Q_COLS), lambda b, n, s: (b, n, 0)),
                  pl.BlockSpec((1, WINDOW, KV_COLS), lambda b, n, s: (b, n, kcol)),
                  pl.BlockSpec((1, WINDOW, KV_COLS), prev),
                  pl.BlockSpec((1, WINDOW, KV_COLS), lambda b, n, s: (b, n, kcol + 1)),
                  pl.BlockSpec((1, WINDOW, KV_COLS), prev_v)],
        out_specs=pl.BlockSpec((1, WINDOW, Q_COLS), lambda b, n, s: (b, n, 0)),
    )
    return pl.pallas_call(
        _attn_kernel,
        grid_spec=gs,
        out_shape=jax.ShapeDtypeStruct((B, T, Q_COLS), BF16),
        compiler_params=pltpu.CompilerParams(dimension_semantics=("arbitrary", "arbitrary")),
        name="attn",
    )(sinks, za3, za3, za3, za3, za3)


def _rwkv_kernel(r_ref, k_ref, v_ref, lora_ref, w0_ref, wdu_ref, a0_ref, wau_ref, wgu_ref,
                 kk_ref, ka_ref, rk_ref, lnw_ref, lnb_ref, o_ref, s_ref):
    c = pl.program_id(0)
    nseq = r_ref.shape[0]
    C = r_ref.shape[1]
    N = RWKV_N
    G = RWKV_GROUP
    GW = G * N
    ng = RWKV_HEADS // G

    @pl.when(c == 0)
    def _():
        s_ref[...] = jnp.zeros_like(s_ref)

    ti = lax.broadcasted_iota(I32, (C, C), 0)
    si = lax.broadcasted_iota(I32, (C, C), 1)
    tril = jnp.where(si <= ti, 1.0, 0.0).astype(BF16)
    a_seq, g_seq, e_in_seq, e_ex_seq, e_neg_seq = [], [], [], [], []
    for b in range(nseq):
        lora = lora_ref[b]
        wd = lora[:, 0:DECAY_LORA]
        ad = lora[:, DECAY_LORA:DECAY_LORA + A_LORA]
        gd = lora[:, DECAY_LORA + A_LORA:LORA_COLS]
        wlin = w0_ref[...] + _dot(jnp.tanh(wd), wdu_ref[...])
        neg = -wlin
        softplus = jnp.maximum(neg, 0.0) + jnp.log(1.0 + jnp.exp(-jnp.abs(neg)))
        w = -softplus - 0.5
        logdec = -jnp.exp(w)
        a_seq.append(1.0 / (1.0 + jnp.exp(-(a0_ref[...] + _dot(ad, wau_ref[...])))))
        g_seq.append(_dot(1.0 / (1.0 + jnp.exp(-gd)), wgu_ref[...]))
        ld_hi = logdec.astype(BF16)
        ld_lo = (logdec - ld_hi.astype(F32)).astype(BF16)
        cum = _dot(tril, ld_hi) + _dot(tril, ld_lo)
        e_in_seq.append(jnp.exp(cum))
        e_ex_seq.append(jnp.exp(cum - logdec))
        e_neg_seq.append(jnp.exp(-cum))

    lane_head = lax.broadcasted_iota(I32, (1, GW), 1) // N
    t_row = lax.broadcasted_iota(I32, (C, GW), 0)
    s_lane = lax.broadcasted_iota(I32, (C, GW), 1) % N
    strict = s_lane < t_row
    incl = s_lane <= t_row
    bi = lax.broadcasted_iota(I32, (GW, GW), 0)
    bj = lax.broadcasted_iota(I32, (GW, GW), 1)
    same_head = (bi // N) == (bj // N)
    ones_bd = jnp.where(same_head, 1.0, 0.0).astype(BF16)
    eye = jnp.where(s_lane == t_row, 1.0, 0.0)

    def expand(xc):
        return jnp.concatenate([jnp.where(lane_head == h, xc, 0.0) for h in range(G)], axis=0).astype(BF16)

    def head_sum(xs):
        s = _dot(jnp.concatenate(xs, axis=0).astype(BF16), ones_bd)
        return [s[i * C:(i + 1) * C] for i in range(len(xs))]

    seq = [b for b in range(nseq) for _ in range(ng)]
    col = [slice(gi * GW, (gi + 1) * GW) for _ in range(nseq) for gi in range(ng)]
    rng = range(nseq * ng)
    r_l = [r_ref[seq[i], :, col[i]] for i in rng]
    k_l = [k_ref[seq[i], :, col[i]] for i in rng]
    v_l = [v_ref[seq[i], :, col[i]] for i in rng]
    a_l = [a_seq[seq[i]][:, col[i]] for i in rng]
    e_in = [e_in_seq[seq[i]][:, col[i]] for i in rng]
    e_neg = [e_neg_seq[seq[i]][:, col[i]] for i in rng]
    kk0 = [k_l[i] * kk_ref[:, col[i]] for i in rng]
    nrm2 = head_sum([kk0[i] * kk0[i] for i in rng])
    kk = [kk0[i] / jnp.maximum(jnp.sqrt(nrm2[i]), 1e-12) for i in rng]
    k2 = [k_l[i] * (1.0 + (a_l[i] - 1.0) * ka_ref[:, col[i]]) for i in rng]
    bt = [kk[i] * a_l[i] * e_neg[i] for i in rng]
    kt = [k2[i] * e_neg[i] for i in rng]
    left = [jnp.concatenate([-kk[i] * e_ex_seq[seq[i]][:, col[i]], r_l[i] * e_in[i]], axis=0).astype(BF16)
            for i in rng]
    right = [jnp.concatenate([bt[i], kt[i]], axis=0).astype(BF16) for i in rng]
    zed = [jnp.concatenate([expand(bt[i]), expand(kt[i])], axis=0) for i in rng]
    big = [_dot_nt(left[i], zed[i]) for i in rng]
    s_old = [s_ref[i] for i in rng]
    ls = [_dot_nt(left[i], s_old[i].astype(BF16)) for i in rng]
    v_bd = [expand(v_l[i]) for i in rng]
    pw = [jnp.where(strict, big[i][:C, :G * C], 0.0) for i in rng]
    tinv = [eye + pw[i] for i in rng]
    pw = [_dot(pw[i].astype(BF16), expand(pw[i])) for i in rng]
    a_kk = [jnp.concatenate([jnp.where(strict, big[i][:C, G * C:], 0.0),
                             jnp.where(incl, big[i][C:, G * C:], 0.0)], axis=0).astype(BF16) for i in rng]
    kv = [_dot(a_kk[i], v_bd[i]) for i in rng]
    x = [ls[i][:C] + kv[i][:C] for i in rng]
    span = 2
    while span < C:
        pw_bd = [expand(pw[i]) for i in rng]
        if 2 * span < C:
            both = [_dot(jnp.concatenate([pw[i], tinv[i]], axis=0).astype(BF16), pw_bd[i]) for i in rng]
            pw = [both[i][:C] for i in rng]
            tinv = [tinv[i] + both[i][C:] for i in rng]
        else:
            tinv = [tinv[i] + _dot(tinv[i].astype(BF16), pw_bd[i]) for i in rng]
        span *= 2
    u = [_dot(tinv[i].astype(BF16), expand(x[i])) for i in rng]
    y = [ls[i][C:] + kv[i][C:]
         + _dot(jnp.where(incl, big[i][C:, :G * C], 0.0).astype(BF16), expand(u[i])) for i in rng]
    uv = [jnp.concatenate([u[i], v_l[i]], axis=0).astype(BF16) for i in rng]
    for i in rng:
        s_ref[i] = ((s_old[i] + jnp.where(same_head, _dot_tn(uv[i], right[i]), 0.0))
                    * e_in[i][C - 1:C, :])
    ysum = head_sum(y)
    yc = [y[i] - ysum[i] * (1.0 / N) for i in rng]
    ysq = head_sum([yc[i] * yc[i] for i in rng])
    var = [ysq[i] * (1.0 / N) for i in rng]
    rksum = head_sum([r_l[i] * k2[i] * rk_ref[:, col[i]] for i in rng])
    bonus = [rksum[i] * v_l[i] for i in rng]
    for i in rng:
        yn = yc[i] * lax.rsqrt(var[i] + RWKV_LN_EPS) * lnw_ref[:, col[i]] + lnb_ref[:, col[i]]
        o_ref[seq[i], :, col[i]] = ((yn + bonus[i]) * g_seq[seq[i]][:, col[i]]).astype(o_ref.dtype)


def _rwkv(zr3, w0, wdu, a0, wau, wgu, k_k, k_a, r_k, ln_w, ln_b):
    B, T, _ = zr3.shape
    C = CHUNK
    W = RWKV_W
    GW = RWKV_GROUP * RWKV_N
    vec = lambda v: v.reshape(1, W)
    full = lambda shape: pl.BlockSpec(shape, lambda c: (0,) * len(shape))
    return pl.pallas_call(
        _rwkv_kernel,
        grid=(T // C,),
        in_specs=[pl.BlockSpec((B, C, W), lambda c: (0, c, 0)),
                  pl.BlockSpec((B, C, W), lambda c: (0, c, 1)),
                  pl.BlockSpec((B, C, W), lambda c: (0, c, 2)),
                  pl.BlockSpec((B, C, LORA_PAD), lambda c: (0, c, 3 * W // LORA_PAD)),
                  full((1, W)), full((DECAY_LORA, W)), full((1, W)), full((A_LORA, W)),
                  full((GATE_LORA, W)), full((1, W)), full((1, W)), full((1, W)),
                  full((1, W)), full((1, W))],
        out_specs=pl.BlockSpec((B, C, W), lambda c: (0, c, 0)),
        out_shape=jax.ShapeDtypeStruct((B, T, W), BF16),
        scratch_shapes=[pltpu.VMEM((B * RWKV_HEADS // RWKV_GROUP, GW, GW), F32)],
        compiler_params=pltpu.CompilerParams(dimension_semantics=("arbitrary",)),
        name="rwkv",
    )(zr3, zr3, zr3, zr3, vec(w0), wdu, vec(a0), wau, wgu, vec(k_k), vec(k_a), vec(r_k),
      vec(ln_w), vec(ln_b))


def _outproj_kernel(oa_ref, or_ref, x_ref, wo_ref, gt_ref, g_ref, sh_ref, sc_ref,
                    wrb_ref, wrh_ref, br_ref, x1_ref, h2_ref, lg_ref):
    mixed = _dot(oa_ref[...], wo_ref[0:Q_COLS, :]) + _dot(or_ref[...], wo_ref[Q_COLS:, :])
    x1 = x_ref[...] + gt_ref[0] * mixed
    x1_ref[...] = x1
    ms = jnp.mean(x1 * x1, axis=-1, keepdims=True)
    h2 = x1 * lax.rsqrt(ms + NORM_EPS) * g_ref[...] * (1.0 + sc_ref[0]) + sh_ref[0]
    half = h2.shape[1] // 2
    h2p = _bf16_pair_pack(h2[:, :half], h2[:, half:])
    for j in range(ROW_SLABS):
        w = h2_ref.shape[2]
        h2_ref[:, j, :] = h2p[:, j * w:(j + 1) * w]
    hh = h2.astype(BF16)
    hl = (h2 - hh.astype(F32)).astype(BF16)
    both = _dot(hh, wrb_ref[...])
    lg_ref[...] = both[:, :LANES] + both[:, LANES:] + _dot(hl, wrh_ref[...]) + br_ref[...]


def _outproj(oa2, or2, x2, wo_bf, gt, g2, sh, sc, wr_both, wr_hi, br, T):
    M, D = x2.shape
    tm = 256
    tps = T // tm
    bvec = pl.BlockSpec((1, 1, D), lambda i: (i // tps, 0, 0))
    vm = D * D * 2 + 2 * tm * D * (2 + 4 + 4 + 4) + tm * D * 16 + (8 << 20)
    return pl.pallas_call(
        _outproj_kernel,
        grid=(M // tm,),
        in_specs=[pl.BlockSpec((tm, Q_COLS), lambda i: (i, 0)),
                  pl.BlockSpec((tm, RWKV_W), lambda i: (i, 0)),
                  pl.BlockSpec((tm, D), lambda i: (i, 0)),
                  pl.BlockSpec((D, D), lambda i: (0, 0), pipeline_mode=pl.Buffered(1)),
                  bvec,
                  pl.BlockSpec((1, D), lambda i: (0, 0)),
                  bvec, bvec,
                  pl.BlockSpec((D, 2 * LANES), lambda i: (0, 0)),
                  pl.BlockSpec((D, LANES), lambda i: (0, 0)),
                  pl.BlockSpec((1, LANES), lambda i: (0, 0))],
        out_specs=[pl.BlockSpec((tm, D), lambda i: (i, 0)),
                   pl.BlockSpec((tm, ROW_SLABS, D // (2 * ROW_SLABS)), lambda i: (i, 0, 0)),
                   pl.BlockSpec((tm, LANES), lambda i: (i, 0))],
        out_shape=[jax.ShapeDtypeStruct((M, D), F32),
                   jax.ShapeDtypeStruct((M, ROW_SLABS, D // (2 * ROW_SLABS)), jnp.uint32),
                   jax.ShapeDtypeStruct((M, LANES), F32)],
        compiler_params=pltpu.CompilerParams(
            dimension_semantics=("arbitrary",), vmem_limit_bytes=_vmem_limit(vm)),
        name="outproj",
    )(oa2, or2, x2, wo_bf, gt, g2, sh, sc, wr_both, wr_hi, br)


def _route_kernel(lg_ref, info_ref, cnt_ref, run_ref):
    i = pl.program_id(0)
    tm = lg_ref.shape[0]

    @pl.when(i == 0)
    def _():
        run_ref[...] = jnp.zeros_like(run_ref)

    lg = lg_ref[...]
    lane = lax.broadcasted_iota(I32, lg.shape, 1)
    gl = jnp.where(lane < N_GROUPS, lg, NEG_BIG)
    gmax = jnp.max(gl, axis=-1, keepdims=True)
    gsum = jnp.sum(jnp.exp(gl - gmax), axis=-1, keepdims=True)
    g_gate = 1.0 / gsum
    g_idx = jnp.min(jnp.where(gl == gmax, lane, LANES), axis=-1, keepdims=True)
    lo = N_GROUPS + EXPERTS_PER_GROUP * g_idx
    el = jnp.where((lane >= lo) & (lane < lo + EXPERTS_PER_GROUP), lg, NEG_BIG)
    e1max = jnp.max(el, axis=-1, keepdims=True)
    l1 = jnp.min(jnp.where(el == e1max, lane, LANES), axis=-1, keepdims=True)
    el2 = jnp.where(lane == l1, NEG_BIG, el)
    e2max = jnp.max(el2, axis=-1, keepdims=True)
    l2 = jnp.min(jnp.where(el2 == e2max, lane, LANES), axis=-1, keepdims=True)
    t2 = jnp.exp(e2max - e1max)
    w1 = g_gate / (1.0 + t2)
    w2 = g_gate * t2 / (1.0 + t2)
    ex1 = l1 - N_GROUPS
    ex2 = l2 - N_GROUPS

    oh1 = jnp.where(lane == ex1, 1.0, 0.0)
    oh2 = jnp.where(lane == ex2, 1.0, 0.0)
    ti = lax.broadcasted_iota(I32, (tm, tm), 0)
    si = lax.broadcasted_iota(I32, (tm, tm), 1)
    lower = jnp.where(si < ti, 1.0, 0.0).astype(BF16)
    pre1 = _dot(lower, oh1.astype(BF16))
    pre2 = _dot(lower, oh2.astype(BF16))
    cnt1 = jnp.sum(oh1, axis=0, keepdims=True)
    cnt2 = jnp.sum(oh2, axis=0, keepdims=True)
    run = run_ref[...]
    rank1 = jnp.sum(oh1 * (pre1 + run), axis=-1, keepdims=True)
    rank2 = jnp.sum(oh2 * (pre2 + run + cnt1), axis=-1, keepdims=True)
    run = run + cnt1 + cnt2
    run_ref[...] = run
    cnt_ref[...] = run

    info = jnp.where(lane == 0, ex1.astype(F32), 0.0)
    info = jnp.where(lane == 1, ex2.astype(F32), info)
    info = jnp.where(lane == 2, rank1, info)
    info = jnp.where(lane == 3, rank2, info)
    info = jnp.where(lane == 4, w1, info)
    info = jnp.where(lane == 5, w2, info)
    info_ref[...] = info


def _route(lg):
    M = lg.shape[0]
    tm = 512
    return pl.pallas_call(
        _route_kernel,
        grid=(M // tm,),
        in_specs=[pl.BlockSpec((tm, LANES), lambda i: (i, 0))],
        out_specs=[pl.BlockSpec((tm, LANES), lambda i: (i, 0)),
                   pl.BlockSpec((1, LANES), lambda i: (0, 0))],
        out_shape=[jax.ShapeDtypeStruct((M, LANES), F32),
                   jax.ShapeDtypeStruct((1, LANES), F32)],
        scratch_shapes=[pltpu.VMEM((1, LANES), F32)],
        compiler_params=pltpu.CompilerParams(dimension_semantics=("arbitrary",)),
        name="route",
    )(lg)


def _plan_kernel(info_ref, cnt_ref, slot_ref, meta_ref):
    cnt = cnt_ref[...]
    lane_r = lax.broadcasted_iota(I32, (1, LANES), 1)
    nblk = jnp.floor((cnt + (MOE_BLOCK - 1)) * (1.0 / MOE_BLOCK))
    ei = lax.broadcasted_iota(I32, (LANES, LANES), 0)
    ej = lax.broadcasted_iota(I32, (LANES, LANES), 1)
    upper = jnp.where(ei <= ej, 1.0, 0.0).astype(BF16)
    nb8 = jnp.broadcast_to(nblk, (8, LANES)).astype(BF16)
    bend = _dot(nb8, upper)[0:1, :]
    bstart = bend - nblk
    pstart = bstart * MOE_BLOCK

    info = info_ref[...]
    lane = lax.broadcasted_iota(I32, info.shape, 1)
    ex1 = info[:, 0:1].astype(I32)
    ex2 = info[:, 1:2].astype(I32)
    s1 = jnp.sum(jnp.where(lane == ex1, pstart, 0.0), axis=-1, keepdims=True) + info[:, 2:3]
    s2 = jnp.sum(jnp.where(lane == ex2, pstart, 0.0), axis=-1, keepdims=True) + info[:, 3:4]
    slot = jnp.where(lane == 0, s1, 0.0)
    slot = jnp.where(lane == 1, s2, slot)
    slot_ref[...] = slot.astype(I32)

    blk = lax.broadcasted_iota(I32, (LANES, 2 * LANES), 1).astype(F32)
    bend_col = jnp.sum(jnp.where(ei == ej, jnp.broadcast_to(bend, (LANES, LANES)), 0.0),
                       axis=-1, keepdims=True)
    erow = lax.broadcasted_iota(I32, (LANES, 2 * LANES), 0)
    hit = jnp.where((bend_col <= blk) & (erow < N_EXPERTS), 1.0, 0.0)
    bexp = jnp.minimum(jnp.sum(hit, axis=0, keepdims=True), N_EXPERTS - 1.0)
    n_used = jnp.max(jnp.where(lane_r < N_EXPERTS, bend, 0.0), axis=-1, keepdims=True)
    lastblk = jnp.where(nblk > 0, bend - 1.0, -1.0)
    r8 = lax.broadcasted_iota(I32, (8, 2 * LANES), 0)
    last2 = jnp.concatenate([lastblk, jnp.full((1, LANES), -1.0)], axis=1)
    meta = jnp.where(r8 == 0, jnp.broadcast_to(bexp, (8, 2 * LANES)), 0.0)
    meta = jnp.where(r8 == 1, jnp.broadcast_to(n_used, (8, 2 * LANES)), meta)
    meta = jnp.where(r8 == 2, jnp.broadcast_to(last2, (8, 2 * LANES)), meta)
    meta_ref[...] = meta.astype(I32)


def _plan(info, cnt):
    M = info.shape[0]
    tm = 1024
    return pl.pallas_call(
        _plan_kernel,
        grid=(M // tm,),
        in_specs=[pl.BlockSpec((tm, LANES), lambda i: (i, 0)),
                  pl.BlockSpec((1, LANES), lambda i: (0, 0))],
        out_specs=[pl.BlockSpec((tm, LANES), lambda i: (i, 0)),
                   pl.BlockSpec((8, 2 * LANES), lambda i: (0, 0))],
        out_shape=[jax.ShapeDtypeStruct((M, LANES), I32),
                   jax.ShapeDtypeStruct((8, 2 * LANES), I32)],
        compiler_params=pltpu.CompilerParams(dimension_semantics=("arbitrary",)),
        name="plan",
    )(info, cnt)


ROW_DEPTH = 3
DMA_GROUP = 16
WEIGHT_DEPTH = 3


def _slot_table_kernel(slots_ref, pad_ref, inv_ref, buf_ref, sem, *, n_assign):
    fill = pltpu.make_async_copy(pad_ref, buf_ref, sem)
    fill.start()
    fill.wait()

    def put(a, c):
        buf_ref[slots_ref[a] + MOE_BLOCK] = a
        return c

    lax.fori_loop(0, n_assign, put, 0, unroll=16)
    out = pltpu.make_async_copy(buf_ref, inv_ref, sem)
    out.start()
    out.wait()


def _slot_table(slots_km, n_blocks):
    n_assign = slots_km.shape[0]
    n_slots = (n_blocks + ROW_DEPTH) * MOE_BLOCK
    row = np.arange(n_slots)
    pad_ids = n_assign + ((row // MOE_BLOCK + ROW_DEPTH - 1) % ROW_DEPTH) * MOE_BLOCK + row % MOE_BLOCK
    gs = pltpu.PrefetchScalarGridSpec(
        num_scalar_prefetch=1, grid=(1,),
        in_specs=[pl.BlockSpec(memory_space=pl.ANY)],
        out_specs=pl.BlockSpec(memory_space=pl.ANY),
        scratch_shapes=[pltpu.SMEM((n_slots,), I32), pltpu.SemaphoreType.DMA(())])
    return pl.pallas_call(
        functools.partial(_slot_table_kernel, n_assign=n_assign),
        grid_spec=gs,
        out_shape=jax.ShapeDtypeStruct((n_slots,), I32),
        compiler_params=pltpu.CompilerParams(dimension_semantics=("arbitrary",)),
        name="slot_table",
    )(slots_km, jnp.asarray(pad_ids, I32))


def _experts_kernel(bexp_ref, seg_ref, inv_ref, h2_ref, w1_ref, w3_ref, w2_ref, y2_ref,
                    xbuf, ybuf, w1buf, w3buf, w2buf, sem_x, sem_y, sem_w, par_ref, *, n_assign):
    i = pl.program_id(0)
    n_used = seg_ref[N_EXPERTS]
    n_tok = n_assign // 2
    B = MOE_BLOCK
    FF = w1buf.shape[2]
    D = w1buf.shape[1]
    ND = ROW_DEPTH
    NW = w1buf.shape[0]

    def src_row(a):
        if n_tok & (n_tok - 1) == 0:
            return a & (n_tok - 1)
        return lax.rem(a, n_tok)

    def weight_copies(e, slot):
        return (pltpu.make_async_copy(w1_ref.at[e], w1buf.at[slot], sem_w.at[slot]),
                pltpu.make_async_copy(w3_ref.at[e], w3buf.at[slot], sem_w.at[slot]),
                pltpu.make_async_copy(w2_ref.at[e], w2buf.at[slot], sem_w.at[slot]))

    def start_weights(e, slot):
        c1, c3, c2 = weight_copies(e, slot)
        c1.start(priority=1)
        c3.start(priority=1)
        c2.start(priority=0)

    def prefetch_expert_after(e, hops, slot):
        ok = hops >= 0
        cur = e
        for _ in range(hops):
            nb = seg_ref[cur] + 1
            ok = jnp.logical_and(ok, nb < n_used)
            cur = bexp_ref[jnp.where(ok, nb, 0)]

        @pl.when(ok)
        def _():
            start_weights(cur, slot)

    def gather_group(b, g):
        slot = lax.rem(b + ND, ND)
        for r in range(g * DMA_GROUP, (g + 1) * DMA_GROUP):
            a = inv_ref[(b + 1) * B + r]
            pltpu.make_async_copy(h2_ref.at[pl.ds(src_row(a), 1)], xbuf.at[slot, pl.ds(r, 1)],
                                  sem_x.at[slot]).start()

    def scatter_group(b, g):
        slot = lax.rem(b + ND, ND)
        for r in range(g * DMA_GROUP, (g + 1) * DMA_GROUP):
            a = inv_ref[(b + 1) * B + r]
            pltpu.make_async_copy(ybuf.at[slot, pl.ds(r, 1)], y2_ref.at[pl.ds(a, 1)],
                                  sem_y.at[slot]).start()

    def wait_block(buf, sem, b):
        slot = lax.rem(b + ND, ND)
        pltpu.make_async_copy(buf.at[slot], buf.at[slot], sem.at[slot]).wait()

    n_groups = B // DMA_GROUP

    @pl.when(i == 0)
    def _():
        ybuf[...] = jnp.zeros_like(ybuf)
        for s in range(ND - 1):
            pltpu.make_async_copy(ybuf.at[s], y2_ref.at[pl.ds(n_assign + s * B, B)], sem_y.at[s]).start()
        par_ref[0] = 0
        e0 = bexp_ref[0]
        start_weights(e0, 0)
        for hops in range(1, NW):
            prefetch_expert_after(e0, hops, hops)
        for b in range(ND - 1):
            for g in range(n_groups):
                gather_group(b, g)

    active = i < n_used

    @pl.when(active)
    def _():
        e = bexp_ref[i]
        first = jnp.logical_or(i == 0, e != bexp_ref[jnp.maximum(i - 1, 0)])

        @pl.when(jnp.logical_and(first, i > 0))
        def _():
            par_ref[0] = par_ref[0] + 1

        @pl.when(first)
        def _():
            q = par_ref[0]
            for cp in weight_copies(e, lax.rem(q, NW)):
                cp.wait()

            @pl.when(i > 0)
            def _():
                prefetch_expert_after(e, NW - 1, lax.rem(q + NW - 1, NW))

        p = lax.rem(par_ref[0], NW)
        slot = lax.rem(i, ND)
        wait_block(xbuf, sem_x, i)
        wait_block(ybuf, sem_y, i)
        pending = [functools.partial(scatter_group, i - 1, g) for g in range(n_groups)]
        pending += [functools.partial(gather_group, i + ND - 1, g) for g in range(n_groups)]
        n_chunks = 2 * (FF // 256) + ROW_SLABS
        for _ in range(max(len(pending) - n_chunks, 0)):
            pending.pop(0)()

        def after_chunk():
            if pending:
                pending.pop(0)()

        xa, xb = _bf16_pair_unpack(jnp.concatenate([xbuf[slot, :, j, :] for j in range(ROW_SLABS)], axis=1))
        x = jnp.concatenate([xa, xb], axis=1).astype(BF16)
        sw = ybuf.shape[3]
        h1c, h3c = [], []
        for c0 in range(0, FF, 256):
            h1c.append(_dot(x, w1buf[p, :, c0:c0 + 256]))
            after_chunk()
            h3c.append(_dot(x, w3buf[p, :, c0:c0 + 256]))
            after_chunk()
        h1 = jnp.concatenate(h1c, axis=1)
        h3 = jnp.concatenate(h3c, axis=1)
        hid = (h1 * (1.0 / (1.0 + jnp.exp(-h1))) * h3).astype(BF16)
        for j in range(0, ROW_SLABS, 2):
            c0 = j * sw
            ya = _dot(hid, w2buf[p, :, c0:c0 + 2 * sw])
            after_chunk()
            yb = _dot(hid, w2buf[p, :, D // 2 + c0:D // 2 + c0 + 2 * sw])
            yp = _bf16_pair_pack(ya, yb)
            ybuf[slot, :, j, :] = yp[:, :sw]
            ybuf[slot, :, j + 1, :] = yp[:, sw:]
            after_chunk()
        while pending:
            pending.pop(0)()

    @pl.when(i == n_used)
    def _():
        wait_block(xbuf, sem_x, i)
        wait_block(ybuf, sem_y, i)
        for g in range(n_groups):
            scatter_group(i - 1, g)

    @pl.when(jnp.logical_and(i > n_used, i < n_used + ND - 1))
    def _():
        wait_block(xbuf, sem_x, i)
        wait_block(ybuf, sem_y, i)

    @pl.when(i == n_used + ND - 1)
    def _():
        wait_block(ybuf, sem_y, i)


def _experts(bexp, seg_meta, inv, h2, w1, w3, w2, n_blocks):
    M, DJ, SW = h2.shape
    D = 2 * DJ * SW
    FF = w1.shape[2]
    n_assign = 2 * M
    any_spec = pl.BlockSpec(memory_space=pl.ANY)
    row_buf = pltpu.VMEM((ROW_DEPTH, MOE_BLOCK, DJ, SW), jnp.uint32)
    gs = pltpu.PrefetchScalarGridSpec(
        num_scalar_prefetch=3,
        grid=(n_blocks + ROW_DEPTH,),
        in_specs=[any_spec, any_spec, any_spec, any_spec],
        out_specs=any_spec,
        scratch_shapes=[row_buf, row_buf,
                        pltpu.VMEM((WEIGHT_DEPTH, D, FF), F32), pltpu.VMEM((WEIGHT_DEPTH, D, FF), F32),
                        pltpu.VMEM((WEIGHT_DEPTH, FF, D), F32),
                        pltpu.SemaphoreType.DMA((ROW_DEPTH,)), pltpu.SemaphoreType.DMA((ROW_DEPTH,)),
                        pltpu.SemaphoreType.DMA((WEIGHT_DEPTH,)), pltpu.SMEM((1,), I32)],
    )
    vm = WEIGHT_DEPTH * 3 * D * FF * 4 + 2 * ROW_DEPTH * MOE_BLOCK * D * 4 + (12 << 20)
    return pl.pallas_call(
        functools.partial(_experts_kernel, n_assign=n_assign),
        grid_spec=gs,
        out_shape=jax.ShapeDtypeStruct((n_assign + ROW_DEPTH * MOE_BLOCK, DJ, SW), jnp.uint32),
        compiler_params=pltpu.CompilerParams(
            dimension_semantics=("arbitrary",), vmem_limit_bytes=_vmem_limit(vm),
            has_side_effects=True),
        name="experts",
    )(bexp, seg_meta, inv, h2, w1, w3, w2)


def _combine_kernel(x1_ref, info_ref, gt_ref, fg_ref, ya_ref, yb_ref, o_ref):
    info = info_ref[...]
    nj = ya_ref.shape[1]
    ya = jnp.concatenate(_bf16_pair_unpack(jnp.concatenate([ya_ref[:, j, :] for j in range(nj)], axis=1)), axis=1)
    yb = jnp.concatenate(_bf16_pair_unpack(jnp.concatenate([yb_ref[:, j, :] for j in range(nj)], axis=1)), axis=1)
    moe = info[:, 4:5] * ya + info[:, 5:6] * yb
    x = x1_ref[...] + gt_ref[0] * moe
    ms = jnp.mean(x * x, axis=-1, keepdims=True)
    o_ref[...] = x * lax.rsqrt(ms + NORM_EPS) * fg_ref[...]


def _combine(x1, info, gt, final_g, y2, T):
    M, D = x1.shape
    tm = 256
    tps = T // tm
    nt = M // tm
    vm = 2 * 4 * tm * D * 4 + tm * D * 16 + (8 << 20)
    return pl.pallas_call(
        _combine_kernel,
        grid=(nt,),
        in_specs=[pl.BlockSpec((tm, D), lambda i: (i, 0)),
                  pl.BlockSpec((tm, LANES), lambda i: (i, 0)),
                  pl.BlockSpec((1, 1, D), lambda i: (i // tps, 0, 0)),
                  pl.BlockSpec((1, D), lambda i: (0, 0)),
                  pl.BlockSpec((tm, ROW_SLABS, D // (2 * ROW_SLABS)), lambda i: (i, 0, 0)),
                  pl.BlockSpec((tm, ROW_SLABS, D // (2 * ROW_SLABS)), lambda i: (nt + i, 0, 0))],
        out_specs=pl.BlockSpec((tm, D), lambda i: (i, 0)),
        out_shape=jax.ShapeDtypeStruct((M, D), F32),
        compiler_params=pltpu.CompilerParams(
            dimension_semantics=("arbitrary",), vmem_limit_bytes=_vmem_limit(vm)),
        name="combine",
    )(x1, info, gt, final_g.reshape(1, D), y2, y2)


def _rope_tables(T):
    inv_freq = ROPE_THETA ** (-np.arange(0, HEAD_DIM, 2, dtype=np.float64) / HEAD_DIM)
    ang = np.arange(T, dtype=np.float64)[:, None] * inv_freq[None, :]
    cos, sin = np.cos(ang), np.sin(ang)
    cos_h = np.concatenate([cos, cos], axis=1)
    sin_h = np.concatenate([-sin, sin], axis=1)
    return (jnp.asarray(np.tile(cos_h, (1, 2)), F32), jnp.asarray(np.tile(sin_h, (1, 2)), F32))


def _layer(x, c, w_ada, b_ada, norm1_g, w_in, mu_shift, sinks, w0, w_decay_up, a0, w_a_up, w_g_up,
           k_k, k_a, r_k, ln_x_w, ln_x_b, w_o, norm2_g, w_rg, b_rg, w_re, b_re, w1, w3, w2, final_g):
    B, T, D = x.shape
    M = B * T
    mod = _ada(c, w_ada, b_ada)
    sh1, sc1, gt1, sh2, sc2, gt2 = [m.reshape(B, 1, D) for m in jnp.split(mod, 6, axis=-1)]

    pad = RWKV_PAD - RWKV_COLS
    w_bf = _cast_pad(jnp.swapaxes(w_in, 0, 1), ATTN_COLS + RWKV_PAD)
    mu_pad = jnp.pad(mu_shift, (0, pad)).reshape(1, RWKV_PAD)
    cos_t, sin_t = _rope_tables(T)
    x2 = x.reshape(M, D)
    za, zr = _inproj(x2, norm1_g.reshape(1, D), sh1, sc1, w_bf, mu_pad, cos_t, sin_t, T)

    o_attn = _attention(za.reshape(B, T, ATTN_COLS), sinks)
    o_rwkv = _rwkv(zr.reshape(B, T, RWKV_PAD), w0, w_decay_up, a0, w_a_up, w_g_up, k_k, k_a,
                   r_k.reshape(-1), ln_x_w, ln_x_b)

    wr = jnp.concatenate([w_rg, w_re], axis=1)
    wr = jnp.pad(wr, ((0, 0), (0, LANES - wr.shape[1])))
    wr_hi = wr.astype(BF16)
    wr_lo = (wr - wr_hi.astype(F32)).astype(BF16)
    br = jnp.pad(jnp.concatenate([b_rg, b_re]), (0, LANES - N_GROUPS - N_EXPERTS)).reshape(1, LANES)
    x1, h2, lg = _outproj(o_attn.reshape(M, Q_COLS), o_rwkv.reshape(M, RWKV_W), x2, w_o.astype(BF16),
                          gt1, norm2_g.reshape(1, D), sh2, sc2, jnp.concatenate([wr_hi, wr_lo], axis=1),
                          wr_hi, br, T)

    n_blocks = -(-(2 * M) // MOE_BLOCK) + N_EXPERTS
    info, cnt = _route(lg)
    slots, meta = _plan(info, cnt)
    slots_km = slots[:, :2].T.reshape(-1)
    seg_meta = jnp.concatenate([meta[2, :N_EXPERTS], meta[1, :1]])
    inv = _slot_table(slots_km, n_blocks)
    y2 = _experts(meta[0, :n_blocks + ROW_DEPTH], seg_meta, inv, h2, w1, w3, w2, n_blocks)
    out = _combine(x1, info, gt2, final_g, y2, T)
    return out.reshape(B, T, D)


def kernel(x, c, w_ada, b_ada, norm1_g, w_in, mu_shift, sinks, w0, w_decay_up, a0, w_a_up, w_g_up, k_k, k_a, r_k, ln_x_w, ln_x_b, w_o, norm2_g, w_router_group, b_router_group, w_router_expert, b_router_expert, w1, w3, w2, final_g):
    depth = w_ada.shape[0]
    assert depth == 1, "single-layer stack"
    l = 0
    return _layer(x, c, w_ada[l], b_ada[l], norm1_g[l], w_in[l], mu_shift[l], sinks[l], w0[l],
                  w_decay_up[l], a0[l], w_a_up[l], w_g_up[l], k_k[l], k_a[l], r_k[l], ln_x_w[l],
                  ln_x_b[l], w_o[l], norm2_g[l], w_router_group[l], b_router_group[l],
                  w_router_expert[l], b_router_expert[l], w1[l], w3[l], w2[l], final_g)
```

```python
import functools
import math

import jax
import jax.numpy as jnp
import numpy as np
from jax import lax
from jax.experimental import pallas as pl
from jax.experimental.pallas import tpu as pltpu

F32 = jnp.float32
BF16 = jnp.bfloat16
I32 = jnp.int32

LANES = 128
VMEM_BYTES_V7X = 64 * 1024 * 1024

HEAD_DIM = 64
ATTN_HEADS = 16
ATTN_KV_HEADS = 2
ATTN_GROUP = ATTN_HEADS // ATTN_KV_HEADS
WINDOW = 128
ROPE_THETA = 10000.0
RWKV_HEADS = 16
RWKV_N = 64
DECAY_LORA = 64
A_LORA = 64
GATE_LORA = 160
RWKV_LN_EPS = 64e-5
N_GROUPS = 8
EXPERTS_PER_GROUP = 8
N_EXPERTS = N_GROUPS * EXPERTS_PER_GROUP
EXPERT_FF = 512
MOE_BLOCK = 128
ROW_SLABS = 8
NORM_EPS = 1e-6

Q_COLS = ATTN_HEADS * HEAD_DIM
KV_COLS = ATTN_KV_HEADS * HEAD_DIM
ATTN_COLS = Q_COLS + 2 * KV_COLS
RWKV_W = RWKV_HEADS * RWKV_N
LORA_COLS = DECAY_LORA + A_LORA + GATE_LORA
LORA_PAD = 384
RWKV_COLS = 3 * RWKV_W + LORA_COLS
RWKV_PAD = 3 * RWKV_W + LORA_PAD
CHUNK = 64
RWKV_GROUP = 4
NEG_BIG = -1e30


def _vmem_limit(nbytes):
    return int(min(nbytes, VMEM_BYTES_V7X - 4 * 1024 * 1024))


def _dot(a, b):
    return jnp.dot(a, b, preferred_element_type=F32)


def _dot_nt(a, b):
    return lax.dot_general(a, b, (((1,), (1,)), ((), ())), preferred_element_type=F32)


def _dot_tn(a, b):
    return lax.dot_general(a, b, (((0,), (0,)), ((), ())), preferred_element_type=F32)


def _bf16_pair_pack(a, b):
    def hi16(x):
        u = pltpu.bitcast(x, jnp.uint32)
        r = u + jnp.uint32(0x7FFF) + ((u >> 16) & jnp.uint32(1))
        return jnp.where(x != x, u | jnp.uint32(0x00400000), r)
    return (hi16(a) & jnp.uint32(0xFFFF0000)) | (hi16(b) >> 16)


def _bf16_pair_unpack(w):
    return (pltpu.bitcast(w & jnp.uint32(0xFFFF0000), F32), pltpu.bitcast(w << 16, F32))


def _ada_kernel(c_ref, w_ref, b_ref, o_ref):
    c = c_ref[...]
    s = c * (1.0 / (1.0 + jnp.exp(-c)))
    o_ref[...] = _dot(s, w_ref[...]) + b_ref[...]


def _ada(c, w_ada, b_ada):
    B, D = c.shape
    N = w_ada.shape[1]
    tn = 1024
    cp = jnp.zeros((8, D), F32).at[:B].set(c)
    out = pl.pallas_call(
        _ada_kernel,
        grid=(N // tn,),
        in_specs=[pl.BlockSpec((8, D), lambda j: (0, 0)),
                  pl.BlockSpec((D, tn), lambda j: (0, j)),
                  pl.BlockSpec((1, tn), lambda j: (0, j))],
        out_specs=pl.BlockSpec((8, tn), lambda j: (0, j)),
        out_shape=jax.ShapeDtypeStruct((8, N), F32),
        compiler_params=pltpu.CompilerParams(
            dimension_semantics=("arbitrary",),
            vmem_limit_bytes=_vmem_limit(2 * D * tn * 4 + (8 << 20))),
        name="ada",
    )(cp, w_ada, b_ada.reshape(1, N))
    return out[:B]


def _cast_pad_kernel(wt_ref, o_ref, *, n_valid):
    j = pl.program_id(0)
    tn = wt_ref.shape[0]
    row = j * tn + lax.broadcasted_iota(I32, wt_ref.shape, 0)
    wt = jnp.where(row < n_valid, wt_ref[...], 0.0)
    o_ref[...] = wt.T.astype(o_ref.dtype)


def _cast_pad(w_t, n_out):
    N, K = w_t.shape
    tn = 512
    return pl.pallas_call(
        functools.partial(_cast_pad_kernel, n_valid=N),
        grid=(pl.cdiv(n_out, tn),),
        in_specs=[pl.BlockSpec((tn, K), lambda j: (j, 0))],
        out_specs=pl.BlockSpec((K, tn), lambda j: (0, j)),
        out_shape=jax.ShapeDtypeStruct((K, n_out), BF16),
        compiler_params=pltpu.CompilerParams(dimension_semantics=("arbitrary",)),
        name="cast_pad",
    )(w_t)


def _rope(z, cos, sin):
    w = z.shape[-1]
    lane = lax.broadcasted_iota(I32, z.shape, 1)
    first_half = (lane % HEAD_DIM) < (HEAD_DIM // 2)
    partner = jnp.where(first_half, pltpu.roll(z, w - HEAD_DIM // 2, 1), pltpu.roll(z, HEAD_DIM // 2, 1))
    return z * cos + partner * sin


def _inproj_kernel(sink_ref, x_ref, g_ref, sh_ref, sc_ref, w_ref, mu_ref, cos_ref, sin_ref,
                   oa_ref, zr_ref, carry_ref, q_s, kv_s, *, tiles_per_seq):
    i = pl.program_id(0)
    tm = x_ref.shape[0]
    first = (i % tiles_per_seq) == 0
    x = x_ref[...]
    ms = jnp.mean(x * x, axis=-1, keepdims=True)
    y = x * lax.rsqrt(ms + NORM_EPS) * g_ref[...]
    h = (y * (1.0 + sc_ref[0]) + sh_ref[0]).astype(BF16)

    @pl.when(first)
    def _():
        kv_s[0:WINDOW, :] = jnp.zeros((WINDOW, 2 * KV_COLS), kv_s.dtype)

    cos2 = jnp.concatenate([cos_ref[...], cos_ref[...]], axis=1)
    sin2 = jnp.concatenate([sin_ref[...], sin_ref[...]], axis=1)
    for c0 in range(0, ATTN_COLS, 256):
        z = _dot(h, w_ref[:, c0:c0 + 256])
        if c0 < Q_COLS:
            q_s[:, c0:c0 + 256] = _rope(z, cos2, sin2).astype(q_s.dtype)
        else:
            z = jnp.concatenate([_rope(z[:, :KV_COLS], cos_ref[...], sin_ref[...]), z[:, KV_COLS:]], axis=1)
            kv_s[WINDOW:WINDOW + tm, :] = z.astype(kv_s.dtype)

    row = lax.broadcasted_iota(I32, (WINDOW, WINDOW), 0)
    col = lax.broadcasted_iota(I32, (WINDOW, WINDOW), 1)
    scale = 1.0 / math.sqrt(HEAD_DIM)

    def attend(j, pair):
        rows = slice(j * WINDOW, (j + 1) * WINDOW)
        keys = slice(j * WINDOW, (j + 2) * WINDOW)
        has_prev = jnp.logical_or(j > 0, jnp.logical_not(first))
        mask = jnp.concatenate([(col > row) & has_prev, col <= row], axis=1)
        outs = []
        for hd in (2 * pair, 2 * pair + 1):
            kvh = hd // ATTN_GROUP
            kmat = kv_s[keys, kvh * HEAD_DIM:(kvh + 1) * HEAD_DIM]
            vmat = kv_s[keys, KV_COLS + kvh * HEAD_DIM:KV_COLS + (kvh + 1) * HEAD_DIM]
            qh = q_s[rows, hd * HEAD_DIM:(hd + 1) * HEAD_DIM]
            s = _dot_nt(qh, kmat) * scale
            s = jnp.where(mask, s, NEG_BIG)
            sink = sink_ref[hd]
            m = jnp.maximum(jnp.max(s, axis=-1, keepdims=True), sink)
            p = jnp.exp(s - m)
            denom = jnp.sum(p, axis=-1, keepdims=True) + jnp.exp(sink - m)
            outs.append(_dot(p.astype(BF16), vmat) / denom)
        oa_ref[rows, 2 * pair * HEAD_DIM:(2 * pair + 2) * HEAD_DIM] = (
            jnp.concatenate(outs, axis=1).astype(oa_ref.dtype))

    pending = [functools.partial(attend, j, pair) for j in range(tm // WINDOW)
               for pair in range(ATTN_HEADS // 2)]

    rowt = lax.broadcasted_iota(I32, (tm, 1), 0)
    chunks = []
    c0 = 0
    while c0 < RWKV_PAD:
        n = min(512, RWKV_PAD - c0)
        chunks.append((c0, n))
        c0 += n
    per_chunk = -(-len(pending) // len(chunks))
    for c0, n in chunks:
        z = _dot(h, w_ref[:, ATTN_COLS + c0:ATTN_COLS + c0 + n])
        prev_last = jnp.where(first, 0.0, carry_ref[7:8, c0:c0 + n])
        z_prev = jnp.where(rowt == 0, prev_last, pltpu.roll(z, 1, 0))
        carry_ref[:, c0:c0 + n] = z[tm - 8:tm, :]
        zr_ref[:, c0:c0 + n] = z + (z_prev - z) * mu_ref[:, c0:c0 + n]
        for _ in range(per_chunk):
            if pending:
                pending.pop(0)()
    while pending:
        pending.pop(0)()
    kv_s[0:WINDOW, :] = kv_s[tm:tm + WINDOW, :]


def _inproj(x2, g, sh, sc, w_bf, mu_pad, cos_t, sin_t, sinks, T):
    M, D = x2.shape
    tm = 256
    tps = T // tm
    NW = w_bf.shape[1]
    kern = functools.partial(_inproj_kernel, tiles_per_seq=tps)
    vm = (D * NW * 2 + 2 * tm * D * 4 + 2 * tm * Q_COLS * 2 + 2 * tm * RWKV_PAD * 4
          + 4 * tm * 512 * 4 + tm * D * 8 + (12 << 20))
    gs = pltpu.PrefetchScalarGridSpec(
        num_scalar_prefetch=1,
        grid=(M // tm,),
        in_specs=[pl.BlockSpec((tm, D), lambda i, s: (i, 0)),
                  pl.BlockSpec((1, D), lambda i, s: (0, 0)),
                  pl.BlockSpec((1, 1, D), lambda i, s: (i // tps, 0, 0)),
                  pl.BlockSpec((1, 1, D), lambda i, s: (i // tps, 0, 0)),
                  pl.BlockSpec((D, NW), lambda i, s: (0, 0), pipeline_mode=pl.Buffered(1)),
                  pl.BlockSpec((1, RWKV_PAD), lambda i, s: (0, 0)),
                  pl.BlockSpec((tm, 2 * HEAD_DIM), lambda i, s: (i % tps, 0)),
                  pl.BlockSpec((tm, 2 * HEAD_DIM), lambda i, s: (i % tps, 0))],
        out_specs=[pl.BlockSpec((tm, Q_COLS), lambda i, s: (i, 0)),
                   pl.BlockSpec((tm, RWKV_PAD), lambda i, s: (i, 0))],
        scratch_shapes=[pltpu.VMEM((8, RWKV_PAD), F32),
                        pltpu.VMEM((tm, Q_COLS), BF16),
                        pltpu.VMEM((WINDOW + tm, 2 * KV_COLS), BF16)],
    )
    return pl.pallas_call(
        kern,
        grid_spec=gs,
        out_shape=[jax.ShapeDtypeStruct((M, Q_COLS), BF16),
                   jax.ShapeDtypeStruct((M, RWKV_PAD), F32)],
        compiler_params=pltpu.CompilerParams(
            dimension_semantics=("arbitrary",), vmem_limit_bytes=_vmem_limit(vm)),
        name="inproj",
    )(sinks, x2, g, sh, sc, w_bf, mu_pad, cos_t, sin_t)


def _rwkv_kernel(r_ref, k_ref, v_ref, lora_ref, w0_ref, wdu_ref, a0_ref, wau_ref, wgu_ref,
                 kk_ref, ka_ref, rk_ref, lnw_ref, lnb_ref, o_ref, s_ref):
    c = pl.program_id(0)
    nseq = r_ref.shape[0]
    C = r_ref.shape[1]
    N = RWKV_N
    G = RWKV_GROUP
    GW = G * N
    ng = RWKV_HEADS // G

    @pl.when(c == 0)
    def _():
        s_ref[...] = jnp.zeros_like(s_ref)

    ti = lax.broadcasted_iota(I32, (C, C), 0)
    si = lax.broadcasted_iota(I32, (C, C), 1)
    tril = jnp.where(si <= ti, 1.0, 0.0).astype(BF16)
    a_seq, g_seq, e_in_seq, e_ex_seq, e_neg_seq = [], [], [], [], []
    for b in range(nseq):
        lora = lora_ref[b]
        wd = lora[:, 0:DECAY_LORA]
        ad = lora[:, DECAY_LORA:DECAY_LORA + A_LORA]
        gd = lora[:, DECAY_LORA + A_LORA:LORA_COLS]
        wlin = w0_ref[...] + _dot(jnp.tanh(wd), wdu_ref[...])
        neg = -wlin
        softplus = jnp.maximum(neg, 0.0) + jnp.log(1.0 + jnp.exp(-jnp.abs(neg)))
        w = -softplus - 0.5
        logdec = -jnp.exp(w)
        a_seq.append(1.0 / (1.0 + jnp.exp(-(a0_ref[...] + _dot(ad, wau_ref[...])))))
        g_seq.append(_dot(1.0 / (1.0 + jnp.exp(-gd)), wgu_ref[...]))
        ld_hi = logdec.astype(BF16)
        ld_lo = (logdec - ld_hi.astype(F32)).astype(BF16)
        cum = _dot(tril, ld_hi) + _dot(tril, ld_lo)
        e_in_seq.append(jnp.exp(cum))
        e_ex_seq.append(jnp.exp(cum - logdec))
        e_neg_seq.append(jnp.exp(-cum))

    lane_head = lax.broadcasted_iota(I32, (1, GW), 1) // N
    t_row = lax.broadcasted_iota(I32, (C, GW), 0)
    s_lane = lax.broadcasted_iota(I32, (C, GW), 1) % N
    strict = s_lane < t_row
    incl = s_lane <= t_row
    bi = lax.broadcasted_iota(I32, (GW, GW), 0)
    bj = lax.broadcasted_iota(I32, (GW, GW), 1)
    same_head = (bi // N) == (bj // N)
    ones_bd = jnp.where(same_head, 1.0, 0.0).astype(BF16)
    eye = jnp.where(s_lane == t_row, 1.0, 0.0)

    def expand(xc):
        return jnp.concatenate([jnp.where(lane_head == h, xc, 0.0) for h in range(G)], axis=0).astype(BF16)

    def head_sum(xs):
        s = _dot(jnp.concatenate(xs, axis=0).astype(BF16), ones_bd)
        return [s[i * C:(i + 1) * C] for i in range(len(xs))]

    seq = [b for b in range(nseq) for _ in range(ng)]
    col = [slice(gi * GW, (gi + 1) * GW) for _ in range(nseq) for gi in range(ng)]
    rng = range(nseq * ng)
    r_l = [r_ref[seq[i], :, col[i]] for i in rng]
    k_l = [k_ref[seq[i], :, col[i]] for i in rng]
    v_l = [v_ref[seq[i], :, col[i]] for i in rng]
    a_l = [a_seq[seq[i]][:, col[i]] for i in rng]
    e_in = [e_in_seq[seq[i]][:, col[i]] for i in rng]
    e_neg = [e_neg_seq[seq[i]][:, col[i]] for i in rng]
    kk0 = [k_l[i] * kk_ref[:, col[i]] for i in rng]
    nrm2 = head_sum([kk0[i] * kk0[i] for i in rng])
    kk = [kk0[i] / jnp.maximum(jnp.sqrt(nrm2[i]), 1e-12) for i in rng]
    k2 = [k_l[i] * (1.0 + (a_l[i] - 1.0) * ka_ref[:, col[i]]) for i in rng]
    bt = [kk[i] * a_l[i] * e_neg[i] for i in rng]
    kt = [k2[i] * e_neg[i] for i in rng]
    left = [jnp.concatenate([-kk[i] * e_ex_seq[seq[i]][:, col[i]], r_l[i] * e_in[i]], axis=0).astype(BF16)
            for i in rng]
    right = [jnp.concatenate([bt[i], kt[i]], axis=0).astype(BF16) for i in rng]
    zed = [jnp.concatenate([expand(bt[i]), expand(kt[i])], axis=0) for i in rng]
    big = [_dot_nt(left[i], zed[i]) for i in rng]
    s_old = [s_ref[i] for i in rng]
    ls = [_dot_nt(left[i], s_old[i].astype(BF16)) for i in rng]
    v_bd = [expand(v_l[i]) for i in rng]
    pw = [jnp.where(strict, big[i][:C, :G * C], 0.0) for i in rng]
    tinv = [eye + pw[i] for i in rng]
    pw = [_dot(pw[i].astype(BF16), expand(pw[i])) for i in rng]
    a_kk = [jnp.concatenate([jnp.where(strict, big[i][:C, G * C:], 0.0),
                             jnp.where(incl, big[i][C:, G * C:], 0.0)], axis=0).astype(BF16) for i in rng]
    kv = [_dot(a_kk[i], v_bd[i]) for i in rng]
    x = [ls[i][:C] + kv[i][:C] for i in rng]
    span = 2
    while span < C:
        pw_bd = [expand(pw[i]) for i in rng]
        if 2 * span < C:
            both = [_dot(jnp.concatenate([pw[i], tinv[i]], axis=0).astype(BF16), pw_bd[i]) for i in rng]
            pw = [both[i][:C] for i in rng]
            tinv = [tinv[i] + both[i][C:] for i in rng]
        else:
            tinv = [tinv[i] + _dot(tinv[i].astype(BF16), pw_bd[i]) for i in rng]
        span *= 2
    u = [_dot(tinv[i].astype(BF16), expand(x[i])) for i in rng]
    y = [ls[i][C:] + kv[i][C:]
         + _dot(jnp.where(incl, big[i][C:, :G * C], 0.0).astype(BF16), expand(u[i])) for i in rng]
    uv = [jnp.concatenate([u[i], v_l[i]], axis=0).astype(BF16) for i in rng]
    for i in rng:
        s_ref[i] = ((s_old[i] + jnp.where(same_head, _dot_tn(uv[i], right[i]), 0.0))
                    * e_in[i][C - 1:C, :])
    ysum = head_sum(y)
    yc = [y[i] - ysum[i] * (1.0 / N) for i in rng]
    ysq = head_sum([yc[i] * yc[i] for i in rng])
    var = [ysq[i] * (1.0 / N) for i in rng]
    rksum = head_sum([r_l[i] * k2[i] * rk_ref[:, col[i]] for i in rng])
    bonus = [rksum[i] * v_l[i] for i in rng]
    for i in rng:
        yn = yc[i] * lax.rsqrt(var[i] + RWKV_LN_EPS) * lnw_ref[:, col[i]] + lnb_ref[:, col[i]]
        o_ref[seq[i], :, col[i]] = ((yn + bonus[i]) * g_seq[seq[i]][:, col[i]]).astype(o_ref.dtype)


def _rwkv(zr3, w0, wdu, a0, wau, wgu, k_k, k_a, r_k, ln_w, ln_b):
    B, T, _ = zr3.shape
    C = CHUNK
    W = RWKV_W
    GW = RWKV_GROUP * RWKV_N
    vec = lambda v: v.reshape(1, W)
    full = lambda shape: pl.BlockSpec(shape, lambda c: (0,) * len(shape))
    return pl.pallas_call(
        _rwkv_kernel,
        grid=(T // C,),
        in_specs=[pl.BlockSpec((B, C, W), lambda c: (0, c, 0)),
                  pl.BlockSpec((B, C, W), lambda c: (0, c, 1)),
                  pl.BlockSpec((B, C, W), lambda c: (0, c, 2)),
                  pl.BlockSpec((B, C, LORA_PAD), lambda c: (0, c, 3 * W // LORA_PAD)),
                  full((1, W)), full((DECAY_LORA, W)), full((1, W)), full((A_LORA, W)),
                  full((GATE_LORA, W)), full((1, W)), full((1, W)), full((1, W)),
                  full((1, W)), full((1, W))],
        out_specs=pl.BlockSpec((B, C, W), lambda c: (0, c, 0)),
        out_shape=jax.ShapeDtypeStruct((B, T, W), BF16),
        scratch_shapes=[pltpu.VMEM((B * RWKV_HEADS // RWKV_GROUP, GW, GW), F32)],
        compiler_params=pltpu.CompilerParams(dimension_semantics=("arbitrary",)),
        name="rwkv",
    )(zr3, zr3, zr3, zr3, vec(w0), wdu, vec(a0), wau, wgu, vec(k_k), vec(k_a), vec(r_k),
      vec(ln_w), vec(ln_b))


def _outproj_kernel(oa_ref, or_ref, x_ref, wo_ref, gt_ref, g_ref, sh_ref, sc_ref,
                    wrb_ref, wrh_ref, br_ref, x1_ref, h2_ref, lg_ref):
    mixed = _dot(oa_ref[...], wo_ref[0:Q_COLS, :]) + _dot(or_ref[...], wo_ref[Q_COLS:, :])
    x1 = x_ref[...] + gt_ref[0] * mixed
    x1_ref[...] = x1
    ms = jnp.mean(x1 * x1, axis=-1, keepdims=True)
    h2 = x1 * lax.rsqrt(ms + NORM_EPS) * g_ref[...] * (1.0 + sc_ref[0]) + sh_ref[0]
    half = h2.shape[1] // 2
    h2p = _bf16_pair_pack(h2[:, :half], h2[:, half:])
    for j in range(ROW_SLABS):
        w = h2_ref.shape[2]
        h2_ref[:, j, :] = h2p[:, j * w:(j + 1) * w]
    hh = h2.astype(BF16)
    hl = (h2 - hh.astype(F32)).astype(BF16)
    both = _dot(hh, wrb_ref[...])
    lg_ref[...] = both[:, :LANES] + both[:, LANES:] + _dot(hl, wrh_ref[...]) + br_ref[...]


def _outproj(oa2, or2, x2, wo_bf, gt, g2, sh, sc, wr_both, wr_hi, br, T):
    M, D = x2.shape
    tm = 256
    tps = T // tm
    bvec = pl.BlockSpec((1, 1, D), lambda i: (i // tps, 0, 0))
    vm = D * D * 2 + 2 * tm * D * (2 + 4 + 4 + 4) + tm * D * 16 + (8 << 20)
    return pl.pallas_call(
        _outproj_kernel,
        grid=(M // tm,),
        in_specs=[pl.BlockSpec((tm, Q_COLS), lambda i: (i, 0)),
                  pl.BlockSpec((tm, RWKV_W), lambda i: (i, 0)),
                  pl.BlockSpec((tm, D), lambda i: (i, 0)),
                  pl.BlockSpec((D, D), lambda i: (0, 0), pipeline_mode=pl.Buffered(1)),
                  bvec,
                  pl.BlockSpec((1, D), lambda i: (0, 0)),
                  bvec, bvec,
                  pl.BlockSpec((D, 2 * LANES), lambda i: (0, 0)),
                  pl.BlockSpec((D, LANES), lambda i: (0, 0)),
                  pl.BlockSpec((1, LANES), lambda i: (0, 0))],
        out_specs=[pl.BlockSpec((tm, D), lambda i: (i, 0)),
                   pl.BlockSpec((tm, ROW_SLABS, D // (2 * ROW_SLABS)), lambda i: (i, 0, 0)),
                   pl.BlockSpec((tm, LANES), lambda i: (i, 0))],
        out_shape=[jax.ShapeDtypeStruct((M, D), F32),
                   jax.ShapeDtypeStruct((M, ROW_SLABS, D // (2 * ROW_SLABS)), jnp.uint32),
                   jax.ShapeDtypeStruct((M, LANES), F32)],
        compiler_params=pltpu.CompilerParams(
            dimension_semantics=("arbitrary",), vmem_limit_bytes=_vmem_limit(vm)),
        name="outproj",
    )(oa2, or2, x2, wo_bf, gt, g2, sh, sc, wr_both, wr_hi, br)


def _route_kernel(lg_ref, info_ref, cnt_ref, run_ref):
    i = pl.program_id(0)
    tm = lg_ref.shape[0]

    @pl.when(i == 0)
    def _():
        run_ref[...] = jnp.zeros_like(run_ref)

    lg = lg_ref[...]
    lane = lax.broadcasted_iota(I32, lg.shape, 1)
    gl = jnp.where(lane < N_GROUPS, lg, NEG_BIG)
    gmax = jnp.max(gl, axis=-1, keepdims=True)
    gsum = jnp.sum(jnp.exp(gl - gmax), axis=-1, keepdims=True)
    g_gate = 1.0 / gsum
    g_idx = jnp.min(jnp.where(gl == gmax, lane, LANES), axis=-1, keepdims=True)
    lo = N_GROUPS + EXPERTS_PER_GROUP * g_idx
    el = jnp.where((lane >= lo) & (lane < lo + EXPERTS_PER_GROUP), lg, NEG_BIG)
    e1max = jnp.max(el, axis=-1, keepdims=True)
    l1 = jnp.min(jnp.where(el == e1max, lane, LANES), axis=-1, keepdims=True)
    el2 = jnp.where(lane == l1, NEG_BIG, el)
    e2max = jnp.max(el2, axis=-1, keepdims=True)
    l2 = jnp.min(jnp.where(el2 == e2max, lane, LANES), axis=-1, keepdims=True)
    t2 = jnp.exp(e2max - e1max)
    w1 = g_gate / (1.0 + t2)
    w2 = g_gate * t2 / (1.0 + t2)
    ex1 = l1 - N_GROUPS
    ex2 = l2 - N_GROUPS

    oh1 = jnp.where(lane == ex1, 1.0, 0.0)
    oh2 = jnp.where(lane == ex2, 1.0, 0.0)
    ti = lax.broadcasted_iota(I32, (tm, tm), 0)
    si = lax.broadcasted_iota(I32, (tm, tm), 1)
    lower = jnp.where(si < ti, 1.0, 0.0).astype(BF16)
    pre1 = _dot(lower, oh1.astype(BF16))
    pre2 = _dot(lower, oh2.astype(BF16))
    cnt1 = jnp.sum(oh1, axis=0, keepdims=True)
    cnt2 = jnp.sum(oh2, axis=0, keepdims=True)
    run = run_ref[...]
    rank1 = jnp.sum(oh1 * (pre1 + run), axis=-1, keepdims=True)
    rank2 = jnp.sum(oh2 * (pre2 + run + cnt1), axis=-1, keepdims=True)
    run = run + cnt1 + cnt2
    run_ref[...] = run
    cnt_ref[...] = run

    info = jnp.where(lane == 0, ex1.astype(F32), 0.0)
    info = jnp.where(lane == 1, ex2.astype(F32), info)
    info = jnp.where(lane == 2, rank1, info)
    info = jnp.where(lane == 3, rank2, info)
    info = jnp.where(lane == 4, w1, info)
    info = jnp.where(lane == 5, w2, info)
    info_ref[...] = info


def _route(lg):
    M = lg.shape[0]
    tm = 512
    return pl.pallas_call(
        _route_kernel,
        grid=(M // tm,),
        in_specs=[pl.BlockSpec((tm, LANES), lambda i: (i, 0))],
        out_specs=[pl.BlockSpec((tm, LANES), lambda i: (i, 0)),
                   pl.BlockSpec((1, LANES), lambda i: (0, 0))],
        out_shape=[jax.ShapeDtypeStruct((M, LANES), F32),
                   jax.ShapeDtypeStruct((1, LANES), F32)],
        scratch_shapes=[pltpu.VMEM((1, LANES), F32)],
        compiler_params=pltpu.CompilerParams(dimension_semantics=("arbitrary",)),
        name="route",
    )(lg)


def _plan_kernel(info_ref, cnt_ref, slot_ref, meta_ref):
    cnt = cnt_ref[...]
    lane_r = lax.broadcasted_iota(I32, (1, LANES), 1)
    nblk = jnp.floor((cnt + (MOE_BLOCK - 1)) * (1.0 / MOE_BLOCK))
    ei = lax.broadcasted_iota(I32, (LANES, LANES), 0)
    ej = lax.broadcasted_iota(I32, (LANES, LANES), 1)
    upper = jnp.where(ei <= ej, 1.0, 0.0).astype(BF16)
    nb8 = jnp.broadcast_to(nblk, (8, LANES)).astype(BF16)
    bend = _dot(nb8, upper)[0:1, :]
    bstart = bend - nblk
    pstart = bstart * MOE_BLOCK

    info = info_ref[...]
    lane = lax.broadcasted_iota(I32, info.shape, 1)
    ex1 = info[:, 0:1].astype(I32)
    ex2 = info[:, 1:2].astype(I32)
    s1 = jnp.sum(jnp.where(lane == ex1, pstart, 0.0), axis=-1, keepdims=True) + info[:, 2:3]
    s2 = jnp.sum(jnp.where(lane == ex2, pstart, 0.0), axis=-1, keepdims=True) + info[:, 3:4]
    slot = jnp.where(lane == 0, s1, 0.0)
    slot = jnp.where(lane == 1, s2, slot)
    slot_ref[...] = slot.astype(I32)

    blk = lax.broadcasted_iota(I32, (LANES, 2 * LANES), 1).astype(F32)
    bend_col = jnp.sum(jnp.where(ei == ej, jnp.broadcast_to(bend, (LANES, LANES)), 0.0),
                       axis=-1, keepdims=True)
    erow = lax.broadcasted_iota(I32, (LANES, 2 * LANES), 0)
    hit = jnp.where((bend_col <= blk) & (erow < N_EXPERTS), 1.0, 0.0)
    bexp = jnp.minimum(jnp.sum(hit, axis=0, keepdims=True), N_EXPERTS - 1.0)
    n_used = jnp.max(jnp.where(lane_r < N_EXPERTS, bend, 0.0), axis=-1, keepdims=True)
    lastblk = jnp.where(nblk > 0, bend - 1.0, -1.0)
    r8 = lax.broadcasted_iota(I32, (8, 2 * LANES), 0)
    last2 = jnp.concatenate([lastblk, jnp.full((1, LANES), -1.0)], axis=1)
    meta = jnp.where(r8 == 0, jnp.broadcast_to(bexp, (8, 2 * LANES)), 0.0)
    meta = jnp.where(r8 == 1, jnp.broadcast_to(n_used, (8, 2 * LANES)), meta)
    meta = jnp.where(r8 == 2, jnp.broadcast_to(last2, (8, 2 * LANES)), meta)
    meta_ref[...] = meta.astype(I32)


def _plan(info, cnt):
    M = info.shape[0]
    tm = 1024
    return pl.pallas_call(
        _plan_kernel,
        grid=(M // tm,),
        in_specs=[pl.BlockSpec((tm, LANES), lambda i: (i, 0)),
                  pl.BlockSpec((1, LANES), lambda i: (0, 0))],
        out_specs=[pl.BlockSpec((tm, LANES), lambda i: (i, 0)),
                   pl.BlockSpec((8, 2 * LANES), lambda i: (0, 0))],
        out_shape=[jax.ShapeDtypeStruct((M, LANES), I32),
                   jax.ShapeDtypeStruct((8, 2 * LANES), I32)],
        compiler_params=pltpu.CompilerParams(dimension_semantics=("arbitrary",)),
        name="plan",
    )(info, cnt)


ROW_DEPTH = 3
DMA_GROUP = 16
WEIGHT_DEPTH = 3


def _slot_table_kernel(slots_ref, pad_ref, inv_ref, buf_ref, sem, *, n_assign):
    fill = pltpu.make_async_copy(pad_ref, buf_ref, sem)
    fill.start()
    fill.wait()

    def put(a, c):
        buf_ref[slots_ref[a] + MOE_BLOCK] = a
        return c

    lax.fori_loop(0, n_assign, put, 0, unroll=16)
    out = pltpu.make_async_copy(buf_ref, inv_ref, sem)
    out.start()
    out.wait()


def _slot_table(slots_km, n_blocks):
    n_assign = slots_km.shape[0]
    n_slots = (n_blocks + ROW_DEPTH) * MOE_BLOCK
    row = np.arange(n_slots)
    pad_ids = n_assign + ((row // MOE_BLOCK + ROW_DEPTH - 1) % ROW_DEPTH) * MOE_BLOCK + row % MOE_BLOCK
    gs = pltpu.PrefetchScalarGridSpec(
        num_scalar_prefetch=1, grid=(1,),
        in_specs=[pl.BlockSpec(memory_space=pl.ANY)],
        out_specs=pl.BlockSpec(memory_space=pl.ANY),
        scratch_shapes=[pltpu.SMEM((n_slots,), I32), pltpu.SemaphoreType.DMA(())])
    return pl.pallas_call(
        functools.partial(_slot_table_kernel, n_assign=n_assign),
        grid_spec=gs,
        out_shape=jax.ShapeDtypeStruct((n_slots,), I32),
        compiler_params=pltpu.CompilerParams(dimension_semantics=("arbitrary",)),
        name="slot_table",
    )(slots_km, jnp.asarray(pad_ids, I32))


def _experts_kernel(bexp_ref, seg_ref, inv_ref, h2_ref, w1_ref, w3_ref, w2_ref, y2_ref,
                    xbuf, ybuf, w1buf, w3buf, w2buf, sem_x, sem_y, sem_w, par_ref, *, n_assign):
    i = pl.program_id(0)
    n_used = seg_ref[N_EXPERTS]
    n_tok = n_assign // 2
    B = MOE_BLOCK
    FF = w1buf.shape[2]
    D = w1buf.shape[1]
    ND = ROW_DEPTH
    NW = w1buf.shape[0]

    def src_row(a):
        if n_tok & (n_tok - 1) == 0:
            return a & (n_tok - 1)
        return lax.rem(a, n_tok)

    def weight_copies(e, slot):
        return (pltpu.make_async_copy(w1_ref.at[e], w1buf.at[slot], sem_w.at[slot]),
                pltpu.make_async_copy(w3_ref.at[e], w3buf.at[slot], sem_w.at[slot]),
                pltpu.make_async_copy(w2_ref.at[e], w2buf.at[slot], sem_w.at[slot]))

    def start_weights(e, slot):
        c1, c3, c2 = weight_copies(e, slot)
        c1.start(priority=1)
        c3.start(priority=1)
        c2.start(priority=0)

    def prefetch_expert_after(e, hops, slot):
        ok = hops >= 0
        cur = e
        for _ in range(hops):
            nb = seg_ref[cur] + 1
            ok = jnp.logical_and(ok, nb < n_used)
            cur = bexp_ref[jnp.where(ok, nb, 0)]

        @pl.when(ok)
        def _():
            start_weights(cur, slot)

    def gather_group(b, g):
        slot = lax.rem(b + ND, ND)
        for r in range(g * DMA_GROUP, (g + 1) * DMA_GROUP):
            a = inv_ref[(b + 1) * B + r]
            pltpu.make_async_copy(h2_ref.at[pl.ds(src_row(a), 1)], xbuf.at[slot, pl.ds(r, 1)],
                                  sem_x.at[slot]).start()

    def scatter_group(b, g):
        slot = lax.rem(b + ND, ND)
        for r in range(g * DMA_GROUP, (g + 1) * DMA_GROUP):
            a = inv_ref[(b + 1) * B + r]
            pltpu.make_async_copy(ybuf.at[slot, pl.ds(r, 1)], y2_ref.at[pl.ds(a, 1)],
                                  sem_y.at[slot]).start()

    def wait_block(buf, sem, b):
        slot = lax.rem(b + ND, ND)
        pltpu.make_async_copy(buf.at[slot], buf.at[slot], sem.at[slot]).wait()

    n_groups = B // DMA_GROUP

    @pl.when(i == 0)
    def _():
        ybuf[...] = jnp.zeros_like(ybuf)
        for s in range(ND - 1):
            pltpu.make_async_copy(ybuf.at[s], y2_ref.at[pl.ds(n_assign + s * B, B)], sem_y.at[s]).start()
        par_ref[0] = 0
        e0 = bexp_ref[0]
        start_weights(e0, 0)
        for hops in range(1, NW):
            prefetch_expert_after(e0, hops, hops)
        for b in range(ND - 1):
            for g in range(n_groups):
                gather_group(b, g)

    active = i < n_used

    @pl.when(active)
    def _():
        e = bexp_ref[i]
        first = jnp.logical_or(i == 0, e != bexp_ref[jnp.maximum(i - 1, 0)])

        @pl.when(jnp.logical_and(first, i > 0))
        def _():
            par_ref[0] = par_ref[0] + 1

        @pl.when(first)
        def _():
            q = par_ref[0]
            for cp in weight_copies(e, lax.rem(q, NW)):
                cp.wait()

            @pl.when(i > 0)
            def _():
                prefetch_expert_after(e, NW - 1, lax.rem(q + NW - 1, NW))

        p = lax.rem(par_ref[0], NW)
        slot = lax.rem(i, ND)
        wait_block(xbuf, sem_x, i)
        wait_block(ybuf, sem_y, i)
        pending = [functools.partial(scatter_group, i - 1, g) for g in range(n_groups)]
        pending += [functools.partial(gather_group, i + ND - 1, g) for g in range(n_groups)]
        n_chunks = 2 * (FF // 256) + ROW_SLABS
        for _ in range(max(len(pending) - n_chunks, 0)):
            pending.pop(0)()

        def after_chunk():
            if pending:
                pending.pop(0)()

        xa, xb = _bf16_pair_unpack(jnp.concatenate([xbuf[slot, :, j, :] for j in range(ROW_SLABS)], axis=1))
        x = jnp.concatenate([xa, xb], axis=1).astype(BF16)
        sw = ybuf.shape[3]
        h1c, h3c = [], []
        for c0 in range(0, FF, 256):
            h1c.append(_dot(x, w1buf[p, :, c0:c0 + 256]))
            after_chunk()
            h3c.append(_dot(x, w3buf[p, :, c0:c0 + 256]))
            after_chunk()
        h1 = jnp.concatenate(h1c, axis=1)
        h3 = jnp.concatenate(h3c, axis=1)
        hid = (h1 * (1.0 / (1.0 + jnp.exp(-h1))) * h3).astype(BF16)
        for j in range(0, ROW_SLABS, 2):
            c0 = j * sw
            ya = _dot(hid, w2buf[p, :, c0:c0 + 2 * sw])
            after_chunk()
            yb = _dot(hid, w2buf[p, :, D // 2 + c0:D // 2 + c0 + 2 * sw])
            yp = _bf16_pair_pack(ya, yb)
            ybuf[slot, :, j, :] = yp[:, :sw]
            ybuf[slot, :, j + 1, :] = yp[:, sw:]
            after_chunk()
        while pending:
            pending.pop(0)()

    @pl.when(i == n_used)
    def _():
        wait_block(xbuf, sem_x, i)
        wait_block(ybuf, sem_y, i)
        for g in range(n_groups):
            scatter_group(i - 1, g)

    @pl.when(jnp.logical_and(i > n_used, i < n_used + ND - 1))
    def _():
        wait_block(xbuf, sem_x, i)
        wait_block(ybuf, sem_y, i)

    @pl.when(i == n_used + ND - 1)
    def _():
        wait_block(ybuf, sem_y, i)


def _experts(bexp, seg_meta, inv, h2, w1, w3, w2, n_blocks):
    M, DJ, SW = h2.shape
    D = 2 * DJ * SW
    FF = w1.shape[2]
    n_assign = 2 * M
    any_spec = pl.BlockSpec(memory_space=pl.ANY)
    row_buf = pltpu.VMEM((ROW_DEPTH, MOE_BLOCK, DJ, SW), jnp.uint32)
    gs = pltpu.PrefetchScalarGridSpec(
        num_scalar_prefetch=3,
        grid=(n_blocks + ROW_DEPTH,),
        in_specs=[any_spec, any_spec, any_spec, any_spec],
        out_specs=any_spec,
        scratch_shapes=[row_buf, row_buf,
                        pltpu.VMEM((WEIGHT_DEPTH, D, FF), F32), pltpu.VMEM((WEIGHT_DEPTH, D, FF), F32),
                        pltpu.VMEM((WEIGHT_DEPTH, FF, D), F32),
                        pltpu.SemaphoreType.DMA((ROW_DEPTH,)), pltpu.SemaphoreType.DMA((ROW_DEPTH,)),
                        pltpu.SemaphoreType.DMA((WEIGHT_DEPTH,)), pltpu.SMEM((1,), I32)],
    )
    vm = WEIGHT_DEPTH * 3 * D * FF * 4 + 2 * ROW_DEPTH * MOE_BLOCK * D * 4 + (12 << 20)
    return pl.pallas_call(
        functools.partial(_experts_kernel, n_assign=n_assign),
        grid_spec=gs,
        out_shape=jax.ShapeDtypeStruct((n_assign + ROW_DEPTH * MOE_BLOCK, DJ, SW), jnp.uint32),
        compiler_params=pltpu.CompilerParams(
            dimension_semantics=("arbitrary",), vmem_limit_bytes=_vmem_limit(vm),
            has_side_effects=True),
        name="experts",
    )(bexp, seg_meta, inv, h2, w1, w3, w2)


def _combine_kernel(x1_ref, info_ref, gt_ref, fg_ref, ya_ref, yb_ref, o_ref):
    info = info_ref[...]
    nj = ya_ref.shape[1]
    ya = jnp.concatenate(_bf16_pair_unpack(jnp.concatenate([ya_ref[:, j, :] for j in range(nj)], axis=1)), axis=1)
    yb = jnp.concatenate(_bf16_pair_unpack(jnp.concatenate([yb_ref[:, j, :] for j in range(nj)], axis=1)), axis=1)
    moe = info[:, 4:5] * ya + info[:, 5:6] * yb
    x = x1_ref[...] + gt_ref[0] * moe
    ms = jnp.mean(x * x, axis=-1, keepdims=True)
    o_ref[...] = x * lax.rsqrt(ms + NORM_EPS) * fg_ref[...]


def _combine(x1, info, gt, final_g, y2, T):
    M, D = x1.shape
    tm = 256
    tps = T // tm
    nt = M // tm
    vm = 2 * 4 * tm * D * 4 + tm * D * 16 + (8 << 20)
    return pl.pallas_call(
        _combine_kernel,
        grid=(nt,),
        in_specs=[pl.BlockSpec((tm, D), lambda i: (i, 0)),
                  pl.BlockSpec((tm, LANES), lambda i: (i, 0)),
                  pl.BlockSpec((1, 1, D), lambda i: (i // tps, 0, 0)),
                  pl.BlockSpec((1, D), lambda i: (0, 0)),
                  pl.BlockSpec((tm, ROW_SLABS, D // (2 * ROW_SLABS)), lambda i: (i, 0, 0)),
                  pl.BlockSpec((tm, ROW_SLABS, D // (2 * ROW_SLABS)), lambda i: (nt + i, 0, 0))],
        out_specs=pl.BlockSpec((tm, D), lambda i: (i, 0)),
        out_shape=jax.ShapeDtypeStruct((M, D), F32),
        compiler_params=pltpu.CompilerParams(
            dimension_semantics=("arbitrary",), vmem_limit_bytes=_vmem_limit(vm)),
        name="combine",
    )(x1, info, gt, final_g.reshape(1, D), y2, y2)


def _rope_tables(T):
    inv_freq = ROPE_THETA ** (-np.arange(0, HEAD_DIM, 2, dtype=np.float64) / HEAD_DIM)
    ang = np.arange(T, dtype=np.float64)[:, None] * inv_freq[None, :]
    cos, sin = np.cos(ang), np.sin(ang)
    cos_h = np.concatenate([cos, cos], axis=1)
    sin_h = np.concatenate([-sin, sin], axis=1)
    return (jnp.asarray(np.tile(cos_h, (1, 2)), F32), jnp.asarray(np.tile(sin_h, (1, 2)), F32))


def _layer(x, c, w_ada, b_ada, norm1_g, w_in, mu_shift, sinks, w0, w_decay_up, a0, w_a_up, w_g_up,
           k_k, k_a, r_k, ln_x_w, ln_x_b, w_o, norm2_g, w_rg, b_rg, w_re, b_re, w1, w3, w2, final_g):
    B, T, D = x.shape
    M = B * T
    mod = _ada(c, w_ada, b_ada)
    sh1, sc1, gt1, sh2, sc2, gt2 = [m.reshape(B, 1, D) for m in jnp.split(mod, 6, axis=-1)]

    pad = RWKV_PAD - RWKV_COLS
    w_bf = _cast_pad(jnp.swapaxes(w_in, 0, 1), ATTN_COLS + RWKV_PAD)
    mu_pad = jnp.pad(mu_shift, (0, pad)).reshape(1, RWKV_PAD)
    cos_t, sin_t = _rope_tables(T)
    x2 = x.reshape(M, D)
    o_attn, zr = _inproj(x2, norm1_g.reshape(1, D), sh1, sc1, w_bf, mu_pad, cos_t, sin_t, sinks, T)
    o_rwkv = _rwkv(zr.reshape(B, T, RWKV_PAD), w0, w_decay_up, a0, w_a_up, w_g_up, k_k, k_a,
                   r_k.reshape(-1), ln_x_w, ln_x_b)

    wr = jnp.concatenate([w_rg, w_re], axis=1)
    wr = jnp.pad(wr, ((0, 0), (0, LANES - wr.shape[1])))
    wr_hi = wr.astype(BF16)
    wr_lo = (wr - wr_hi.astype(F32)).astype(BF16)
    br = jnp.pad(jnp.concatenate([b_rg, b_re]), (0, LANES - N_GROUPS - N_EXPERTS)).reshape(1, LANES)
    x1, h2, lg = _outproj(o_attn.reshape(M, Q_COLS), o_rwkv.reshape(M, RWKV_W), x2, w_o.astype(BF16),
                          gt1, norm2_g.reshape(1, D), sh2, sc2, jnp.concatenate([wr_hi, wr_lo], axis=1),
                          wr_hi, br, T)

    n_blocks = -(-(2 * M) // MOE_BLOCK) + N_EXPERTS
    info, cnt = _route(lg)
    slots, meta = _plan(info, cnt)
    slots_km = slots[:, :2].T.reshape(-1)
    seg_meta = jnp.concatenate([meta[2, :N_EXPERTS], meta[1, :1]])
    inv = _slot_table(slots_km, n_blocks)
    y2 = _experts(meta[0, :n_blocks + ROW_DEPTH], seg_meta, inv, h2, w1, w3, w2, n_blocks)
    out = _combine(x1, info, gt2, final_g, y2, T)
    return out.reshape(B, T, D)


def kernel(x, c, w_ada, b_ada, norm1_g, w_in, mu_shift, sinks, w0, w_decay_up, a0, w_a_up, w_g_up, k_k, k_a, r_k, ln_x_w, ln_x_b, w_o, norm2_g, w_router_group, b_router_group, w_router_expert, b_router_expert, w1, w3, w2, final_g):
    depth = w_ada.shape[0]
    assert depth == 1, "single-layer stack"
    l = 0
    return _layer(x, c, w_ada[l], b_ada[l], norm1_g[l], w_in[l], mu_shift[l], sinks[l], w0[l],
                  w_decay_up[l], a0[l], w_a_up[l], w_g_up[l], k_k[l], k_a[l], r_k[l], ln_x_w[l],
                  ln_x_b[l], w_o[l], norm2_g[l], w_router_group[l], b_router_group[l],
                  w_router_expert[l], b_router_expert[l], w1[l], w3[l], w2[l], final_g)
```

```python
import functools
import math

import jax
import jax.numpy as jnp
import numpy as np
from jax import lax
from jax.experimental import pallas as pl
from jax.experimental.pallas import tpu as pltpu

F32 = jnp.float32
BF16 = jnp.bfloat16
I32 = jnp.int32

LANES = 128
VMEM_BYTES_V7X = 64 * 1024 * 1024

HEAD_DIM = 64
ATTN_HEADS = 16
ATTN_KV_HEADS = 2
ATTN_GROUP = ATTN_HEADS // ATTN_KV_HEADS
WINDOW = 128
ROPE_THETA = 10000.0
RWKV_HEADS = 16
RWKV_N = 64
DECAY_LORA = 64
A_LORA = 64
GATE_LORA = 160
RWKV_LN_EPS = 64e-5
N_GROUPS = 8
EXPERTS_PER_GROUP = 8
N_EXPERTS = N_GROUPS * EXPERTS_PER_GROUP
EXPERT_FF = 512
MOE_BLOCK = 128
ROW_SLABS = 8
NORM_EPS = 1e-6

Q_COLS = ATTN_HEADS * HEAD_DIM
KV_COLS = ATTN_KV_HEADS * HEAD_DIM
ATTN_COLS = Q_COLS + 2 * KV_COLS
RWKV_W = RWKV_HEADS * RWKV_N
LORA_COLS = DECAY_LORA + A_LORA + GATE_LORA
LORA_PAD = 384
RWKV_COLS = 3 * RWKV_W + LORA_COLS
RWKV_PAD = 3 * RWKV_W + LORA_PAD
CHUNK = 64
RWKV_GROUP = 4
NEG_BIG = -1e30


def _vmem_limit(nbytes):
    return int(min(nbytes, VMEM_BYTES_V7X - 4 * 1024 * 1024))


def _dot(a, b):
    return jnp.dot(a, b, preferred_element_type=F32)


def _dot_nt(a, b):
    return lax.dot_general(a, b, (((1,), (1,)), ((), ())), preferred_element_type=F32)


def _dot_tn(a, b):
    return lax.dot_general(a, b, (((0,), (0,)), ((), ())), preferred_element_type=F32)


def _bf16_pair_pack(a, b):
    def hi16(x):
        u = pltpu.bitcast(x, jnp.uint32)
        r = u + jnp.uint32(0x7FFF) + ((u >> 16) & jnp.uint32(1))
        return jnp.where(x != x, u | jnp.uint32(0x00400000), r)
    return (hi16(a) & jnp.uint32(0xFFFF0000)) | (hi16(b) >> 16)


def _bf16_pair_unpack(w):
    return (pltpu.bitcast(w & jnp.uint32(0xFFFF0000), F32), pltpu.bitcast(w << 16, F32))


def _ada_kernel(c_ref, w_ref, b_ref, o_ref):
    c = c_ref[...]
    s = c * (1.0 / (1.0 + jnp.exp(-c)))
    o_ref[...] = _dot(s, w_ref[...]) + b_ref[...]


def _ada(c, w_ada, b_ada):
    B, D = c.shape
    N = w_ada.shape[1]
    tn = 1024
    cp = jnp.zeros((8, D), F32).at[:B].set(c)
    out = pl.pallas_call(
        _ada_kernel,
        grid=(N // tn,),
        in_specs=[pl.BlockSpec((8, D), lambda j: (0, 0)),
                  pl.BlockSpec((D, tn), lambda j: (0, j)),
                  pl.BlockSpec((1, tn), lambda j: (0, j))],
        out_specs=pl.BlockSpec((8, tn), lambda j: (0, j)),
        out_shape=jax.ShapeDtypeStruct((8, N), F32),
        compiler_params=pltpu.CompilerParams(
            dimension_semantics=("arbitrary",),
            vmem_limit_bytes=_vmem_limit(2 * D * tn * 4 + (8 << 20))),
        name="ada",
    )(cp, w_ada, b_ada.reshape(1, N))
    return out[:B]


def _cast_pad_kernel(wt_ref, o_ref, *, n_valid):
    j = pl.program_id(0)
    tn = wt_ref.shape[0]
    row = j * tn + lax.broadcasted_iota(I32, wt_ref.shape, 0)
    wt = jnp.where(row < n_valid, wt_ref[...], 0.0)
    o_ref[...] = wt.T.astype(o_ref.dtype)


def _cast_pad(w_t, n_out):
    N, K = w_t.shape
    tn = 512
    return pl.pallas_call(
        functools.partial(_cast_pad_kernel, n_valid=N),
        grid=(pl.cdiv(n_out, tn),),
        in_specs=[pl.BlockSpec((tn, K), lambda j: (j, 0))],
        out_specs=pl.BlockSpec((K, tn), lambda j: (0, j)),
        out_shape=jax.ShapeDtypeStruct((K, n_out), BF16),
        compiler_params=pltpu.CompilerParams(dimension_semantics=("arbitrary",)),
        name="cast_pad",
    )(w_t)


def _rope(z, cos, sin):
    w = z.shape[-1]
    lane = lax.broadcasted_iota(I32, z.shape, 1)
    first_half = (lane % HEAD_DIM) < (HEAD_DIM // 2)
    partner = jnp.where(first_half, pltpu.roll(z, w - HEAD_DIM // 2, 1), pltpu.roll(z, HEAD_DIM // 2, 1))
    return z * cos + partner * sin


def _inproj_kernel(x_ref, g_ref, sh_ref, sc_ref, w_ref, mu_ref, cos_ref, sin_ref,
                   za_ref, zr_ref, carry_ref, *, tiles_per_seq):
    i = pl.program_id(0)
    tm = x_ref.shape[0]
    x = x_ref[...]
    ms = jnp.mean(x * x, axis=-1, keepdims=True)
    y = x * lax.rsqrt(ms + NORM_EPS) * g_ref[...]
    h = (y * (1.0 + sc_ref[0]) + sh_ref[0]).astype(BF16)

    cos2 = jnp.concatenate([cos_ref[...], cos_ref[...]], axis=1)
    sin2 = jnp.concatenate([sin_ref[...], sin_ref[...]], axis=1)
    for c0 in range(0, ATTN_COLS, 256):
        z = _dot(h, w_ref[:, c0:c0 + 256])
        if c0 < Q_COLS:
            z = _rope(z, cos2, sin2)
        else:
            z = jnp.concatenate([_rope(z[:, :KV_COLS], cos_ref[...], sin_ref[...]), z[:, KV_COLS:]], axis=1)
        za_ref[:, c0:c0 + 256] = z.astype(za_ref.dtype)

    first = (i % tiles_per_seq) == 0
    row = lax.broadcasted_iota(I32, (tm, 1), 0)
    c0 = 0
    while c0 < RWKV_PAD:
        n = min(512, RWKV_PAD - c0)
        z = _dot(h, w_ref[:, ATTN_COLS + c0:ATTN_COLS + c0 + n])
        prev_last = jnp.where(first, 0.0, carry_ref[7:8, c0:c0 + n])
        z_prev = jnp.where(row == 0, prev_last, pltpu.roll(z, 1, 0))
        carry_ref[:, c0:c0 + n] = z[tm - 8:tm, :]
        zr_ref[:, c0:c0 + n] = z + (z_prev - z) * mu_ref[:, c0:c0 + n]
        c0 += n


def _inproj(x2, g, sh, sc, w_bf, mu_pad, cos_t, sin_t, T):
    M, D = x2.shape
    tm = 256
    tps = T // tm
    NW = w_bf.shape[1]
    kern = functools.partial(_inproj_kernel, tiles_per_seq=tps)
    vm = (D * NW * 2 + 2 * tm * D * 4 + 2 * tm * ATTN_COLS * 2 + 2 * tm * RWKV_PAD * 4
          + 4 * tm * 512 * 4 + tm * D * 8 + (8 << 20))
    return pl.pallas_call(
        kern,
        grid=(M // tm,),
        in_specs=[pl.BlockSpec((tm, D), lambda i: (i, 0)),
                  pl.BlockSpec((1, D), lambda i: (0, 0)),
                  pl.BlockSpec((1, 1, D), lambda i: (i // tps, 0, 0)),
                  pl.BlockSpec((1, 1, D), lambda i: (i // tps, 0, 0)),
                  pl.BlockSpec((D, NW), lambda i: (0, 0), pipeline_mode=pl.Buffered(1)),
                  pl.BlockSpec((1, RWKV_PAD), lambda i: (0, 0)),
                  pl.BlockSpec((tm, 2 * HEAD_DIM), lambda i: (i % tps, 0)),
                  pl.BlockSpec((tm, 2 * HEAD_DIM), lambda i: (i % tps, 0))],
        out_specs=[pl.BlockSpec((tm, ATTN_COLS), lambda i: (i, 0)),
                   pl.BlockSpec((tm, RWKV_PAD), lambda i: (i, 0))],
        out_shape=[jax.ShapeDtypeStruct((M, ATTN_COLS), BF16),
                   jax.ShapeDtypeStruct((M, RWKV_PAD), F32)],
        scratch_shapes=[pltpu.VMEM((8, RWKV_PAD), F32)],
        compiler_params=pltpu.CompilerParams(
            dimension_semantics=("arbitrary",), vmem_limit_bytes=_vmem_limit(vm)),
        name="inproj",
    )(x2, g, sh, sc, w_bf, mu_pad, cos_t, sin_t)


def _attn_kernel(sink_ref, q_ref, kc_ref, kp_ref, vc_ref, vp_ref, o_ref):
    n = pl.program_id(1)
    blk = q_ref.shape[1]
    row = lax.broadcasted_iota(I32, (blk, blk), 0)
    col = lax.broadcasted_iota(I32, (blk, blk), 1)
    mask = jnp.concatenate([(col > row) & (n > 0), col <= row], axis=1)
    scale = 1.0 / math.sqrt(HEAD_DIM)
    outs = []
    for kvh in range(ATTN_KV_HEADS):
        ks = slice(kvh * HEAD_DIM, (kvh + 1) * HEAD_DIM)
        kmat = jnp.concatenate([kp_ref[0, :, ks], kc_ref[0, :, ks]], axis=0)
        vmat = jnp.concatenate([vp_ref[0, :, ks], vc_ref[0, :, ks]], axis=0)
        for g in range(ATTN_GROUP):
            hd = kvh * ATTN_GROUP + g
            qh = q_ref[0, :, hd * HEAD_DIM:(hd + 1) * HEAD_DIM]
            s = _dot_nt(qh, kmat) * scale
            s = jnp.where(mask, s, NEG_BIG)
            sink = sink_ref[hd]
            m = jnp.maximum(jnp.max(s, axis=-1, keepdims=True), sink)
            p = jnp.exp(s - m)
            denom = jnp.sum(p, axis=-1, keepdims=True) + jnp.exp(sink - m)
            o = _dot(p.astype(BF16), vmat)
            outs.append(o / denom)
    o_ref[0] = jnp.concatenate(outs, axis=1).astype(o_ref.dtype)


def _attention(za3, sinks):
    B, T, _ = za3.shape
    nb = T // WINDOW
    kcol = Q_COLS // KV_COLS
    prev = lambda b, n, s: (b, jnp.maximum(n - 1, 0), kcol)
    prev_v = lambda b, n, s: (b, jnp.maximum(n - 1, 0), kcol + 1)
    gs = pltpu.PrefetchScalarGridSpec(
        num_scalar_prefetch=1,
        grid=(B, nb),
        in_specs=[pl.BlockSpec((1, WINDOW, Q_COLS), lambda b, n, s: (b, n, 0)),
                  pl.BlockSpec((1, WINDOW, KV_COLS), lambda b, n, s: (b, n, kcol)),
                  pl.BlockSpec((1, WINDOW, KV_COLS), prev),
                  pl.BlockSpec((1, WINDOW, KV_COLS), lambda b, n, s: (b, n, kcol + 1)),
                  pl.BlockSpec((1, WINDOW, KV_COLS), prev_v)],
        out_specs=pl.BlockSpec((1, WINDOW, Q_COLS), lambda b, n, s: (b, n, 0)),
    )
    return pl.pallas_call(
        _attn_kernel,
        grid_spec=gs,
        out_shape=jax.ShapeDtypeStruct((B, T, Q_COLS), BF16),
        compiler_params=pltpu.CompilerParams(dimension_semantics=("arbitrary", "arbitrary")),
        name="attn",
    )(sinks, za3, za3, za3, za3, za3)


def _rwkv_kernel(r_ref, k_ref, v_ref, lora_ref, w0_ref, wdu_ref, a0_ref, wau_ref, wgu_ref,
                 kk_ref, ka_ref, rk_ref, lnw_ref, lnb_ref, o_ref, s_ref):
    c = pl.program_id(0)
    nseq = r_ref.shape[0]
    C = r_ref.shape[1]
    N = RWKV_N
    G = RWKV_GROUP
    GW = G * N
    ng = RWKV_HEADS // G

    @pl.when(c == 0)
    def _():
        s_ref[...] = jnp.zeros_like(s_ref)

    ti = lax.broadcasted_iota(I32, (C, C), 0)
    si = lax.broadcasted_iota(I32, (C, C), 1)
    tril = jnp.where(si <= ti, 1.0, 0.0).astype(BF16)
    a_seq, g_seq, e_in_seq, e_ex_seq, e_neg_seq = [], [], [], [], []
    for b in range(nseq):
        lora = lora_ref[b]
        wd = lora[:, 0:DECAY_LORA]
        ad = lora[:, DECAY_LORA:DECAY_LORA + A_LORA]
        gd = lora[:, DECAY_LORA + A_LORA:LORA_COLS]
        wlin = w0_ref[...] + _dot(jnp.tanh(wd), wdu_ref[...])
        neg = -wlin
        softplus = jnp.maximum(neg, 0.0) + jnp.log(1.0 + jnp.exp(-jnp.abs(neg)))
        w = -softplus - 0.5
        logdec = -jnp.exp(w)
        a_seq.append(1.0 / (1.0 + jnp.exp(-(a0_ref[...] + _dot(ad, wau_ref[...])))))
        g_seq.append(_dot(1.0 / (1.0 + jnp.exp(-gd)), wgu_ref[...]))
        ld_hi = logdec.astype(BF16)
        ld_lo = (logdec - ld_hi.astype(F32)).astype(BF16)
        cum = _dot(tril, ld_hi) + _dot(tril, ld_lo)
        e_in_seq.append(jnp.exp(cum))
        e_ex_seq.append(jnp.exp(cum - logdec))
        e_neg_seq.append(jnp.exp(-cum))

    lane_head = lax.broadcasted_iota(I32, (1, GW), 1) // N
    t_row = lax.broadcasted_iota(I32, (C, GW), 0)
    s_lane = lax.broadcasted_iota(I32, (C, GW), 1) % N
    strict = s_lane < t_row
    incl = s_lane <= t_row
    bi = lax.broadcasted_iota(I32, (GW, GW), 0)
    bj = lax.broadcasted_iota(I32, (GW, GW), 1)
    same_head = (bi // N) == (bj // N)
    ones_bd = jnp.where(same_head, 1.0, 0.0).astype(BF16)
    eye = jnp.where(s_lane == t_row, 1.0, 0.0)

    def expand(xc):
        return jnp.concatenate([jnp.where(lane_head == h, xc, 0.0) for h in range(G)], axis=0).astype(BF16)

    def head_sum(xs):
        s = _dot(jnp.concatenate(xs, axis=0).astype(BF16), ones_bd)
        return [s[i * C:(i + 1) * C] for i in range(len(xs))]

    seq = [b for b in range(nseq) for _ in range(ng)]
    col = [slice(gi * GW, (gi + 1) * GW) for _ in range(nseq) for gi in range(ng)]
    rng = range(nseq * ng)
    r_l = [r_ref[seq[i], :, col[i]] for i in rng]
    k_l = [k_ref[seq[i], :, col[i]] for i in rng]
    v_l = [v_ref[seq[i], :, col[i]] for i in rng]
    a_l = [a_seq[seq[i]][:, col[i]] for i in rng]
    e_in = [e_in_seq[seq[i]][:, col[i]] for i in rng]
    e_neg = [e_neg_seq[seq[i]][:, col[i]] for i in rng]
    kk0 = [k_l[i] * kk_ref[:, col[i]] for i in rng]
    nrm2 = head_sum([kk0[i] * kk0[i] for i in rng])
    kk = [kk0[i] * lax.rsqrt(jnp.maximum(nrm2[i], 1e-24)) for i in rng]
    k2 = [k_l[i] * (1.0 + (a_l[i] - 1.0) * ka_ref[:, col[i]]) for i in rng]
    bt = [kk[i] * a_l[i] * e_neg[i] for i in rng]
    kt = [k2[i] * e_neg[i] for i in rng]
    left = [jnp.concatenate([-kk[i] * e_ex_seq[seq[i]][:, col[i]], r_l[i] * e_in[i]], axis=0).astype(BF16)
            for i in rng]
    right = [jnp.concatenate([bt[i], kt[i]], axis=0).astype(BF16) for i in rng]
    zed = [jnp.concatenate([expand(bt[i]), expand(kt[i])], axis=0) for i in rng]
    big = [_dot_nt(left[i], zed[i]) for i in rng]
    s_old = [s_ref[i] for i in rng]
    ls = [_dot_nt(left[i], s_old[i].astype(BF16)) for i in rng]
    v_bd = [expand(v_l[i]) for i in rng]
    pw = [jnp.where(strict, big[i][:C, :G * C], 0.0) for i in rng]
    tinv = [eye + pw[i] for i in rng]
    pw = [_dot(pw[i].astype(BF16), expand(pw[i])) for i in rng]
    a_kk = [jnp.concatenate([jnp.where(strict, big[i][:C, G * C:], 0.0),
                             jnp.where(incl, big[i][C:, G * C:], 0.0)], axis=0).astype(BF16) for i in rng]
    kv = [_dot(a_kk[i], v_bd[i]) for i in rng]
    x = [ls[i][:C] + kv[i][:C] for i in rng]
    span = 2
    while span < C:
        pw_bd = [expand(pw[i]) for i in rng]
        if 2 * span < C:
            both = [_dot(jnp.concatenate([pw[i], tinv[i]], axis=0).astype(BF16), pw_bd[i]) for i in rng]
            pw = [both[i][:C] for i in rng]
            tinv = [tinv[i] + both[i][C:] for i in rng]
        else:
            tinv = [tinv[i] + _dot(tinv[i].astype(BF16), pw_bd[i]) for i in rng]
        span *= 2
    u = [_dot(tinv[i].astype(BF16), expand(x[i])) for i in rng]
    y = [ls[i][C:] + kv[i][C:]
         + _dot(jnp.where(incl, big[i][C:, :G * C], 0.0).astype(BF16), expand(u[i])) for i in rng]
    uv = [jnp.concatenate([u[i], v_l[i]], axis=0).astype(BF16) for i in rng]
    for i in rng:
        s_ref[i] = ((s_old[i] + jnp.where(same_head, _dot_tn(uv[i], right[i]), 0.0))
                    * e_in[i][C - 1:C, :])
    ysum = head_sum(y)
    yc = [y[i] - ysum[i] * (1.0 / N) for i in rng]
    ysq = head_sum([yc[i] * yc[i] for i in rng])
    var = [ysq[i] * (1.0 / N) for i in rng]
    rksum = head_sum([r_l[i] * k2[i] * rk_ref[:, col[i]] for i in rng])
    bonus = [rksum[i] * v_l[i] for i in rng]
    for i in rng:
        yn = yc[i] * lax.rsqrt(var[i] + RWKV_LN_EPS) * lnw_ref[:, col[i]] + lnb_ref[:, col[i]]
        o_ref[seq[i], :, col[i]] = ((yn + bonus[i]) * g_seq[seq[i]][:, col[i]]).astype(o_ref.dtype)


def _rwkv(zr3, w0, wdu, a0, wau, wgu, k_k, k_a, r_k, ln_w, ln_b):
    B, T, _ = zr3.shape
    C = CHUNK
    W = RWKV_W
    GW = RWKV_GROUP * RWKV_N
    vec = lambda v: v.reshape(1, W)
    full = lambda shape: pl.BlockSpec(shape, lambda c: (0,) * len(shape))
    return pl.pallas_call(
        _rwkv_kernel,
        grid=(T // C,),
        in_specs=[pl.BlockSpec((B, C, W), lambda c: (0, c, 0)),
                  pl.BlockSpec((B, C, W), lambda c: (0, c, 1)),
                  pl.BlockSpec((B, C, W), lambda c: (0, c, 2)),
                  pl.BlockSpec((B, C, LORA_PAD), lambda c: (0, c, 3 * W // LORA_PAD)),
                  full((1, W)), full((DECAY_LORA, W)), full((1, W)), full((A_LORA, W)),
                  full((GATE_LORA, W)), full((1, W)), full((1, W)), full((1, W)),
                  full((1, W)), full((1, W))],
        out_specs=pl.BlockSpec((B, C, W), lambda c: (0, c, 0)),
        out_shape=jax.ShapeDtypeStruct((B, T, W), BF16),
        scratch_shapes=[pltpu.VMEM((B * RWKV_HEADS // RWKV_GROUP, GW, GW), F32)],
        compiler_params=pltpu.CompilerParams(dimension_semantics=("arbitrary",)),
        name="rwkv",
    )(zr3, zr3, zr3, zr3, vec(w0), wdu, vec(a0), wau, wgu, vec(k_k), vec(k_a), vec(r_k),
      vec(ln_w), vec(ln_b))


def _outproj_kernel(oa_ref, or_ref, x_ref, wo_ref, gt_ref, g_ref, sh_ref, sc_ref,
                    wrb_ref, wrh_ref, br_ref, x1_ref, h2_ref, lg_ref):
    mixed = _dot(oa_ref[...], wo_ref[0:Q_COLS, :]) + _dot(or_ref[...], wo_ref[Q_COLS:, :])
    x1 = x_ref[...] + gt_ref[0] * mixed
    x1_ref[...] = x1
    ms = jnp.mean(x1 * x1, axis=-1, keepdims=True)
    h2 = x1 * lax.rsqrt(ms + NORM_EPS) * g_ref[...] * (1.0 + sc_ref[0]) + sh_ref[0]
    half = h2.shape[1] // 2
    h2p = _bf16_pair_pack(h2[:, :half], h2[:, half:])
    for j in range(ROW_SLABS):
        w = h2_ref.shape[2]
        h2_ref[:, j, :] = h2p[:, j * w:(j + 1) * w]
    hh = h2.astype(BF16)
    hl = (h2 - hh.astype(F32)).astype(BF16)
    both = _dot(hh, wrb_ref[...])
    lg_ref[...] = both[:, :LANES] + both[:, LANES:] + _dot(hl, wrh_ref[...]) + br_ref[...]


def _outproj(oa2, or2, x2, wo_bf, gt, g2, sh, sc, wr_both, wr_hi, br, T):
    M, D = x2.shape
    tm = 256
    tps = T // tm
    bvec = pl.BlockSpec((1, 1, D), lambda i: (i // tps, 0, 0))
    vm = D * D * 2 + 2 * tm * D * (2 + 4 + 4 + 4) + tm * D * 16 + (8 << 20)
    return pl.pallas_call(
        _outproj_kernel,
        grid=(M // tm,),
        in_specs=[pl.BlockSpec((tm, Q_COLS), lambda i: (i, 0)),
                  pl.BlockSpec((tm, RWKV_W), lambda i: (i, 0)),
                  pl.BlockSpec((tm, D), lambda i: (i, 0)),
                  pl.BlockSpec((D, D), lambda i: (0, 0), pipeline_mode=pl.Buffered(1)),
                  bvec,
                  pl.BlockSpec((1, D), lambda i: (0, 0)),
                  bvec, bvec,
                  pl.BlockSpec((D, 2 * LANES), lambda i: (0, 0)),
                  pl.BlockSpec((D, LANES), lambda i: (0, 0)),
                  pl.BlockSpec((1, LANES), lambda i: (0, 0))],
        out_specs=[pl.BlockSpec((tm, D), lambda i: (i, 0)),
                   pl.BlockSpec((tm, ROW_SLABS, D // (2 * ROW_SLABS)), lambda i: (i, 0, 0)),
                   pl.BlockSpec((tm, LANES), lambda i: (i, 0))],
        out_shape=[jax.ShapeDtypeStruct((M, D), F32),
                   jax.ShapeDtypeStruct((M, ROW_SLABS, D // (2 * ROW_SLABS)), jnp.uint32),
                   jax.ShapeDtypeStruct((M, LANES), F32)],
        compiler_params=pltpu.CompilerParams(
            dimension_semantics=("arbitrary",), vmem_limit_bytes=_vmem_limit(vm)),
        name="outproj",
    )(oa2, or2, x2, wo_bf, gt, g2, sh, sc, wr_both, wr_hi, br)


def _route_kernel(lg_ref, info_ref, cnt_ref, run_ref):
    i = pl.program_id(0)
    tm = lg_ref.shape[0]

    @pl.when(i == 0)
    def _():
        run_ref[...] = jnp.zeros_like(run_ref)

    lg = lg_ref[...]
    lane = lax.broadcasted_iota(I32, lg.shape, 1)
    gl = jnp.where(lane < N_GROUPS, lg, NEG_BIG)
    gmax = jnp.max(gl, axis=-1, keepdims=True)
    gsum = jnp.sum(jnp.exp(gl - gmax), axis=-1, keepdims=True)
    g_gate = 1.0 / gsum
    g_idx = jnp.min(jnp.where(gl == gmax, lane, LANES), axis=-1, keepdims=True)
    lo = N_GROUPS + EXPERTS_PER_GROUP * g_idx
    el = jnp.where((lane >= lo) & (lane < lo + EXPERTS_PER_GROUP), lg, NEG_BIG)
    e1max = jnp.max(el, axis=-1, keepdims=True)
    l1 = jnp.min(jnp.where(el == e1max, lane, LANES), axis=-1, keepdims=True)
    el2 = jnp.where(lane == l1, NEG_BIG, el)
    e2max = jnp.max(el2, axis=-1, keepdims=True)
    l2 = jnp.min(jnp.where(el2 == e2max, lane, LANES), axis=-1, keepdims=True)
    t2 = jnp.exp(e2max - e1max)
    w1 = g_gate / (1.0 + t2)
    w2 = g_gate * t2 / (1.0 + t2)
    ex1 = l1 - N_GROUPS
    ex2 = l2 - N_GROUPS

    oh1 = jnp.where(lane == ex1, 1.0, 0.0)
    oh2 = jnp.where(lane == ex2, 1.0, 0.0)
    ti = lax.broadcasted_iota(I32, (tm, tm), 0)
    si = lax.broadcasted_iota(I32, (tm, tm), 1)
    lower = jnp.where(si < ti, 1.0, 0.0).astype(BF16)
    pre1 = _dot(lower, oh1.astype(BF16))
    pre2 = _dot(lower, oh2.astype(BF16))
    cnt1 = jnp.sum(oh1, axis=0, keepdims=True)
    cnt2 = jnp.sum(oh2, axis=0, keepdims=True)
    run = run_ref[...]
    rank1 = jnp.sum(oh1 * (pre1 + run), axis=-1, keepdims=True)
    rank2 = jnp.sum(oh2 * (pre2 + run + cnt1), axis=-1, keepdims=True)
    run = run + cnt1 + cnt2
    run_ref[...] = run
    cnt_ref[...] = run

    info = jnp.where(lane == 0, ex1.astype(F32), 0.0)
    info = jnp.where(lane == 1, ex2.astype(F32), info)
    info = jnp.where(lane == 2, rank1, info)
    info = jnp.where(lane == 3, rank2, info)
    info = jnp.where(lane == 4, w1, info)
    info = jnp.where(lane == 5, w2, info)
    info_ref[...] = info


def _route(lg):
    M = lg.shape[0]
    tm = 1024
    return pl.pallas_call(
        _route_kernel,
        grid=(M // tm,),
        in_specs=[pl.BlockSpec((tm, LANES), lambda i: (i, 0))],
        out_specs=[pl.BlockSpec((tm, LANES), lambda i: (i, 0)),
                   pl.BlockSpec((1, LANES), lambda i: (0, 0))],
        out_shape=[jax.ShapeDtypeStruct((M, LANES), F32),
                   jax.ShapeDtypeStruct((1, LANES), F32)],
        scratch_shapes=[pltpu.VMEM((1, LANES), F32)],
        compiler_params=pltpu.CompilerParams(dimension_semantics=("arbitrary",)),
        name="route",
    )(lg)


def _plan_kernel(info_ref, cnt_ref, slot_ref, meta_ref):
    cnt = cnt_ref[...]
    lane_r = lax.broadcasted_iota(I32, (1, LANES), 1)
    nblk = jnp.floor((cnt + (MOE_BLOCK - 1)) * (1.0 / MOE_BLOCK))
    ei = lax.broadcasted_iota(I32, (LANES, LANES), 0)
    ej = lax.broadcasted_iota(I32, (LANES, LANES), 1)
    upper = jnp.where(ei <= ej, 1.0, 0.0).astype(BF16)
    nb8 = jnp.broadcast_to(nblk, (8, LANES)).astype(BF16)
    bend = _dot(nb8, upper)[0:1, :]
    bstart = bend - nblk
    pstart = bstart * MOE_BLOCK

    info = info_ref[...]
    lane = lax.broadcasted_iota(I32, info.shape, 1)
    ex1 = info[:, 0:1].astype(I32)
    ex2 = info[:, 1:2].astype(I32)
    s1 = jnp.sum(jnp.where(lane == ex1, pstart, 0.0), axis=-1, keepdims=True) + info[:, 2:3]
    s2 = jnp.sum(jnp.where(lane == ex2, pstart, 0.0), axis=-1, keepdims=True) + info[:, 3:4]
    slot = jnp.where(lane == 0, s1, 0.0)
    slot = jnp.where(lane == 1, s2, slot)
    slot_ref[...] = slot.astype(I32)

    blk = lax.broadcasted_iota(I32, (LANES, 2 * LANES), 1).astype(F32)
    bend_col = jnp.sum(jnp.where(ei == ej, jnp.broadcast_to(bend, (LANES, LANES)), 0.0),
                       axis=-1, keepdims=True)
    erow = lax.broadcasted_iota(I32, (LANES, 2 * LANES), 0)
    hit = jnp.where((bend_col <= blk) & (erow < N_EXPERTS), 1.0, 0.0)
    bexp = jnp.minimum(jnp.sum(hit, axis=0, keepdims=True), N_EXPERTS - 1.0)
    n_used = jnp.max(jnp.where(lane_r < N_EXPERTS, bend, 0.0), axis=-1, keepdims=True)
    lastblk = jnp.where(nblk > 0, bend - 1.0, -1.0)
    r8 = lax.broadcasted_iota(I32, (8, 2 * LANES), 0)
    last2 = jnp.concatenate([lastblk, jnp.full((1, LANES), -1.0)], axis=1)
    meta = jnp.where(r8 == 0, jnp.broadcast_to(bexp, (8, 2 * LANES)), 0.0)
    meta = jnp.where(r8 == 1, jnp.broadcast_to(n_used, (8, 2 * LANES)), meta)
    meta = jnp.where(r8 == 2, jnp.broadcast_to(last2, (8, 2 * LANES)), meta)
    meta_ref[...] = meta.astype(I32)


def _plan(info, cnt):
    M = info.shape[0]
    tm = 1024
    return pl.pallas_call(
        _plan_kernel,
        grid=(M // tm,),
        in_specs=[pl.BlockSpec((tm, LANES), lambda i: (i, 0)),
                  pl.BlockSpec((1, LANES), lambda i: (0, 0))],
        out_specs=[pl.BlockSpec((tm, LANES), lambda i: (i, 0)),
                   pl.BlockSpec((8, 2 * LANES), lambda i: (0, 0))],
        out_shape=[jax.ShapeDtypeStruct((M, LANES), I32),
                   jax.ShapeDtypeStruct((8, 2 * LANES), I32)],
        compiler_params=pltpu.CompilerParams(dimension_semantics=("arbitrary",)),
        name="plan",
    )(info, cnt)


ROW_DEPTH = 3
DMA_GROUP = 16
WEIGHT_DEPTH = 3


def _slot_table_kernel(slots_ref, pad_ref, inv_ref, buf_ref, sem, *, n_assign):
    fill = pltpu.make_async_copy(pad_ref, buf_ref, sem)
    fill.start()
    fill.wait()

    def put(a, c):
        buf_ref[slots_ref[a] + MOE_BLOCK] = a
        return c

    lax.fori_loop(0, n_assign, put, 0, unroll=16)
    out = pltpu.make_async_copy(buf_ref, inv_ref, sem)
    out.start()
    out.wait()


def _slot_table(slots_km, n_blocks):
    n_assign = slots_km.shape[0]
    n_slots = (n_blocks + ROW_DEPTH) * MOE_BLOCK
    row = np.arange(n_slots)
    pad_ids = n_assign + ((row // MOE_BLOCK + ROW_DEPTH - 1) % ROW_DEPTH) * MOE_BLOCK + row % MOE_BLOCK
    gs = pltpu.PrefetchScalarGridSpec(
        num_scalar_prefetch=1, grid=(1,),
        in_specs=[pl.BlockSpec(memory_space=pl.ANY)],
        out_specs=pl.BlockSpec(memory_space=pl.ANY),
        scratch_shapes=[pltpu.SMEM((n_slots,), I32), pltpu.SemaphoreType.DMA(())])
    return pl.pallas_call(
        functools.partial(_slot_table_kernel, n_assign=n_assign),
        grid_spec=gs,
        out_shape=jax.ShapeDtypeStruct((n_slots,), I32),
        compiler_params=pltpu.CompilerParams(dimension_semantics=("arbitrary",)),
        name="slot_table",
    )(slots_km, jnp.asarray(pad_ids, I32))


def _experts_kernel(bexp_ref, seg_ref, inv_ref, h2_ref, w1_ref, w3_ref, w2_ref, y2_ref,
                    xbuf, ybuf, w1buf, w3buf, w2buf, sem_x, sem_y, sem_w, par_ref, *, n_assign):
    i = pl.program_id(0)
    n_used = seg_ref[N_EXPERTS]
    n_tok = n_assign // 2
    B = MOE_BLOCK
    FF = w1buf.shape[2]
    D = w1buf.shape[1]
    ND = ROW_DEPTH
    NW = w1buf.shape[0]

    def src_row(a):
        if n_tok & (n_tok - 1) == 0:
            return a & (n_tok - 1)
        return lax.rem(a, n_tok)

    def weight_copies(e, slot):
        return (pltpu.make_async_copy(w1_ref.at[e], w1buf.at[slot], sem_w.at[slot]),
                pltpu.make_async_copy(w3_ref.at[e], w3buf.at[slot], sem_w.at[slot]),
                pltpu.make_async_copy(w2_ref.at[e], w2buf.at[slot], sem_w.at[slot]))

    def start_weights(e, slot):
        c1, c3, c2 = weight_copies(e, slot)
        c1.start(priority=1)
        c3.start(priority=1)
        c2.start(priority=0)

    def prefetch_expert_after(e, hops, slot):
        ok = hops >= 0
        cur = e
        for _ in range(hops):
            nb = seg_ref[cur] + 1
            ok = jnp.logical_and(ok, nb < n_used)
            cur = bexp_ref[jnp.where(ok, nb, 0)]

        @pl.when(ok)
        def _():
            start_weights(cur, slot)

    def gather_group(b, g):
        slot = lax.rem(b + ND, ND)
        for r in range(g * DMA_GROUP, (g + 1) * DMA_GROUP):
            a = inv_ref[(b + 1) * B + r]
            pltpu.make_async_copy(h2_ref.at[pl.ds(src_row(a), 1)], xbuf.at[slot, pl.ds(r, 1)],
                                  sem_x.at[slot]).start()

    def scatter_group(b, g):
        slot = lax.rem(b + ND, ND)
        for r in range(g * DMA_GROUP, (g + 1) * DMA_GROUP):
            a = inv_ref[(b + 1) * B + r]
            pltpu.make_async_copy(ybuf.at[slot, pl.ds(r, 1)], y2_ref.at[pl.ds(a, 1)],
                                  sem_y.at[slot]).start()

    def wait_block(buf, sem, b):
        slot = lax.rem(b + ND, ND)
        pltpu.make_async_copy(buf.at[slot], buf.at[slot], sem.at[slot]).wait()

    n_groups = B // DMA_GROUP

    @pl.when(i == 0)
    def _():
        ybuf[...] = jnp.zeros_like(ybuf)
        for s in range(ND - 1):
            pltpu.make_async_copy(ybuf.at[s], y2_ref.at[pl.ds(n_assign + s * B, B)], sem_y.at[s]).start()
        par_ref[0] = 0
        e0 = bexp_ref[0]
        start_weights(e0, 0)
        for hops in range(1, NW):
            prefetch_expert_after(e0, hops, hops)
        for b in range(ND - 1):
            for g in range(n_groups):
                gather_group(b, g)

    active = i < n_used

    @pl.when(active)
    def _():
        e = bexp_ref[i]
        first = jnp.logical_or(i == 0, e != bexp_ref[jnp.maximum(i - 1, 0)])

        @pl.when(jnp.logical_and(first, i > 0))
        def _():
            par_ref[0] = par_ref[0] + 1

        @pl.when(first)
        def _():
            q = par_ref[0]
            for cp in weight_copies(e, lax.rem(q, NW)):
                cp.wait()

            @pl.when(i > 0)
            def _():
                prefetch_expert_after(e, NW - 1, lax.rem(q + NW - 1, NW))

        p = lax.rem(par_ref[0], NW)
        slot = lax.rem(i, ND)
        wait_block(xbuf, sem_x, i)
        wait_block(ybuf, sem_y, i)
        pending = [functools.partial(scatter_group, i - 1, g) for g in range(n_groups)]
        pending += [functools.partial(gather_group, i + ND - 1, g) for g in range(n_groups)]
        n_chunks = 2 * (FF // 256) + ROW_SLABS
        for _ in range(max(len(pending) - n_chunks, 0)):
            pending.pop(0)()

        def after_chunk():
            if pending:
                pending.pop(0)()

        xa, xb = _bf16_pair_unpack(jnp.concatenate([xbuf[slot, :, j, :] for j in range(ROW_SLABS)], axis=1))
        x = jnp.concatenate([xa, xb], axis=1).astype(BF16)
        sw = ybuf.shape[3]
        h1c, h3c = [], []
        for c0 in range(0, FF, 256):
            h1c.append(_dot(x, w1buf[p, :, c0:c0 + 256]))
            after_chunk()
            h3c.append(_dot(x, w3buf[p, :, c0:c0 + 256]))
            after_chunk()
        h1 = jnp.concatenate(h1c, axis=1)
        h3 = jnp.concatenate(h3c, axis=1)
        hid = (h1 * (1.0 / (1.0 + jnp.exp(-h1))) * h3).astype(BF16)
        for j in range(0, ROW_SLABS, 2):
            c0 = j * sw
            ya = _dot(hid, w2buf[p, :, c0:c0 + 2 * sw])
            after_chunk()
            yb = _dot(hid, w2buf[p, :, D // 2 + c0:D // 2 + c0 + 2 * sw])
            yp = _bf16_pair_pack(ya, yb)
            ybuf[slot, :, j, :] = yp[:, :sw]
            ybuf[slot, :, j + 1, :] = yp[:, sw:]
            after_chunk()
        while pending:
            pending.pop(0)()

    @pl.when(i == n_used)
    def _():
        wait_block(xbuf, sem_x, i)
        wait_block(ybuf, sem_y, i)
        for g in range(n_groups):
            scatter_group(i - 1, g)

    @pl.when(jnp.logical_and(i > n_used, i < n_used + ND - 1))
    def _():
        wait_block(xbuf, sem_x, i)
        wait_block(ybuf, sem_y, i)

    @pl.when(i == n_used + ND - 1)
    def _():
        wait_block(ybuf, sem_y, i)


def _experts(bexp, seg_meta, inv, h2, w1, w3, w2, n_blocks):
    M, DJ, SW = h2.shape
    D = 2 * DJ * SW
    FF = w1.shape[2]
    n_assign = 2 * M
    any_spec = pl.BlockSpec(memory_space=pl.ANY)
    row_buf = pltpu.VMEM((ROW_DEPTH, MOE_BLOCK, DJ, SW), jnp.uint32)
    gs = pltpu.PrefetchScalarGridSpec(
        num_scalar_prefetch=3,
        grid=(n_blocks + ROW_DEPTH,),
        in_specs=[any_spec, any_spec, any_spec, any_spec],
        out_specs=any_spec,
        scratch_shapes=[row_buf, row_buf,
                        pltpu.VMEM((WEIGHT_DEPTH, D, FF), F32), pltpu.VMEM((WEIGHT_DEPTH, D, FF), F32),
                        pltpu.VMEM((WEIGHT_DEPTH, FF, D), F32),
                        pltpu.SemaphoreType.DMA((ROW_DEPTH,)), pltpu.SemaphoreType.DMA((ROW_DEPTH,)),
                        pltpu.SemaphoreType.DMA((WEIGHT_DEPTH,)), pltpu.SMEM((1,), I32)],
    )
    vm = WEIGHT_DEPTH * 3 * D * FF * 4 + 2 * ROW_DEPTH * MOE_BLOCK * D * 4 + (12 << 20)
    return pl.pallas_call(
        functools.partial(_experts_kernel, n_assign=n_assign),
        grid_spec=gs,
        out_shape=jax.ShapeDtypeStruct((n_assign + ROW_DEPTH * MOE_BLOCK, DJ, SW), jnp.uint32),
        compiler_params=pltpu.CompilerParams(
            dimension_semantics=("arbitrary",), vmem_limit_bytes=_vmem_limit(vm),
            has_side_effects=True),
        name="experts",
    )(bexp, seg_meta, inv, h2, w1, w3, w2)


def _combine_kernel(x1_ref, info_ref, gt_ref, fg_ref, ya_ref, yb_ref, o_ref):
    info = info_ref[...]
    nj = ya_ref.shape[1]
    ya = jnp.concatenate(_bf16_pair_unpack(jnp.concatenate([ya_ref[:, j, :] for j in range(nj)], axis=1)), axis=1)
    yb = jnp.concatenate(_bf16_pair_unpack(jnp.concatenate([yb_ref[:, j, :] for j in range(nj)], axis=1)), axis=1)
    moe = info[:, 4:5] * ya + info[:, 5:6] * yb
    x = x1_ref[...] + gt_ref[0] * moe
    ms = jnp.mean(x * x, axis=-1, keepdims=True)
    o_ref[...] = x * lax.rsqrt(ms + NORM_EPS) * fg_ref[...]


def _combine(x1, info, gt, final_g, y2, T):
    M, D = x1.shape
    tm = 512
    tps = T // tm
    nt = M // tm
    vm = 2 * 4 * tm * D * 4 + tm * D * 16 + (8 << 20)
    return pl.pallas_call(
        _combine_kernel,
        grid=(nt,),
        in_specs=[pl.BlockSpec((tm, D), lambda i: (i, 0)),
                  pl.BlockSpec((tm, LANES), lambda i: (i, 0)),
                  pl.BlockSpec((1, 1, D), lambda i: (i // tps, 0, 0)),
                  pl.BlockSpec((1, D), lambda i: (0, 0)),
                  pl.BlockSpec((tm, ROW_SLABS, D // (2 * ROW_SLABS)), lambda i: (i, 0, 0)),
                  pl.BlockSpec((tm, ROW_SLABS, D // (2 * ROW_SLABS)), lambda i: (nt + i, 0, 0))],
        out_specs=pl.BlockSpec((tm, D), lambda i: (i, 0)),
        out_shape=jax.ShapeDtypeStruct((M, D), F32),
        compiler_params=pltpu.CompilerParams(
            dimension_semantics=("arbitrary",), vmem_limit_bytes=_vmem_limit(vm)),
        name="combine",
    )(x1, info, gt, final_g.reshape(1, D), y2, y2)


def _rope_tables(T):
    inv_freq = ROPE_THETA ** (-np.arange(0, HEAD_DIM, 2, dtype=np.float64) / HEAD_DIM)
    ang = np.arange(T, dtype=np.float64)[:, None] * inv_freq[None, :]
    cos, sin = np.cos(ang), np.sin(ang)
    cos_h = np.concatenate([cos, cos], axis=1)
    sin_h = np.concatenate([-sin, sin], axis=1)
    return (jnp.asarray(np.tile(cos_h, (1, 2)), F32), jnp.asarray(np.tile(sin_h, (1, 2)), F32))


def _layer(x, c, w_ada, b_ada, norm1_g, w_in, mu_shift, sinks, w0, w_decay_up, a0, w_a_up, w_g_up,
           k_k, k_a, r_k, ln_x_w, ln_x_b, w_o, norm2_g, w_rg, b_rg, w_re, b_re, w1, w3, w2, final_g):
    B, T, D = x.shape
    M = B * T
    mod = _ada(c, w_ada, b_ada)
    sh1, sc1, gt1, sh2, sc2, gt2 = [m.reshape(B, 1, D) for m in jnp.split(mod, 6, axis=-1)]

    pad = RWKV_PAD - RWKV_COLS
    w_bf = _cast_pad(jnp.swapaxes(w_in, 0, 1), ATTN_COLS + RWKV_PAD)
    mu_pad = jnp.pad(mu_shift, (0, pad)).reshape(1, RWKV_PAD)
    cos_t, sin_t = _rope_tables(T)
    x2 = x.reshape(M, D)
    za, zr = _inproj(x2, norm1_g.reshape(1, D), sh1, sc1, w_bf, mu_pad, cos_t, sin_t, T)

    o_attn = _attention(za.reshape(B, T, ATTN_COLS), sinks)
    o_rwkv = _rwkv(zr.reshape(B, T, RWKV_PAD), w0, w_decay_up, a0, w_a_up, w_g_up, k_k, k_a,
                   r_k.reshape(-1), ln_x_w, ln_x_b)

    wr = jnp.concatenate([w_rg, w_re], axis=1)
    wr = jnp.pad(wr, ((0, 0), (0, LANES - wr.shape[1])))
    wr_hi = wr.astype(BF16)
    wr_lo = (wr - wr_hi.astype(F32)).astype(BF16)
    br = jnp.pad(jnp.concatenate([b_rg, b_re]), (0, LANES - N_GROUPS - N_EXPERTS)).reshape(1, LANES)
    x1, h2, lg = _outproj(o_attn.reshape(M, Q_COLS), o_rwkv.reshape(M, RWKV_W), x2, w_o.astype(BF16),
                          gt1, norm2_g.reshape(1, D), sh2, sc2, jnp.concatenate([wr_hi, wr_lo], axis=1),
                          wr_hi, br, T)

    n_blocks = -(-(2 * M) // MOE_BLOCK) + N_EXPERTS
    info, cnt = _route(lg)
    slots, meta = _plan(info, cnt)
    slots_km = slots[:, :2].T.reshape(-1)
    seg_meta = jnp.concatenate([meta[2, :N_EXPERTS], meta[1, :1]])
    inv = _slot_table(slots_km, n_blocks)
    y2 = _experts(meta[0, :n_blocks + ROW_DEPTH], seg_meta, inv, h2, w1, w3, w2, n_blocks)
    out = _combine(x1, info, gt2, final_g, y2, T)
    return out.reshape(B, T, D)


def kernel(x, c, w_ada, b_ada, norm1_g, w_in, mu_shift, sinks, w0, w_decay_up, a0, w_a_up, w_g_up, k_k, k_a, r_k, ln_x_w, ln_x_b, w_o, norm2_g, w_router_group, b_router_group, w_router_expert, b_router_expert, w1, w3, w2, final_g):
    depth = w_ada.shape[0]
    assert depth == 1, "single-layer stack"
    l = 0
    return _layer(x, c, w_ada[l], b_ada[l], norm1_g[l], w_in[l], mu_shift[l], sinks[l], w0[l],
                  w_decay_up[l], a0[l], w_a_up[l], w_g_up[l], k_k[l], k_a[l], r_k[l], ln_x_w[l],
                  ln_x_b[l], w_o[l], norm2_g[l], w_router_group[l], b_router_group[l],
                  w_router_expert[l], b_router_expert[l], w1[l], w3[l], w2[l], final_g)
```

```python
import functools
import math

import jax
import jax.numpy as jnp
import numpy as np
from jax import lax
from jax.experimental import pallas as pl
from jax.experimental.pallas import tpu as pltpu

F32 = jnp.float32
BF16 = jnp.bfloat16
I32 = jnp.int32

LANES = 128
VMEM_BYTES_V7X = 64 * 1024 * 1024

HEAD_DIM = 64
ATTN_HEADS = 16
ATTN_KV_HEADS = 2
ATTN_GROUP = ATTN_HEADS // ATTN_KV_HEADS
WINDOW = 128
ROPE_THETA = 10000.0
RWKV_HEADS = 16
RWKV_N = 64
DECAY_LORA = 64
A_LORA = 64
GATE_LORA = 160
RWKV_LN_EPS = 64e-5
N_GROUPS = 8
EXPERTS_PER_GROUP = 8
N_EXPERTS = N_GROUPS * EXPERTS_PER_GROUP
EXPERT_FF = 512
MOE_BLOCK = 128
ROW_SLABS = 8
NORM_EPS = 1e-6

Q_COLS = ATTN_HEADS * HEAD_DIM
KV_COLS = ATTN_KV_HEADS * HEAD_DIM
ATTN_COLS = Q_COLS + 2 * KV_COLS
RWKV_W = RWKV_HEADS * RWKV_N
LORA_COLS = DECAY_LORA + A_LORA + GATE_LORA
LORA_PAD = 384
RWKV_COLS = 3 * RWKV_W + LORA_COLS
RWKV_PAD = 3 * RWKV_W + LORA_PAD
CHUNK = 64
RWKV_GROUP = 4
NEG_BIG = -1e30


def _vmem_limit(nbytes):
    return int(min(nbytes, VMEM_BYTES_V7X - 4 * 1024 * 1024))


def _dot(a, b):
    return jnp.dot(a, b, preferred_element_type=F32)


def _dot_nt(a, b):
    return lax.dot_general(a, b, (((1,), (1,)), ((), ())), preferred_element_type=F32)


def _dot_tn(a, b):
    return lax.dot_general(a, b, (((0,), (0,)), ((), ())), preferred_element_type=F32)


def _bf16_pair_pack(a, b):
    def hi16(x):
        return pltpu.bitcast(x.astype(BF16).astype(F32), jnp.uint32)
    return hi16(a) | (hi16(b) >> 16)


def _bf16_pair_unpack(w):
    return (pltpu.bitcast(w & jnp.uint32(0xFFFF0000), F32), pltpu.bitcast(w << 16, F32))


def _ada_kernel(c_ref, w_ref, b_ref, o_ref):
    c = c_ref[...]
    s = c * (1.0 / (1.0 + jnp.exp(-c)))
    o_ref[...] = _dot(s, w_ref[...]) + b_ref[...]


def _ada(c, w_ada, b_ada):
    B, D = c.shape
    N = w_ada.shape[1]
    tn = 1024
    cp = jnp.zeros((8, D), F32).at[:B].set(c)
    out = pl.pallas_call(
        _ada_kernel,
        grid=(N // tn,),
        in_specs=[pl.BlockSpec((8, D), lambda j: (0, 0)),
                  pl.BlockSpec((D, tn), lambda j: (0, j)),
                  pl.BlockSpec((1, tn), lambda j: (0, j))],
        out_specs=pl.BlockSpec((8, tn), lambda j: (0, j)),
        out_shape=jax.ShapeDtypeStruct((8, N), F32),
        compiler_params=pltpu.CompilerParams(
            dimension_semantics=("arbitrary",),
            vmem_limit_bytes=_vmem_limit(2 * D * tn * 4 + (8 << 20))),
        name="ada",
    )(cp, w_ada, b_ada.reshape(1, N))
    return out[:B]


def _cast_pad_kernel(wt_ref, o_ref, *, n_valid):
    j = pl.program_id(0)
    tn = wt_ref.shape[0]
    row = j * tn + lax.broadcasted_iota(I32, wt_ref.shape, 0)
    wt = jnp.where(row < n_valid, wt_ref[...], 0.0)
    o_ref[...] = wt.T.astype(o_ref.dtype)


def _cast_pad(w_t, n_out):
    N, K = w_t.shape
    tn = 512
    return pl.pallas_call(
        functools.partial(_cast_pad_kernel, n_valid=N),
        grid=(pl.cdiv(n_out, tn),),
        in_specs=[pl.BlockSpec((tn, K), lambda j: (j, 0))],
        out_specs=pl.BlockSpec((K, tn), lambda j: (0, j)),
        out_shape=jax.ShapeDtypeStruct((K, n_out), BF16),
        compiler_params=pltpu.CompilerParams(dimension_semantics=("arbitrary",)),
        name="cast_pad",
    )(w_t)


def _rope(z, cos, sin):
    w = z.shape[-1]
    lane = lax.broadcasted_iota(I32, z.shape, 1)
    first_half = (lane % HEAD_DIM) < (HEAD_DIM // 2)
    partner = jnp.where(first_half, pltpu.roll(z, w - HEAD_DIM // 2, 1), pltpu.roll(z, HEAD_DIM // 2, 1))
    return z * cos + partner * sin


def _inproj_kernel(x_ref, g_ref, sh_ref, sc_ref, w_ref, mu_ref, cos_ref, sin_ref,
                   za_ref, zr_ref, carry_ref, *, tiles_per_seq):
    i = pl.program_id(0)
    tm = x_ref.shape[0]
    x = x_ref[...]
    ms = jnp.mean(x * x, axis=-1, keepdims=True)
    y = x * lax.rsqrt(ms + NORM_EPS) * g_ref[...]
    h = (y * (1.0 + sc_ref[0]) + sh_ref[0]).astype(BF16)

    cos2 = jnp.concatenate([cos_ref[...], cos_ref[...]], axis=1)
    sin2 = jnp.concatenate([sin_ref[...], sin_ref[...]], axis=1)
    for c0 in range(0, ATTN_COLS, 256):
        z = _dot(h, w_ref[:, c0:c0 + 256])
        if c0 < Q_COLS:
            z = _rope(z, cos2, sin2)
        else:
            z = jnp.concatenate([_rope(z[:, :KV_COLS], cos_ref[...], sin_ref[...]), z[:, KV_COLS:]], axis=1)
        za_ref[:, c0:c0 + 256] = z.astype(za_ref.dtype)

    first = (i % tiles_per_seq) == 0
    row = lax.broadcasted_iota(I32, (tm, 1), 0)
    c0 = 0
    while c0 < RWKV_PAD:
        n = min(512, RWKV_PAD - c0)
        z = _dot(h, w_ref[:, ATTN_COLS + c0:ATTN_COLS + c0 + n])
        prev_last = jnp.where(first, 0.0, carry_ref[7:8, c0:c0 + n])
        z_prev = jnp.where(row == 0, prev_last, pltpu.roll(z, 1, 0))
        carry_ref[:, c0:c0 + n] = z[tm - 8:tm, :]
        zr_ref[:, c0:c0 + n] = z + (z_prev - z) * mu_ref[:, c0:c0 + n]
        c0 += n


def _inproj(x2, g, sh, sc, w_bf, mu_pad, cos_t, sin_t, T):
    M, D = x2.shape
    tm = 256
    tps = T // tm
    NW = w_bf.shape[1]
    kern = functools.partial(_inproj_kernel, tiles_per_seq=tps)
    vm = (D * NW * 2 + 2 * tm * D * 4 + 2 * tm * ATTN_COLS * 2 + 2 * tm * RWKV_PAD * 4
          + 4 * tm * 512 * 4 + tm * D * 8 + (8 << 20))
    return pl.pallas_call(
        kern,
        grid=(M // tm,),
        in_specs=[pl.BlockSpec((tm, D), lambda i: (i, 0)),
                  pl.BlockSpec((1, D), lambda i: (0, 0)),
                  pl.BlockSpec((1, 1, D), lambda i: (i // tps, 0, 0)),
                  pl.BlockSpec((1, 1, D), lambda i: (i // tps, 0, 0)),
                  pl.BlockSpec((D, NW), lambda i: (0, 0), pipeline_mode=pl.Buffered(1)),
                  pl.BlockSpec((1, RWKV_PAD), lambda i: (0, 0)),
                  pl.BlockSpec((tm, 2 * HEAD_DIM), lambda i: (i % tps, 0)),
                  pl.BlockSpec((tm, 2 * HEAD_DIM), lambda i: (i % tps, 0))],
        out_specs=[pl.BlockSpec((tm, ATTN_COLS), lambda i: (i, 0)),
                   pl.BlockSpec((tm, RWKV_PAD), lambda i: (i, 0))],
        out_shape=[jax.ShapeDtypeStruct((M, ATTN_COLS), BF16),
                   jax.ShapeDtypeStruct((M, RWKV_PAD), F32)],
        scratch_shapes=[pltpu.VMEM((8, RWKV_PAD), F32)],
        compiler_params=pltpu.CompilerParams(
            dimension_semantics=("arbitrary",), vmem_limit_bytes=_vmem_limit(vm)),
        name="inproj",
    )(x2, g, sh, sc, w_bf, mu_pad, cos_t, sin_t)


def _attn_kernel(sink_ref, q_ref, kc_ref, kp_ref, vc_ref, vp_ref, o_ref):
    n = pl.program_id(1)
    blk = q_ref.shape[1]
    row = lax.broadcasted_iota(I32, (blk, blk), 0)
    col = lax.broadcasted_iota(I32, (blk, blk), 1)
    mask = jnp.concatenate([(col > row) & (n > 0), col <= row], axis=1)
    scale = 1.0 / math.sqrt(HEAD_DIM)
    outs = []
    for kvh in range(ATTN_KV_HEADS):
        ks = slice(kvh * HEAD_DIM, (kvh + 1) * HEAD_DIM)
        kmat = jnp.concatenate([kp_ref[0, :, ks], kc_ref[0, :, ks]], axis=0)
        vmat = jnp.concatenate([vp_ref[0, :, ks], vc_ref[0, :, ks]], axis=0)
        for g in range(ATTN_GROUP):
            hd = kvh * ATTN_GROUP + g
            qh = q_ref[0, :, hd * HEAD_DIM:(hd + 1) * HEAD_DIM]
            s = _dot_nt(qh, kmat) * scale
            s = jnp.where(mask, s, NEG_BIG)
            sink = sink_ref[hd]
            m = jnp.maximum(jnp.max(s, axis=-1, keepdims=True), sink)
            p = jnp.exp(s - m)
            denom = jnp.sum(p, axis=-1, keepdims=True) + jnp.exp(sink - m)
            o = _dot(p.astype(BF16), vmat)
            outs.append(o / denom)
    o_ref[0] = jnp.concatenate(outs, axis=1).astype(o_ref.dtype)


def _attention(za3, sinks):
    B, T, _ = za3.shape
    nb = T // WINDOW
    kcol = Q_COLS // KV_COLS
    prev = lambda b, n, s: (b, jnp.maximum(n - 1, 0), kcol)
    prev_v = lambda b, n, s: (b, jnp.maximum(n - 1, 0), kcol + 1)
    gs = pltpu.PrefetchScalarGridSpec(
        num_scalar_prefetch=1,
        grid=(B, nb),
        in_specs=[pl.BlockSpec((1, WINDOW, Q_COLS), lambda b, n, s: (b, n, 0)),
                  pl.BlockSpec((1, WINDOW, KV_COLS), lambda b, n, s: (b, n, kcol)),
                  pl.BlockSpec((1, WINDOW, KV_COLS), prev),
                  pl.BlockSpec((1, WINDOW, KV_COLS), lambda b, n, s: (b, n, kcol + 1)),
                  pl.BlockSpec((1, WINDOW, KV_COLS), prev_v)],
        out_specs=pl.BlockSpec((1, WINDOW, Q_COLS), lambda b, n, s: (b, n, 0)),
    )
    return pl.pallas_call(
        _attn_kernel,
        grid_spec=gs,
        out_shape=jax.ShapeDtypeStruct((B, T, Q_COLS), BF16),
        compiler_params=pltpu.CompilerParams(dimension_semantics=("arbitrary", "arbitrary")),
        name="attn",
    )(sinks, za3, za3, za3, za3, za3)


def _rwkv_kernel(r_ref, k_ref, v_ref, lora_ref, w0_ref, wdu_ref, a0_ref, wau_ref, wgu_ref,
                 kk_ref, ka_ref, rk_ref, lnw_ref, lnb_ref, o_ref, s_ref):
    c = pl.program_id(0)
    nseq = r_ref.shape[0]
    C = r_ref.shape[1]
    N = RWKV_N
    G = RWKV_GROUP
    GW = G * N
    ng = RWKV_HEADS // G

    @pl.when(c == 0)
    def _():
        s_ref[...] = jnp.zeros_like(s_ref)

    ti = lax.broadcasted_iota(I32, (C, C), 0)
    si = lax.broadcasted_iota(I32, (C, C), 1)
    tril = jnp.where(si <= ti, 1.0, 0.0).astype(BF16)
    a_seq, g_seq, e_in_seq, e_ex_seq, e_neg_seq = [], [], [], [], []
    for b in range(nseq):
        lora = lora_ref[b]
        wd = lora[:, 0:DECAY_LORA]
        ad = lora[:, DECAY_LORA:DECAY_LORA + A_LORA]
        gd = lora[:, DECAY_LORA + A_LORA:LORA_COLS]
        wlin = w0_ref[...] + _dot(jnp.tanh(wd), wdu_ref[...])
        neg = -wlin
        softplus = jnp.maximum(neg, 0.0) + jnp.log(1.0 + jnp.exp(-jnp.abs(neg)))
        w = -softplus - 0.5
        logdec = -jnp.exp(w)
        a_seq.append(1.0 / (1.0 + jnp.exp(-(a0_ref[...] + _dot(ad, wau_ref[...])))))
        g_seq.append(_dot(1.0 / (1.0 + jnp.exp(-gd)), wgu_ref[...]))
        ld_hi = logdec.astype(BF16)
        ld_lo = (logdec - ld_hi.astype(F32)).astype(BF16)
        cum = _dot(tril, ld_hi) + _dot(tril, ld_lo)
        e_in_seq.append(jnp.exp(cum))
        e_ex_seq.append(jnp.exp(cum - logdec))
        e_neg_seq.append(jnp.exp(-cum))

    lane_head = lax.broadcasted_iota(I32, (1, GW), 1) // N
    t_row = lax.broadcasted_iota(I32, (C, GW), 0)
    s_lane = lax.broadcasted_iota(I32, (C, GW), 1) % N
    strict = s_lane < t_row
    incl = s_lane <= t_row
    bi = lax.broadcasted_iota(I32, (GW, GW), 0)
    bj = lax.broadcasted_iota(I32, (GW, GW), 1)
    same_head = (bi // N) == (bj // N)
    ones_bd = jnp.where(same_head, 1.0, 0.0).astype(BF16)
    eye = jnp.where(s_lane == t_row, 1.0, 0.0)

    def expand(xc):
        return jnp.concatenate([jnp.where(lane_head == h, xc, 0.0) for h in range(G)], axis=0).astype(BF16)

    def head_sum(xs):
        s = _dot(jnp.concatenate(xs, axis=0).astype(BF16), ones_bd)
        return [s[i * C:(i + 1) * C] for i in range(len(xs))]

    seq = [b for b in range(nseq) for _ in range(ng)]
    col = [slice(gi * GW, (gi + 1) * GW) for _ in range(nseq) for gi in range(ng)]
    rng = range(nseq * ng)
    r_l = [r_ref[seq[i], :, col[i]] for i in rng]
    k_l = [k_ref[seq[i], :, col[i]] for i in rng]
    v_l = [v_ref[seq[i], :, col[i]] for i in rng]
    a_l = [a_seq[seq[i]][:, col[i]] for i in rng]
    e_in = [e_in_seq[seq[i]][:, col[i]] for i in rng]
    e_neg = [e_neg_seq[seq[i]][:, col[i]] for i in rng]
    kk0 = [k_l[i] * kk_ref[:, col[i]] for i in rng]
    nrm2 = head_sum([kk0[i] * kk0[i] for i in rng])
    kk = [kk0[i] * lax.rsqrt(jnp.maximum(nrm2[i], 1e-24)) for i in rng]
    k2 = [k_l[i] * (1.0 + (a_l[i] - 1.0) * ka_ref[:, col[i]]) for i in rng]
    bt = [kk[i] * a_l[i] * e_neg[i] for i in rng]
    kt = [k2[i] * e_neg[i] for i in rng]
    left = [jnp.concatenate([-kk[i] * e_ex_seq[seq[i]][:, col[i]], r_l[i] * e_in[i]], axis=0).astype(BF16)
            for i in rng]
    right = [jnp.concatenate([bt[i], kt[i]], axis=0).astype(BF16) for i in rng]
    zed = [jnp.concatenate([expand(bt[i]), expand(kt[i])], axis=0) for i in rng]
    big = [_dot_nt(left[i], zed[i]) for i in rng]
    s_old = [s_ref[i] for i in rng]
    ls = [_dot_nt(left[i], s_old[i].astype(BF16)) for i in rng]
    v_bd = [expand(v_l[i]) for i in rng]
    pw = [jnp.where(strict, big[i][:C, :G * C], 0.0) for i in rng]
    tinv = [eye + pw[i] for i in rng]
    pw = [_dot(pw[i].astype(BF16), expand(pw[i])) for i in rng]
    a_kk = [jnp.concatenate([jnp.where(strict, big[i][:C, G * C:], 0.0),
                             jnp.where(incl, big[i][C:, G * C:], 0.0)], axis=0).astype(BF16) for i in rng]
    kv = [_dot(a_kk[i], v_bd[i]) for i in rng]
    x = [ls[i][:C] + kv[i][:C] for i in rng]
    span = 2
    while span < C:
        pw_bd = [expand(pw[i]) for i in rng]
        if 2 * span < C:
            both = [_dot(jnp.concatenate([pw[i], tinv[i]], axis=0).astype(BF16), pw_bd[i]) for i in rng]
            pw = [both[i][:C] for i in rng]
            tinv = [tinv[i] + both[i][C:] for i in rng]
        else:
            tinv = [tinv[i] + _dot(tinv[i].astype(BF16), pw_bd[i]) for i in rng]
        span *= 2
    u = [_dot(tinv[i].astype(BF16), expand(x[i])) for i in rng]
    y = [ls[i][C:] + kv[i][C:]
         + _dot(jnp.where(incl, big[i][C:, :G * C], 0.0).astype(BF16), expand(u[i])) for i in rng]
    uv = [jnp.concatenate([u[i], v_l[i]], axis=0).astype(BF16) for i in rng]
    for i in rng:
        s_ref[i] = ((s_old[i] + jnp.where(same_head, _dot_tn(uv[i], right[i]), 0.0))
                    * e_in[i][C - 1:C, :])
    ysum = head_sum(y)
    yc = [y[i] - ysum[i] * (1.0 / N) for i in rng]
    ysq = head_sum([yc[i] * yc[i] for i in rng])
    var = [ysq[i] * (1.0 / N) for i in rng]
    rksum = head_sum([r_l[i] * k2[i] * rk_ref[:, col[i]] for i in rng])
    bonus = [rksum[i] * v_l[i] for i in rng]
    for i in rng:
        yn = yc[i] * lax.rsqrt(var[i] + RWKV_LN_EPS) * lnw_ref[:, col[i]] + lnb_ref[:, col[i]]
        o_ref[seq[i], :, col[i]] = ((yn + bonus[i]) * g_seq[seq[i]][:, col[i]]).astype(o_ref.dtype)


def _rwkv(zr3, w0, wdu, a0, wau, wgu, k_k, k_a, r_k, ln_w, ln_b):
    B, T, _ = zr3.shape
    C = CHUNK
    W = RWKV_W
    GW = RWKV_GROUP * RWKV_N
    vec = lambda v: v.reshape(1, W)
    full = lambda shape: pl.BlockSpec(shape, lambda c: (0,) * len(shape))
    return pl.pallas_call(
        _rwkv_kernel,
        grid=(T // C,),
        in_specs=[pl.BlockSpec((B, C, W), lambda c: (0, c, 0)),
                  pl.BlockSpec((B, C, W), lambda c: (0, c, 1)),
                  pl.BlockSpec((B, C, W), lambda c: (0, c, 2)),
                  pl.BlockSpec((B, C, LORA_PAD), lambda c: (0, c, 3 * W // LORA_PAD)),
                  full((1, W)), full((DECAY_LORA, W)), full((1, W)), full((A_LORA, W)),
                  full((GATE_LORA, W)), full((1, W)), full((1, W)), full((1, W)),
                  full((1, W)), full((1, W))],
        out_specs=pl.BlockSpec((B, C, W), lambda c: (0, c, 0)),
        out_shape=jax.ShapeDtypeStruct((B, T, W), BF16),
        scratch_shapes=[pltpu.VMEM((B * RWKV_HEADS // RWKV_GROUP, GW, GW), F32)],
        compiler_params=pltpu.CompilerParams(dimension_semantics=("arbitrary",)),
        name="rwkv",
    )(zr3, zr3, zr3, zr3, vec(w0), wdu, vec(a0), wau, wgu, vec(k_k), vec(k_a), vec(r_k),
      vec(ln_w), vec(ln_b))


def _outproj_kernel(oa_ref, or_ref, x_ref, wo_ref, gt_ref, g_ref, sh_ref, sc_ref,
                    wrb_ref, wrh_ref, br_ref, x1_ref, h2_ref, lg_ref):
    mixed = _dot(oa_ref[...], wo_ref[0:Q_COLS, :]) + _dot(or_ref[...], wo_ref[Q_COLS:, :])
    x1 = x_ref[...] + gt_ref[0] * mixed
    x1_ref[...] = x1
    ms = jnp.mean(x1 * x1, axis=-1, keepdims=True)
    h2 = x1 * lax.rsqrt(ms + NORM_EPS) * g_ref[...] * (1.0 + sc_ref[0]) + sh_ref[0]
    half = h2.shape[1] // 2
    h2p = _bf16_pair_pack(h2[:, :half], h2[:, half:])
    for j in range(ROW_SLABS):
        w = h2_ref.shape[2]
        h2_ref[:, j, :] = h2p[:, j * w:(j + 1) * w]
    hh = h2.astype(BF16)
    hl = (h2 - hh.astype(F32)).astype(BF16)
    both = _dot(hh, wrb_ref[...])
    lg_ref[...] = both[:, :LANES] + both[:, LANES:] + _dot(hl, wrh_ref[...]) + br_ref[...]


def _outproj(oa2, or2, x2, wo_bf, gt, g2, sh, sc, wr_both, wr_hi, br, T):
    M, D = x2.shape
    tm = 256
    tps = T // tm
    bvec = pl.BlockSpec((1, 1, D), lambda i: (i // tps, 0, 0))
    vm = D * D * 2 + 2 * tm * D * (2 + 4 + 4 + 4) + tm * D * 16 + (8 << 20)
    return pl.pallas_call(
        _outproj_kernel,
        grid=(M // tm,),
        in_specs=[pl.BlockSpec((tm, Q_COLS), lambda i: (i, 0)),
                  pl.BlockSpec((tm, RWKV_W), lambda i: (i, 0)),
                  pl.BlockSpec((tm, D), lambda i: (i, 0)),
                  pl.BlockSpec((D, D), lambda i: (0, 0), pipeline_mode=pl.Buffered(1)),
                  bvec,
                  pl.BlockSpec((1, D), lambda i: (0, 0)),
                  bvec, bvec,
                  pl.BlockSpec((D, 2 * LANES), lambda i: (0, 0)),
                  pl.BlockSpec((D, LANES), lambda i: (0, 0)),
                  pl.BlockSpec((1, LANES), lambda i: (0, 0))],
        out_specs=[pl.BlockSpec((tm, D), lambda i: (i, 0)),
                   pl.BlockSpec((tm, ROW_SLABS, D // (2 * ROW_SLABS)), lambda i: (i, 0, 0)),
                   pl.BlockSpec((tm, LANES), lambda i: (i, 0))],
        out_shape=[jax.ShapeDtypeStruct((M, D), F32),
                   jax.ShapeDtypeStruct((M, ROW_SLABS, D // (2 * ROW_SLABS)), jnp.uint32),
                   jax.ShapeDtypeStruct((M, LANES), F32)],
        compiler_params=pltpu.CompilerParams(
            dimension_semantics=("arbitrary",), vmem_limit_bytes=_vmem_limit(vm)),
        name="outproj",
    )(oa2, or2, x2, wo_bf, gt, g2, sh, sc, wr_both, wr_hi, br)


def _route_kernel(lg_ref, info_ref, cnt_ref, run_ref):
    i = pl.program_id(0)
    tm = lg_ref.shape[0]

    @pl.when(i == 0)
    def _():
        run_ref[...] = jnp.zeros_like(run_ref)

    lg = lg_ref[...]
    lane = lax.broadcasted_iota(I32, lg.shape, 1)
    gl = jnp.where(lane < N_GROUPS, lg, NEG_BIG)
    gmax = jnp.max(gl, axis=-1, keepdims=True)
    gsum = jnp.sum(jnp.exp(gl - gmax), axis=-1, keepdims=True)
    g_gate = 1.0 / gsum
    g_idx = jnp.min(jnp.where(gl == gmax, lane, LANES), axis=-1, keepdims=True)
    lo = N_GROUPS + EXPERTS_PER_GROUP * g_idx
    el = jnp.where((lane >= lo) & (lane < lo + EXPERTS_PER_GROUP), lg, NEG_BIG)
    e1max = jnp.max(el, axis=-1, keepdims=True)
    l1 = jnp.min(jnp.where(el == e1max, lane, LANES), axis=-1, keepdims=True)
    el2 = jnp.where(lane == l1, NEG_BIG, el)
    e2max = jnp.max(el2, axis=-1, keepdims=True)
    l2 = jnp.min(jnp.where(el2 == e2max, lane, LANES), axis=-1, keepdims=True)
    t2 = jnp.exp(e2max - e1max)
    w1 = g_gate / (1.0 + t2)
    w2 = g_gate * t2 / (1.0 + t2)
    ex1 = l1 - N_GROUPS
    ex2 = l2 - N_GROUPS

    oh1 = jnp.where(lane == ex1, 1.0, 0.0)
    oh2 = jnp.where(lane == ex2, 1.0, 0.0)
    ti = lax.broadcasted_iota(I32, (tm, tm), 0)
    si = lax.broadcasted_iota(I32, (tm, tm), 1)
    lower = jnp.where(si < ti, 1.0, 0.0).astype(BF16)
    pre1 = _dot(lower, oh1.astype(BF16))
    pre2 = _dot(lower, oh2.astype(BF16))
    cnt1 = jnp.sum(oh1, axis=0, keepdims=True)
    cnt2 = jnp.sum(oh2, axis=0, keepdims=True)
    run = run_ref[...]
    rank1 = jnp.sum(oh1 * (pre1 + run), axis=-1, keepdims=True)
    rank2 = jnp.sum(oh2 * (pre2 + run + cnt1), axis=-1, keepdims=True)
    run = run + cnt1 + cnt2
    run_ref[...] = run
    cnt_ref[...] = run

    info = jnp.where(lane == 0, ex1.astype(F32), 0.0)
    info = jnp.where(lane == 1, ex2.astype(F32), info)
    info = jnp.where(lane == 2, rank1, info)
    info = jnp.where(lane == 3, rank2, info)
    info = jnp.where(lane == 4, w1, info)
    info = jnp.where(lane == 5, w2, info)
    info_ref[...] = info


def _route(lg):
    M = lg.shape[0]
    tm = 1024
    return pl.pallas_call(
        _route_kernel,
        grid=(M // tm,),
        in_specs=[pl.BlockSpec((tm, LANES), lambda i: (i, 0))],
        out_specs=[pl.BlockSpec((tm, LANES), lambda i: (i, 0)),
                   pl.BlockSpec((1, LANES), lambda i: (0, 0))],
        out_shape=[jax.ShapeDtypeStruct((M, LANES), F32),
                   jax.ShapeDtypeStruct((1, LANES), F32)],
        scratch_shapes=[pltpu.VMEM((1, LANES), F32)],
        compiler_params=pltpu.CompilerParams(dimension_semantics=("arbitrary",)),
        name="route",
    )(lg)


def _plan_kernel(info_ref, cnt_ref, slot_ref, meta_ref):
    cnt = cnt_ref[...]
    lane_r = lax.broadcasted_iota(I32, (1, LANES), 1)
    nblk = jnp.floor((cnt + (MOE_BLOCK - 1)) * (1.0 / MOE_BLOCK))
    ei = lax.broadcasted_iota(I32, (LANES, LANES), 0)
    ej = lax.broadcasted_iota(I32, (LANES, LANES), 1)
    upper = jnp.where(ei <= ej, 1.0, 0.0).astype(BF16)
    nb8 = jnp.broadcast_to(nblk, (8, LANES)).astype(BF16)
    bend = _dot(nb8, upper)[0:1, :]
    bstart = bend - nblk
    pstart = bstart * MOE_BLOCK

    info = info_ref[...]
    lane = lax.broadcasted_iota(I32, info.shape, 1)
    ex1 = info[:, 0:1].astype(I32)
    ex2 = info[:, 1:2].astype(I32)
    s1 = jnp.sum(jnp.where(lane == ex1, pstart, 0.0), axis=-1, keepdims=True) + info[:, 2:3]
    s2 = jnp.sum(jnp.where(lane == ex2, pstart, 0.0), axis=-1, keepdims=True) + info[:, 3:4]
    slot = jnp.where(lane == 0, s1, 0.0)
    slot = jnp.where(lane == 1, s2, slot)
    slot_ref[...] = slot.astype(I32)

    blk = lax.broadcasted_iota(I32, (LANES, 2 * LANES), 1).astype(F32)
    bend_col = jnp.sum(jnp.where(ei == ej, jnp.broadcast_to(bend, (LANES, LANES)), 0.0),
                       axis=-1, keepdims=True)
    erow = lax.broadcasted_iota(I32, (LANES, 2 * LANES), 0)
    hit = jnp.where((bend_col <= blk) & (erow < N_EXPERTS), 1.0, 0.0)
    bexp = jnp.minimum(jnp.sum(hit, axis=0, keepdims=True), N_EXPERTS - 1.0)
    n_used = jnp.max(jnp.where(lane_r < N_EXPERTS, bend, 0.0), axis=-1, keepdims=True)
    lastblk = jnp.where(nblk > 0, bend - 1.0, -1.0)
    r8 = lax.broadcasted_iota(I32, (8, 2 * LANES), 0)
    last2 = jnp.concatenate([lastblk, jnp.full((1, LANES), -1.0)], axis=1)
    meta = jnp.where(r8 == 0, jnp.broadcast_to(bexp, (8, 2 * LANES)), 0.0)
    meta = jnp.where(r8 == 1, jnp.broadcast_to(n_used, (8, 2 * LANES)), meta)
    meta = jnp.where(r8 == 2, jnp.broadcast_to(last2, (8, 2 * LANES)), meta)
    meta_ref[...] = meta.astype(I32)


def _plan(info, cnt):
    M = info.shape[0]
    tm = 1024
    return pl.pallas_call(
        _plan_kernel,
        grid=(M // tm,),
        in_specs=[pl.BlockSpec((tm, LANES), lambda i: (i, 0)),
                  pl.BlockSpec((1, LANES), lambda i: (0, 0))],
        out_specs=[pl.BlockSpec((tm, LANES), lambda i: (i, 0)),
                   pl.BlockSpec((8, 2 * LANES), lambda i: (0, 0))],
        out_shape=[jax.ShapeDtypeStruct((M, LANES), I32),
                   jax.ShapeDtypeStruct((8, 2 * LANES), I32)],
        compiler_params=pltpu.CompilerParams(dimension_semantics=("arbitrary",)),
        name="plan",
    )(info, cnt)


ROW_DEPTH = 3
DMA_GROUP = 16
WEIGHT_DEPTH = 3


def _slot_table_kernel(slots_ref, pad_ref, inv_ref, buf_ref, sem, *, n_assign):
    fill = pltpu.make_async_copy(pad_ref, buf_ref, sem)
    fill.start()
    fill.wait()

    def put(a, c):
        buf_ref[slots_ref[a] + MOE_BLOCK] = a
        return c

    lax.fori_loop(0, n_assign, put, 0, unroll=16)
    out = pltpu.make_async_copy(buf_ref, inv_ref, sem)
    out.start()
    out.wait()


def _slot_table(slots_km, n_blocks):
    n_assign = slots_km.shape[0]
    n_slots = (n_blocks + ROW_DEPTH) * MOE_BLOCK
    row = np.arange(n_slots)
    pad_ids = n_assign + ((row // MOE_BLOCK + ROW_DEPTH - 1) % ROW_DEPTH) * MOE_BLOCK + row % MOE_BLOCK
    gs = pltpu.PrefetchScalarGridSpec(
        num_scalar_prefetch=1, grid=(1,),
        in_specs=[pl.BlockSpec(memory_space=pl.ANY)],
        out_specs=pl.BlockSpec(memory_space=pl.ANY),
        scratch_shapes=[pltpu.SMEM((n_slots,), I32), pltpu.SemaphoreType.DMA(())])
    return pl.pallas_call(
        functools.partial(_slot_table_kernel, n_assign=n_assign),
        grid_spec=gs,
        out_shape=jax.ShapeDtypeStruct((n_slots,), I32),
        compiler_params=pltpu.CompilerParams(dimension_semantics=("arbitrary",)),
        name="slot_table",
    )(slots_km, jnp.asarray(pad_ids, I32))


def _experts_kernel(bexp_ref, seg_ref, inv_ref, h2_ref, w1_ref, w3_ref, w2_ref, y2_ref,
                    xbuf, ybuf, w1buf, w3buf, w2buf, sem_x, sem_y, sem_w, par_ref, *, n_assign):
    i = pl.program_id(0)
    n_used = seg_ref[N_EXPERTS]
    n_tok = n_assign // 2
    B = MOE_BLOCK
    FF = w1buf.shape[2]
    D = w1buf.shape[1]
    ND = ROW_DEPTH
    NW = w1buf.shape[0]

    def src_row(a):
        if n_tok & (n_tok - 1) == 0:
            return a & (n_tok - 1)
        return lax.rem(a, n_tok)

    def weight_copies(e, slot):
        return (pltpu.make_async_copy(w1_ref.at[e], w1buf.at[slot], sem_w.at[slot]),
                pltpu.make_async_copy(w3_ref.at[e], w3buf.at[slot], sem_w.at[slot]),
                pltpu.make_async_copy(w2_ref.at[e], w2buf.at[slot], sem_w.at[slot]))

    def start_weights(e, slot):
        c1, c3, c2 = weight_copies(e, slot)
        c1.start(priority=1)
        c3.start(priority=1)
        c2.start(priority=0)

    def prefetch_expert_after(e, hops, slot):
        ok = hops >= 0
        cur = e
        for _ in range(hops):
            nb = seg_ref[cur] + 1
            ok = jnp.logical_and(ok, nb < n_used)
            cur = bexp_ref[jnp.where(ok, nb, 0)]

        @pl.when(ok)
        def _():
            start_weights(cur, slot)

    def gather_group(b, g):
        slot = lax.rem(b + ND, ND)
        for r in range(g * DMA_GROUP, (g + 1) * DMA_GROUP):
            a = inv_ref[(b + 1) * B + r]
            pltpu.make_async_copy(h2_ref.at[pl.ds(src_row(a), 1)], xbuf.at[slot, pl.ds(r, 1)],
                                  sem_x.at[slot]).start()

    def scatter_group(b, g):
        slot = lax.rem(b + ND, ND)
        for r in range(g * DMA_GROUP, (g + 1) * DMA_GROUP):
            a = inv_ref[(b + 1) * B + r]
            pltpu.make_async_copy(ybuf.at[slot, pl.ds(r, 1)], y2_ref.at[pl.ds(a, 1)],
                                  sem_y.at[slot]).start()

    def wait_block(buf, sem, b):
        slot = lax.rem(b + ND, ND)
        pltpu.make_async_copy(buf.at[slot], buf.at[slot], sem.at[slot]).wait()

    n_groups = B // DMA_GROUP

    @pl.when(i == 0)
    def _():
        ybuf[...] = jnp.zeros_like(ybuf)
        for s in range(ND - 1):
            pltpu.make_async_copy(ybuf.at[s], y2_ref.at[pl.ds(n_assign + s * B, B)], sem_y.at[s]).start()
        par_ref[0] = 0
        e0 = bexp_ref[0]
        start_weights(e0, 0)
        for hops in range(1, NW):
            prefetch_expert_after(e0, hops, hops)
        for b in range(ND - 1):
            for g in range(n_groups):
                gather_group(b, g)

    active = i < n_used

    @pl.when(active)
    def _():
        e = bexp_ref[i]
        first = jnp.logical_or(i == 0, e != bexp_ref[jnp.maximum(i - 1, 0)])

        @pl.when(jnp.logical_and(first, i > 0))
        def _():
            par_ref[0] = par_ref[0] + 1

        @pl.when(first)
        def _():
            q = par_ref[0]
            for cp in weight_copies(e, lax.rem(q, NW)):
                cp.wait()

            @pl.when(i > 0)
            def _():
                prefetch_expert_after(e, NW - 1, lax.rem(q + NW - 1, NW))

        p = lax.rem(par_ref[0], NW)
        slot = lax.rem(i, ND)
        wait_block(xbuf, sem_x, i)
        wait_block(ybuf, sem_y, i)
        pending = [functools.partial(scatter_group, i - 1, g) for g in range(n_groups)]
        pending += [functools.partial(gather_group, i + ND - 1, g) for g in range(n_groups)]
        n_chunks = 2 * (FF // 256) + ROW_SLABS
        for _ in range(max(len(pending) - n_chunks, 0)):
            pending.pop(0)()

        def after_chunk():
            if pending:
                pending.pop(0)()

        xa, xb = _bf16_pair_unpack(jnp.concatenate([xbuf[slot, :, j, :] for j in range(ROW_SLABS)], axis=1))
        x = jnp.concatenate([xa, xb], axis=1).astype(BF16)
        sw = ybuf.shape[3]
        h1c, h3c = [], []
        for c0 in range(0, FF, 256):
            h1c.append(_dot(x, w1buf[p, :, c0:c0 + 256]))
            after_chunk()
            h3c.append(_dot(x, w3buf[p, :, c0:c0 + 256]))
            after_chunk()
        h1 = jnp.concatenate(h1c, axis=1)
        h3 = jnp.concatenate(h3c, axis=1)
        hid = (h1 * (1.0 / (1.0 + jnp.exp(-h1))) * h3).astype(BF16)
        for j in range(0, ROW_SLABS, 2):
            c0 = j * sw
            ya = _dot(hid, w2buf[p, :, c0:c0 + 2 * sw])
            after_chunk()
            yb = _dot(hid, w2buf[p, :, D // 2 + c0:D // 2 + c0 + 2 * sw])
            yp = _bf16_pair_pack(ya, yb)
            ybuf[slot, :, j, :] = yp[:, :sw]
            ybuf[slot, :, j + 1, :] = yp[:, sw:]
            after_chunk()
        while pending:
            pending.pop(0)()

    @pl.when(i == n_used)
    def _():
        wait_block(xbuf, sem_x, i)
        wait_block(ybuf, sem_y, i)
        for g in range(n_groups):
            scatter_group(i - 1, g)

    @pl.when(jnp.logical_and(i > n_used, i < n_used + ND - 1))
    def _():
        wait_block(xbuf, sem_x, i)
        wait_block(ybuf, sem_y, i)

    @pl.when(i == n_used + ND - 1)
    def _():
        wait_block(ybuf, sem_y, i)


def _experts(bexp, seg_meta, inv, h2, w1, w3, w2, n_blocks):
    M, DJ, SW = h2.shape
    D = 2 * DJ * SW
    FF = w1.shape[2]
    n_assign = 2 * M
    any_spec = pl.BlockSpec(memory_space=pl.ANY)
    row_buf = pltpu.VMEM((ROW_DEPTH, MOE_BLOCK, DJ, SW), jnp.uint32)
    gs = pltpu.PrefetchScalarGridSpec(
        num_scalar_prefetch=3,
        grid=(n_blocks + ROW_DEPTH,),
        in_specs=[any_spec, any_spec, any_spec, any_spec],
        out_specs=any_spec,
        scratch_shapes=[row_buf, row_buf,
                        pltpu.VMEM((WEIGHT_DEPTH, D, FF), F32), pltpu.VMEM((WEIGHT_DEPTH, D, FF), F32),
                        pltpu.VMEM((WEIGHT_DEPTH, FF, D), F32),
                        pltpu.SemaphoreType.DMA((ROW_DEPTH,)), pltpu.SemaphoreType.DMA((ROW_DEPTH,)),
                        pltpu.SemaphoreType.DMA((WEIGHT_DEPTH,)), pltpu.SMEM((1,), I32)],
    )
    vm = WEIGHT_DEPTH * 3 * D * FF * 4 + 2 * ROW_DEPTH * MOE_BLOCK * D * 4 + (12 << 20)
    return pl.pallas_call(
        functools.partial(_experts_kernel, n_assign=n_assign),
        grid_spec=gs,
        out_shape=jax.ShapeDtypeStruct((n_assign + ROW_DEPTH * MOE_BLOCK, DJ, SW), jnp.uint32),
        compiler_params=pltpu.CompilerParams(
            dimension_semantics=("arbitrary",), vmem_limit_bytes=_vmem_limit(vm),
            has_side_effects=True),
        name="experts",
    )(bexp, seg_meta, inv, h2, w1, w3, w2)


def _combine_kernel(x1_ref, info_ref, gt_ref, fg_ref, ya_ref, yb_ref, o_ref):
    info = info_ref[...]
    nj = ya_ref.shape[1]
    ya = jnp.concatenate(_bf16_pair_unpack(jnp.concatenate([ya_ref[:, j, :] for j in range(nj)], axis=1)), axis=1)
    yb = jnp.concatenate(_bf16_pair_unpack(jnp.concatenate([yb_ref[:, j, :] for j in range(nj)], axis=1)), axis=1)
    moe = info[:, 4:5] * ya + info[:, 5:6] * yb
    x = x1_ref[...] + gt_ref[0] * moe
    ms = jnp.mean(x * x, axis=-1, keepdims=True)
    o_ref[...] = x * lax.rsqrt(ms + NORM_EPS) * fg_ref[...]


def _combine(x1, info, gt, final_g, y2, T):
    M, D = x1.shape
    tm = 512
    tps = T // tm
    nt = M // tm
    vm = 2 * 4 * tm * D * 4 + tm * D * 16 + (8 << 20)
    return pl.pallas_call(
        _combine_kernel,
        grid=(nt,),
        in_specs=[pl.BlockSpec((tm, D), lambda i: (i, 0)),
                  pl.BlockSpec((tm, LANES), lambda i: (i, 0)),
                  pl.BlockSpec((1, 1, D), lambda i: (i // tps, 0, 0)),
                  pl.BlockSpec((1, D), lambda i: (0, 0)),
                  pl.BlockSpec((tm, ROW_SLABS, D // (2 * ROW_SLABS)), lambda i: (i, 0, 0)),
                  pl.BlockSpec((tm, ROW_SLABS, D // (2 * ROW_SLABS)), lambda i: (nt + i, 0, 0))],
        out_specs=pl.BlockSpec((tm, D), lambda i: (i, 0)),
        out_shape=jax.ShapeDtypeStruct((M, D), F32),
        compiler_params=pltpu.CompilerParams(
            dimension_semantics=("arbitrary",), vmem_limit_bytes=_vmem_limit(vm)),
        name="combine",
    )(x1, info, gt, final_g.reshape(1, D), y2, y2)


def _rope_tables(T):
    inv_freq = ROPE_THETA ** (-np.arange(0, HEAD_DIM, 2, dtype=np.float64) / HEAD_DIM)
    ang = np.arange(T, dtype=np.float64)[:, None] * inv_freq[None, :]
    cos, sin = np.cos(ang), np.sin(ang)
    cos_h = np.concatenate([cos, cos], axis=1)
    sin_h = np.concatenate([-sin, sin], axis=1)
    return (jnp.asarray(np.tile(cos_h, (1, 2)), F32), jnp.asarray(np.tile(sin_h, (1, 2)), F32))


def _layer(x, c, w_ada, b_ada, norm1_g, w_in, mu_shift, sinks, w0, w_decay_up, a0, w_a_up, w_g_up,
           k_k, k_a, r_k, ln_x_w, ln_x_b, w_o, norm2_g, w_rg, b_rg, w_re, b_re, w1, w3, w2, final_g):
    B, T, D = x.shape
    M = B * T
    mod = _ada(c, w_ada, b_ada)
    sh1, sc1, gt1, sh2, sc2, gt2 = [m.reshape(B, 1, D) for m in jnp.split(mod, 6, axis=-1)]

    pad = RWKV_PAD - RWKV_COLS
    w_bf = _cast_pad(jnp.swapaxes(w_in, 0, 1), ATTN_COLS + RWKV_PAD)
    mu_pad = jnp.pad(mu_shift, (0, pad)).reshape(1, RWKV_PAD)
    cos_t, sin_t = _rope_tables(T)
    x2 = x.reshape(M, D)
    za, zr = _inproj(x2, norm1_g.reshape(1, D), sh1, sc1, w_bf, mu_pad, cos_t, sin_t, T)

    o_attn = _attention(za.reshape(B, T, ATTN_COLS), sinks)
    o_rwkv = _rwkv(zr.reshape(B, T, RWKV_PAD), w0, w_decay_up, a0, w_a_up, w_g_up, k_k, k_a,
                   r_k.reshape(-1), ln_x_w, ln_x_b)

    wr = jnp.concatenate([w_rg, w_re], axis=1)
    wr = jnp.pad(wr, ((0, 0), (0, LANES - wr.shape[1])))
    wr_hi = wr.astype(BF16)
    wr_lo = (wr - wr_hi.astype(F32)).astype(BF16)
    br = jnp.pad(jnp.concatenate([b_rg, b_re]), (0, LANES - N_GROUPS - N_EXPERTS)).reshape(1, LANES)
    x1, h2, lg = _outproj(o_attn.reshape(M, Q_COLS), o_rwkv.reshape(M, RWKV_W), x2, w_o.astype(BF16),
                          gt1, norm2_g.reshape(1, D), sh2, sc2, jnp.concatenate([wr_hi, wr_lo], axis=1),
                          wr_hi, br, T)

    n_blocks = -(-(2 * M) // MOE_BLOCK) + N_EXPERTS
    info, cnt = _route(lg)
    slots, meta = _plan(info, cnt)
    slots_km = slots[:, :2].T.reshape(-1)
    seg_meta = jnp.concatenate([meta[2, :N_EXPERTS], meta[1, :1]])
    inv = _slot_table(slots_km, n_blocks)
    y2 = _experts(meta[0, :n_blocks + ROW_DEPTH], seg_meta, inv, h2, w1, w3, w2, n_blocks)
    out = _combine(x1, info, gt2, final_g, y2, T)
    return out.reshape(B, T, D)


def kernel(x, c, w_ada, b_ada, norm1_g, w_in, mu_shift, sinks, w0, w_decay_up, a0, w_a_up, w_g_up, k_k, k_a, r_k, ln_x_w, ln_x_b, w_o, norm2_g, w_router_group, b_router_group, w_router_expert, b_router_expert, w1, w3, w2, final_g):
    depth = w_ada.shape[0]
    assert depth == 1, "single-layer stack"
    l = 0
    return _layer(x, c, w_ada[l], b_ada[l], norm1_g[l], w_in[l], mu_shift[l], sinks[l], w0[l],
                  w_decay_up[l], a0[l], w_a_up[l], w_g_up[l], k_k[l], k_a[l], r_k[l], ln_x_w[l],
                  ln_x_b[l], w_o[l], norm2_g[l], w_router_group[l], b_router_group[l],
                  w_router_expert[l], b_router_expert[l], w1[l], w3[l], w2[l], final_g)
```

```python
import functools
import math

import jax
import jax.numpy as jnp
import numpy as np
from jax import lax
from jax.experimental import pallas as pl
from jax.experimental.pallas import tpu as pltpu

F32 = jnp.float32
BF16 = jnp.bfloat16
I32 = jnp.int32

LANES = 128
VMEM_BYTES_V7X = 64 * 1024 * 1024

HEAD_DIM = 64
ATTN_HEADS = 16
ATTN_KV_HEADS = 2
ATTN_GROUP = ATTN_HEADS // ATTN_KV_HEADS
WINDOW = 128
ROPE_THETA = 10000.0
RWKV_HEADS = 16
RWKV_N = 64
DECAY_LORA = 64
A_LORA = 64
GATE_LORA = 160
RWKV_LN_EPS = 64e-5
N_GROUPS = 8
EXPERTS_PER_GROUP = 8
N_EXPERTS = N_GROUPS * EXPERTS_PER_GROUP
EXPERT_FF = 512
MOE_BLOCK = 128
ROW_SLABS = 8
NORM_EPS = 1e-6

Q_COLS = ATTN_HEADS * HEAD_DIM
KV_COLS = ATTN_KV_HEADS * HEAD_DIM
ATTN_COLS = Q_COLS + 2 * KV_COLS
RWKV_W = RWKV_HEADS * RWKV_N
LORA_COLS = DECAY_LORA + A_LORA + GATE_LORA
LORA_PAD = 384
RWKV_COLS = 3 * RWKV_W + LORA_COLS
RWKV_PAD = 3 * RWKV_W + LORA_PAD
CHUNK = 64
RWKV_GROUP = 4
NEG_BIG = -1e30


def _vmem_limit(nbytes):
    return int(min(nbytes, VMEM_BYTES_V7X - 4 * 1024 * 1024))


def _dot(a, b):
    return jnp.dot(a, b, preferred_element_type=F32)


def _dot_nt(a, b):
    return lax.dot_general(a, b, (((1,), (1,)), ((), ())), preferred_element_type=F32)


def _dot_tn(a, b):
    return lax.dot_general(a, b, (((0,), (0,)), ((), ())), preferred_element_type=F32)


def _bf16_pair_pack(a, b):
    def hi16(x):
        return pltpu.bitcast(x.astype(BF16).astype(F32), jnp.uint32)
    return hi16(a) | (hi16(b) >> 16)


def _bf16_pair_unpack(w):
    return (pltpu.bitcast(w & jnp.uint32(0xFFFF0000), F32), pltpu.bitcast(w << 16, F32))


def _ada_kernel(c_ref, w_ref, b_ref, o_ref):
    c = c_ref[...]
    s = c * (1.0 / (1.0 + jnp.exp(-c)))
    o_ref[...] = _dot(s, w_ref[...]) + b_ref[...]


def _ada(c, w_ada, b_ada):
    B, D = c.shape
    N = w_ada.shape[1]
    tn = 1024
    cp = jnp.zeros((8, D), F32).at[:B].set(c)
    out = pl.pallas_call(
        _ada_kernel,
        grid=(N // tn,),
        in_specs=[pl.BlockSpec((8, D), lambda j: (0, 0)),
                  pl.BlockSpec((D, tn), lambda j: (0, j)),
                  pl.BlockSpec((1, tn), lambda j: (0, j))],
        out_specs=pl.BlockSpec((8, tn), lambda j: (0, j)),
        out_shape=jax.ShapeDtypeStruct((8, N), F32),
        compiler_params=pltpu.CompilerParams(
            dimension_semantics=("arbitrary",),
            vmem_limit_bytes=_vmem_limit(2 * D * tn * 4 + (8 << 20))),
        name="ada",
    )(cp, w_ada, b_ada.reshape(1, N))
    return out[:B]


def _cast_pad_kernel(wt_ref, o_ref, *, n_valid):
    j = pl.program_id(0)
    tn = wt_ref.shape[0]
    row = j * tn + lax.broadcasted_iota(I32, wt_ref.shape, 0)
    wt = jnp.where(row < n_valid, wt_ref[...], 0.0)
    o_ref[...] = wt.T.astype(o_ref.dtype)


def _cast_pad(w_t, n_out):
    N, K = w_t.shape
    tn = 512
    return pl.pallas_call(
        functools.partial(_cast_pad_kernel, n_valid=N),
        grid=(pl.cdiv(n_out, tn),),
        in_specs=[pl.BlockSpec((tn, K), lambda j: (j, 0))],
        out_specs=pl.BlockSpec((K, tn), lambda j: (0, j)),
        out_shape=jax.ShapeDtypeStruct((K, n_out), BF16),
        compiler_params=pltpu.CompilerParams(dimension_semantics=("arbitrary",)),
        name="cast_pad",
    )(w_t)


def _rope(z, cos, sin):
    w = z.shape[-1]
    lane = lax.broadcasted_iota(I32, z.shape, 1)
    first_half = (lane % HEAD_DIM) < (HEAD_DIM // 2)
    partner = jnp.where(first_half, pltpu.roll(z, w - HEAD_DIM // 2, 1), pltpu.roll(z, HEAD_DIM // 2, 1))
    return z * cos + partner * sin


def _inproj_kernel(x_ref, g_ref, sh_ref, sc_ref, w_ref, mu_ref, cos_ref, sin_ref,
                   za_ref, zr_ref, carry_ref, *, tiles_per_seq):
    i = pl.program_id(0)
    tm = x_ref.shape[0]
    x = x_ref[...]
    ms = jnp.mean(x * x, axis=-1, keepdims=True)
    y = x * lax.rsqrt(ms + NORM_EPS) * g_ref[...]
    h = (y * (1.0 + sc_ref[0]) + sh_ref[0]).astype(BF16)

    cos2 = jnp.concatenate([cos_ref[...], cos_ref[...]], axis=1)
    sin2 = jnp.concatenate([sin_ref[...], sin_ref[...]], axis=1)
    for c0 in range(0, ATTN_COLS, 256):
        z = _dot(h, w_ref[:, c0:c0 + 256])
        if c0 < Q_COLS:
            z = _rope(z, cos2, sin2)
        else:
            z = jnp.concatenate([_rope(z[:, :KV_COLS], cos_ref[...], sin_ref[...]), z[:, KV_COLS:]], axis=1)
        za_ref[:, c0:c0 + 256] = z.astype(za_ref.dtype)

    first = (i % tiles_per_seq) == 0
    row = lax.broadcasted_iota(I32, (tm, 1), 0)
    c0 = 0
    while c0 < RWKV_PAD:
        n = min(512, RWKV_PAD - c0)
        z = _dot(h, w_ref[:, ATTN_COLS + c0:ATTN_COLS + c0 + n])
        prev_last = jnp.where(first, 0.0, carry_ref[7:8, c0:c0 + n])
        z_prev = jnp.where(row == 0, prev_last, pltpu.roll(z, 1, 0))
        carry_ref[:, c0:c0 + n] = z[tm - 8:tm, :]
        zr_ref[:, c0:c0 + n] = z + (z_prev - z) * mu_ref[:, c0:c0 + n]
        c0 += n


def _inproj(x2, g, sh, sc, w_bf, mu_pad, cos_t, sin_t, T):
    M, D = x2.shape
    tm = 256
    tps = T // tm
    NW = w_bf.shape[1]
    kern = functools.partial(_inproj_kernel, tiles_per_seq=tps)
    vm = (D * NW * 2 + 2 * tm * D * 4 + 2 * tm * ATTN_COLS * 2 + 2 * tm * RWKV_PAD * 4
          + 4 * tm * 512 * 4 + tm * D * 8 + (8 << 20))
    return pl.pallas_call(
        kern,
        grid=(M // tm,),
        in_specs=[pl.BlockSpec((tm, D), lambda i: (i, 0)),
                  pl.BlockSpec((1, D), lambda i: (0, 0)),
                  pl.BlockSpec((1, 1, D), lambda i: (i // tps, 0, 0)),
                  pl.BlockSpec((1, 1, D), lambda i: (i // tps, 0, 0)),
                  pl.BlockSpec((D, NW), lambda i: (0, 0), pipeline_mode=pl.Buffered(1)),
                  pl.BlockSpec((1, RWKV_PAD), lambda i: (0, 0)),
                  pl.BlockSpec((tm, 2 * HEAD_DIM), lambda i: (i % tps, 0)),
                  pl.BlockSpec((tm, 2 * HEAD_DIM), lambda i: (i % tps, 0))],
        out_specs=[pl.BlockSpec((tm, ATTN_COLS), lambda i: (i, 0)),
                   pl.BlockSpec((tm, RWKV_PAD), lambda i: (i, 0))],
        out_shape=[jax.ShapeDtypeStruct((M, ATTN_COLS), BF16),
                   jax.ShapeDtypeStruct((M, RWKV_PAD), F32)],
        scratch_shapes=[pltpu.VMEM((8, RWKV_PAD), F32)],
        compiler_params=pltpu.CompilerParams(
            dimension_semantics=("arbitrary",), vmem_limit_bytes=_vmem_limit(vm)),
        name="inproj",
    )(x2, g, sh, sc, w_bf, mu_pad, cos_t, sin_t)


def _attn_kernel(sink_ref, q_ref, kc_ref, kp_ref, vc_ref, vp_ref, o_ref):
    n = pl.program_id(1)
    blk = q_ref.shape[1]
    row = lax.broadcasted_iota(I32, (blk, blk), 0)
    col = lax.broadcasted_iota(I32, (blk, blk), 1)
    mask = jnp.concatenate([(col > row) & (n > 0), col <= row], axis=1)
    scale = 1.0 / math.sqrt(HEAD_DIM)
    outs = []
    for kvh in range(ATTN_KV_HEADS):
        ks = slice(kvh * HEAD_DIM, (kvh + 1) * HEAD_DIM)
        kmat = jnp.concatenate([kp_ref[0, :, ks], kc_ref[0, :, ks]], axis=0)
        vmat = jnp.concatenate([vp_ref[0, :, ks], vc_ref[0, :, ks]], axis=0)
        for g in range(ATTN_GROUP):
            hd = kvh * ATTN_GROUP + g
            qh = q_ref[0, :, hd * HEAD_DIM:(hd + 1) * HEAD_DIM]
            s = _dot_nt(qh, kmat) * scale
            s = jnp.where(mask, s, NEG_BIG)
            sink = sink_ref[hd]
            m = jnp.maximum(jnp.max(s, axis=-1, keepdims=True), sink)
            p = jnp.exp(s - m)
            denom = jnp.sum(p, axis=-1, keepdims=True) + jnp.exp(sink - m)
            o = _dot(p.astype(BF16), vmat)
            outs.append(o / denom)
    o_ref[0] = jnp.concatenate(outs, axis=1).astype(o_ref.dtype)


def _attention(za3, sinks):
    B, T, _ = za3.shape
    nb = T // WINDOW
    kcol = Q_COLS // KV_COLS
    prev = lambda b, n, s: (b, jnp.maximum(n - 1, 0), kcol)
    prev_v = lambda b, n, s: (b, jnp.maximum(n - 1, 0), kcol + 1)
    gs = pltpu.PrefetchScalarGridSpec(
        num_scalar_prefetch=1,
        grid=(B, nb),
        in_specs=[pl.BlockSpec((1, WINDOW, Q_COLS), lambda b, n, s: (b, n, 0)),
                  pl.BlockSpec((1, WINDOW, KV_COLS), lambda b, n, s: (b, n, kcol)),
                  pl.BlockSpec((1, WINDOW, KV_COLS), prev),
                  pl.BlockSpec((1, WINDOW, KV_COLS), lambda b, n, s: (b, n, kcol + 1)),
                  pl.BlockSpec((1, WINDOW, KV_COLS), prev_v)],
        out_specs=pl.BlockSpec((1, WINDOW, Q_COLS), lambda b, n, s: (b, n, 0)),
    )
    return pl.pallas_call(
        _attn_kernel,
        grid_spec=gs,
        out_shape=jax.ShapeDtypeStruct((B, T, Q_COLS), BF16),
        compiler_params=pltpu.CompilerParams(dimension_semantics=("arbitrary", "arbitrary")),
        name="attn",
    )(sinks, za3, za3, za3, za3, za3)


def _rwkv_kernel(r_ref, k_ref, v_ref, lora_ref, w0_ref, wdu_ref, a0_ref, wau_ref, wgu_ref,
                 kk_ref, ka_ref, rk_ref, lnw_ref, lnb_ref, o_ref, s_ref):
    c = pl.program_id(0)
    nseq = r_ref.shape[0]
    C = r_ref.shape[1]
    N = RWKV_N
    G = RWKV_GROUP
    GW = G * N
    ng = RWKV_HEADS // G

    @pl.when(c == 0)
    def _():
        s_ref[...] = jnp.zeros_like(s_ref)

    ti = lax.broadcasted_iota(I32, (C, C), 0)
    si = lax.broadcasted_iota(I32, (C, C), 1)
    tril = jnp.where(si <= ti, 1.0, 0.0).astype(BF16)
    a_seq, g_seq, e_in_seq, e_ex_seq, e_neg_seq = [], [], [], [], []
    for b in range(nseq):
        lora = lora_ref[b]
        wd = lora[:, 0:DECAY_LORA]
        ad = lora[:, DECAY_LORA:DECAY_LORA + A_LORA]
        gd = lora[:, DECAY_LORA + A_LORA:LORA_COLS]
        wlin = w0_ref[...] + _dot(jnp.tanh(wd), wdu_ref[...])
        neg = -wlin
        softplus = jnp.maximum(neg, 0.0) + jnp.log(1.0 + jnp.exp(-jnp.abs(neg)))
        w = -softplus - 0.5
        logdec = -jnp.exp(w)
        a_seq.append(1.0 / (1.0 + jnp.exp(-(a0_ref[...] + _dot(ad, wau_ref[...])))))
        g_seq.append(_dot(1.0 / (1.0 + jnp.exp(-gd)), wgu_ref[...]))
        ld_hi = logdec.astype(BF16)
        ld_lo = (logdec - ld_hi.astype(F32)).astype(BF16)
        cum = _dot(tril, ld_hi) + _dot(tril, ld_lo)
        e_in_seq.append(jnp.exp(cum))
        e_ex_seq.append(jnp.exp(cum - logdec))
        e_neg_seq.append(jnp.exp(-cum))

    lane_head = lax.broadcasted_iota(I32, (1, GW), 1) // N
    t_row = lax.broadcasted_iota(I32, (C, GW), 0)
    s_lane = lax.broadcasted_iota(I32, (C, GW), 1) % N
    strict = s_lane < t_row
    incl = s_lane <= t_row
    bi = lax.broadcasted_iota(I32, (GW, GW), 0)
    bj = lax.broadcasted_iota(I32, (GW, GW), 1)
    same_head = (bi // N) == (bj // N)
    ones_bd = jnp.where(same_head, 1.0, 0.0).astype(BF16)
    eye = jnp.where(s_lane == t_row, 1.0, 0.0)

    def expand(xc):
        return jnp.concatenate([jnp.where(lane_head == h, xc, 0.0) for h in range(G)], axis=0).astype(BF16)

    def head_sum(xs):
        s = _dot(jnp.concatenate(xs, axis=0).astype(BF16), ones_bd)
        return [s[i * C:(i + 1) * C] for i in range(len(xs))]

    seq = [b for b in range(nseq) for _ in range(ng)]
    col = [slice(gi * GW, (gi + 1) * GW) for _ in range(nseq) for gi in range(ng)]
    rng = range(nseq * ng)
    r_l = [r_ref[seq[i], :, col[i]] for i in rng]
    k_l = [k_ref[seq[i], :, col[i]] for i in rng]
    v_l = [v_ref[seq[i], :, col[i]] for i in rng]
    a_l = [a_seq[seq[i]][:, col[i]] for i in rng]
    e_in = [e_in_seq[seq[i]][:, col[i]] for i in rng]
    e_neg = [e_neg_seq[seq[i]][:, col[i]] for i in rng]
    kk0 = [k_l[i] * kk_ref[:, col[i]] for i in rng]
    nrm2 = head_sum([kk0[i] * kk0[i] for i in rng])
    kk = [kk0[i] * lax.rsqrt(jnp.maximum(nrm2[i], 1e-24)) for i in rng]
    k2 = [k_l[i] * (1.0 + (a_l[i] - 1.0) * ka_ref[:, col[i]]) for i in rng]
    bt = [kk[i] * a_l[i] * e_neg[i] for i in rng]
    kt = [k2[i] * e_neg[i] for i in rng]
    left = [jnp.concatenate([-kk[i] * e_ex_seq[seq[i]][:, col[i]], r_l[i] * e_in[i]], axis=0).astype(BF16)
            for i in rng]
    right = [jnp.concatenate([bt[i], kt[i]], axis=0).astype(BF16) for i in rng]
    zed = [jnp.concatenate([expand(bt[i]), expand(kt[i])], axis=0) for i in rng]
    big = [_dot_nt(left[i], zed[i]) for i in rng]
    s_old = [s_ref[i] for i in rng]
    ls = [_dot_nt(left[i], s_old[i].astype(BF16)) for i in rng]
    v_bd = [expand(v_l[i]) for i in rng]
    pw = [jnp.where(strict, big[i][:C, :G * C], 0.0) for i in rng]
    tinv = [eye + pw[i] for i in rng]
    pw = [_dot(pw[i].astype(BF16), expand(pw[i])) for i in rng]
    a_kk = [jnp.concatenate([jnp.where(strict, big[i][:C, G * C:], 0.0),
                             jnp.where(incl, big[i][C:, G * C:], 0.0)], axis=0).astype(BF16) for i in rng]
    kv = [_dot(a_kk[i], v_bd[i]) for i in rng]
    x = [ls[i][:C] + kv[i][:C] for i in rng]
    span = 2
    while span < C:
        pw_bd = [expand(pw[i]) for i in rng]
        if 2 * span < C:
            both = [_dot(jnp.concatenate([pw[i], tinv[i]], axis=0).astype(BF16), pw_bd[i]) for i in rng]
            pw = [both[i][:C] for i in rng]
            tinv = [tinv[i] + both[i][C:] for i in rng]
        else:
            tinv = [tinv[i] + _dot(tinv[i].astype(BF16), pw_bd[i]) for i in rng]
        span *= 2
    u = [_dot(tinv[i].astype(BF16), expand(x[i])) for i in rng]
    y = [ls[i][C:] + kv[i][C:]
         + _dot(jnp.where(incl, big[i][C:, :G * C], 0.0).astype(BF16), expand(u[i])) for i in rng]
    uv = [jnp.concatenate([u[i], v_l[i]], axis=0).astype(BF16) for i in rng]
    for i in rng:
        s_ref[i] = ((s_old[i] + jnp.where(same_head, _dot_tn(uv[i], right[i]), 0.0))
                    * e_in[i][C - 1:C, :])
    ysum = head_sum(y)
    yc = [y[i] - ysum[i] * (1.0 / N) for i in rng]
    ysq = head_sum([yc[i] * yc[i] for i in rng])
    var = [ysq[i] * (1.0 / N) for i in rng]
    rksum = head_sum([r_l[i] * k2[i] * rk_ref[:, col[i]] for i in rng])
    bonus = [rksum[i] * v_l[i] for i in rng]
    for i in rng:
        yn = yc[i] * lax.rsqrt(var[i] + RWKV_LN_EPS) * lnw_ref[:, col[i]] + lnb_ref[:, col[i]]
        o_ref[seq[i], :, col[i]] = ((yn + bonus[i]) * g_seq[seq[i]][:, col[i]]).astype(o_ref.dtype)


def _rwkv(zr3, w0, wdu, a0, wau, wgu, k_k, k_a, r_k, ln_w, ln_b):
    B, T, _ = zr3.shape
    C = CHUNK
    W = RWKV_W
    GW = RWKV_GROUP * RWKV_N
    vec = lambda v: v.reshape(1, W)
    full = lambda shape: pl.BlockSpec(shape, lambda c: (0,) * len(shape))
    return pl.pallas_call(
        _rwkv_kernel,
        grid=(T // C,),
        in_specs=[pl.BlockSpec((B, C, W), lambda c: (0, c, 0)),
                  pl.BlockSpec((B, C, W), lambda c: (0, c, 1)),
                  pl.BlockSpec((B, C, W), lambda c: (0, c, 2)),
                  pl.BlockSpec((B, C, LORA_PAD), lambda c: (0, c, 3 * W // LORA_PAD)),
                  full((1, W)), full((DECAY_LORA, W)), full((1, W)), full((A_LORA, W)),
                  full((GATE_LORA, W)), full((1, W)), full((1, W)), full((1, W)),
                  full((1, W)), full((1, W))],
        out_specs=pl.BlockSpec((B, C, W), lambda c: (0, c, 0)),
        out_shape=jax.ShapeDtypeStruct((B, T, W), BF16),
        scratch_shapes=[pltpu.VMEM((B * RWKV_HEADS // RWKV_GROUP, GW, GW), F32)],
        compiler_params=pltpu.CompilerParams(dimension_semantics=("arbitrary",)),
        name="rwkv",
    )(zr3, zr3, zr3, zr3, vec(w0), wdu, vec(a0), wau, wgu, vec(k_k), vec(k_a), vec(r_k),
      vec(ln_w), vec(ln_b))


def _outproj_kernel(oa_ref, or_ref, x_ref, wo_ref, gt_ref, g_ref, sh_ref, sc_ref,
                    wrb_ref, wrh_ref, br_ref, x1_ref, h2_ref, lg_ref):
    mixed = _dot(oa_ref[...], wo_ref[0:Q_COLS, :]) + _dot(or_ref[...], wo_ref[Q_COLS:, :])
    x1 = x_ref[...] + gt_ref[0] * mixed
    x1_ref[...] = x1
    ms = jnp.mean(x1 * x1, axis=-1, keepdims=True)
    h2 = x1 * lax.rsqrt(ms + NORM_EPS) * g_ref[...] * (1.0 + sc_ref[0]) + sh_ref[0]
    half = h2.shape[1] // 2
    h2p = _bf16_pair_pack(h2[:, :half], h2[:, half:])
    for j in range(ROW_SLABS):
        w = h2_ref.shape[2]
        h2_ref[:, j, :] = h2p[:, j * w:(j + 1) * w]
    hh = h2.astype(BF16)
    hl = (h2 - hh.astype(F32)).astype(BF16)
    both = _dot(hh, wrb_ref[...])
    lg_ref[...] = both[:, :LANES] + both[:, LANES:] + _dot(hl, wrh_ref[...]) + br_ref[...]


def _outproj(oa2, or2, x2, wo_bf, gt, g2, sh, sc, wr_both, wr_hi, br, T):
    M, D = x2.shape
    tm = 256
    tps = T // tm
    bvec = pl.BlockSpec((1, 1, D), lambda i: (i // tps, 0, 0))
    vm = D * D * 2 + 2 * tm * D * (2 + 4 + 4 + 4) + tm * D * 16 + (8 << 20)
    return pl.pallas_call(
        _outproj_kernel,
        grid=(M // tm,),
        in_specs=[pl.BlockSpec((tm, Q_COLS), lambda i: (i, 0)),
                  pl.BlockSpec((tm, RWKV_W), lambda i: (i, 0)),
                  pl.BlockSpec((tm, D), lambda i: (i, 0)),
                  pl.BlockSpec((D, D), lambda i: (0, 0), pipeline_mode=pl.Buffered(1)),
                  bvec,
                  pl.BlockSpec((1, D), lambda i: (0, 0)),
                  bvec, bvec,
                  pl.BlockSpec((D, 2 * LANES), lambda i: (0, 0)),
                  pl.BlockSpec((D, LANES), lambda i: (0, 0)),
                  pl.BlockSpec((1, LANES), lambda i: (0, 0))],
        out_specs=[pl.BlockSpec((tm, D), lambda i: (i, 0)),
                   pl.BlockSpec((tm, ROW_SLABS, D // (2 * ROW_SLABS)), lambda i: (i, 0, 0)),
                   pl.BlockSpec((tm, LANES), lambda i: (i, 0))],
        out_shape=[jax.ShapeDtypeStruct((M, D), F32),
                   jax.ShapeDtypeStruct((M, ROW_SLABS, D // (2 * ROW_SLABS)), jnp.uint32),
                   jax.ShapeDtypeStruct((M, LANES), F32)],
        compiler_params=pltpu.CompilerParams(
            dimension_semantics=("arbitrary",), vmem_limit_bytes=_vmem_limit(vm)),
        name="outproj",
    )(oa2, or2, x2, wo_bf, gt, g2, sh, sc, wr_both, wr_hi, br)


def _route_kernel(lg_ref, info_ref, cnt_ref, run_ref):
    i = pl.program_id(0)
    tm = lg_ref.shape[0]

    @pl.when(i == 0)
    def _():
        run_ref[...] = jnp.zeros_like(run_ref)

    lg = lg_ref[...]
    lane = lax.broadcasted_iota(I32, lg.shape, 1)
    gl = jnp.where(lane < N_GROUPS, lg, NEG_BIG)
    gmax = jnp.max(gl, axis=-1, keepdims=True)
    gsum = jnp.sum(jnp.exp(gl - gmax), axis=-1, keepdims=True)
    g_gate = 1.0 / gsum
    g_idx = jnp.min(jnp.where(gl == gmax, lane, LANES), axis=-1, keepdims=True)
    lo = N_GROUPS + EXPERTS_PER_GROUP * g_idx
    el = jnp.where((lane >= lo) & (lane < lo + EXPERTS_PER_GROUP), lg, NEG_BIG)
    e1max = jnp.max(el, axis=-1, keepdims=True)
    l1 = jnp.min(jnp.where(el == e1max, lane, LANES), axis=-1, keepdims=True)
    el2 = jnp.where(lane == l1, NEG_BIG, el)
    e2max = jnp.max(el2, axis=-1, keepdims=True)
    l2 = jnp.min(jnp.where(el2 == e2max, lane, LANES), axis=-1, keepdims=True)
    t2 = jnp.exp(e2max - e1max)
    w1 = g_gate / (1.0 + t2)
    w2 = g_gate * t2 / (1.0 + t2)
    ex1 = l1 - N_GROUPS
    ex2 = l2 - N_GROUPS

    oh1 = jnp.where(lane == ex1, 1.0, 0.0)
    oh2 = jnp.where(lane == ex2, 1.0, 0.0)
    ti = lax.broadcasted_iota(I32, (tm, tm), 0)
    si = lax.broadcasted_iota(I32, (tm, tm), 1)
    lower = jnp.where(si < ti, 1.0, 0.0).astype(BF16)
    pre1 = _dot(lower, oh1.astype(BF16))
    pre2 = _dot(lower, oh2.astype(BF16))
    cnt1 = jnp.sum(oh1, axis=0, keepdims=True)
    cnt2 = jnp.sum(oh2, axis=0, keepdims=True)
    run = run_ref[...]
    rank1 = jnp.sum(oh1 * (pre1 + run), axis=-1, keepdims=True)
    rank2 = jnp.sum(oh2 * (pre2 + run + cnt1), axis=-1, keepdims=True)
    run = run + cnt1 + cnt2
    run_ref[...] = run
    cnt_ref[...] = run

    info = jnp.where(lane == 0, ex1.astype(F32), 0.0)
    info = jnp.where(lane == 1, ex2.astype(F32), info)
    info = jnp.where(lane == 2, rank1, info)
    info = jnp.where(lane == 3, rank2, info)
    info = jnp.where(lane == 4, w1, info)
    info = jnp.where(lane == 5, w2, info)
    info_ref[...] = info


def _route(lg):
    M = lg.shape[0]
    tm = 1024
    return pl.pallas_call(
        _route_kernel,
        grid=(M // tm,),
        in_specs=[pl.BlockSpec((tm, LANES), lambda i: (i, 0))],
        out_specs=[pl.BlockSpec((tm, LANES), lambda i: (i, 0)),
                   pl.BlockSpec((1, LANES), lambda i: (0, 0))],
        out_shape=[jax.ShapeDtypeStruct((M, LANES), F32),
                   jax.ShapeDtypeStruct((1, LANES), F32)],
        scratch_shapes=[pltpu.VMEM((1, LANES), F32)],
        compiler_params=pltpu.CompilerParams(dimension_semantics=("arbitrary",)),
        name="route",
    )(lg)


def _plan_kernel(info_ref, cnt_ref, slot_ref, meta_ref):
    cnt = cnt_ref[...]
    lane_r = lax.broadcasted_iota(I32, (1, LANES), 1)
    nblk = jnp.floor((cnt + (MOE_BLOCK - 1)) * (1.0 / MOE_BLOCK))
    ei = lax.broadcasted_iota(I32, (LANES, LANES), 0)
    ej = lax.broadcasted_iota(I32, (LANES, LANES), 1)
    upper = jnp.where(ei <= ej, 1.0, 0.0).astype(BF16)
    nb8 = jnp.broadcast_to(nblk, (8, LANES)).astype(BF16)
    bend = _dot(nb8, upper)[0:1, :]
    bstart = bend - nblk
    pstart = bstart * MOE_BLOCK

    info = info_ref[...]
    lane = lax.broadcasted_iota(I32, info.shape, 1)
    ex1 = info[:, 0:1].astype(I32)
    ex2 = info[:, 1:2].astype(I32)
    s1 = jnp.sum(jnp.where(lane == ex1, pstart, 0.0), axis=-1, keepdims=True) + info[:, 2:3]
    s2 = jnp.sum(jnp.where(lane == ex2, pstart, 0.0), axis=-1, keepdims=True) + info[:, 3:4]
    slot = jnp.where(lane == 0, s1, 0.0)
    slot = jnp.where(lane == 1, s2, slot)
    slot_ref[...] = slot.astype(I32)

    blk = lax.broadcasted_iota(I32, (LANES, 2 * LANES), 1).astype(F32)
    bend_col = jnp.sum(jnp.where(ei == ej, jnp.broadcast_to(bend, (LANES, LANES)), 0.0),
                       axis=-1, keepdims=True)
    erow = lax.broadcasted_iota(I32, (LANES, 2 * LANES), 0)
    hit = jnp.where((bend_col <= blk) & (erow < N_EXPERTS), 1.0, 0.0)
    bexp = jnp.minimum(jnp.sum(hit, axis=0, keepdims=True), N_EXPERTS - 1.0)
    n_used = jnp.max(jnp.where(lane_r < N_EXPERTS, bend, 0.0), axis=-1, keepdims=True)
    lastblk = jnp.where(nblk > 0, bend - 1.0, -1.0)
    r8 = lax.broadcasted_iota(I32, (8, 2 * LANES), 0)
    last2 = jnp.concatenate([lastblk, jnp.full((1, LANES), -1.0)], axis=1)
    meta = jnp.where(r8 == 0, jnp.broadcast_to(bexp, (8, 2 * LANES)), 0.0)
    meta = jnp.where(r8 == 1, jnp.broadcast_to(n_used, (8, 2 * LANES)), meta)
    meta = jnp.where(r8 == 2, jnp.broadcast_to(last2, (8, 2 * LANES)), meta)
    meta_ref[...] = meta.astype(I32)


def _plan(info, cnt):
    M = info.shape[0]
    tm = 1024
    return pl.pallas_call(
        _plan_kernel,
        grid=(M // tm,),
        in_specs=[pl.BlockSpec((tm, LANES), lambda i: (i, 0)),
                  pl.BlockSpec((1, LANES), lambda i: (0, 0))],
        out_specs=[pl.BlockSpec((tm, LANES), lambda i: (i, 0)),
                   pl.BlockSpec((8, 2 * LANES), lambda i: (0, 0))],
        out_shape=[jax.ShapeDtypeStruct((M, LANES), I32),
                   jax.ShapeDtypeStruct((8, 2 * LANES), I32)],
        compiler_params=pltpu.CompilerParams(dimension_semantics=("arbitrary",)),
        name="plan",
    )(info, cnt)


ROW_DEPTH = 3
DMA_GROUP = 16
WEIGHT_DEPTH = 3


def _slot_table_kernel(slots_ref, pad_ref, inv_ref, buf_ref, sem, *, n_assign):
    fill = pltpu.make_async_copy(pad_ref, buf_ref, sem)
    fill.start()
    fill.wait()

    def put(a, c):
        buf_ref[slots_ref[a] + MOE_BLOCK] = a
        return c

    lax.fori_loop(0, n_assign, put, 0, unroll=16)
    out = pltpu.make_async_copy(buf_ref, inv_ref, sem)
    out.start()
    out.wait()


def _slot_table(slots_km, n_blocks):
    n_assign = slots_km.shape[0]
    n_slots = (n_blocks + ROW_DEPTH) * MOE_BLOCK
    row = np.arange(n_slots)
    pad_ids = n_assign + ((row // MOE_BLOCK + ROW_DEPTH - 1) % ROW_DEPTH) * MOE_BLOCK + row % MOE_BLOCK
    gs = pltpu.PrefetchScalarGridSpec(
        num_scalar_prefetch=1, grid=(1,),
        in_specs=[pl.BlockSpec(memory_space=pl.ANY)],
        out_specs=pl.BlockSpec(memory_space=pl.ANY),
        scratch_shapes=[pltpu.SMEM((n_slots,), I32), pltpu.SemaphoreType.DMA(())])
    return pl.pallas_call(
        functools.partial(_slot_table_kernel, n_assign=n_assign),
        grid_spec=gs,
        out_shape=jax.ShapeDtypeStruct((n_slots,), I32),
        compiler_params=pltpu.CompilerParams(dimension_semantics=("arbitrary",)),
        name="slot_table",
    )(slots_km, jnp.asarray(pad_ids, I32))


def _experts_kernel(bexp_ref, seg_ref, inv_ref, h2_ref, w1_ref, w3_ref, w2_ref, y2_ref,
                    xbuf, ybuf, w1buf, w3buf, w2buf, sem_x, sem_y, sem_w, par_ref, *, n_assign):
    i = pl.program_id(0)
    n_used = seg_ref[N_EXPERTS]
    n_tok = n_assign // 2
    B = MOE_BLOCK
    FF = w1buf.shape[2]
    D = w1buf.shape[1]
    ND = ROW_DEPTH
    NW = w1buf.shape[0]

    def src_row(a):
        if n_tok & (n_tok - 1) == 0:
            return a & (n_tok - 1)
        return lax.rem(a, n_tok)

    def weight_copies(e, slot):
        cps = []
        for w_ref, wbuf in ((w1_ref, w1buf), (w3_ref, w3buf), (w2_ref, w2buf)):
            half = w_ref.shape[1] // 2
            for h in range(2):
                rows = pl.ds(h * half, half)
                cps.append(pltpu.make_async_copy(w_ref.at[e, rows], wbuf.at[slot, rows], sem_w.at[slot]))
        return cps

    def start_weights(e, slot):
        for n, cp in enumerate(weight_copies(e, slot)):
            cp.start(priority=(n + 1) % 2)

    def prefetch_expert_after(e, hops, slot):
        ok = hops >= 0
        cur = e
        for _ in range(hops):
            nb = seg_ref[cur] + 1
            ok = jnp.logical_and(ok, nb < n_used)
            cur = bexp_ref[jnp.where(ok, nb, 0)]

        @pl.when(ok)
        def _():
            start_weights(cur, slot)

    def gather_group(b, g):
        slot = lax.rem(b + ND, ND)
        for r in range(g * DMA_GROUP, (g + 1) * DMA_GROUP):
            a = inv_ref[(b + 1) * B + r]
            pltpu.make_async_copy(h2_ref.at[pl.ds(src_row(a), 1)], xbuf.at[slot, pl.ds(r, 1)],
                                  sem_x.at[slot]).start()

    def scatter_group(b, g):
        slot = lax.rem(b + ND, ND)
        for r in range(g * DMA_GROUP, (g + 1) * DMA_GROUP):
            a = inv_ref[(b + 1) * B + r]
            pltpu.make_async_copy(ybuf.at[slot, pl.ds(r, 1)], y2_ref.at[pl.ds(a, 1)],
                                  sem_y.at[slot]).start()

    def wait_block(buf, sem, b):
        slot = lax.rem(b + ND, ND)
        pltpu.make_async_copy(buf.at[slot], buf.at[slot], sem.at[slot]).wait()

    n_groups = B // DMA_GROUP

    @pl.when(i == 0)
    def _():
        ybuf[...] = jnp.zeros_like(ybuf)
        for s in range(ND - 1):
            pltpu.make_async_copy(ybuf.at[s], y2_ref.at[pl.ds(n_assign + s * B, B)], sem_y.at[s]).start()
        par_ref[0] = 0
        e0 = bexp_ref[0]
        start_weights(e0, 0)
        for hops in range(1, NW):
            prefetch_expert_after(e0, hops, hops)
        for b in range(ND - 1):
            for g in range(n_groups):
                gather_group(b, g)

    active = i < n_used

    @pl.when(active)
    def _():
        e = bexp_ref[i]
        first = jnp.logical_or(i == 0, e != bexp_ref[jnp.maximum(i - 1, 0)])

        @pl.when(jnp.logical_and(first, i > 0))
        def _():
            par_ref[0] = par_ref[0] + 1

        @pl.when(first)
        def _():
            q = par_ref[0]
            for cp in weight_copies(e, lax.rem(q, NW)):
                cp.wait()

            @pl.when(i > 0)
            def _():
                prefetch_expert_after(e, NW - 1, lax.rem(q + NW - 1, NW))

        p = lax.rem(par_ref[0], NW)
        slot = lax.rem(i, ND)
        wait_block(xbuf, sem_x, i)
        wait_block(ybuf, sem_y, i)
        pending = [functools.partial(scatter_group, i - 1, g) for g in range(n_groups)]
        pending += [functools.partial(gather_group, i + ND - 1, g) for g in range(n_groups)]
        n_chunks = 2 * (FF // 256) + ROW_SLABS
        for _ in range(max(len(pending) - n_chunks, 0)):
            pending.pop(0)()

        def after_chunk():
            if pending:
                pending.pop(0)()

        xa, xb = _bf16_pair_unpack(jnp.concatenate([xbuf[slot, :, j, :] for j in range(ROW_SLABS)], axis=1))
        x = jnp.concatenate([xa, xb], axis=1).astype(BF16)
        sw = ybuf.shape[3]
        h1c, h3c = [], []
        for c0 in range(0, FF, 256):
            h1c.append(_dot(x, w1buf[p, :, c0:c0 + 256]))
            after_chunk()
            h3c.append(_dot(x, w3buf[p, :, c0:c0 + 256]))
            after_chunk()
        h1 = jnp.concatenate(h1c, axis=1)
        h3 = jnp.concatenate(h3c, axis=1)
        hid = (h1 * (1.0 / (1.0 + jnp.exp(-h1))) * h3).astype(BF16)
        for j in range(0, ROW_SLABS, 2):
            c0 = j * sw
            ya = _dot(hid, w2buf[p, :, c0:c0 + 2 * sw])
            after_chunk()
            yb = _dot(hid, w2buf[p, :, D // 2 + c0:D // 2 + c0 + 2 * sw])
            yp = _bf16_pair_pack(ya, yb)
            ybuf[slot, :, j, :] = yp[:, :sw]
            ybuf[slot, :, j + 1, :] = yp[:, sw:]
            after_chunk()
        while pending:
            pending.pop(0)()

    @pl.when(i == n_used)
    def _():
        wait_block(xbuf, sem_x, i)
        wait_block(ybuf, sem_y, i)
        for g in range(n_groups):
            scatter_group(i - 1, g)

    @pl.when(jnp.logical_and(i > n_used, i < n_used + ND - 1))
    def _():
        wait_block(xbuf, sem_x, i)
        wait_block(ybuf, sem_y, i)

    @pl.when(i == n_used + ND - 1)
    def _():
        wait_block(ybuf, sem_y, i)


def _experts(bexp, seg_meta, inv, h2, w1, w3, w2, n_blocks):
    M, DJ, SW = h2.shape
    D = 2 * DJ * SW
    FF = w1.shape[2]
    n_assign = 2 * M
    any_spec = pl.BlockSpec(memory_space=pl.ANY)
    row_buf = pltpu.VMEM((ROW_DEPTH, MOE_BLOCK, DJ, SW), jnp.uint32)
    gs = pltpu.PrefetchScalarGridSpec(
        num_scalar_prefetch=3,
        grid=(n_blocks + ROW_DEPTH,),
        in_specs=[any_spec, any_spec, any_spec, any_spec],
        out_specs=any_spec,
        scratch_shapes=[row_buf, row_buf,
                        pltpu.VMEM((WEIGHT_DEPTH, D, FF), F32), pltpu.VMEM((WEIGHT_DEPTH, D, FF), F32),
                        pltpu.VMEM((WEIGHT_DEPTH, FF, D), F32),
                        pltpu.SemaphoreType.DMA((ROW_DEPTH,)), pltpu.SemaphoreType.DMA((ROW_DEPTH,)),
                        pltpu.SemaphoreType.DMA((WEIGHT_DEPTH,)), pltpu.SMEM((1,), I32)],
    )
    vm = WEIGHT_DEPTH * 3 * D * FF * 4 + 2 * ROW_DEPTH * MOE_BLOCK * D * 4 + (12 << 20)
    return pl.pallas_call(
        functools.partial(_experts_kernel, n_assign=n_assign),
        grid_spec=gs,
        out_shape=jax.ShapeDtypeStruct((n_assign + ROW_DEPTH * MOE_BLOCK, DJ, SW), jnp.uint32),
        compiler_params=pltpu.CompilerParams(
            dimension_semantics=("arbitrary",), vmem_limit_bytes=_vmem_limit(vm),
            has_side_effects=True),
        name="experts",
    )(bexp, seg_meta, inv, h2, w1, w3, w2)


def _combine_kernel(x1_ref, info_ref, gt_ref, fg_ref, ya_ref, yb_ref, o_ref):
    info = info_ref[...]
    nj = ya_ref.shape[1]
    ya = jnp.concatenate(_bf16_pair_unpack(jnp.concatenate([ya_ref[:, j, :] for j in range(nj)], axis=1)), axis=1)
    yb = jnp.concatenate(_bf16_pair_unpack(jnp.concatenate([yb_ref[:, j, :] for j in range(nj)], axis=1)), axis=1)
    moe = info[:, 4:5] * ya + info[:, 5:6] * yb
    x = x1_ref[...] + gt_ref[0] * moe
    ms = jnp.mean(x * x, axis=-1, keepdims=True)
    o_ref[...] = x * lax.rsqrt(ms + NORM_EPS) * fg_ref[...]


def _combine(x1, info, gt, final_g, y2, T):
    M, D = x1.shape
    tm = 512
    tps = T // tm
    nt = M // tm
    vm = 2 * 4 * tm * D * 4 + tm * D * 16 + (8 << 20)
    return pl.pallas_call(
        _combine_kernel,
        grid=(nt,),
        in_specs=[pl.BlockSpec((tm, D), lambda i: (i, 0)),
                  pl.BlockSpec((tm, LANES), lambda i: (i, 0)),
                  pl.BlockSpec((1, 1, D), lambda i: (i // tps, 0, 0)),
                  pl.BlockSpec((1, D), lambda i: (0, 0)),
                  pl.BlockSpec((tm, ROW_SLABS, D // (2 * ROW_SLABS)), lambda i: (i, 0, 0)),
                  pl.BlockSpec((tm, ROW_SLABS, D // (2 * ROW_SLABS)), lambda i: (nt + i, 0, 0))],
        out_specs=pl.BlockSpec((tm, D), lambda i: (i, 0)),
        out_shape=jax.ShapeDtypeStruct((M, D), F32),
        compiler_params=pltpu.CompilerParams(
            dimension_semantics=("arbitrary",), vmem_limit_bytes=_vmem_limit(vm)),
        name="combine",
    )(x1, info, gt, final_g.reshape(1, D), y2, y2)


def _rope_tables(T):
    inv_freq = ROPE_THETA ** (-np.arange(0, HEAD_DIM, 2, dtype=np.float64) / HEAD_DIM)
    ang = np.arange(T, dtype=np.float64)[:, None] * inv_freq[None, :]
    cos, sin = np.cos(ang), np.sin(ang)
    cos_h = np.concatenate([cos, cos], axis=1)
    sin_h = np.concatenate([-sin, sin], axis=1)
    return (jnp.asarray(np.tile(cos_h, (1, 2)), F32), jnp.asarray(np.tile(sin_h, (1, 2)), F32))


def _layer(x, c, w_ada, b_ada, norm1_g, w_in, mu_shift, sinks, w0, w_decay_up, a0, w_a_up, w_g_up,
           k_k, k_a, r_k, ln_x_w, ln_x_b, w_o, norm2_g, w_rg, b_rg, w_re, b_re, w1, w3, w2, final_g):
    B, T, D = x.shape
    M = B * T
    mod = _ada(c, w_ada, b_ada)
    sh1, sc1, gt1, sh2, sc2, gt2 = [m.reshape(B, 1, D) for m in jnp.split(mod, 6, axis=-1)]

    pad = RWKV_PAD - RWKV_COLS
    w_bf = _cast_pad(jnp.swapaxes(w_in, 0, 1), ATTN_COLS + RWKV_PAD)
    mu_pad = jnp.pad(mu_shift, (0, pad)).reshape(1, RWKV_PAD)
    cos_t, sin_t = _rope_tables(T)
    x2 = x.reshape(M, D)
    za, zr = _inproj(x2, norm1_g.reshape(1, D), sh1, sc1, w_bf, mu_pad, cos_t, sin_t, T)

    o_attn = _attention(za.reshape(B, T, ATTN_COLS), sinks)
    o_rwkv = _rwkv(zr.reshape(B, T, RWKV_PAD), w0, w_decay_up, a0, w_a_up, w_g_up, k_k, k_a,
                   r_k.reshape(-1), ln_x_w, ln_x_b)

    wr = jnp.concatenate([w_rg, w_re], axis=1)
    wr = jnp.pad(wr, ((0, 0), (0, LANES - wr.shape[1])))
    wr_hi = wr.astype(BF16)
    wr_lo = (wr - wr_hi.astype(F32)).astype(BF16)
    br = jnp.pad(jnp.concatenate([b_rg, b_re]), (0, LANES - N_GROUPS - N_EXPERTS)).reshape(1, LANES)
    x1, h2, lg = _outproj(o_attn.reshape(M, Q_COLS), o_rwkv.reshape(M, RWKV_W), x2, w_o.astype(BF16),
                          gt1, norm2_g.reshape(1, D), sh2, sc2, jnp.concatenate([wr_hi, wr_lo], axis=1),
                          wr_hi, br, T)

    n_blocks = -(-(2 * M) // MOE_BLOCK) + N_EXPERTS
    info, cnt = _route(lg)
    slots, meta = _plan(info, cnt)
    slots_km = slots[:, :2].T.reshape(-1)
    seg_meta = jnp.concatenate([meta[2, :N_EXPERTS], meta[1, :1]])
    inv = _slot_table(slots_km, n_blocks)
    y2 = _experts(meta[0, :n_blocks + ROW_DEPTH], seg_meta, inv, h2, w1, w3, w2, n_blocks)
    out = _combine(x1, info, gt2, final_g, y2, T)
    return out.reshape(B, T, D)


def kernel(x, c, w_ada, b_ada, norm1_g, w_in, mu_shift, sinks, w0, w_decay_up, a0, w_a_up, w_g_up, k_k, k_a, r_k, ln_x_w, ln_x_b, w_o, norm2_g, w_router_group, b_router_group, w_router_expert, b_router_expert, w1, w3, w2, final_g):
    depth = w_ada.shape[0]
    assert depth == 1, "single-layer stack"
    l = 0
    return _layer(x, c, w_ada[l], b_ada[l], norm1_g[l], w_in[l], mu_shift[l], sinks[l], w0[l],
                  w_decay_up[l], a0[l], w_a_up[l], w_g_up[l], k_k[l], k_a[l], r_k[l], ln_x_w[l],
                  ln_x_b[l], w_o[l], norm2_g[l], w_router_group[l], b_router_group[l],
                  w_router_expert[l], b_router_expert[l], w1[l], w3[l], w2[l], final_g)
```

```python
import functools
import math

import jax
import jax.numpy as jnp
import numpy as np
from jax import lax
from jax.experimental import pallas as pl
from jax.experimental.pallas import tpu as pltpu

F32 = jnp.float32
BF16 = jnp.bfloat16
I32 = jnp.int32

LANES = 128
VMEM_BYTES_V7X = 64 * 1024 * 1024

HEAD_DIM = 64
ATTN_HEADS = 16
ATTN_KV_HEADS = 2
ATTN_GROUP = ATTN_HEADS // ATTN_KV_HEADS
WINDOW = 128
ROPE_THETA = 10000.0
RWKV_HEADS = 16
RWKV_N = 64
DECAY_LORA = 64
A_LORA = 64
GATE_LORA = 160
RWKV_LN_EPS = 64e-5
N_GROUPS = 8
EXPERTS_PER_GROUP = 8
N_EXPERTS = N_GROUPS * EXPERTS_PER_GROUP
EXPERT_FF = 512
MOE_BLOCK = 128
ROW_SLABS = 8
NORM_EPS = 1e-6

Q_COLS = ATTN_HEADS * HEAD_DIM
KV_COLS = ATTN_KV_HEADS * HEAD_DIM
ATTN_COLS = Q_COLS + 2 * KV_COLS
RWKV_W = RWKV_HEADS * RWKV_N
LORA_COLS = DECAY_LORA + A_LORA + GATE_LORA
LORA_PAD = 384
RWKV_COLS = 3 * RWKV_W + LORA_COLS
RWKV_PAD = 3 * RWKV_W + LORA_PAD
CHUNK = 64
RWKV_GROUP = 4
NEG_BIG = -1e30


def _vmem_limit(nbytes):
    return int(min(nbytes, VMEM_BYTES_V7X - 4 * 1024 * 1024))


def _dot(a, b):
    return jnp.dot(a, b, preferred_element_type=F32)


def _dot_nt(a, b):
    return lax.dot_general(a, b, (((1,), (1,)), ((), ())), preferred_element_type=F32)


def _dot_tn(a, b):
    return lax.dot_general(a, b, (((0,), (0,)), ((), ())), preferred_element_type=F32)


def _bf16_pair_pack(a, b):
    def hi16(x):
        return pltpu.bitcast(x.astype(BF16).astype(F32), jnp.uint32)
    return hi16(a) | (hi16(b) >> 16)


def _bf16_pair_unpack(w):
    return (pltpu.bitcast(w & jnp.uint32(0xFFFF0000), F32), pltpu.bitcast(w << 16, F32))


def _ada_kernel(c_ref, w_ref, b_ref, o_ref):
    c = c_ref[...]
    s = c * (1.0 / (1.0 + jnp.exp(-c)))
    o_ref[...] = _dot(s, w_ref[...]) + b_ref[...]


def _ada(c, w_ada, b_ada):
    B, D = c.shape
    N = w_ada.shape[1]
    tn = 1024
    cp = jnp.zeros((8, D), F32).at[:B].set(c)
    out = pl.pallas_call(
        _ada_kernel,
        grid=(N // tn,),
        in_specs=[pl.BlockSpec((8, D), lambda j: (0, 0)),
                  pl.BlockSpec((D, tn), lambda j: (0, j)),
                  pl.BlockSpec((1, tn), lambda j: (0, j))],
        out_specs=pl.BlockSpec((8, tn), lambda j: (0, j)),
        out_shape=jax.ShapeDtypeStruct((8, N), F32),
        compiler_params=pltpu.CompilerParams(
            dimension_semantics=("arbitrary",),
            vmem_limit_bytes=_vmem_limit(2 * D * tn * 4 + (8 << 20))),
        name="ada",
    )(cp, w_ada, b_ada.reshape(1, N))
    return out[:B]


def _cast_pad_kernel(wt_ref, o_ref, *, n_valid):
    j = pl.program_id(0)
    tn = wt_ref.shape[0]
    row = j * tn + lax.broadcasted_iota(I32, wt_ref.shape, 0)
    wt = jnp.where(row < n_valid, wt_ref[...], 0.0)
    o_ref[...] = wt.T.astype(o_ref.dtype)


def _cast_pad(w_t, n_out):
    N, K = w_t.shape
    tn = 512
    return pl.pallas_call(
        functools.partial(_cast_pad_kernel, n_valid=N),
        grid=(pl.cdiv(n_out, tn),),
        in_specs=[pl.BlockSpec((tn, K), lambda j: (j, 0))],
        out_specs=pl.BlockSpec((K, tn), lambda j: (0, j)),
        out_shape=jax.ShapeDtypeStruct((K, n_out), BF16),
        compiler_params=pltpu.CompilerParams(dimension_semantics=("arbitrary",)),
        name="cast_pad",
    )(w_t)


def _rope(z, cos, sin):
    w = z.shape[-1]
    lane = lax.broadcasted_iota(I32, z.shape, 1)
    first_half = (lane % HEAD_DIM) < (HEAD_DIM // 2)
    partner = jnp.where(first_half, pltpu.roll(z, w - HEAD_DIM // 2, 1), pltpu.roll(z, HEAD_DIM // 2, 1))
    return z * cos + partner * sin


def _inproj_kernel(x_ref, g_ref, sh_ref, sc_ref, w_ref, mu_ref, cos_ref, sin_ref,
                   za_ref, zr_ref, carry_ref, *, tiles_per_seq):
    i = pl.program_id(0)
    tm = x_ref.shape[0]
    x = x_ref[...]
    ms = jnp.mean(x * x, axis=-1, keepdims=True)
    y = x * lax.rsqrt(ms + NORM_EPS) * g_ref[...]
    h = (y * (1.0 + sc_ref[0]) + sh_ref[0]).astype(BF16)

    cos2 = jnp.concatenate([cos_ref[...], cos_ref[...]], axis=1)
    sin2 = jnp.concatenate([sin_ref[...], sin_ref[...]], axis=1)
    for c0 in range(0, ATTN_COLS, 256):
        z = _dot(h, w_ref[:, c0:c0 + 256])
        if c0 < Q_COLS:
            z = _rope(z, cos2, sin2)
        else:
            z = jnp.concatenate([_rope(z[:, :KV_COLS], cos_ref[...], sin_ref[...]), z[:, KV_COLS:]], axis=1)
        za_ref[:, c0:c0 + 256] = z.astype(za_ref.dtype)

    first = (i % tiles_per_seq) == 0
    row = lax.broadcasted_iota(I32, (tm, 1), 0)
    c0 = 0
    while c0 < RWKV_PAD:
        n = min(512, RWKV_PAD - c0)
        z = _dot(h, w_ref[:, ATTN_COLS + c0:ATTN_COLS + c0 + n])
        prev_last = jnp.where(first, 0.0, carry_ref[7:8, c0:c0 + n])
        z_prev = jnp.where(row == 0, prev_last, pltpu.roll(z, 1, 0))
        carry_ref[:, c0:c0 + n] = z[tm - 8:tm, :]
        zr_ref[:, c0:c0 + n] = z + (z_prev - z) * mu_ref[:, c0:c0 + n]
        c0 += n


def _inproj(x2, g, sh, sc, w_bf, mu_pad, cos_t, sin_t, T):
    M, D = x2.shape
    tm = 256
    tps = T // tm
    NW = w_bf.shape[1]
    kern = functools.partial(_inproj_kernel, tiles_per_seq=tps)
    vm = (D * NW * 2 + 2 * tm * D * 4 + 2 * tm * ATTN_COLS * 2 + 2 * tm * RWKV_PAD * 4
          + 4 * tm * 512 * 4 + tm * D * 8 + (8 << 20))
    return pl.pallas_call(
        kern,
        grid=(M // tm,),
        in_specs=[pl.BlockSpec((tm, D), lambda i: (i, 0)),
                  pl.BlockSpec((1, D), lambda i: (0, 0)),
                  pl.BlockSpec((1, 1, D), lambda i: (i // tps, 0, 0)),
                  pl.BlockSpec((1, 1, D), lambda i: (i // tps, 0, 0)),
                  pl.BlockSpec((D, NW), lambda i: (0, 0), pipeline_mode=pl.Buffered(1)),
                  pl.BlockSpec((1, RWKV_PAD), lambda i: (0, 0)),
                  pl.BlockSpec((tm, 2 * HEAD_DIM), lambda i: (i % tps, 0)),
                  pl.BlockSpec((tm, 2 * HEAD_DIM), lambda i: (i % tps, 0))],
        out_specs=[pl.BlockSpec((tm, ATTN_COLS), lambda i: (i, 0)),
                   pl.BlockSpec((tm, RWKV_PAD), lambda i: (i, 0))],
        out_shape=[jax.ShapeDtypeStruct((M, ATTN_COLS), BF16),
                   jax.ShapeDtypeStruct((M, RWKV_PAD), F32)],
        scratch_shapes=[pltpu.VMEM((8, RWKV_PAD), F32)],
        compiler_params=pltpu.CompilerParams(
            dimension_semantics=("arbitrary",), vmem_limit_bytes=_vmem_limit(vm)),
        name="inproj",
    )(x2, g, sh, sc, w_bf, mu_pad, cos_t, sin_t)


def _attn_kernel(sink_ref, q_ref, kc_ref, kp_ref, vc_ref, vp_ref, o_ref):
    n = pl.program_id(1)
    blk = q_ref.shape[1]
    row = lax.broadcasted_iota(I32, (blk, blk), 0)
    col = lax.broadcasted_iota(I32, (blk, blk), 1)
    mask = jnp.concatenate([(col > row) & (n > 0), col <= row], axis=1)
    scale = 1.0 / math.sqrt(HEAD_DIM)
    outs = []
    for kvh in range(ATTN_KV_HEADS):
        ks = slice(kvh * HEAD_DIM, (kvh + 1) * HEAD_DIM)
        kmat = jnp.concatenate([kp_ref[0, :, ks], kc_ref[0, :, ks]], axis=0)
        vmat = jnp.concatenate([vp_ref[0, :, ks], vc_ref[0, :, ks]], axis=0)
        for g in range(ATTN_GROUP):
            hd = kvh * ATTN_GROUP + g
            qh = q_ref[0, :, hd * HEAD_DIM:(hd + 1) * HEAD_DIM]
            s = _dot_nt(qh, kmat) * scale
            s = jnp.where(mask, s, NEG_BIG)
            sink = sink_ref[hd]
            m = jnp.maximum(jnp.max(s, axis=-1, keepdims=True), sink)
            p = jnp.exp(s - m)
            denom = jnp.sum(p, axis=-1, keepdims=True) + jnp.exp(sink - m)
            o = _dot(p.astype(BF16), vmat)
            outs.append(o / denom)
    o_ref[0] = jnp.concatenate(outs, axis=1).astype(o_ref.dtype)


def _attention(za3, sinks):
    B, T, _ = za3.shape
    nb = T // WINDOW
    kcol = Q_COLS // KV_COLS
    prev = lambda b, n, s: (b, jnp.maximum(n - 1, 0), kcol)
    prev_v = lambda b, n, s: (b, jnp.maximum(n - 1, 0), kcol + 1)
    gs = pltpu.PrefetchScalarGridSpec(
        num_scalar_prefetch=1,
        grid=(B, nb),
        in_specs=[pl.BlockSpec((1, WINDOW, Q_COLS), lambda b, n, s: (b, n, 0)),
                  pl.BlockSpec((1, WINDOW, KV_COLS), lambda b, n, s: (b, n, kcol)),
                  pl.BlockSpec((1, WINDOW, KV_COLS), prev),
                  pl.BlockSpec((1, WINDOW, KV_COLS), lambda b, n, s: (b, n, kcol + 1)),
                  pl.BlockSpec((1, WINDOW, KV_COLS), prev_v)],
        out_specs=pl.BlockSpec((1, WINDOW, Q_COLS), lambda b, n, s: (b, n, 0)),
    )
    return pl.pallas_call(
        _attn_kernel,
        grid_spec=gs,
        out_shape=jax.ShapeDtypeStruct((B, T, Q_COLS), BF16),
        compiler_params=pltpu.CompilerParams(dimension_semantics=("arbitrary", "arbitrary")),
        name="attn",
    )(sinks, za3, za3, za3, za3, za3)


def _rwkv_kernel(r_ref, k_ref, v_ref, lora_ref, w0_ref, wdu_ref, a0_ref, wau_ref, wgu_ref,
                 kk_ref, ka_ref, rk_ref, lnw_ref, lnb_ref, o_ref, s_ref):
    c = pl.program_id(0)
    nseq = r_ref.shape[0]
    C = r_ref.shape[1]
    N = RWKV_N
    G = RWKV_GROUP
    GW = G * N
    ng = RWKV_HEADS // G

    @pl.when(c == 0)
    def _():
        s_ref[...] = jnp.zeros_like(s_ref)

    ti = lax.broadcasted_iota(I32, (C, C), 0)
    si = lax.broadcasted_iota(I32, (C, C), 1)
    tril = jnp.where(si <= ti, 1.0, 0.0).astype(BF16)
    a_seq, g_seq, e_in_seq, e_ex_seq, e_neg_seq = [], [], [], [], []
    for b in range(nseq):
        lora = lora_ref[b]
        wd = lora[:, 0:DECAY_LORA]
        ad = lora[:, DECAY_LORA:DECAY_LORA + A_LORA]
        gd = lora[:, DECAY_LORA + A_LORA:LORA_COLS]
        wlin = w0_ref[...] + _dot(jnp.tanh(wd), wdu_ref[...])
        neg = -wlin
        softplus = jnp.maximum(neg, 0.0) + jnp.log(1.0 + jnp.exp(-jnp.abs(neg)))
        w = -softplus - 0.5
        logdec = -jnp.exp(w)
        a_seq.append(1.0 / (1.0 + jnp.exp(-(a0_ref[...] + _dot(ad, wau_ref[...])))))
        g_seq.append(_dot(1.0 / (1.0 + jnp.exp(-gd)), wgu_ref[...]))
        ld_hi = logdec.astype(BF16)
        ld_lo = (logdec - ld_hi.astype(F32)).astype(BF16)
        cum = _dot(tril, ld_hi) + _dot(tril, ld_lo)
        e_in_seq.append(jnp.exp(cum))
        e_ex_seq.append(jnp.exp(cum - logdec))
        e_neg_seq.append(jnp.exp(-cum))

    lane_head = lax.broadcasted_iota(I32, (1, GW), 1) // N
    t_row = lax.broadcasted_iota(I32, (C, GW), 0)
    s_lane = lax.broadcasted_iota(I32, (C, GW), 1) % N
    strict = s_lane < t_row
    incl = s_lane <= t_row
    bi = lax.broadcasted_iota(I32, (GW, GW), 0)
    bj = lax.broadcasted_iota(I32, (GW, GW), 1)
    same_head = (bi // N) == (bj // N)
    ones_bd = jnp.where(same_head, 1.0, 0.0).astype(BF16)
    eye = jnp.where(s_lane == t_row, 1.0, 0.0)

    def expand(xc):
        return jnp.concatenate([jnp.where(lane_head == h, xc, 0.0) for h in range(G)], axis=0).astype(BF16)

    def head_sum(xs):
        s = _dot(jnp.concatenate(xs, axis=0).astype(BF16), ones_bd)
        return [s[i * C:(i + 1) * C] for i in range(len(xs))]

    seq = [b for b in range(nseq) for _ in range(ng)]
    col = [slice(gi * GW, (gi + 1) * GW) for _ in range(nseq) for gi in range(ng)]
    rng = range(nseq * ng)
    r_l = [r_ref[seq[i], :, col[i]] for i in rng]
    k_l = [k_ref[seq[i], :, col[i]] for i in rng]
    v_l = [v_ref[seq[i], :, col[i]] for i in rng]
    a_l = [a_seq[seq[i]][:, col[i]] for i in rng]
    e_in = [e_in_seq[seq[i]][:, col[i]] for i in rng]
    e_neg = [e_neg_seq[seq[i]][:, col[i]] for i in rng]
    kk0 = [k_l[i] * kk_ref[:, col[i]] for i in rng]
    nrm2 = head_sum([kk0[i] * kk0[i] for i in rng])
    kk = [kk0[i] * lax.rsqrt(jnp.maximum(nrm2[i], 1e-24)) for i in rng]
    k2 = [k_l[i] * (1.0 + (a_l[i] - 1.0) * ka_ref[:, col[i]]) for i in rng]
    bt = [kk[i] * a_l[i] * e_neg[i] for i in rng]
    kt = [k2[i] * e_neg[i] for i in rng]
    left = [jnp.concatenate([-kk[i] * e_ex_seq[seq[i]][:, col[i]], r_l[i] * e_in[i]], axis=0).astype(BF16)
            for i in rng]
    right = [jnp.concatenate([bt[i], kt[i]], axis=0).astype(BF16) for i in rng]
    zed = [jnp.concatenate([expand(bt[i]), expand(kt[i])], axis=0) for i in rng]
    big = [_dot_nt(left[i], zed[i]) for i in rng]
    s_old = [s_ref[i] for i in rng]
    ls = [_dot_nt(left[i], s_old[i].astype(BF16)) for i in rng]
    v_bd = [expand(v_l[i]) for i in rng]
    pw = [jnp.where(strict, big[i][:C, :G * C], 0.0) for i in rng]
    tinv = [eye + pw[i] for i in rng]
    pw = [_dot(pw[i].astype(BF16), expand(pw[i])) for i in rng]
    a_kk = [jnp.concatenate([jnp.where(strict, big[i][:C, G * C:], 0.0),
                             jnp.where(incl, big[i][C:, G * C:], 0.0)], axis=0).astype(BF16) for i in rng]
    kv = [_dot(a_kk[i], v_bd[i]) for i in rng]
    x = [ls[i][:C] + kv[i][:C] for i in rng]
    span = 2
    while span < C:
        pw_bd = [expand(pw[i]) for i in rng]
        if 2 * span < C:
            both = [_dot(jnp.concatenate([pw[i], tinv[i]], axis=0).astype(BF16), pw_bd[i]) for i in rng]
            pw = [both[i][:C] for i in rng]
            tinv = [tinv[i] + both[i][C:] for i in rng]
        else:
            tinv = [tinv[i] + _dot(tinv[i].astype(BF16), pw_bd[i]) for i in rng]
        span *= 2
    u = [_dot(tinv[i].astype(BF16), expand(x[i])) for i in rng]
    y = [ls[i][C:] + kv[i][C:]
         + _dot(jnp.where(incl, big[i][C:, :G * C], 0.0).astype(BF16), expand(u[i])) for i in rng]
    uv = [jnp.concatenate([u[i], v_l[i]], axis=0).astype(BF16) for i in rng]
    for i in rng:
        s_ref[i] = ((s_old[i] + jnp.where(same_head, _dot_tn(uv[i], right[i]), 0.0))
                    * e_in[i][C - 1:C, :])
    ysum = head_sum(y)
    yc = [y[i] - ysum[i] * (1.0 / N) for i in rng]
    ysq = head_sum([yc[i] * yc[i] for i in rng])
    var = [ysq[i] * (1.0 / N) for i in rng]
    rksum = head_sum([r_l[i] * k2[i] * rk_ref[:, col[i]] for i in rng])
    bonus = [rksum[i] * v_l[i] for i in rng]
    for i in rng:
        yn = yc[i] * lax.rsqrt(var[i] + RWKV_LN_EPS) * lnw_ref[:, col[i]] + lnb_ref[:, col[i]]
        o_ref[seq[i], :, col[i]] = ((yn + bonus[i]) * g_seq[seq[i]][:, col[i]]).astype(o_ref.dtype)


def _rwkv(zr3, w0, wdu, a0, wau, wgu, k_k, k_a, r_k, ln_w, ln_b):
    B, T, _ = zr3.shape
    C = CHUNK
    W = RWKV_W
    GW = RWKV_GROUP * RWKV_N
    vec = lambda v: v.reshape(1, W)
    full = lambda shape: pl.BlockSpec(shape, lambda c: (0,) * len(shape))
    return pl.pallas_call(
        _rwkv_kernel,
        grid=(T // C,),
        in_specs=[pl.BlockSpec((B, C, W), lambda c: (0, c, 0)),
                  pl.BlockSpec((B, C, W), lambda c: (0, c, 1)),
                  pl.BlockSpec((B, C, W), lambda c: (0, c, 2)),
                  pl.BlockSpec((B, C, LORA_PAD), lambda c: (0, c, 3 * W // LORA_PAD)),
                  full((1, W)), full((DECAY_LORA, W)), full((1, W)), full((A_LORA, W)),
                  full((GATE_LORA, W)), full((1, W)), full((1, W)), full((1, W)),
                  full((1, W)), full((1, W))],
        out_specs=pl.BlockSpec((B, C, W), lambda c: (0, c, 0)),
        out_shape=jax.ShapeDtypeStruct((B, T, W), BF16),
        scratch_shapes=[pltpu.VMEM((B * RWKV_HEADS // RWKV_GROUP, GW, GW), F32)],
        compiler_params=pltpu.CompilerParams(dimension_semantics=("arbitrary",)),
        name="rwkv",
    )(zr3, zr3, zr3, zr3, vec(w0), wdu, vec(a0), wau, wgu, vec(k_k), vec(k_a), vec(r_k),
      vec(ln_w), vec(ln_b))


def _outproj_kernel(oa_ref, or_ref, x_ref, wo_ref, gt_ref, g_ref, sh_ref, sc_ref,
                    wrb_ref, wrh_ref, br_ref, x1_ref, h2_ref, lg_ref):
    mixed = _dot(oa_ref[...], wo_ref[0:Q_COLS, :]) + _dot(or_ref[...], wo_ref[Q_COLS:, :])
    x1 = x_ref[...] + gt_ref[0] * mixed
    x1_ref[...] = x1
    ms = jnp.mean(x1 * x1, axis=-1, keepdims=True)
    h2 = x1 * lax.rsqrt(ms + NORM_EPS) * g_ref[...] * (1.0 + sc_ref[0]) + sh_ref[0]
    half = h2.shape[1] // 2
    h2p = _bf16_pair_pack(h2[:, :half], h2[:, half:])
    for j in range(ROW_SLABS):
        w = h2_ref.shape[2]
        h2_ref[:, j, :] = h2p[:, j * w:(j + 1) * w]
    hh = h2.astype(BF16)
    hl = (h2 - hh.astype(F32)).astype(BF16)
    both = _dot(hh, wrb_ref[...])
    lg_ref[...] = both[:, :LANES] + both[:, LANES:] + _dot(hl, wrh_ref[...]) + br_ref[...]


def _outproj(oa2, or2, x2, wo_bf, gt, g2, sh, sc, wr_both, wr_hi, br, T):
    M, D = x2.shape
    tm = 256
    tps = T // tm
    bvec = pl.BlockSpec((1, 1, D), lambda i: (i // tps, 0, 0))
    vm = D * D * 2 + 2 * tm * D * (2 + 4 + 4 + 4) + tm * D * 16 + (8 << 20)
    return pl.pallas_call(
        _outproj_kernel,
        grid=(M // tm,),
        in_specs=[pl.BlockSpec((tm, Q_COLS), lambda i: (i, 0)),
                  pl.BlockSpec((tm, RWKV_W), lambda i: (i, 0)),
                  pl.BlockSpec((tm, D), lambda i: (i, 0)),
                  pl.BlockSpec((D, D), lambda i: (0, 0), pipeline_mode=pl.Buffered(1)),
                  bvec,
                  pl.BlockSpec((1, D), lambda i: (0, 0)),
                  bvec, bvec,
                  pl.BlockSpec((D, 2 * LANES), lambda i: (0, 0)),
                  pl.BlockSpec((D, LANES), lambda i: (0, 0)),
                  pl.BlockSpec((1, LANES), lambda i: (0, 0))],
        out_specs=[pl.BlockSpec((tm, D), lambda i: (i, 0)),
                   pl.BlockSpec((tm, ROW_SLABS, D // (2 * ROW_SLABS)), lambda i: (i, 0, 0)),
                   pl.BlockSpec((tm, LANES), lambda i: (i, 0))],
        out_shape=[jax.ShapeDtypeStruct((M, D), F32),
                   jax.ShapeDtypeStruct((M, ROW_SLABS, D // (2 * ROW_SLABS)), jnp.uint32),
                   jax.ShapeDtypeStruct((M, LANES), F32)],
        compiler_params=pltpu.CompilerParams(
            dimension_semantics=("arbitrary",), vmem_limit_bytes=_vmem_limit(vm)),
        name="outproj",
    )(oa2, or2, x2, wo_bf, gt, g2, sh, sc, wr_both, wr_hi, br)


def _route_kernel(lg_ref, info_ref, cnt_ref, run_ref):
    i = pl.program_id(0)
    tm = lg_ref.shape[0]

    @pl.when(i == 0)
    def _():
        run_ref[...] = jnp.zeros_like(run_ref)

    lg = lg_ref[...]
    lane = lax.broadcasted_iota(I32, lg.shape, 1)
    gl = jnp.where(lane < N_GROUPS, lg, NEG_BIG)
    gmax = jnp.max(gl, axis=-1, keepdims=True)
    gsum = jnp.sum(jnp.exp(gl - gmax), axis=-1, keepdims=True)
    g_gate = 1.0 / gsum
    g_idx = jnp.min(jnp.where(gl == gmax, lane, LANES), axis=-1, keepdims=True)
    lo = N_GROUPS + EXPERTS_PER_GROUP * g_idx
    el = jnp.where((lane >= lo) & (lane < lo + EXPERTS_PER_GROUP), lg, NEG_BIG)
    e1max = jnp.max(el, axis=-1, keepdims=True)
    l1 = jnp.min(jnp.where(el == e1max, lane, LANES), axis=-1, keepdims=True)
    el2 = jnp.where(lane == l1, NEG_BIG, el)
    e2max = jnp.max(el2, axis=-1, keepdims=True)
    l2 = jnp.min(jnp.where(el2 == e2max, lane, LANES), axis=-1, keepdims=True)
    t2 = jnp.exp(e2max - e1max)
    w1 = g_gate / (1.0 + t2)
    w2 = g_gate * t2 / (1.0 + t2)
    ex1 = l1 - N_GROUPS
    ex2 = l2 - N_GROUPS

    oh1 = jnp.where(lane == ex1, 1.0, 0.0)
    oh2 = jnp.where(lane == ex2, 1.0, 0.0)
    ti = lax.broadcasted_iota(I32, (tm, tm), 0)
    si = lax.broadcasted_iota(I32, (tm, tm), 1)
    lower = jnp.where(si < ti, 1.0, 0.0).astype(BF16)
    pre1 = _dot(lower, oh1.astype(BF16))
    pre2 = _dot(lower, oh2.astype(BF16))
    cnt1 = jnp.sum(oh1, axis=0, keepdims=True)
    cnt2 = jnp.sum(oh2, axis=0, keepdims=True)
    run = run_ref[...]
    rank1 = jnp.sum(oh1 * (pre1 + run), axis=-1, keepdims=True)
    rank2 = jnp.sum(oh2 * (pre2 + run + cnt1), axis=-1, keepdims=True)
    run = run + cnt1 + cnt2
    run_ref[...] = run
    cnt_ref[...] = run

    info = jnp.where(lane == 0, ex1.astype(F32), 0.0)
    info = jnp.where(lane == 1, ex2.astype(F32), info)
    info = jnp.where(lane == 2, rank1, info)
    info = jnp.where(lane == 3, rank2, info)
    info = jnp.where(lane == 4, w1, info)
    info = jnp.where(lane == 5, w2, info)
    info_ref[...] = info


def _route(lg):
    M = lg.shape[0]
    tm = 1024
    return pl.pallas_call(
        _route_kernel,
        grid=(M // tm,),
        in_specs=[pl.BlockSpec((tm, LANES), lambda i: (i, 0))],
        out_specs=[pl.BlockSpec((tm, LANES), lambda i: (i, 0)),
                   pl.BlockSpec((1, LANES), lambda i: (0, 0))],
        out_shape=[jax.ShapeDtypeStruct((M, LANES), F32),
                   jax.ShapeDtypeStruct((1, LANES), F32)],
        scratch_shapes=[pltpu.VMEM((1, LANES), F32)],
        compiler_params=pltpu.CompilerParams(dimension_semantics=("arbitrary",)),
        name="route",
    )(lg)


def _plan_kernel(info_ref, cnt_ref, slot_ref, meta_ref):
    cnt = cnt_ref[...]
    lane_r = lax.broadcasted_iota(I32, (1, LANES), 1)
    nblk = jnp.floor((cnt + (MOE_BLOCK - 1)) * (1.0 / MOE_BLOCK))
    ei = lax.broadcasted_iota(I32, (LANES, LANES), 0)
    ej = lax.broadcasted_iota(I32, (LANES, LANES), 1)
    upper = jnp.where(ei <= ej, 1.0, 0.0).astype(BF16)
    nb8 = jnp.broadcast_to(nblk, (8, LANES)).astype(BF16)
    bend = _dot(nb8, upper)[0:1, :]
    bstart = bend - nblk
    pstart = bstart * MOE_BLOCK

    info = info_ref[...]
    lane = lax.broadcasted_iota(I32, info.shape, 1)
    ex1 = info[:, 0:1].astype(I32)
    ex2 = info[:, 1:2].astype(I32)
    s1 = jnp.sum(jnp.where(lane == ex1, pstart, 0.0), axis=-1, keepdims=True) + info[:, 2:3]
    s2 = jnp.sum(jnp.where(lane == ex2, pstart, 0.0), axis=-1, keepdims=True) + info[:, 3:4]
    slot = jnp.where(lane == 0, s1, 0.0)
    slot = jnp.where(lane == 1, s2, slot)
    slot_ref[...] = slot.astype(I32)

    blk = lax.broadcasted_iota(I32, (LANES, 2 * LANES), 1).astype(F32)
    bend_col = jnp.sum(jnp.where(ei == ej, jnp.broadcast_to(bend, (LANES, LANES)), 0.0),
                       axis=-1, keepdims=True)
    erow = lax.broadcasted_iota(I32, (LANES, 2 * LANES), 0)
    hit = jnp.where((bend_col <= blk) & (erow < N_EXPERTS), 1.0, 0.0)
    bexp = jnp.minimum(jnp.sum(hit, axis=0, keepdims=True), N_EXPERTS - 1.0)
    n_used = jnp.max(jnp.where(lane_r < N_EXPERTS, bend, 0.0), axis=-1, keepdims=True)
    lastblk = jnp.where(nblk > 0, bend - 1.0, -1.0)
    r8 = lax.broadcasted_iota(I32, (8, 2 * LANES), 0)
    last2 = jnp.concatenate([lastblk, jnp.full((1, LANES), -1.0)], axis=1)
    meta = jnp.where(r8 == 0, jnp.broadcast_to(bexp, (8, 2 * LANES)), 0.0)
    meta = jnp.where(r8 == 1, jnp.broadcast_to(n_used, (8, 2 * LANES)), meta)
    meta = jnp.where(r8 == 2, jnp.broadcast_to(last2, (8, 2 * LANES)), meta)
    meta_ref[...] = meta.astype(I32)


def _plan(info, cnt):
    M = info.shape[0]
    tm = 1024
    return pl.pallas_call(
        _plan_kernel,
        grid=(M // tm,),
        in_specs=[pl.BlockSpec((tm, LANES), lambda i: (i, 0)),
                  pl.BlockSpec((1, LANES), lambda i: (0, 0))],
        out_specs=[pl.BlockSpec((tm, LANES), lambda i: (i, 0)),
                   pl.BlockSpec((8, 2 * LANES), lambda i: (0, 0))],
        out_shape=[jax.ShapeDtypeStruct((M, LANES), I32),
                   jax.ShapeDtypeStruct((8, 2 * LANES), I32)],
        compiler_params=pltpu.CompilerParams(dimension_semantics=("arbitrary",)),
        name="plan",
    )(info, cnt)


ROW_DEPTH = 4
DMA_GROUP = 16
WEIGHT_DEPTH = 3


def _slot_table_kernel(slots_ref, pad_ref, inv_ref, buf_ref, sem, *, n_assign):
    fill = pltpu.make_async_copy(pad_ref, buf_ref, sem)
    fill.start()
    fill.wait()

    def put(a, c):
        buf_ref[slots_ref[a] + MOE_BLOCK] = a
        return c

    lax.fori_loop(0, n_assign, put, 0, unroll=16)
    out = pltpu.make_async_copy(buf_ref, inv_ref, sem)
    out.start()
    out.wait()


def _slot_table(slots_km, n_blocks):
    n_assign = slots_km.shape[0]
    n_slots = (n_blocks + ROW_DEPTH) * MOE_BLOCK
    row = np.arange(n_slots)
    pad_ids = n_assign + ((row // MOE_BLOCK + ROW_DEPTH - 1) % ROW_DEPTH) * MOE_BLOCK + row % MOE_BLOCK
    gs = pltpu.PrefetchScalarGridSpec(
        num_scalar_prefetch=1, grid=(1,),
        in_specs=[pl.BlockSpec(memory_space=pl.ANY)],
        out_specs=pl.BlockSpec(memory_space=pl.ANY),
        scratch_shapes=[pltpu.SMEM((n_slots,), I32), pltpu.SemaphoreType.DMA(())])
    return pl.pallas_call(
        functools.partial(_slot_table_kernel, n_assign=n_assign),
        grid_spec=gs,
        out_shape=jax.ShapeDtypeStruct((n_slots,), I32),
        compiler_params=pltpu.CompilerParams(dimension_semantics=("arbitrary",)),
        name="slot_table",
    )(slots_km, jnp.asarray(pad_ids, I32))


def _experts_kernel(bexp_ref, seg_ref, inv_ref, h2_ref, w1_ref, w3_ref, w2_ref, y2_ref,
                    xbuf, ybuf, w1buf, w3buf, w2buf, sem_x, sem_y, sem_w, par_ref, *, n_assign):
    i = pl.program_id(0)
    n_used = seg_ref[N_EXPERTS]
    n_tok = n_assign // 2
    B = MOE_BLOCK
    FF = w1buf.shape[2]
    D = w1buf.shape[1]
    ND = ROW_DEPTH
    NW = w1buf.shape[0]

    def src_row(a):
        if n_tok & (n_tok - 1) == 0:
            return a & (n_tok - 1)
        return lax.rem(a, n_tok)

    def weight_copies(e, slot):
        return (pltpu.make_async_copy(w1_ref.at[e], w1buf.at[slot], sem_w.at[slot]),
                pltpu.make_async_copy(w3_ref.at[e], w3buf.at[slot], sem_w.at[slot]),
                pltpu.make_async_copy(w2_ref.at[e], w2buf.at[slot], sem_w.at[slot]))

    def start_weights(e, slot):
        c1, c3, c2 = weight_copies(e, slot)
        c1.start(priority=1)
        c3.start(priority=1)
        c2.start(priority=0)

    def prefetch_expert_after(e, hops, slot):
        ok = hops >= 0
        cur = e
        for _ in range(hops):
            nb = seg_ref[cur] + 1
            ok = jnp.logical_and(ok, nb < n_used)
            cur = bexp_ref[jnp.where(ok, nb, 0)]

        @pl.when(ok)
        def _():
            start_weights(cur, slot)

    def gather_group(b, g):
        slot = lax.rem(b + ND, ND)
        for r in range(g * DMA_GROUP, (g + 1) * DMA_GROUP):
            a = inv_ref[(b + 1) * B + r]
            pltpu.make_async_copy(h2_ref.at[pl.ds(src_row(a), 1)], xbuf.at[slot, pl.ds(r, 1)],
                                  sem_x.at[slot]).start()

    def scatter_group(b, g):
        slot = lax.rem(b + ND, ND)
        for r in range(g * DMA_GROUP, (g + 1) * DMA_GROUP):
            a = inv_ref[(b + 1) * B + r]
            pltpu.make_async_copy(ybuf.at[slot, pl.ds(r, 1)], y2_ref.at[pl.ds(a, 1)],
                                  sem_y.at[slot]).start()

    def wait_block(buf, sem, b):
        slot = lax.rem(b + ND, ND)
        pltpu.make_async_copy(buf.at[slot], buf.at[slot], sem.at[slot]).wait()

    n_groups = B // DMA_GROUP

    @pl.when(i == 0)
    def _():
        ybuf[...] = jnp.zeros_like(ybuf)
        for s in range(ND - 1):
            pltpu.make_async_copy(ybuf.at[s], y2_ref.at[pl.ds(n_assign + s * B, B)], sem_y.at[s]).start()
        par_ref[0] = 0
        e0 = bexp_ref[0]
        start_weights(e0, 0)
        for hops in range(1, NW):
            prefetch_expert_after(e0, hops, hops)
        for b in range(ND - 1):
            for g in range(n_groups):
                gather_group(b, g)

    active = i < n_used

    @pl.when(active)
    def _():
        e = bexp_ref[i]
        first = jnp.logical_or(i == 0, e != bexp_ref[jnp.maximum(i - 1, 0)])

        @pl.when(jnp.logical_and(first, i > 0))
        def _():
            par_ref[0] = par_ref[0] + 1

        @pl.when(first)
        def _():
            q = par_ref[0]
            for cp in weight_copies(e, lax.rem(q, NW)):
                cp.wait()

            @pl.when(i > 0)
            def _():
                prefetch_expert_after(e, NW - 1, lax.rem(q + NW - 1, NW))

        p = lax.rem(par_ref[0], NW)
        slot = lax.rem(i, ND)
        wait_block(xbuf, sem_x, i)
        wait_block(ybuf, sem_y, i)
        pending = [functools.partial(scatter_group, i - 1, g) for g in range(n_groups)]
        pending += [functools.partial(gather_group, i + ND - 1, g) for g in range(n_groups)]
        n_chunks = 2 * (FF // 256) + ROW_SLABS
        for _ in range(max(len(pending) - n_chunks, 0)):
            pending.pop(0)()

        def after_chunk():
            if pending:
                pending.pop(0)()

        xa, xb = _bf16_pair_unpack(jnp.concatenate([xbuf[slot, :, j, :] for j in range(ROW_SLABS)], axis=1))
        x = jnp.concatenate([xa, xb], axis=1).astype(BF16)
        sw = ybuf.shape[3]
        h1c, h3c = [], []
        for c0 in range(0, FF, 256):
            h1c.append(_dot(x, w1buf[p, :, c0:c0 + 256]))
            after_chunk()
            h3c.append(_dot(x, w3buf[p, :, c0:c0 + 256]))
            after_chunk()
        h1 = jnp.concatenate(h1c, axis=1)
        h3 = jnp.concatenate(h3c, axis=1)
        hid = (h1 * (1.0 / (1.0 + jnp.exp(-h1))) * h3).astype(BF16)
        for j in range(0, ROW_SLABS, 2):
            c0 = j * sw
            ya = _dot(hid, w2buf[p, :, c0:c0 + 2 * sw])
            after_chunk()
            yb = _dot(hid, w2buf[p, :, D // 2 + c0:D // 2 + c0 + 2 * sw])
            yp = _bf16_pair_pack(ya, yb)
            ybuf[slot, :, j, :] = yp[:, :sw]
            ybuf[slot, :, j + 1, :] = yp[:, sw:]
            after_chunk()
        while pending:
            pending.pop(0)()

    @pl.when(i == n_used)
    def _():
        wait_block(xbuf, sem_x, i)
        wait_block(ybuf, sem_y, i)
        for g in range(n_groups):
            scatter_group(i - 1, g)

    @pl.when(jnp.logical_and(i > n_used, i < n_used + ND - 1))
    def _():
        wait_block(xbuf, sem_x, i)
        wait_block(ybuf, sem_y, i)

    @pl.when(i == n_used + ND - 1)
    def _():
        wait_block(ybuf, sem_y, i)


def _experts(bexp, seg_meta, inv, h2, w1, w3, w2, n_blocks):
    M, DJ, SW = h2.shape
    D = 2 * DJ * SW
    FF = w1.shape[2]
    n_assign = 2 * M
    any_spec = pl.BlockSpec(memory_space=pl.ANY)
    row_buf = pltpu.VMEM((ROW_DEPTH, MOE_BLOCK, DJ, SW), jnp.uint32)
    gs = pltpu.PrefetchScalarGridSpec(
        num_scalar_prefetch=3,
        grid=(n_blocks + ROW_DEPTH,),
        in_specs=[any_spec, any_spec, any_spec, any_spec],
        out_specs=any_spec,
        scratch_shapes=[row_buf, row_buf,
                        pltpu.VMEM((WEIGHT_DEPTH, D, FF), F32), pltpu.VMEM((WEIGHT_DEPTH, D, FF), F32),
                        pltpu.VMEM((WEIGHT_DEPTH, FF, D), F32),
                        pltpu.SemaphoreType.DMA((ROW_DEPTH,)), pltpu.SemaphoreType.DMA((ROW_DEPTH,)),
                        pltpu.SemaphoreType.DMA((WEIGHT_DEPTH,)), pltpu.SMEM((1,), I32)],
    )
    vm = WEIGHT_DEPTH * 3 * D * FF * 4 + 2 * ROW_DEPTH * MOE_BLOCK * D * 4 + (12 << 20)
    return pl.pallas_call(
        functools.partial(_experts_kernel, n_assign=n_assign),
        grid_spec=gs,
        out_shape=jax.ShapeDtypeStruct((n_assign + ROW_DEPTH * MOE_BLOCK, DJ, SW), jnp.uint32),
        compiler_params=pltpu.CompilerParams(
            dimension_semantics=("arbitrary",), vmem_limit_bytes=_vmem_limit(vm),
            has_side_effects=True),
        name="experts",
    )(bexp, seg_meta, inv, h2, w1, w3, w2)


def _combine_kernel(x1_ref, info_ref, gt_ref, fg_ref, ya_ref, yb_ref, o_ref):
    info = info_ref[...]
    nj = ya_ref.shape[1]
    ya = jnp.concatenate(_bf16_pair_unpack(jnp.concatenate([ya_ref[:, j, :] for j in range(nj)], axis=1)), axis=1)
    yb = jnp.concatenate(_bf16_pair_unpack(jnp.concatenate([yb_ref[:, j, :] for j in range(nj)], axis=1)), axis=1)
    moe = info[:, 4:5] * ya + info[:, 5:6] * yb
    x = x1_ref[...] + gt_ref[0] * moe
    ms = jnp.mean(x * x, axis=-1, keepdims=True)
    o_ref[...] = x * lax.rsqrt(ms + NORM_EPS) * fg_ref[...]


def _combine(x1, info, gt, final_g, y2, T):
    M, D = x1.shape
    tm = 512
    tps = T // tm
    nt = M // tm
    vm = 2 * 4 * tm * D * 4 + tm * D * 16 + (8 << 20)
    return pl.pallas_call(
        _combine_kernel,
        grid=(nt,),
        in_specs=[pl.BlockSpec((tm, D), lambda i: (i, 0)),
                  pl.BlockSpec((tm, LANES), lambda i: (i, 0)),
                  pl.BlockSpec((1, 1, D), lambda i: (i // tps, 0, 0)),
                  pl.BlockSpec((1, D), lambda i: (0, 0)),
                  pl.BlockSpec((tm, ROW_SLABS, D // (2 * ROW_SLABS)), lambda i: (i, 0, 0)),
                  pl.BlockSpec((tm, ROW_SLABS, D // (2 * ROW_SLABS)), lambda i: (nt + i, 0, 0))],
        out_specs=pl.BlockSpec((tm, D), lambda i: (i, 0)),
        out_shape=jax.ShapeDtypeStruct((M, D), F32),
        compiler_params=pltpu.CompilerParams(
            dimension_semantics=("arbitrary",), vmem_limit_bytes=_vmem_limit(vm)),
        name="combine",
    )(x1, info, gt, final_g.reshape(1, D), y2, y2)


def _rope_tables(T):
    inv_freq = ROPE_THETA ** (-np.arange(0, HEAD_DIM, 2, dtype=np.float64) / HEAD_DIM)
    ang = np.arange(T, dtype=np.float64)[:, None] * inv_freq[None, :]
    cos, sin = np.cos(ang), np.sin(ang)
    cos_h = np.concatenate([cos, cos], axis=1)
    sin_h = np.concatenate([-sin, sin], axis=1)
    return (jnp.asarray(np.tile(cos_h, (1, 2)), F32), jnp.asarray(np.tile(sin_h, (1, 2)), F32))


def _layer(x, c, w_ada, b_ada, norm1_g, w_in, mu_shift, sinks, w0, w_decay_up, a0, w_a_up, w_g_up,
           k_k, k_a, r_k, ln_x_w, ln_x_b, w_o, norm2_g, w_rg, b_rg, w_re, b_re, w1, w3, w2, final_g):
    B, T, D = x.shape
    M = B * T
    mod = _ada(c, w_ada, b_ada)
    sh1, sc1, gt1, sh2, sc2, gt2 = [m.reshape(B, 1, D) for m in jnp.split(mod, 6, axis=-1)]

    pad = RWKV_PAD - RWKV_COLS
    w_bf = _cast_pad(jnp.swapaxes(w_in, 0, 1), ATTN_COLS + RWKV_PAD)
    mu_pad = jnp.pad(mu_shift, (0, pad)).reshape(1, RWKV_PAD)
    cos_t, sin_t = _rope_tables(T)
    x2 = x.reshape(M, D)
    za, zr = _inproj(x2, norm1_g.reshape(1, D), sh1, sc1, w_bf, mu_pad, cos_t, sin_t, T)

    o_attn = _attention(za.reshape(B, T, ATTN_COLS), sinks)
    o_rwkv = _rwkv(zr.reshape(B, T, RWKV_PAD), w0, w_decay_up, a0, w_a_up, w_g_up, k_k, k_a,
                   r_k.reshape(-1), ln_x_w, ln_x_b)

    wr = jnp.concatenate([w_rg, w_re], axis=1)
    wr = jnp.pad(wr, ((0, 0), (0, LANES - wr.shape[1])))
    wr_hi = wr.astype(BF16)
    wr_lo = (wr - wr_hi.astype(F32)).astype(BF16)
    br = jnp.pad(jnp.concatenate([b_rg, b_re]), (0, LANES - N_GROUPS - N_EXPERTS)).reshape(1, LANES)
    x1, h2, lg = _outproj(o_attn.reshape(M, Q_COLS), o_rwkv.reshape(M, RWKV_W), x2, w_o.astype(BF16),
                          gt1, norm2_g.reshape(1, D), sh2, sc2, jnp.concatenate([wr_hi, wr_lo], axis=1),
                          wr_hi, br, T)

    n_blocks = -(-(2 * M) // MOE_BLOCK) + N_EXPERTS
    info, cnt = _route(lg)
    slots, meta = _plan(info, cnt)
    slots_km = slots[:, :2].T.reshape(-1)
    seg_meta = jnp.concatenate([meta[2, :N_EXPERTS], meta[1, :1]])
    inv = _slot_table(slots_km, n_blocks)
    y2 = _experts(meta[0, :n_blocks + ROW_DEPTH], seg_meta, inv, h2, w1, w3, w2, n_blocks)
    out = _combine(x1, info, gt2, final_g, y2, T)
    return out.reshape(B, T, D)


def kernel(x, c, w_ada, b_ada, norm1_g, w_in, mu_shift, sinks, w0, w_decay_up, a0, w_a_up, w_g_up, k_k, k_a, r_k, ln_x_w, ln_x_b, w_o, norm2_g, w_router_group, b_router_group, w_router_expert, b_router_expert, w1, w3, w2, final_g):
    depth = w_ada.shape[0]
    assert depth == 1, "single-layer stack"
    l = 0
    return _layer(x, c, w_ada[l], b_ada[l], norm1_g[l], w_in[l], mu_shift[l], sinks[l], w0[l],
                  w_decay_up[l], a0[l], w_a_up[l], w_g_up[l], k_k[l], k_a[l], r_k[l], ln_x_w[l],
                  ln_x_b[l], w_o[l], norm2_g[l], w_router_group[l], b_router_group[l],
                  w_router_expert[l], b_router_expert[l], w1[l], w3[l], w2[l], final_g)
```
